```python
import jax
import jax.numpy as jnp
from jax import lax
import numpy as np

D_MODEL = 1024
BATCH = 16
SEQ = 4096
DEPTH = 4

GRID_W = 64
CTX_LEN = 256
N_MIXERS = 3

DN_ALPHA = (2 * DEPTH) ** 0.25
DN_BETA = (8 * DEPTH) ** -0.25
LN_EPS = 1e-5
RMS_EPS = 1e-6
ROPE_THETA = 10000.0
NEG_INF = -1e30

D_RNN = 3 * D_MODEL // 2
RG_BLOCKS = 6
RG_BW = D_RNN // RG_BLOCKS
CONV_W = 4
RG_C = 8.0

GQA_HEADS = 16
GQA_KV = 2
GQA_GROUP = GQA_HEADS // GQA_KV
GQA_HD = 64
WINDOW = 128
Q_BLOCK = 128

MLA_HEADS = 16
Q_LORA = 256
KV_LORA = 128
QK_NOPE = 64
QK_ROPE = 32
V_HD = 64

N_EXPERTS = 32
TOP_K = 4
D_FF = 1024
SWIGLU_LIMIT = 7.0
SWIGLU_ALPHA = 1.702
MOE_BLOCK = 256

N_RG = (DEPTH + 2) // 3
N_GQA = (DEPTH + 1) // 3
N_MLA = DEPTH // 3

kernel_name = 'hybrid_rglru_swa_mla_moe_flow_backbone'


def layer_norm(x, g, b):
    xf = x.astype(jnp.float32)
    mu = jnp.mean(xf, axis=-1, keepdims=True)
    var = jnp.mean(jnp.square(xf - mu), axis=-1, keepdims=True)
    return ((xf - mu) * lax.rsqrt(var + LN_EPS)).astype(x.dtype) * g + b


def rms_norm(x, g):
    xf = x.astype(jnp.float32)
    return (xf * lax.rsqrt(jnp.mean(xf * xf, axis=-1, keepdims=True) + RMS_EPS)).astype(x.dtype) * g


def modulate(h, shift, scale):
    return h * (1 + scale) + shift


def axial_rope_tables(n_tok, rot_dim):
    rows = n_tok // GRID_W
    row = jnp.repeat(jnp.arange(rows, dtype=jnp.float32), GRID_W)
    col = jnp.tile(jnp.arange(GRID_W, dtype=jnp.float32), rows)
    n_freq = rot_dim // 4
    inv_freq = ROPE_THETA ** (-jnp.arange(n_freq, dtype=jnp.float32) / n_freq)
    ang = jnp.concatenate([row[:, None] * inv_freq, col[:, None] * inv_freq], axis=-1)
    return jnp.cos(ang), jnp.sin(ang)


def apply_rope(x, cos, sin):
    half = x.shape[-1] // 2
    xf = x.astype(jnp.float32)
    x1, x2 = xf[..., :half], xf[..., half:]
    cs, sn = cos[None, :, None, :], sin[None, :, None, :]
    return jnp.concatenate([x1 * cs - x2 * sn, x2 * cs + x1 * sn], axis=-1).astype(x.dtype)


def depthwise_conv(x, w, b):
    y = lax.conv_general_dilated(x, w[:, None, :], window_strides=(1,), padding=[(1, 2)],
                                 dimension_numbers=('NWC', 'WIO', 'NWC'), feature_group_count=x.shape[-1])
    return y + b


def rglru_gates(xt, wa, ba, wx, bx, lam):
    L, B, C = xt.shape
    xb = xt.reshape(L, B, RG_BLOCKS, RG_BW)
    r = jax.nn.sigmoid(jnp.einsum('lbnc,ncd->lbnd', xb, wa.astype(jnp.float32)).reshape(L, B, C) + ba)
    i = jax.nn.sigmoid(jnp.einsum('lbnc,ncd->lbnd', xb, wx.astype(jnp.float32)).reshape(L, B, C) + bx)
    log_a = -RG_C * r * jax.nn.softplus(-lam.astype(jnp.float32))
    a = jnp.exp(log_a)
    u = jnp.sqrt(-jnp.expm1(2.0 * log_a)) * (i * xt)
    return a, u


def linear_scan(a, u, h0, reverse, emit):
    def step(h, au):
        h = au[0] * h + au[1]
        return h, (h if emit else None)
    return lax.scan(step, h0, (a, u), reverse=reverse)


def rglru_mixer(hc, hl, w_in, conv_w, conv_b, gate_a_w, gate_a_b, gate_x_w, gate_x_b, lam, w_out, need_ctx):
    B = hl.shape[0]
    pl = hl @ w_in
    pc = hc @ w_in
    xl_t = jnp.swapaxes(depthwise_conv(pl[..., D_RNN:], conv_w, conv_b), 0, 1).astype(jnp.float32)
    xc_t = jnp.swapaxes(depthwise_conv(pc[..., D_RNN:], conv_w, conv_b), 0, 1).astype(jnp.float32)
    h_lat = 0.0
    h_ctx = 0.0
    for d, rev in ((0, False), (1, True)):
        prm = (gate_a_w[d], gate_a_b[d], gate_x_w[d], gate_x_b[d], lam[d])
        ac, uc = rglru_gates(xc_t, *prm)
        hc_final, hc_seq = linear_scan(ac, uc, jnp.zeros((B, D_RNN), jnp.float32), rev, need_ctx)
        al, ul = rglru_gates(xl_t, *prm)
        _, hl_seq = linear_scan(al, ul, hc_final, rev, True)
        h_lat = h_lat + hl_seq
        if need_ctx:
            h_ctx = h_ctx + hc_seq
    yl = (jnp.swapaxes(h_lat, 0, 1).astype(hl.dtype) * jax.nn.gelu(pl[..., :D_RNN])) @ w_out
    yc = None
    if need_ctx:
        yc = (jnp.swapaxes(h_ctx, 0, 1).astype(hc.dtype) * jax.nn.gelu(pc[..., :D_RNN])) @ w_out
    return yc, yl


def gqa_window_mixer(hc, hl, w_qkv, b_qkv, sinks, w_o, b_o, need_ctx):
    B, S, _ = hl.shape
    n_ctx = hc.shape[1]
    q_dim, kv_dim = GQA_HEADS * GQA_HD, GQA_KV * GQA_HD

    def project(h):
        L = h.shape[1]
        p = h @ w_qkv + b_qkv
        q = p[..., :q_dim].reshape(B, L, GQA_HEADS, GQA_HD)
        k = p[..., q_dim:q_dim + kv_dim].reshape(B, L, GQA_KV, GQA_HD)
        v = p[..., q_dim + kv_dim:].reshape(B, L, GQA_KV, GQA_HD)
        return q, k, v

    qc, kc, vc = project(hc)
    ql, kl, vl = project(hl)
    cos, sin = axial_rope_tables(S, GQA_HD)
    ql = apply_rope(ql, cos, sin).reshape(B, S, GQA_KV, GQA_GROUP, GQA_HD)
    kl = apply_rope(kl, cos, sin)
    sink = sinks.astype(jnp.float32).reshape(GQA_KV, GQA_GROUP)
    scale = GQA_HD ** -0.5

    def attend(q, k, v, mask):
        s = jnp.einsum('bqkgd,bskd->bkgqs', q, k).astype(jnp.float32) * scale
        if mask is not None:
            s = jnp.where(mask, s, NEG_INF)
        s_sink = jnp.broadcast_to(sink[None, :, :, None, None], s.shape[:-1] + (1,))
        p = jax.nn.softmax(jnp.concatenate([s, s_sink], axis=-1), axis=-1)[..., :-1]
        return jnp.einsum('bkgqs,bskd->bqkgd', p.astype(v.dtype), v)

    span = Q_BLOCK + 2 * WINDOW
    k_pad = jnp.pad(kl, ((0, 0), (WINDOW, WINDOW), (0, 0), (0, 0)))
    v_pad = jnp.pad(vl, ((0, 0), (WINDOW, WINDOW), (0, 0), (0, 0)))
    rel = jnp.arange(span)[None, :] - WINDOW - jnp.arange(Q_BLOCK)[:, None]
    band = jnp.abs(rel) <= WINDOW
    ctx_mask = jnp.ones((Q_BLOCK, n_ctx), dtype=bool)

    def block(j):
        start = j * Q_BLOCK
        q_j = lax.dynamic_slice_in_dim(ql, start, Q_BLOCK, axis=1)
        k_j = jnp.concatenate([lax.dynamic_slice_in_dim(k_pad, start, span, axis=1), kc], axis=1)
        v_j = jnp.concatenate([lax.dynamic_slice_in_dim(v_pad, start, span, axis=1), vc], axis=1)
        key_pos = start - WINDOW + jnp.arange(span)
        lat_mask = band & ((key_pos >= 0) & (key_pos < S))[None, :]
        return attend(q_j, k_j, v_j, jnp.concatenate([lat_mask, ctx_mask], axis=-1))

    o = lax.map(block, jnp.arange(S // Q_BLOCK))
    o = jnp.moveaxis(o, 0, 1).reshape(B, S, q_dim)
    yl = o @ w_o + b_o
    yc = None
    if need_ctx:
        oc = attend(qc.reshape(B, n_ctx, GQA_KV, GQA_GROUP, GQA_HD), kc, vc, None)
        yc = oc.reshape(B, n_ctx, q_dim) @ w_o + b_o
    return yc, yl


def mla_mixer(hc, hl, w_down, q_norm, kv_norm, w_uq, w_ukv, w_o, need_ctx):
    B, S, _ = hl.shape
    n_ctx = hc.shape[1]
    cos, sin = axial_rope_tables(S, QK_ROPE)

    def project(h, rotate):
        L = h.shape[1]
        p = h @ w_down
        cq = rms_norm(p[..., :Q_LORA], q_norm)
        ckv = rms_norm(p[..., Q_LORA:Q_LORA + KV_LORA], kv_norm)
        k_rope = p[..., Q_LORA + KV_LORA:][:, :, None, :]
        q = (cq @ w_uq).reshape(B, L, MLA_HEADS, QK_NOPE + QK_ROPE)
        kv = (ckv @ w_ukv).reshape(B, L, MLA_HEADS, QK_NOPE + V_HD)
        q_nope, q_rope = q[..., :QK_NOPE], q[..., QK_NOPE:]
        if rotate:
            q_rope = apply_rope(q_rope, cos, sin)
            k_rope = apply_rope(k_rope, cos, sin)
        q = jnp.concatenate([q_nope, q_rope], axis=-1)
        k = jnp.concatenate([kv[..., :QK_NOPE], jnp.broadcast_to(k_rope, (B, L, MLA_HEADS, QK_ROPE))], axis=-1)
        return q, k, kv[..., QK_NOPE:]

    qc, kc, vc = project(hc, False)
    ql, kl, vl = project(hl, True)
    k_all = jnp.concatenate([kc, kl], axis=1)
    v_all = jnp.concatenate([vc, vl], axis=1)
    scale = (QK_NOPE + QK_ROPE) ** -0.5

    def attend(q, k, v):
        s = jnp.einsum('bqhd,bshd->bhqs', q, k).astype(jnp.float32) * scale
        p = jax.nn.softmax(s, axis=-1)
        return jnp.einsum('bhqs,bshd->bqhd', p.astype(v.dtype), v)

    def block(j):
        return attend(lax.dynamic_slice_in_dim(ql, j * Q_BLOCK, Q_BLOCK, axis=1), k_all, v_all)

    o = lax.map(block, jnp.arange(S // Q_BLOCK))
    yl = jnp.moveaxis(o, 0, 1).reshape(B, S, MLA_HEADS * V_HD) @ w_o
    yc = None
    if need_ctx:
        yc = attend(qc, kc, vc).reshape(B, n_ctx, MLA_HEADS * V_HD) @ w_o
    return yc, yl


def moe_ffn(h, router_w, router_b, w_gu, b_gu, w_down, b_down):
    n_tok, d = h.shape
    logits = (h @ router_w + router_b).astype(jnp.float32)
    top_logit, top_idx = lax.top_k(logits, TOP_K)
    gate = jax.nn.softmax(top_logit, axis=-1)
    n_asg = n_tok * TOP_K
    e_flat = top_idx.reshape(-1)
    order = jnp.argsort(e_flat)
    e_sorted = e_flat[order]
    tok_sorted = (order // TOP_K).astype(jnp.int32)
    gate_sorted = gate.reshape(-1)[order]
    count = jnp.bincount(e_flat, length=N_EXPERTS)
    start = jnp.cumsum(count) - count
    padded = (count + MOE_BLOCK - 1) // MOE_BLOCK * MOE_BLOCK
    pad_end = jnp.cumsum(padded)
    dest = pad_end[e_sorted] - padded[e_sorted] + jnp.arange(n_asg) - start[e_sorted]
    n_blocks = (n_asg + N_EXPERTS * (MOE_BLOCK - 1) + MOE_BLOCK - 1) // MOE_BLOCK
    n_slots = n_blocks * MOE_BLOCK
    slot_tok = jnp.full((n_slots,), n_tok, jnp.int32).at[dest].set(tok_sorted)
    slot_gate = jnp.zeros((n_slots,), jnp.float32).at[dest].set(gate_sorted)
    block_expert = jnp.minimum(jnp.searchsorted(pad_end, jnp.arange(n_blocks) * MOE_BLOCK, side='right'),
                               N_EXPERTS - 1)
    h_pad = jnp.concatenate([h, jnp.zeros((1, d), h.dtype)], axis=0)

    def expert_block(args):
        idx, e = args
        hu = h_pad[idx] @ w_gu[e] + b_gu[e]
        g = jnp.minimum(hu[:, 0::2], SWIGLU_LIMIT)
        u = jnp.clip(hu[:, 1::2], -SWIGLU_LIMIT, SWIGLU_LIMIT)
        return ((u + 1) * (g * jax.nn.sigmoid(SWIGLU_ALPHA * g))) @ w_down[e] + b_down[e]

    y = lax.map(expert_block, (slot_tok.reshape(n_blocks, MOE_BLOCK), block_expert))
    y = y.reshape(n_slots, d) * slot_gate[:, None]
    return jax.ops.segment_sum(y, slot_tok, num_segments=n_tok + 1)[:n_tok].astype(h.dtype)


def setup_inputs(seed: int = 0) -> dict:
    key = jax.random.key(seed)
    ks = iter(jax.random.split(key, 40))
    f32 = jnp.float32
    D = D_MODEL

    def nrm(shape, scale):
        return jax.random.normal(next(ks), shape, f32) * scale

    a0 = jax.random.uniform(next(ks), (N_RG, 2, D_RNN), f32, 0.9, 0.999)
    rg_lambda = jnp.log(a0) - jnp.log1p(-a0)
    return {
        'x': nrm((BATCH, SEQ, D), 1.0),
        'c': nrm((BATCH, D), 1.0),
        'ctx': nrm((BATCH, CTX_LEN, D), 1.0),
        'c_ctx': nrm((D,), 1.0),
        'ada_w': nrm((DEPTH, D, 6 * D), 0.5 * D ** -0.5),
        'ada_b': nrm((DEPTH, 6 * D), 0.02),
        'ln1_g': 1.0 + nrm((DEPTH, D), 0.02),
        'ln1_b': nrm((DEPTH, D), 0.02),
        'ln2_g': 1.0 + nrm((DEPTH, D), 0.02),
        'ln2_b': nrm((DEPTH, D), 0.02),
        'router_w': nrm((DEPTH, D, N_EXPERTS), D ** -0.5),
        'router_b': nrm((DEPTH, N_EXPERTS), 0.01),
        'exp_gu_w': nrm((DEPTH, N_EXPERTS, D, 2 * D_FF), D ** -0.5),
        'exp_gu_b': nrm((DEPTH, N_EXPERTS, 2 * D_FF), 0.02),
        'exp_down_w': nrm((DEPTH, N_EXPERTS, D_FF, D), D_FF ** -0.5 * DN_BETA),
        'exp_down_b': nrm((DEPTH, N_EXPERTS, D), 0.02),
        'rg_w_in': nrm((N_RG, D, 2 * D_RNN), D ** -0.5),
        'rg_conv_w': nrm((N_RG, CONV_W, D_RNN), CONV_W ** -0.5),
        'rg_conv_b': nrm((N_RG, D_RNN), 0.02),
        'rg_gate_a_w': nrm((N_RG, 2, RG_BLOCKS, RG_BW, RG_BW), RG_BW ** -0.5),
        'rg_gate_a_b': nrm((N_RG, 2, D_RNN), 0.02),
        'rg_gate_x_w': nrm((N_RG, 2, RG_BLOCKS, RG_BW, RG_BW), RG_BW ** -0.5),
        'rg_gate_x_b': nrm((N_RG, 2, D_RNN), 0.02),
        'rg_lambda': rg_lambda,
        'rg_w_out': nrm((N_RG, D_RNN, D), D_RNN ** -0.5 * DN_BETA),
        'gqa_w_qkv': nrm((N_GQA, D, (GQA_HEADS + 2 * GQA_KV) * GQA_HD), D ** -0.5),
        'gqa_b_qkv': nrm((N_GQA, (GQA_HEADS + 2 * GQA_KV) * GQA_HD), 0.02),
        'gqa_sinks': nrm((N_GQA, GQA_HEADS), 1.0),
        'gqa_w_o': nrm((N_GQA, GQA_HEADS * GQA_HD, D), (GQA_HEADS * GQA_HD) ** -0.5 * DN_BETA),
        'gqa_b_o': nrm((N_GQA, D), 0.02),
        'mla_w_down': nrm((N_MLA, D, Q_LORA + KV_LORA + QK_ROPE), D ** -0.5),
        'mla_q_norm': 1.0 + nrm((N_MLA, Q_LORA), 0.02),
        'mla_kv_norm': 1.0 + nrm((N_MLA, KV_LORA), 0.02),
        'mla_w_uq': nrm((N_MLA, Q_LORA, MLA_HEADS * (QK_NOPE + QK_ROPE)), Q_LORA ** -0.5),
        'mla_w_ukv': nrm((N_MLA, KV_LORA, MLA_HEADS * (QK_NOPE + V_HD)), KV_LORA ** -0.5),
        'mla_w_o': nrm((N_MLA, MLA_HEADS * V_HD, D), (MLA_HEADS * V_HD) ** -0.5 * DN_BETA),
    }


def reference(x, c, ctx, c_ctx, ada_w, ada_b, ln1_g, ln1_b, ln2_g, ln2_b, router_w, router_b,
              exp_gu_w, exp_gu_b, exp_down_w, exp_down_b, rg_w_in, rg_conv_w, rg_conv_b,
              rg_gate_a_w, rg_gate_a_b, rg_gate_x_w, rg_gate_x_b, rg_lambda, rg_w_out,
              gqa_w_qkv, gqa_b_qkv, gqa_sinks, gqa_w_o, gqa_b_o,
              mla_w_down, mla_q_norm, mla_kv_norm, mla_w_uq, mla_w_ukv, mla_w_o):
    B, S, D = x.shape
    n_ctx = ctx.shape[1]
    xl, xc = x, ctx
    silu_c = jax.nn.silu(c)
    silu_cc = jax.nn.silu(c_ctx)
    for i in range(DEPTH):
        need_ctx = i < DEPTH - 1
        sh1, sc1, g1, sh2, sc2, g2 = jnp.split((silu_c @ ada_w[i] + ada_b[i])[:, None, :], 6, axis=-1)
        csh1, csc1, cg1, csh2, csc2, cg2 = jnp.split(silu_cc @ ada_w[i] + ada_b[i], 6, axis=-1)
        hl = modulate(xl, sh1, sc1)
        hc = modulate(xc, csh1, csc1)
        kind, j = i % N_MIXERS, i // N_MIXERS
        if kind == 0:
            yc, yl = rglru_mixer(hc, hl, rg_w_in[j], rg_conv_w[j], rg_conv_b[j], rg_gate_a_w[j], rg_gate_a_b[j],
                                 rg_gate_x_w[j], rg_gate_x_b[j], rg_lambda[j], rg_w_out[j], need_ctx)
        elif kind == 1:
            yc, yl = gqa_window_mixer(hc, hl, gqa_w_qkv[j], gqa_b_qkv[j], gqa_sinks[j], gqa_w_o[j], gqa_b_o[j],
                                      need_ctx)
        else:
            yc, yl = mla_mixer(hc, hl, mla_w_down[j], mla_q_norm[j], mla_kv_norm[j], mla_w_uq[j], mla_w_ukv[j],
                               mla_w_o[j], need_ctx)
        xl = layer_norm(DN_ALPHA * xl + g1 * yl, ln1_g[i], ln1_b[i])
        hl2 = modulate(xl, sh2, sc2).reshape(B * S, D)
        if need_ctx:
            xc = layer_norm(DN_ALPHA * xc + cg1 * yc, ln1_g[i], ln1_b[i])
            hc2 = modulate(xc, csh2, csc2).reshape(B * n_ctx, D)
            tokens = jnp.concatenate([hc2, hl2], axis=0)
        else:
            tokens = hl2
        y = moe_ffn(tokens, router_w[i], router_b[i], exp_gu_w[i], exp_gu_b[i], exp_down_w[i], exp_down_b[i])
        xl = layer_norm(DN_ALPHA * xl + g2 * y[y.shape[0] - B * S:].reshape(B, S, D), ln2_g[i], ln2_b[i])
        if need_ctx:
            xc = layer_norm(DN_ALPHA * xc + cg2 * y[:B * n_ctx].reshape(B, n_ctx, D), ln2_g[i], ln2_b[i])
    return xl
```

```python
import functools
import math

import jax
import jax.numpy as jnp
from jax import lax
from jax.experimental import pallas as pl
from jax.experimental.pallas import tpu as pltpu

F32 = jnp.float32
BF16 = jnp.bfloat16

GRID_W = 64
LN_EPS = 1e-5
RMS_EPS = 1e-6
ROPE_THETA = 10000.0
NEG_INF = -1e30
RG_C = 8.0
GQA_KV = 2
GQA_HD = 64
WINDOW = 128
MLA_HEADS = 16
QK_NOPE = 64
QK_ROPE = 32
V_HD = 64
TOP_K = 4
SWIGLU_LIMIT = 7.0
SWIGLU_ALPHA = 1.702
MOE_BLOCK = 256

LANES = 128
SUBLANES = 8
ROW_TILE = 256
ATTN_Q = 128
MLA_TQ = 256
MLA_TK = 256
VMEM_LIMIT = 56 * 1024 * 1024


def _params(*sem):
    return pltpu.CompilerParams(dimension_semantics=sem, vmem_limit_bytes=VMEM_LIMIT)


def _bdot(a, b):
    return jnp.dot(a.astype(BF16), b.astype(BF16), preferred_element_type=F32)


def _bdot_nt(a, b):
    return lax.dot_general(a.astype(BF16), b.astype(BF16), (((1,), (1,)), ((), ())),
                           preferred_element_type=F32)


def _layer_norm(r, g, b):
    mu = jnp.mean(r, axis=-1, keepdims=True)
    d = r - mu
    var = jnp.mean(d * d, axis=-1, keepdims=True)
    return d * lax.rsqrt(var + LN_EPS) * g + b


def _gelu_tanh(x):
    return 0.5 * x * (1.0 + jnp.tanh(math.sqrt(2.0 / math.pi) * (x + 0.044715 * (x * x * x))))


class _Rows:
    def __init__(self, batch, n_ctx, seq, tile):
        assert n_ctx % tile == 0 and seq % tile == 0
        self.batch, self.n_ctx, self.seq, self.tile = batch, n_ctx, seq, tile
        self.ctx_tiles = batch * n_ctx // tile
        self.lat_tiles = batch * seq // tile
        self.tiles = self.ctx_tiles + self.lat_tiles
        self.lat_per_batch = seq // tile
        self.ctx_per_batch = n_ctx // tile
        self.rows = batch * (n_ctx + seq)

    def mod_index(self, i):
        return jnp.where(i < self.ctx_tiles, 0, 1 + (i - self.ctx_tiles) // self.lat_per_batch)

    def rope_index(self, i):
        return jnp.where(i < self.ctx_tiles, self.lat_per_batch, (i - self.ctx_tiles) % self.lat_per_batch)


def _ada_kernel(c_ref, w_ref, b_ref, o_ref):
    cv = c_ref[...]
    s = cv * jax.nn.sigmoid(cv)
    o_ref[0] = jnp.dot(s, w_ref[0], preferred_element_type=F32,
                       precision=lax.Precision.HIGHEST) + b_ref[0]


def _ada_table(cvec, ada_w, ada_b):
    depth, d, d6 = ada_w.shape
    n = cvec.shape[0]
    chunk = d
    out = pl.pallas_call(
        _ada_kernel,
        grid=(depth, d6 // chunk),
        in_specs=[pl.BlockSpec((n, d), lambda l, j: (0, 0)),
                  pl.BlockSpec((1, d, chunk), lambda l, j: (l, 0, j)),
                  pl.BlockSpec((1, 1, chunk), lambda l, j: (l, 0, j))],
        out_specs=pl.BlockSpec((1, n, chunk), lambda l, j: (l, 0, j)),
        out_shape=jax.ShapeDtypeStruct((depth, n, d6), F32),
        compiler_params=_params("arbitrary", "arbitrary"),
        name="ada_table",
    )(cvec, ada_w, ada_b.reshape(depth, 1, d6))
    return out.reshape(depth, n, 6, d)


def _out_ln_kernel(z_ref, w_ref, b_ref, x_ref, mod_ref, g_ref, bt_ref, o_ref, *, gate_row, alpha):
    y = _bdot(z_ref[...], w_ref[...]) + b_ref[...]
    m = mod_ref[0]
    r = alpha * x_ref[...] + m[gate_row:gate_row + 1] * y
    o_ref[...] = _layer_norm(r, g_ref[...], bt_ref[...])


def _out_ln(rows, z, w, bias, x, mod, ln_g, ln_b, *, alpha, first_tile=0):
    t, d = rows.tile, x.shape[1]
    kdim = z.shape[1]
    n_tiles = rows.tiles - first_tile
    row = lambda i: (i + first_tile, 0)
    return pl.pallas_call(
        functools.partial(_out_ln_kernel, gate_row=2, alpha=alpha),
        grid=(n_tiles,),
        in_specs=[pl.BlockSpec((t, kdim), row),
                  pl.BlockSpec((kdim, d), lambda i: (0, 0)),
                  pl.BlockSpec((1, d), lambda i: (0, 0)),
                  pl.BlockSpec((t, d), row),
                  pl.BlockSpec((1, 6, d), lambda i: (rows.mod_index(i + first_tile), 0, 0)),
                  pl.BlockSpec((1, d), lambda i: (0, 0)),
                  pl.BlockSpec((1, d), lambda i: (0, 0))],
        out_specs=pl.BlockSpec((t, d), row),
        out_shape=jax.ShapeDtypeStruct(x.shape, F32),
        input_output_aliases={3: 0},
        compiler_params=_params("arbitrary"),
        name="out_proj_ln",
    )(z, w.astype(BF16), bias.reshape(1, d), x, mod, ln_g.reshape(1, d), ln_b.reshape(1, d))


def _rg_in_kernel(x_ref, mod_ref, w_ref, gel_ref, rec_ref, *, d_rnn):
    m = mod_ref[0]
    h = (x_ref[...] * (1.0 + m[1:2]) + m[0:1]).astype(BF16)
    gel_ref[...] = _gelu_tanh(_bdot(h, w_ref[:, :d_rnn])).astype(BF16)
    rec_ref[...] = _bdot(h, w_ref[:, d_rnn:])


def _rg_in(rows, x, mod, w_in):
    t, d = rows.tile, x.shape[1]
    d_rnn = w_in.shape[1] // 2
    return pl.pallas_call(
        functools.partial(_rg_in_kernel, d_rnn=d_rnn),
        grid=(rows.tiles,),
        in_specs=[pl.BlockSpec((t, d), lambda i: (i, 0)),
                  pl.BlockSpec((1, 6, d), lambda i: (rows.mod_index(i), 0, 0)),
                  pl.BlockSpec((d, 2 * d_rnn), lambda i: (0, 0))],
        out_specs=[pl.BlockSpec((t, d_rnn), lambda i: (i, 0)),
                   pl.BlockSpec((t, d_rnn), lambda i: (i, 0))],
        out_shape=[jax.ShapeDtypeStruct((rows.rows, d_rnn), BF16),
                   jax.ShapeDtypeStruct((rows.rows, d_rnn), F32)],
        compiler_params=_params("arbitrary"),
        name="rg_in_proj",
    )(x, mod, w_in.astype(BF16))


def _rg_scan_kernel(*refs, reverse, fuse_out, n_blocks, block_w, tile, ctx_tiles, lat_tiles):
    if fuse_out:
        (x_ref, xp_ref, xn_ref, cw_ref, cb_ref, wa_ref, ba_ref, wx_ref, bx_ref, lam_ref,
         hf_ref, gel_ref, out_ref, a_scr, u_scr, h_scr, carry_scr) = refs
    else:
        (x_ref, xp_ref, xn_ref, cw_ref, cb_ref, wa_ref, ba_ref, wx_ref, bx_ref, lam_ref,
         out_ref, a_scr, u_scr, carry_scr) = refs
        h_scr = out_ref
    j = pl.program_id(1)
    is_ctx = j < ctx_tiles
    n_seq = jnp.where(is_ctx, ctx_tiles, lat_tiles)
    step = jnp.where(is_ctx, j, j - ctx_tiles)
    pos = (n_seq - 1 - step) if reverse else step
    prev_ok = (pos > 0).astype(F32)
    next_ok = (pos < n_seq - 1).astype(F32)

    @pl.when(j == 0)
    def _():
        carry_scr[...] = jnp.zeros_like(carry_scr)

    row = lax.broadcasted_iota(jnp.int32, (tile, block_w), 0)
    for n in range(n_blocks):
        cols = slice(n * block_w, (n + 1) * block_w)
        x = x_ref[:, cols]
        prev = xp_ref[SUBLANES - 1:SUBLANES, cols] * prev_ok
        nxt0 = xn_ref[0:1, cols] * next_ok
        nxt1 = xn_ref[1:2, cols] * next_ok
        x_m1 = jnp.where(row == 0, prev, pltpu.roll(x, 1, 0))
        x_p1 = jnp.where(row == tile - 1, nxt0, pltpu.roll(x, tile - 1, 0))
        x_p2 = jnp.where(row == tile - 2, nxt0, jnp.where(row == tile - 1, nxt1, pltpu.roll(x, tile - 2, 0)))
        xc = (cw_ref[0:1, cols] * x_m1 + cw_ref[1:2, cols] * x + cw_ref[2:3, cols] * x_p1
              + cw_ref[3:4, cols] * x_p2 + cb_ref[:, cols])
        xb = xc.astype(BF16)
        r = jax.nn.sigmoid(_bdot(xb, wa_ref[n]) + ba_ref[:, cols])
        gi = jax.nn.sigmoid(_bdot(xb, wx_ref[n]) + bx_ref[:, cols])
        z = -lam_ref[:, cols]
        softplus = jnp.maximum(z, 0.0) + jnp.log1p(jnp.exp(-jnp.abs(z)))
        log_a = (-RG_C) * r * softplus
        a = jnp.exp(log_a)
        a_scr[:, cols] = a
        u_scr[:, cols] = jnp.sqrt(1.0 - a * a) * (gi * xc)

    width = n_blocks * block_w
    sub = lax.broadcasted_iota(jnp.int32, (SUBLANES, width), 0)
    groups = tile // SUBLANES

    def body(g, carry):
        gg = (groups - 1 - g) if reverse else g
        r0 = pl.multiple_of(gg * SUBLANES, SUBLANES)
        a8 = a_scr[pl.ds(r0, SUBLANES), :]
        u8 = u_scr[pl.ds(r0, SUBLANES), :]
        for s in (1, 2, 4):
            if reverse:
                a_sh, u_sh, ok = pltpu.roll(a8, SUBLANES - s, 0), pltpu.roll(u8, SUBLANES - s, 0), sub < SUBLANES - s
            else:
                a_sh, u_sh, ok = pltpu.roll(a8, s, 0), pltpu.roll(u8, s, 0), sub >= s
            u8 = jnp.where(ok, a8 * u_sh + u8, u8)
            a8 = jnp.where(ok, a8 * a_sh, a8)
        h8 = a8 * carry + u8
        h_scr[pl.ds(r0, SUBLANES), :] = h8
        return h8[0:1, :] if reverse else h8[SUBLANES - 1:SUBLANES, :]

    carry_scr[...] = lax.fori_loop(0, groups, body, carry_scr[...])
    if fuse_out:
        out_ref[...] = ((hf_ref[...] + h_scr[...]) * gel_ref[...].astype(F32)).astype(BF16)


def _rg_scan(rows, rec, conv_w, conv_b, wa, ba, wx, bx, lam, *, reverse, h_fwd=None, gel=None):
    t = rows.tile
    c = rec.shape[1]
    n_blocks, block_w = wa.shape[0], wa.shape[1]
    nc, nl = rows.ctx_per_batch, rows.lat_per_batch
    halo = t // SUBLANES
    last_halo = rows.rows // SUBLANES - 1
    fuse_out = h_fwd is not None

    def tile_index(b, j):
        is_ctx = j < nc
        step = jnp.where(is_ctx, j, j - nc)
        n_seq = jnp.where(is_ctx, nc, nl)
        pos = (n_seq - 1 - step) if reverse else step
        return jnp.where(is_ctx, b * nc + pos, rows.ctx_tiles + b * nl + pos)

    cur = lambda b, j: (tile_index(b, j), 0)
    prv = lambda b, j: (jnp.maximum(tile_index(b, j) * halo - 1, 0), 0)
    nxt = lambda b, j: (jnp.minimum((tile_index(b, j) + 1) * halo, last_halo), 0)
    full2 = lambda b, j: (0, 0)
    full3 = lambda b, j: (0, 0, 0)
    in_specs = [pl.BlockSpec((t, c), cur), pl.BlockSpec((SUBLANES, c), prv), pl.BlockSpec((SUBLANES, c), nxt),
                pl.BlockSpec((4, c), full2), pl.BlockSpec((1, c), full2),
                pl.BlockSpec((n_blocks, block_w, block_w), full3), pl.BlockSpec((1, c), full2),
                pl.BlockSpec((n_blocks, block_w, block_w), full3), pl.BlockSpec((1, c), full2),
                pl.BlockSpec((1, c), full2)]
    args = [rec, rec, rec, conv_w, conv_b.reshape(1, c), wa.astype(BF16), ba.reshape(1, c),
            wx.astype(BF16), bx.reshape(1, c), lam.reshape(1, c)]
    scratch = [pltpu.VMEM((t, c), F32), pltpu.VMEM((t, c), F32)]
    if fuse_out:
        in_specs += [pl.BlockSpec((t, c), cur), pl.BlockSpec((t, c), cur)]
        args += [h_fwd, gel]
        scratch.append(pltpu.VMEM((t, c), F32))
    scratch.append(pltpu.VMEM((1, c), F32))
    return pl.pallas_call(
        functools.partial(_rg_scan_kernel, reverse=reverse, fuse_out=fuse_out, n_blocks=n_blocks,
                          block_w=block_w, tile=t, ctx_tiles=nc, lat_tiles=nl),
        grid=(rows.batch, nc + nl),
        in_specs=in_specs,
        out_specs=pl.BlockSpec((t, c), cur),
        out_shape=jax.ShapeDtypeStruct((rows.rows, c), BF16 if fuse_out else F32),
        scratch_shapes=scratch,
        compiler_params=_params("arbitrary", "arbitrary"),
        name="rg_scan_bwd" if reverse else "rg_scan_fwd",
    )(*args)


def _rglru_layer(rows, x, mod, p, ln_g, ln_b, alpha, need_ctx):
    gel, rec = _rg_in(rows, x, mod, p["w_in"])
    h_fwd = _rg_scan(rows, rec, p["conv_w"], p["conv_b"], p["gate_a_w"][0], p["gate_a_b"][0],
                     p["gate_x_w"][0], p["gate_x_b"][0], p["lam"][0], reverse=False)
    z = _rg_scan(rows, rec, p["conv_w"], p["conv_b"], p["gate_a_w"][1], p["gate_a_b"][1],
                 p["gate_x_w"][1], p["gate_x_b"][1], p["lam"][1], reverse=True, h_fwd=h_fwd, gel=gel)
    d = x.shape[1]
    return _out_ln(rows, z, p["w_out"], jnp.zeros((d,), F32), x, mod, ln_g, ln_b, alpha=alpha,
                   first_tile=0 if need_ctx else rows.ctx_tiles)


def _axial_angles(seq, rot_dim):
    pos = jnp.arange(seq, dtype=jnp.int32)
    row = (pos // GRID_W).astype(F32)
    col = (pos % GRID_W).astype(F32)
    n_freq = rot_dim // 4
    inv_freq = ROPE_THETA ** (-jnp.arange(n_freq, dtype=F32) / n_freq)
    return jnp.concatenate([row[:, None] * inv_freq, col[:, None] * inv_freq], axis=-1)


def _rope_tables(rows, rot_dim, lead, trail):
    ang = _axial_angles(rows.seq, rot_dim)
    cos, sin = jnp.cos(ang), jnp.sin(ang)
    ones = lambda w: jnp.ones((rows.seq, w), F32)
    zeros = lambda w: jnp.zeros((rows.seq, w), F32)
    c = jnp.concatenate([ones(lead), cos, cos, ones(trail)], axis=-1)
    s = jnp.concatenate([zeros(lead), -sin, sin, zeros(trail)], axis=-1)
    reps = LANES // c.shape[1]
    c, s = jnp.tile(c, (1, reps)), jnp.tile(s, (1, reps))
    t = rows.tile
    c = jnp.concatenate([c.reshape(rows.lat_per_batch, t, LANES), jnp.ones((1, t, LANES), F32)], axis=0)
    s = jnp.concatenate([s.reshape(rows.lat_per_batch, t, LANES), jnp.zeros((1, t, LANES), F32)], axis=0)
    return c, s


def _rope_chunk(x, cos, sin, half):
    lane = lax.broadcasted_iota(jnp.int32, x.shape, 1)
    partner = jnp.where((lane % (2 * half)) < half, pltpu.roll(x, LANES - half, 1), pltpu.roll(x, half, 1))
    return x * cos + partner * sin


def _gqa_proj_kernel(x_ref, mod_ref, w_ref, b_ref, cos_ref, sin_ref, q_ref, kv_ref, *, q_dim, scale, half):
    m = mod_ref[0]
    h = (x_ref[...] * (1.0 + m[1:2]) + m[0:1]).astype(BF16)
    cos, sin = cos_ref[0], sin_ref[0]
    n_q = q_dim // LANES
    n_all = w_ref.shape[1] // LANES
    for c in range(n_all):
        cols = slice(c * LANES, (c + 1) * LANES)
        p = _bdot(h, w_ref[:, cols]) + b_ref[:, cols]
        is_v = c in (n_q + 1, n_q + 3)
        if not is_v:
            p = _rope_chunk(p, cos, sin, half)
        if c < n_q:
            q_ref[:, cols] = (p * scale).astype(BF16)
        else:
            kv_ref[:, (c - n_q) * LANES:(c - n_q + 1) * LANES] = p.astype(BF16)


def _gqa_attn_kernel(*refs, windowed, window, seq, n_pairs, pairs_per_kv):
    if windowed:
        q_ref, kv_ref, kvc_ref, sink_ref, o_ref = refs
    else:
        q_ref, kvc_ref, sink_ref, _, o_ref = refs
    tq = q_ref.shape[0]
    kvc = kvc_ref[...]
    if windowed:
        span = tq + 2 * window
        qs = pl.program_id(1) * tq
        ws = pl.multiple_of(jnp.clip(qs - window, 0, seq - span), LANES)
        kv = jnp.concatenate([kv_ref[pl.ds(ws, span), :], kvc], axis=0)
        n_keys = kv.shape[0]
        qpos = qs + lax.broadcasted_iota(jnp.int32, (tq, n_keys), 0)
        col = lax.broadcasted_iota(jnp.int32, (tq, n_keys), 1)
        mask = (jnp.abs(ws + col - qpos) <= window) | (col >= span)
    else:
        kv = kvc
        mask = None
    lane = lax.broadcasted_iota(jnp.int32, (kv.shape[0], LANES), 1)
    low = lane < GQA_HD
    zero = jnp.zeros((kv.shape[0], LANES), BF16)
    k_plain, v_plain = kv[:, 0:LANES], kv[:, LANES:2 * LANES]
    k_swap, v_swap = kv[:, 2 * LANES:3 * LANES], kv[:, 3 * LANES:4 * LANES]
    for pr in range(n_pairs):
        g = pr // pairs_per_kv
        k_lo = jnp.where(low, k_plain if g == 0 else k_swap, zero)
        k_hi = jnp.where(low, zero, k_swap if g == 0 else k_plain)
        v_lo = jnp.where(low, v_plain if g == 0 else v_swap, zero)
        v_hi = jnp.where(low, zero, v_swap if g == 0 else v_plain)
        q2 = q_ref[:, pr * LANES:(pr + 1) * LANES]
        acc = jnp.zeros((tq, LANES), F32)
        for hh, (kx, vx) in enumerate(((k_lo, v_lo), (k_hi, v_hi))):
            head = 2 * pr + hh
            s = _bdot_nt(q2, kx)
            if mask is not None:
                s = jnp.where(mask, s, NEG_INF)
            sk = sink_ref[head:head + 1, 0:1]
            mx = jnp.maximum(jnp.max(s, axis=1, keepdims=True), sk)
            p = jnp.exp(s - mx)
            denom = jnp.sum(p, axis=1, keepdims=True) + jnp.exp(sk - mx)
            acc = acc + _bdot(p * (1.0 / denom), vx)
        o_ref[:, pr * LANES:(pr + 1) * LANES] = acc.astype(BF16)


def _gqa_layer(rows, x, mod, p, ln_g, ln_b, alpha, need_ctx):
    t, d = rows.tile, x.shape[1]
    w_qkv, b_qkv = p["w_qkv"], p["b_qkv"]
    kv_dim = GQA_KV * GQA_HD
    q_dim = w_qkv.shape[1] - 2 * kv_dim
    n_heads = q_dim // GQA_HD
    assert kv_dim == LANES and GQA_KV == 2 and q_dim % LANES == 0
    swap = lambda a: jnp.concatenate([a[..., GQA_HD:], a[..., :GQA_HD]], axis=-1)
    wk, wv = w_qkv[:, q_dim:q_dim + kv_dim], w_qkv[:, q_dim + kv_dim:]
    bk, bv = b_qkv[q_dim:q_dim + kv_dim], b_qkv[q_dim + kv_dim:]
    w_ext = jnp.concatenate([w_qkv, swap(wk), swap(wv)], axis=1).astype(BF16)
    b_ext = jnp.concatenate([b_qkv, swap(bk), swap(bv)]).reshape(1, -1)
    n_ext = w_ext.shape[1]
    cos, sin = _rope_tables(rows, GQA_HD, 0, 0)
    q, kv = pl.pallas_call(
        functools.partial(_gqa_proj_kernel, q_dim=q_dim, scale=GQA_HD ** -0.5, half=GQA_HD // 2),
        grid=(rows.tiles,),
        in_specs=[pl.BlockSpec((t, d), lambda i: (i, 0)),
                  pl.BlockSpec((1, 6, d), lambda i: (rows.mod_index(i), 0, 0)),
                  pl.BlockSpec((d, n_ext), lambda i: (0, 0)),
                  pl.BlockSpec((1, n_ext), lambda i: (0, 0)),
                  pl.BlockSpec((1, t, LANES), lambda i: (rows.rope_index(i), 0, 0)),
                  pl.BlockSpec((1, t, LANES), lambda i: (rows.rope_index(i), 0, 0))],
        out_specs=[pl.BlockSpec((t, q_dim), lambda i: (i, 0)),
                   pl.BlockSpec((t, 4 * LANES), lambda i: (i, 0))],
        out_shape=[jax.ShapeDtypeStruct((rows.rows, q_dim), BF16),
                   jax.ShapeDtypeStruct((rows.rows, 4 * LANES), BF16)],
        compiler_params=_params("arbitrary"),
        name="gqa_qkv_proj",
    )(x, mod, w_ext, b_ext, cos, sin)

    sinks = jnp.broadcast_to(p["sinks"].astype(F32)[:, None], (n_heads, LANES))
    b_, s_, n_ctx = rows.batch, rows.seq, rows.n_ctx
    ctx_rows = b_ * n_ctx
    assert ctx_rows % s_ == 0 and s_ >= ATTN_Q + 2 * WINDOW
    q_blocks = s_ // ATTN_Q
    n_pairs = q_dim // LANES
    common = dict(window=WINDOW, seq=s_, n_pairs=n_pairs, pairs_per_kv=n_pairs // GQA_KV)
    o_shape = jax.ShapeDtypeStruct((rows.rows, q_dim), BF16)
    o_lat = pl.pallas_call(
        functools.partial(_gqa_attn_kernel, windowed=True, **common),
        grid=(b_, q_blocks),
        in_specs=[pl.BlockSpec((ATTN_Q, q_dim), lambda b, j: (ctx_rows // ATTN_Q + b * q_blocks + j, 0)),
                  pl.BlockSpec((s_, 4 * LANES), lambda b, j: (ctx_rows // s_ + b, 0)),
                  pl.BlockSpec((n_ctx, 4 * LANES), lambda b, j: (b, 0)),
                  pl.BlockSpec((n_heads, LANES), lambda b, j: (0, 0))],
        out_specs=pl.BlockSpec((ATTN_Q, q_dim), lambda b, j: (ctx_rows // ATTN_Q + b * q_blocks + j, 0)),
        out_shape=o_shape,
        compiler_params=_params("arbitrary", "arbitrary"),
        name="gqa_window_attn",
    )(q, kv, kv, sinks)
    if need_ctx:
        cq_blocks = n_ctx // ATTN_Q
        o = pl.pallas_call(
            functools.partial(_gqa_attn_kernel, windowed=False, **common),
            grid=(b_, cq_blocks),
            in_specs=[pl.BlockSpec((ATTN_Q, q_dim), lambda b, j: (b * cq_blocks + j, 0)),
                      pl.BlockSpec((n_ctx, 4 * LANES), lambda b, j: (b, 0)),
                      pl.BlockSpec((n_heads, LANES), lambda b, j: (0, 0)),
                      pl.BlockSpec(memory_space=pl.ANY)],
            out_specs=pl.BlockSpec((ATTN_Q, q_dim), lambda b, j: (b * cq_blocks + j, 0)),
            out_shape=o_shape,
            input_output_aliases={3: 0},
            compiler_params=_params("arbitrary", "arbitrary"),
            name="gqa_ctx_attn",
        )(q, kv, sinks, o_lat)
    else:
        o = o_lat
    return _out_ln(rows, o, p["w_o"], p["b_o"], x, mod, ln_g, ln_b, alpha=alpha,
                   first_tile=0 if need_ctx else rows.ctx_tiles)


def _mla_proj_kernel(x_ref, mod_ref, wd_ref, qn_ref, kvn_ref, wq_ref, wk_ref, wv_ref, cos_ref, sin_ref,
                     q_ref, k_ref, v_ref, *, q_lora, kv_lora, scale, n_heads):
    m = mod_ref[0]
    h = (x_ref[...] * (1.0 + m[1:2]) + m[0:1]).astype(BF16)
    p = _bdot(h, wd_ref[...])
    cq, ckv = p[:, :q_lora], p[:, q_lora:q_lora + kv_lora]
    cos, sin = cos_ref[0], sin_ref[0]
    k_rope = _rope_chunk(p[:, q_lora + kv_lora:], cos, sin, QK_ROPE // 2)
    cq = (cq * lax.rsqrt(jnp.mean(cq * cq, axis=-1, keepdims=True) + RMS_EPS) * qn_ref[...]).astype(BF16)
    ckv = (ckv * lax.rsqrt(jnp.mean(ckv * ckv, axis=-1, keepdims=True) + RMS_EPS) * kvn_ref[...]).astype(BF16)
    for hd in range(n_heads):
        cols = slice(hd * LANES, (hd + 1) * LANES)
        qh = _rope_chunk(_bdot(cq, wq_ref[:, cols]), cos, sin, QK_ROPE // 2)
        q_ref[:, cols] = (qh * scale).astype(BF16)
        k_ref[:, cols] = (_bdot(ckv, wk_ref[:, cols]) + k_rope).astype(BF16)
    v_ref[...] = _bdot(ckv, wv_ref[...]).astype(BF16)


def _mla_attn_kernel(*refs, n_heads):
    q_ref, k_ref, v_ref = refs[:3]
    o_ref, m_scr, l_scr, acc_scr = refs[-4:]
    kt = pl.program_id(2)

    @pl.when(kt == 0)
    def _():
        m_scr[...] = jnp.full_like(m_scr, -jnp.inf)
        l_scr[...] = jnp.zeros_like(l_scr)
        acc_scr[...] = jnp.zeros_like(acc_scr)

    tq, tk = q_ref.shape[0], k_ref.shape[0]
    lane_k = lax.broadcasted_iota(jnp.int32, (tk, LANES), 1) < V_HD
    lane_q = lax.broadcasted_iota(jnp.int32, (tq, LANES), 1) < V_HD
    zero = jnp.zeros((tk, LANES), BF16)
    for pr in range(n_heads // 2):
        cols = slice(pr * LANES, (pr + 1) * LANES)
        v2 = v_ref[:, cols]
        v_halves = (jnp.where(lane_k, v2, zero), jnp.where(lane_k, zero, v2))
        pv, corr = [], []
        for hh in range(2):
            hd = 2 * pr + hh
            hcols = slice(hd * LANES, (hd + 1) * LANES)
            s = _bdot_nt(q_ref[:, hcols], k_ref[:, hcols])
            m_old = m_scr[hd]
            m_new = jnp.maximum(m_old, jnp.max(s, axis=1, keepdims=True))
            p = jnp.exp(s - m_new)
            c = jnp.exp(m_old - m_new)
            l_scr[hd] = c * l_scr[hd] + jnp.sum(p, axis=1, keepdims=True)
            m_scr[hd] = m_new
            pv.append(_bdot(p, v_halves[hh]))
            corr.append(c)
        acc_scr[:, cols] = acc_scr[:, cols] * jnp.where(lane_q, corr[0], corr[1]) + pv[0] + pv[1]

    @pl.when(kt == pl.num_programs(2) - 1)
    def _():
        for pr in range(n_heads // 2):
            cols = slice(pr * LANES, (pr + 1) * LANES)
            inv = jnp.where(lane_q, 1.0 / l_scr[2 * pr], 1.0 / l_scr[2 * pr + 1])
            o_ref[:, cols] = (acc_scr[:, cols] * inv).astype(BF16)


def _mla_layer(rows, x, mod, p, ln_g, ln_b, alpha, need_ctx):
    t, d = rows.tile, x.shape[1]
    h_ = MLA_HEADS
    w_down, w_uq, w_ukv = p["w_down"], p["w_uq"], p["w_ukv"]
    q_lora = w_uq.shape[0]
    kv_lora = w_ukv.shape[0]
    qk = QK_NOPE + QK_ROPE
    assert QK_NOPE == V_HD == LANES // 2 and h_ % 2 == 0 and q_lora % LANES == 0 and kv_lora % LANES == 0
    pad = LANES - qk
    zc = lambda r, w: jnp.zeros((r, w), F32)
    wd_p = jnp.concatenate([w_down[:, :q_lora + kv_lora], zc(d, QK_NOPE), w_down[:, q_lora + kv_lora:],
                            zc(d, pad)], axis=1).astype(BF16)
    wq_p = jnp.concatenate([w_uq.reshape(q_lora, h_, qk), jnp.zeros((q_lora, h_, pad), F32)],
                           axis=-1).reshape(q_lora, h_ * LANES).astype(BF16)
    ukv = w_ukv.reshape(kv_lora, h_, QK_NOPE + V_HD)
    wk_p = jnp.concatenate([ukv[..., :QK_NOPE], jnp.zeros((kv_lora, h_, LANES - QK_NOPE), F32)],
                           axis=-1).reshape(kv_lora, h_ * LANES).astype(BF16)
    wv_p = ukv[..., QK_NOPE:].reshape(kv_lora, h_ * V_HD).astype(BF16)
    cos, sin = _rope_tables(rows, QK_ROPE, QK_NOPE, pad)
    n_down = wd_p.shape[1]
    q, k, v = pl.pallas_call(
        functools.partial(_mla_proj_kernel, q_lora=q_lora, kv_lora=kv_lora, scale=qk ** -0.5, n_heads=h_),
        grid=(rows.tiles,),
        in_specs=[pl.BlockSpec((t, d), lambda i: (i, 0)),
                  pl.BlockSpec((1, 6, d), lambda i: (rows.mod_index(i), 0, 0)),
                  pl.BlockSpec((d, n_down), lambda i: (0, 0)),
                  pl.BlockSpec((1, q_lora), lambda i: (0, 0)),
                  pl.BlockSpec((1, kv_lora), lambda i: (0, 0)),
                  pl.BlockSpec((q_lora, h_ * LANES), lambda i: (0, 0)),
                  pl.BlockSpec((kv_lora, h_ * LANES), lambda i: (0, 0)),
                  pl.BlockSpec((kv_lora, h_ * V_HD), lambda i: (0, 0)),
                  pl.BlockSpec((1, t, LANES), lambda i: (rows.rope_index(i), 0, 0)),
                  pl.BlockSpec((1, t, LANES), lambda i: (rows.rope_index(i), 0, 0))],
        out_specs=[pl.BlockSpec((t, h_ * LANES), lambda i: (i, 0)),
                   pl.BlockSpec((t, h_ * LANES), lambda i: (i, 0)),
                   pl.BlockSpec((t, h_ * V_HD), lambda i: (i, 0))],
        out_shape=[jax.ShapeDtypeStruct((rows.rows, h_ * LANES), BF16),
                   jax.ShapeDtypeStruct((rows.rows, h_ * LANES), BF16),
                   jax.ShapeDtypeStruct((rows.rows, h_ * V_HD), BF16)],
        compiler_params=_params("arbitrary"),
        name="mla_proj",
    )(x, mod, wd_p, p["q_norm"].reshape(1, -1), p["kv_norm"].reshape(1, -1), wq_p, wk_p, wv_p, cos, sin)

    b_, s_, n_ctx = rows.batch, rows.seq, rows.n_ctx
    tq, tk = MLA_TQ, MLA_TK
    assert n_ctx % tk == 0 and s_ % tk == 0 and n_ctx % tq == 0 and s_ % tq == 0
    ck, lk = n_ctx // tk, s_ // tk
    ctx_kblocks = b_ * ck

    def kv_block(b, kt):
        return jnp.where(kt < ck, b * ck + kt, ctx_kblocks + b * lk + (kt - ck))

    scratch = [pltpu.VMEM((h_, tq, 1), F32), pltpu.VMEM((h_, tq, 1), F32), pltpu.VMEM((tq, h_ * V_HD), F32)]
    o_shape = jax.ShapeDtypeStruct((rows.rows, h_ * V_HD), BF16)
    lq = s_ // tq
    ctx_qblocks = b_ * n_ctx // tq
    o_lat = pl.pallas_call(
        functools.partial(_mla_attn_kernel, n_heads=h_),
        grid=(b_, lq, ck + lk),
        in_specs=[pl.BlockSpec((tq, h_ * LANES), lambda b, i, kt: (ctx_qblocks + b * lq + i, 0)),
                  pl.BlockSpec((tk, h_ * LANES), lambda b, i, kt: (kv_block(b, kt), 0)),
                  pl.BlockSpec((tk, h_ * V_HD), lambda b, i, kt: (kv_block(b, kt), 0))],
        out_specs=pl.BlockSpec((tq, h_ * V_HD), lambda b, i, kt: (ctx_qblocks + b * lq + i, 0)),
        out_shape=o_shape,
        scratch_shapes=scratch,
        compiler_params=_params("arbitrary", "arbitrary", "arbitrary"),
        name="mla_attn",
    )(q, k, v)
    if need_ctx:
        cq = n_ctx // tq
        o = pl.pallas_call(
            functools.partial(_mla_attn_kernel, n_heads=h_),
            grid=(b_, cq, ck),
            in_specs=[pl.BlockSpec((tq, h_ * LANES), lambda b, i, kt: (b * cq + i, 0)),
                      pl.BlockSpec((tk, h_ * LANES), lambda b, i, kt: (b * ck + kt, 0)),
                      pl.BlockSpec((tk, h_ * V_HD), lambda b, i, kt: (b * ck + kt, 0)),
                      pl.BlockSpec(memory_space=pl.ANY)],
            out_specs=pl.BlockSpec((tq, h_ * V_HD), lambda b, i, kt: (b * cq + i, 0)),
            out_shape=o_shape,
            scratch_shapes=scratch,
            input_output_aliases={3: 0},
            compiler_params=_params("arbitrary", "arbitrary", "arbitrary"),
            name="mla_ctx_attn",
        )(q, k, v, o_lat)
    else:
        o = o_lat
    return _out_ln(rows, o, p["w_o"], jnp.zeros((d,), F32), x, mod, ln_g, ln_b, alpha=alpha,
                   first_tile=0 if need_ctx else rows.ctx_tiles)


def _router_kernel(x_ref, mod_ref, rw_ref, rb_ref, idx_ref, gate_ref, rank_ref, cnt_ref, cnt_scr, *, top_k):
    @pl.when(pl.program_id(0) == 0)
    def _():
        cnt_scr[...] = jnp.zeros_like(cnt_scr)

    m = mod_ref[0]
    h = x_ref[...] * (1.0 + m[4:5]) + m[3:4]
    logits = jnp.dot(h, rw_ref[...], preferred_element_type=F32, precision=lax.Precision.HIGHEST) + rb_ref[...]
    t, n_exp = logits.shape
    lane = lax.broadcasted_iota(jnp.int32, (t, n_exp), 1).astype(F32)
    work = logits
    sel, val = [], []
    for _ in range(top_k):
        mx = jnp.max(work, axis=1, keepdims=True)
        pick = jnp.min(jnp.where(work == mx, lane, float(n_exp)), axis=1, keepdims=True)
        sel.append(pick)
        val.append(mx)
        work = jnp.where(lane == pick, -jnp.inf, work)
    ex = [jnp.exp(v - val[0]) for v in val]
    tot = ex[0]
    for e in ex[1:]:
        tot = tot + e
    onehot = jnp.zeros((t, n_exp), F32)
    for pick in sel:
        onehot = onehot + (lane == pick).astype(F32)
    r_i = lax.broadcasted_iota(jnp.int32, (t, t), 0)
    c_i = lax.broadcasted_iota(jnp.int32, (t, t), 1)
    before = _bdot((c_i < r_i).astype(BF16), onehot) + cnt_scr[...]
    col = lax.broadcasted_iota(jnp.int32, (t, top_k), 1)
    idx = jnp.zeros((t, top_k), jnp.int32)
    gate = jnp.zeros((t, top_k), F32)
    rank = jnp.zeros((t, top_k), jnp.int32)
    for k in range(top_k):
        rk = jnp.sum(jnp.where(lane == sel[k], before, 0.0), axis=1, keepdims=True).astype(jnp.int32)
        idx = jnp.where(col == k, sel[k].astype(jnp.int32), idx)
        gate = jnp.where(col == k, ex[k] / tot, gate)
        rank = jnp.where(col == k, rk, rank)
    idx_ref[...] = idx
    gate_ref[...] = gate
    rank_ref[...] = rank
    cnt_scr[...] = cnt_scr[...] + jnp.sum(onehot, axis=0, keepdims=True)
    cnt_ref[...] = cnt_scr[...]


def _dispatch_row_copy(h_scr, xs_ref, sem, dest_ref, t, k, top_k):
    d = dest_ref[0, 0, t * top_k + k]
    return pltpu.make_async_copy(h_scr.at[pl.ds(t, 1)], xs_ref.at[pl.ds(d, 1)], sem)


def _dispatch_kernel(dest_ref, x_ref, mod_ref, xs_in_ref, xs_ref, h_scr, sem, *, top_k):
    del xs_in_ref
    m = mod_ref[0]
    h_scr[...] = x_ref[...] * (1.0 + m[4:5]) + m[3:4]
    t_rows = h_scr.shape[0]

    def start(t, c):
        for k in range(top_k):
            _dispatch_row_copy(h_scr, xs_ref, sem, dest_ref, t, k, top_k).start()
        return c

    def wait(t, c):
        for k in range(top_k):
            _dispatch_row_copy(h_scr, xs_ref, sem, dest_ref, t, k, top_k).wait()
        return c

    lax.fori_loop(0, t_rows, start, 0)
    lax.fori_loop(0, t_rows, wait, 0)


def _expert_kernel(be_ref, nu_ref, x_ref, wg_ref, wu_ref, bg_ref, bu_ref, wd_ref, bd_ref, y_ref):
    del be_ref

    @pl.when(pl.program_id(0) < nu_ref[0])
    def _():
        xb = x_ref[...].astype(BF16)
        g = jnp.minimum(_bdot(xb, wg_ref[0]) + bg_ref[0], SWIGLU_LIMIT)
        u = jnp.clip(_bdot(xb, wu_ref[0]) + bu_ref[0], -SWIGLU_LIMIT, SWIGLU_LIMIT)
        act = (u + 1.0) * (g * jax.nn.sigmoid(SWIGLU_ALPHA * g))
        y_ref[...] = _bdot(act, wd_ref[0]) + bd_ref[0]


def _combine_row_copy(y_ref, ybuf, sem, dest_ref, t, k, top_k):
    d = dest_ref[0, 0, t * top_k + k]
    return pltpu.make_async_copy(y_ref.at[pl.ds(d, 1)], ybuf.at[k, pl.ds(t, 1)], sem)


def _combine_kernel(dest_ref, x_ref, mod_ref, gate_ref, g_ref, bt_ref, y_ref, o_ref, ybuf, sem, *, top_k, alpha):
    t_rows = x_ref.shape[0]

    def start(t, c):
        for k in range(top_k):
            _combine_row_copy(y_ref, ybuf, sem, dest_ref, t, k, top_k).start()
        return c

    def wait(t, c):
        for k in range(top_k):
            _combine_row_copy(y_ref, ybuf, sem, dest_ref, t, k, top_k).wait()
        return c

    lax.fori_loop(0, t_rows, start, 0)
    lax.fori_loop(0, t_rows, wait, 0)
    gate = gate_ref[...]
    acc = gate[:, 0:1] * ybuf[0]
    for k in range(1, top_k):
        acc = acc + gate[:, k:k + 1] * ybuf[k]
    m = mod_ref[0]
    r = alpha * x_ref[...] + m[5:6] * acc
    o_ref[...] = _layer_norm(r, g_ref[...], bt_ref[...])


def _moe_layer(rows, x, mod, router_w, router_b, w_gu, b_gu, w_down, b_down, ln_g, ln_b, alpha, need_ctx):
    t, d = rows.tile, x.shape[1]
    n_exp = router_w.shape[1]
    d_ff = w_down.shape[1]
    first = 0 if need_ctx else rows.ctx_tiles
    n_tiles = rows.tiles - first
    n_tok = n_tiles * t
    row = lambda i: (i + first, 0)
    modi = lambda i: (rows.mod_index(i + first), 0, 0)
    tok = lambda i: (i, 0)

    idx, gate, rank, cnt = pl.pallas_call(
        functools.partial(_router_kernel, top_k=TOP_K),
        grid=(n_tiles,),
        in_specs=[pl.BlockSpec((t, d), row), pl.BlockSpec((1, 6, d), modi),
                  pl.BlockSpec((d, n_exp), lambda i: (0, 0)), pl.BlockSpec((1, n_exp), lambda i: (0, 0))],
        out_specs=[pl.BlockSpec((t, TOP_K), tok), pl.BlockSpec((t, TOP_K), tok), pl.BlockSpec((t, TOP_K), tok),
                   pl.BlockSpec((1, n_exp), lambda i: (0, 0))],
        out_shape=[jax.ShapeDtypeStruct((n_tok, TOP_K), jnp.int32), jax.ShapeDtypeStruct((n_tok, TOP_K), F32),
                   jax.ShapeDtypeStruct((n_tok, TOP_K), jnp.int32), jax.ShapeDtypeStruct((1, n_exp), F32)],
        scratch_shapes=[pltpu.VMEM((1, n_exp), F32)],
        compiler_params=_params("arbitrary"),
        name="moe_router",
    )(x, mod, router_w, router_b.reshape(1, n_exp))

    count = cnt[0].astype(jnp.int32)
    padded = (count + MOE_BLOCK - 1) // MOE_BLOCK * MOE_BLOCK
    pad_end = jnp.cumsum(padded)
    pad_start = pad_end - padded
    n_asg = n_tok * TOP_K
    n_blocks = (n_asg + n_exp * (MOE_BLOCK - 1) + MOE_BLOCK - 1) // MOE_BLOCK
    n_slots = n_blocks * MOE_BLOCK
    n_used = (pad_end[-1] // MOE_BLOCK).astype(jnp.int32)
    blk = jnp.minimum(jnp.arange(n_blocks, dtype=jnp.int32), n_used - 1) * MOE_BLOCK
    block_expert = jnp.minimum(jnp.searchsorted(pad_end, blk, side="right"), n_exp - 1).astype(jnp.int32)
    dest = (pad_start[idx] + rank).astype(jnp.int32).reshape(n_tiles, 1, t * TOP_K)

    xs = pl.pallas_call(
        functools.partial(_dispatch_kernel, top_k=TOP_K),
        grid_spec=pltpu.PrefetchScalarGridSpec(
            num_scalar_prefetch=0,
            grid=(n_tiles,),
            in_specs=[pl.BlockSpec((1, 1, t * TOP_K), lambda i: (i, 0, 0), memory_space=pltpu.SMEM),
                      pl.BlockSpec((t, d), row), pl.BlockSpec((1, 6, d), modi),
                      pl.BlockSpec(memory_space=pl.ANY)],
            out_specs=pl.BlockSpec(memory_space=pl.ANY),
            scratch_shapes=[pltpu.VMEM((t, d), F32), pltpu.SemaphoreType.DMA(())]),
        out_shape=jax.ShapeDtypeStruct((n_slots, d), F32),
        input_output_aliases={3: 0},
        compiler_params=_params("arbitrary"),
        name="moe_dispatch",
    )(dest, x, mod, jnp.zeros((n_slots, d), F32))

    wg = w_gu[:, :, 0::2].astype(BF16)
    wu = w_gu[:, :, 1::2].astype(BF16)
    bg = b_gu[:, 0::2].reshape(n_exp, 1, d_ff)
    bu = b_gu[:, 1::2].reshape(n_exp, 1, d_ff)
    used = lambda i, be, nu: (jnp.minimum(i, nu[0] - 1), 0)
    wmap = lambda i, be, nu: (be[i], 0, 0)
    y = pl.pallas_call(
        _expert_kernel,
        grid_spec=pltpu.PrefetchScalarGridSpec(
            num_scalar_prefetch=2,
            grid=(n_blocks,),
            in_specs=[pl.BlockSpec((MOE_BLOCK, d), used),
                      pl.BlockSpec((1, d, d_ff), wmap), pl.BlockSpec((1, d, d_ff), wmap),
                      pl.BlockSpec((1, 1, d_ff), wmap), pl.BlockSpec((1, 1, d_ff), wmap),
                      pl.BlockSpec((1, d_ff, d), wmap), pl.BlockSpec((1, 1, d), wmap)],
            out_specs=pl.BlockSpec((MOE_BLOCK, d), used)),
        out_shape=jax.ShapeDtypeStruct((n_slots, d), F32),
        compiler_params=_params("arbitrary"),
        name="moe_experts",
    )(block_expert, n_used.reshape(1), xs, wg, wu, bg, bu, w_down.astype(BF16), b_down.reshape(n_exp, 1, d))

    return pl.pallas_call(
        functools.partial(_combine_kernel, top_k=TOP_K, alpha=alpha),
        grid_spec=pltpu.PrefetchScalarGridSpec(
            num_scalar_prefetch=0,
            grid=(n_tiles,),
            in_specs=[pl.BlockSpec((1, 1, t * TOP_K), lambda i: (i, 0, 0), memory_space=pltpu.SMEM),
                      pl.BlockSpec((t, d), row), pl.BlockSpec((1, 6, d), modi),
                      pl.BlockSpec((t, TOP_K), tok),
                      pl.BlockSpec((1, d), lambda i: (0, 0)), pl.BlockSpec((1, d), lambda i: (0, 0)),
                      pl.BlockSpec(memory_space=pl.ANY)],
            out_specs=pl.BlockSpec((t, d), row),
            scratch_shapes=[pltpu.VMEM((TOP_K, t, d), F32), pltpu.SemaphoreType.DMA(())]),
        out_shape=jax.ShapeDtypeStruct(x.shape, F32),
        input_output_aliases={1: 0},
        compiler_params=_params("arbitrary"),
        name="moe_combine_ln",
    )(dest, x, mod, gate, ln_g.reshape(1, d), ln_b.reshape(1, d), y)


def kernel(x, c, ctx, c_ctx, ada_w, ada_b, ln1_g, ln1_b, ln2_g, ln2_b, router_w, router_b, exp_gu_w, exp_gu_b, exp_down_w, exp_down_b, rg_w_in, rg_conv_w, rg_conv_b, rg_gate_a_w, rg_gate_a_b, rg_gate_x_w, rg_gate_x_b, rg_lambda, rg_w_out, gqa_w_qkv, gqa_b_qkv, gqa_sinks, gqa_w_o, gqa_b_o, mla_w_down, mla_q_norm, mla_kv_norm, mla_w_uq, mla_w_ukv, mla_w_o):
    batch, seq, d = x.shape
    n_ctx = ctx.shape[1]
    depth = ada_w.shape[0]
    alpha = (2 * depth) ** 0.25
    rows = _Rows(batch, n_ctx, seq, ROW_TILE)
    xa = jnp.concatenate([ctx.reshape(batch * n_ctx, d), x.reshape(batch * seq, d)], axis=0)
    mods = _ada_table(jnp.concatenate([c_ctx[None, :], c], axis=0), ada_w, ada_b)
    for i in range(depth):
        need_ctx = i < depth - 1
        kind, j = i % 3, i // 3
        mod = mods[i]
        if kind == 0:
            prm = dict(w_in=rg_w_in[j], conv_w=rg_conv_w[j], conv_b=rg_conv_b[j], gate_a_w=rg_gate_a_w[j],
                       gate_a_b=rg_gate_a_b[j], gate_x_w=rg_gate_x_w[j], gate_x_b=rg_gate_x_b[j],
                       lam=rg_lambda[j], w_out=rg_w_out[j])
            xa = _rglru_layer(rows, xa, mod, prm, ln1_g[i], ln1_b[i], alpha, need_ctx)
        elif kind == 1:
            prm = dict(w_qkv=gqa_w_qkv[j], b_qkv=gqa_b_qkv[j], sinks=gqa_sinks[j], w_o=gqa_w_o[j], b_o=gqa_b_o[j])
            xa = _gqa_layer(rows, xa, mod, prm, ln1_g[i], ln1_b[i], alpha, need_ctx)
        else:
            prm = dict(w_down=mla_w_down[j], q_norm=mla_q_norm[j], kv_norm=mla_kv_norm[j], w_uq=mla_w_uq[j],
                       w_ukv=mla_w_ukv[j], w_o=mla_w_o[j])
            xa = _mla_layer(rows, xa, mod, prm, ln1_g[i], ln1_b[i], alpha, need_ctx)
        xa = _moe_layer(rows, xa, mod, router_w[i], router_b[i], exp_gu_w[i], exp_gu_b[i], exp_down_w[i],
                        exp_down_b[i], ln2_g[i], ln2_b[i], alpha, need_ctx)
    return xa[batch * n_ctx:].reshape(batch, seq, d)
```

```python
import functools
import math

import jax
import jax.numpy as jnp
from jax import lax
from jax.experimental import pallas as pl
from jax.experimental.pallas import tpu as pltpu

F32 = jnp.float32
BF16 = jnp.bfloat16

GRID_W = 64
LN_EPS = 1e-5
RMS_EPS = 1e-6
ROPE_THETA = 10000.0
NEG_INF = -1e30
RG_C = 8.0
GQA_KV = 2
GQA_HD = 64
WINDOW = 128
MLA_HEADS = 16
QK_NOPE = 64
QK_ROPE = 32
V_HD = 64
TOP_K = 4
SWIGLU_LIMIT = 7.0
SWIGLU_ALPHA = 1.702
MOE_BLOCK = 256

LANES = 128
SUBLANES = 8
ROW_TILE = 256
ATTN_Q = 128
MLA_TQ = 512
MLA_TK = 256
MXU_DIM = 256
VMEM_LIMIT = 56 * 1024 * 1024


def _params(*sem):
    return pltpu.CompilerParams(dimension_semantics=sem, vmem_limit_bytes=VMEM_LIMIT)


def _bdot(a, b):
    return jnp.dot(a.astype(BF16), b.astype(BF16), preferred_element_type=F32)


def _bdot_nt(a, b):
    return lax.dot_general(a.astype(BF16), b.astype(BF16), (((1,), (1,)), ((), ())),
                           preferred_element_type=F32)


def _layer_norm(r, g, b):
    mu = jnp.mean(r, axis=-1, keepdims=True)
    d = r - mu
    var = jnp.mean(d * d, axis=-1, keepdims=True)
    return d * lax.rsqrt(var + LN_EPS) * g + b


def _gelu_tanh(x):
    return 0.5 * x * (1.0 + jnp.tanh(math.sqrt(2.0 / math.pi) * (x + 0.044715 * (x * x * x))))


class _Rows:
    def __init__(self, batch, n_ctx, seq, tile):
        assert n_ctx % tile == 0 and seq % tile == 0
        self.batch, self.n_ctx, self.seq, self.tile = batch, n_ctx, seq, tile
        self.ctx_tiles = batch * n_ctx // tile
        self.lat_tiles = batch * seq // tile
        self.tiles = self.ctx_tiles + self.lat_tiles
        self.lat_per_batch = seq // tile
        self.ctx_per_batch = n_ctx // tile
        self.rows = batch * (n_ctx + seq)

    def mod_index(self, i):
        return jnp.where(i < self.ctx_tiles, 0, 1 + (i - self.ctx_tiles) // self.lat_per_batch)

    def rope_index(self, i):
        return jnp.where(i < self.ctx_tiles, self.lat_per_batch, (i - self.ctx_tiles) % self.lat_per_batch)


def _ada_kernel(c_ref, w_ref, b_ref, o_ref):
    cv = c_ref[...]
    s = cv * jax.nn.sigmoid(cv)
    o_ref[0] = jnp.dot(s, w_ref[0], preferred_element_type=F32,
                       precision=lax.Precision.HIGHEST) + b_ref[0]


def _ada_table(cvec, ada_w, ada_b):
    depth, d, d6 = ada_w.shape
    n = cvec.shape[0]
    chunk = d
    out = pl.pallas_call(
        _ada_kernel,
        grid=(depth, d6 // chunk),
        in_specs=[pl.BlockSpec((n, d), lambda l, j: (0, 0)),
                  pl.BlockSpec((1, d, chunk), lambda l, j: (l, 0, j)),
                  pl.BlockSpec((1, 1, chunk), lambda l, j: (l, 0, j))],
        out_specs=pl.BlockSpec((1, n, chunk), lambda l, j: (l, 0, j)),
        out_shape=jax.ShapeDtypeStruct((depth, n, d6), F32),
        compiler_params=_params("arbitrary", "arbitrary"),
        name="ada_table",
    )(cvec, ada_w, ada_b.reshape(depth, 1, d6))
    return out.reshape(depth, n, 6, d)


def _out_ln_kernel(z_ref, w_ref, b_ref, x_ref, mod_ref, g_ref, bt_ref, o_ref, *, gate_row, alpha):
    y = _bdot(z_ref[...], w_ref[...]) + b_ref[...]
    m = mod_ref[0]
    r = alpha * x_ref[...] + m[gate_row:gate_row + 1] * y
    o_ref[...] = _layer_norm(r, g_ref[...], bt_ref[...])


def _out_ln(rows, z, w, bias, x, mod, ln_g, ln_b, *, alpha, first_tile=0):
    t, d = rows.tile, x.shape[1]
    kdim = z.shape[1]
    n_tiles = rows.tiles - first_tile
    row = lambda i: (i + first_tile, 0)
    return pl.pallas_call(
        functools.partial(_out_ln_kernel, gate_row=2, alpha=alpha),
        grid=(n_tiles,),
        in_specs=[pl.BlockSpec((t, kdim), row),
                  pl.BlockSpec((kdim, d), lambda i: (0, 0)),
                  pl.BlockSpec((1, d), lambda i: (0, 0)),
                  pl.BlockSpec((t, d), row),
                  pl.BlockSpec((1, 6, d), lambda i: (rows.mod_index(i + first_tile), 0, 0)),
                  pl.BlockSpec((1, d), lambda i: (0, 0)),
                  pl.BlockSpec((1, d), lambda i: (0, 0))],
        out_specs=pl.BlockSpec((t, d), row),
        out_shape=jax.ShapeDtypeStruct(x.shape, F32),
        input_output_aliases={3: 0},
        compiler_params=_params("arbitrary"),
        name="out_proj_ln",
    )(z, w.astype(BF16), bias.reshape(1, d), x, mod, ln_g.reshape(1, d), ln_b.reshape(1, d))


def _rg_in_kernel(x_ref, mod_ref, w_ref, gel_ref, rec_ref, *, d_rnn):
    m = mod_ref[0]
    h = (x_ref[...] * (1.0 + m[1:2]) + m[0:1]).astype(BF16)
    gel_ref[...] = _gelu_tanh(_bdot(h, w_ref[:, :d_rnn])).astype(BF16)
    rec_ref[...] = _bdot(h, w_ref[:, d_rnn:])


def _rg_in(rows, x, mod, w_in):
    t, d = rows.tile, x.shape[1]
    d_rnn = w_in.shape[1] // 2
    return pl.pallas_call(
        functools.partial(_rg_in_kernel, d_rnn=d_rnn),
        grid=(rows.tiles,),
        in_specs=[pl.BlockSpec((t, d), lambda i: (i, 0)),
                  pl.BlockSpec((1, 6, d), lambda i: (rows.mod_index(i), 0, 0)),
                  pl.BlockSpec((d, 2 * d_rnn), lambda i: (0, 0))],
        out_specs=[pl.BlockSpec((t, d_rnn), lambda i: (i, 0)),
                   pl.BlockSpec((t, d_rnn), lambda i: (i, 0))],
        out_shape=[jax.ShapeDtypeStruct((rows.rows, d_rnn), BF16),
                   jax.ShapeDtypeStruct((rows.rows, d_rnn), F32)],
        compiler_params=_params("arbitrary"),
        name="rg_in_proj",
    )(x, mod, w_in.astype(BF16))


def _rg_scan_kernel(*refs, reverse, fuse_out, n_blocks, block_w, tile, ctx_tiles, lat_tiles):
    if fuse_out:
        (x_ref, xp_ref, xn_ref, cw_ref, cb_ref, wa_ref, ba_ref, wx_ref, bx_ref, lam_ref,
         hf_ref, gel_ref, out_ref, a_scr, u_scr, h_scr, carry_scr) = refs
    else:
        (x_ref, xp_ref, xn_ref, cw_ref, cb_ref, wa_ref, ba_ref, wx_ref, bx_ref, lam_ref,
         out_ref, a_scr, u_scr, carry_scr) = refs
        h_scr = out_ref
    j = pl.program_id(1)
    is_ctx = j < ctx_tiles
    n_seq = jnp.where(is_ctx, ctx_tiles, lat_tiles)
    step = jnp.where(is_ctx, j, j - ctx_tiles)
    pos = (n_seq - 1 - step) if reverse else step
    prev_ok = (pos > 0).astype(F32)
    next_ok = (pos < n_seq - 1).astype(F32)

    @pl.when(j == 0)
    def _():
        carry_scr[...] = jnp.zeros_like(carry_scr)

    row = lax.broadcasted_iota(jnp.int32, (tile, block_w), 0)
    for n in range(n_blocks):
        cols = slice(n * block_w, (n + 1) * block_w)
        x = x_ref[:, cols]
        prev = xp_ref[SUBLANES - 1:SUBLANES, cols] * prev_ok
        nxt0 = xn_ref[0:1, cols] * next_ok
        nxt1 = xn_ref[1:2, cols] * next_ok
        x_m1 = jnp.where(row == 0, prev, pltpu.roll(x, 1, 0))
        x_p1 = jnp.where(row == tile - 1, nxt0, pltpu.roll(x, tile - 1, 0))
        x_p2 = jnp.where(row == tile - 2, nxt0, jnp.where(row == tile - 1, nxt1, pltpu.roll(x, tile - 2, 0)))
        xc = (cw_ref[0:1, cols] * x_m1 + cw_ref[1:2, cols] * x + cw_ref[2:3, cols] * x_p1
              + cw_ref[3:4, cols] * x_p2 + cb_ref[:, cols])
        xb = xc.astype(BF16)
        r = jax.nn.sigmoid(_bdot(xb, wa_ref[n]) + ba_ref[:, cols])
        gi = jax.nn.sigmoid(_bdot(xb, wx_ref[n]) + bx_ref[:, cols])
        z = -lam_ref[:, cols]
        softplus = jnp.maximum(z, 0.0) + jnp.log1p(jnp.exp(-jnp.abs(z)))
        log_a = (-RG_C) * r * softplus
        a = jnp.exp(log_a)
        a_scr[:, cols] = a
        u_scr[:, cols] = jnp.sqrt(1.0 - a * a) * (gi * xc)

    width = n_blocks * block_w
    sub = lax.broadcasted_iota(jnp.int32, (SUBLANES, width), 0)
    groups = tile // SUBLANES

    def body(g, carry):
        gg = (groups - 1 - g) if reverse else g
        r0 = pl.multiple_of(gg * SUBLANES, SUBLANES)
        a8 = a_scr[pl.ds(r0, SUBLANES), :]
        u8 = u_scr[pl.ds(r0, SUBLANES), :]
        for s in (1, 2, 4):
            if reverse:
                a_sh, u_sh, ok = pltpu.roll(a8, SUBLANES - s, 0), pltpu.roll(u8, SUBLANES - s, 0), sub < SUBLANES - s
            else:
                a_sh, u_sh, ok = pltpu.roll(a8, s, 0), pltpu.roll(u8, s, 0), sub >= s
            u8 = jnp.where(ok, a8 * u_sh + u8, u8)
            a8 = jnp.where(ok, a8 * a_sh, a8)
        h8 = a8 * carry + u8
        h_scr[pl.ds(r0, SUBLANES), :] = h8
        return h8[0:1, :] if reverse else h8[SUBLANES - 1:SUBLANES, :]

    carry_scr[...] = lax.fori_loop(0, groups, body, carry_scr[...])
    if fuse_out:
        out_ref[...] = ((hf_ref[...] + h_scr[...]) * gel_ref[...].astype(F32)).astype(BF16)


def _rg_scan(rows, rec, conv_w, conv_b, wa, ba, wx, bx, lam, *, reverse, h_fwd=None, gel=None):
    t = rows.tile
    c = rec.shape[1]
    n_blocks, block_w = wa.shape[0], wa.shape[1]
    nc, nl = rows.ctx_per_batch, rows.lat_per_batch
    halo = t // SUBLANES
    last_halo = rows.rows // SUBLANES - 1
    fuse_out = h_fwd is not None

    def tile_index(b, j):
        is_ctx = j < nc
        step = jnp.where(is_ctx, j, j - nc)
        n_seq = jnp.where(is_ctx, nc, nl)
        pos = (n_seq - 1 - step) if reverse else step
        return jnp.where(is_ctx, b * nc + pos, rows.ctx_tiles + b * nl + pos)

    cur = lambda b, j: (tile_index(b, j), 0)
    prv = lambda b, j: (jnp.maximum(tile_index(b, j) * halo - 1, 0), 0)
    nxt = lambda b, j: (jnp.minimum((tile_index(b, j) + 1) * halo, last_halo), 0)
    full2 = lambda b, j: (0, 0)
    full3 = lambda b, j: (0, 0, 0)
    in_specs = [pl.BlockSpec((t, c), cur), pl.BlockSpec((SUBLANES, c), prv), pl.BlockSpec((SUBLANES, c), nxt),
                pl.BlockSpec((4, c), full2), pl.BlockSpec((1, c), full2),
                pl.BlockSpec((n_blocks, block_w, block_w), full3), pl.BlockSpec((1, c), full2),
                pl.BlockSpec((n_blocks, block_w, block_w), full3), pl.BlockSpec((1, c), full2),
                pl.BlockSpec((1, c), full2)]
    args = [rec, rec, rec, conv_w, conv_b.reshape(1, c), wa.astype(BF16), ba.reshape(1, c),
            wx.astype(BF16), bx.reshape(1, c), lam.reshape(1, c)]
    scratch = [pltpu.VMEM((t, c), F32), pltpu.VMEM((t, c), F32)]
    if fuse_out:
        in_specs += [pl.BlockSpec((t, c), cur), pl.BlockSpec((t, c), cur)]
        args += [h_fwd, gel]
        scratch.append(pltpu.VMEM((t, c), F32))
    scratch.append(pltpu.VMEM((1, c), F32))
    return pl.pallas_call(
        functools.partial(_rg_scan_kernel, reverse=reverse, fuse_out=fuse_out, n_blocks=n_blocks,
                          block_w=block_w, tile=t, ctx_tiles=nc, lat_tiles=nl),
        grid=(rows.batch, nc + nl),
        in_specs=in_specs,
        out_specs=pl.BlockSpec((t, c), cur),
        out_shape=jax.ShapeDtypeStruct((rows.rows, c), BF16 if fuse_out else F32),
        scratch_shapes=scratch,
        compiler_params=_params("arbitrary", "arbitrary"),
        name="rg_scan_bwd" if reverse else "rg_scan_fwd",
    )(*args)


def _rglru_layer(rows, x, mod, p, ln_g, ln_b, alpha, need_ctx):
    gel, rec = _rg_in(rows, x, mod, p["w_in"])
    h_fwd = _rg_scan(rows, rec, p["conv_w"], p["conv_b"], p["gate_a_w"][0], p["gate_a_b"][0],
                     p["gate_x_w"][0], p["gate_x_b"][0], p["lam"][0], reverse=False)
    z = _rg_scan(rows, rec, p["conv_w"], p["conv_b"], p["gate_a_w"][1], p["gate_a_b"][1],
                 p["gate_x_w"][1], p["gate_x_b"][1], p["lam"][1], reverse=True, h_fwd=h_fwd, gel=gel)
    d = x.shape[1]
    return _out_ln(rows, z, p["w_out"], jnp.zeros((d,), F32), x, mod, ln_g, ln_b, alpha=alpha,
                   first_tile=0 if need_ctx else rows.ctx_tiles)


def _axial_angles(seq, rot_dim):
    pos = jnp.arange(seq, dtype=jnp.int32)
    row = (pos // GRID_W).astype(F32)
    col = (pos % GRID_W).astype(F32)
    n_freq = rot_dim // 4
    inv_freq = ROPE_THETA ** (-jnp.arange(n_freq, dtype=F32) / n_freq)
    return jnp.concatenate([row[:, None] * inv_freq, col[:, None] * inv_freq], axis=-1)


def _rope_tables(rows, rot_dim, lead, trail):
    ang = _axial_angles(rows.seq, rot_dim)
    cos, sin = jnp.cos(ang), jnp.sin(ang)
    ones = lambda w: jnp.ones((rows.seq, w), F32)
    zeros = lambda w: jnp.zeros((rows.seq, w), F32)
    c = jnp.concatenate([ones(lead), cos, cos, ones(trail)], axis=-1)
    s = jnp.concatenate([zeros(lead), -sin, sin, zeros(trail)], axis=-1)
    reps = LANES // c.shape[1]
    c, s = jnp.tile(c, (1, reps)), jnp.tile(s, (1, reps))
    t = rows.tile
    c = jnp.concatenate([c.reshape(rows.lat_per_batch, t, LANES), jnp.ones((1, t, LANES), F32)], axis=0)
    s = jnp.concatenate([s.reshape(rows.lat_per_batch, t, LANES), jnp.zeros((1, t, LANES), F32)], axis=0)
    return c, s


def _rope_chunk(x, cos, sin, half):
    lane = lax.broadcasted_iota(jnp.int32, x.shape, 1)
    partner = jnp.where((lane % (2 * half)) < half, pltpu.roll(x, LANES - half, 1), pltpu.roll(x, half, 1))
    return x * cos + partner * sin


def _gqa_proj_kernel(x_ref, mod_ref, w_ref, b_ref, cos_ref, sin_ref, q_ref, kv_ref, *, q_dim, scale, half):
    m = mod_ref[0]
    h = (x_ref[...] * (1.0 + m[1:2]) + m[0:1]).astype(BF16)
    cos, sin = cos_ref[0], sin_ref[0]
    n_q = q_dim // LANES
    n_all = w_ref.shape[1] // LANES
    for c in range(n_all):
        cols = slice(c * LANES, (c + 1) * LANES)
        p = _bdot(h, w_ref[:, cols]) + b_ref[:, cols]
        is_v = c in (n_q + 1, n_q + 3)
        if not is_v:
            p = _rope_chunk(p, cos, sin, half)
        if c < n_q:
            q_ref[:, cols] = (p * scale).astype(BF16)
        else:
            kv_ref[:, (c - n_q) * LANES:(c - n_q + 1) * LANES] = p.astype(BF16)


def _gqa_attn_kernel(*refs, windowed, window, seq, n_pairs, pairs_per_kv):
    if windowed:
        q_ref, kv_ref, kvc_ref, sink_ref, o_ref = refs
    else:
        q_ref, kvc_ref, sink_ref, _, o_ref = refs
    tq = q_ref.shape[0]
    kvc = kvc_ref[...]
    if windowed:
        span = tq + 2 * window
        qs = pl.program_id(1) * tq
        ws = pl.multiple_of(jnp.clip(qs - window, 0, seq - span), LANES)
        kv = jnp.concatenate([kv_ref[pl.ds(ws, span), :], kvc], axis=0)
        n_keys = kv.shape[0]
        qpos = qs + lax.broadcasted_iota(jnp.int32, (tq, n_keys), 0)
        col = lax.broadcasted_iota(jnp.int32, (tq, n_keys), 1)
        mask = (jnp.abs(ws + col - qpos) <= window) | (col >= span)
    else:
        kv = kvc
        mask = None
    lane = lax.broadcasted_iota(jnp.int32, (kv.shape[0], LANES), 1)
    low = lane < GQA_HD
    zero = jnp.zeros((kv.shape[0], LANES), BF16)
    k_plain, v_plain = kv[:, 0:LANES], kv[:, LANES:2 * LANES]
    k_swap, v_swap = kv[:, 2 * LANES:3 * LANES], kv[:, 3 * LANES:4 * LANES]
    for pr in range(n_pairs):
        g = pr // pairs_per_kv
        k_lo = jnp.where(low, k_plain if g == 0 else k_swap, zero)
        k_hi = jnp.where(low, zero, k_swap if g == 0 else k_plain)
        v_lo = jnp.where(low, v_plain if g == 0 else v_swap, zero)
        v_hi = jnp.where(low, zero, v_swap if g == 0 else v_plain)
        q2 = q_ref[:, pr * LANES:(pr + 1) * LANES]
        acc = jnp.zeros((tq, LANES), F32)
        for hh, (kx, vx) in enumerate(((k_lo, v_lo), (k_hi, v_hi))):
            head = 2 * pr + hh
            s = _bdot_nt(q2, kx)
            if mask is not None:
                s = jnp.where(mask, s, NEG_INF)
            sk = sink_ref[head:head + 1, 0:1]
            mx = jnp.maximum(jnp.max(s, axis=1, keepdims=True), sk)
            p = jnp.exp(s - mx)
            denom = jnp.sum(p, axis=1, keepdims=True) + jnp.exp(sk - mx)
            acc = acc + _bdot(p * (1.0 / denom), vx)
        o_ref[:, pr * LANES:(pr + 1) * LANES] = acc.astype(BF16)


def _gqa_layer(rows, x, mod, p, ln_g, ln_b, alpha, need_ctx):
    t, d = rows.tile, x.shape[1]
    w_qkv, b_qkv = p["w_qkv"], p["b_qkv"]
    kv_dim = GQA_KV * GQA_HD
    q_dim = w_qkv.shape[1] - 2 * kv_dim
    n_heads = q_dim // GQA_HD
    assert kv_dim == LANES and GQA_KV == 2 and q_dim % LANES == 0
    swap = lambda a: jnp.concatenate([a[..., GQA_HD:], a[..., :GQA_HD]], axis=-1)
    wk, wv = w_qkv[:, q_dim:q_dim + kv_dim], w_qkv[:, q_dim + kv_dim:]
    bk, bv = b_qkv[q_dim:q_dim + kv_dim], b_qkv[q_dim + kv_dim:]
    w_ext = jnp.concatenate([w_qkv, swap(wk), swap(wv)], axis=1).astype(BF16)
    b_ext = jnp.concatenate([b_qkv, swap(bk), swap(bv)]).reshape(1, -1)
    n_ext = w_ext.shape[1]
    cos, sin = _rope_tables(rows, GQA_HD, 0, 0)
    q, kv = pl.pallas_call(
        functools.partial(_gqa_proj_kernel, q_dim=q_dim, scale=GQA_HD ** -0.5, half=GQA_HD // 2),
        grid=(rows.tiles,),
        in_specs=[pl.BlockSpec((t, d), lambda i: (i, 0)),
                  pl.BlockSpec((1, 6, d), lambda i: (rows.mod_index(i), 0, 0)),
                  pl.BlockSpec((d, n_ext), lambda i: (0, 0)),
                  pl.BlockSpec((1, n_ext), lambda i: (0, 0)),
                  pl.BlockSpec((1, t, LANES), lambda i: (rows.rope_index(i), 0, 0)),
                  pl.BlockSpec((1, t, LANES), lambda i: (rows.rope_index(i), 0, 0))],
        out_specs=[pl.BlockSpec((t, q_dim), lambda i: (i, 0)),
                   pl.BlockSpec((t, 4 * LANES), lambda i: (i, 0))],
        out_shape=[jax.ShapeDtypeStruct((rows.rows, q_dim), BF16),
                   jax.ShapeDtypeStruct((rows.rows, 4 * LANES), BF16)],
        compiler_params=_params("arbitrary"),
        name="gqa_qkv_proj",
    )(x, mod, w_ext, b_ext, cos, sin)

    sinks = jnp.broadcast_to(p["sinks"].astype(F32)[:, None], (n_heads, LANES))
    b_, s_, n_ctx = rows.batch, rows.seq, rows.n_ctx
    ctx_rows = b_ * n_ctx
    assert ctx_rows % s_ == 0 and s_ >= ATTN_Q + 2 * WINDOW
    q_blocks = s_ // ATTN_Q
    n_pairs = q_dim // LANES
    common = dict(window=WINDOW, seq=s_, n_pairs=n_pairs, pairs_per_kv=n_pairs // GQA_KV)
    o_shape = jax.ShapeDtypeStruct((rows.rows, q_dim), BF16)
    o_lat = pl.pallas_call(
        functools.partial(_gqa_attn_kernel, windowed=True, **common),
        grid=(b_, q_blocks),
        in_specs=[pl.BlockSpec((ATTN_Q, q_dim), lambda b, j: (ctx_rows // ATTN_Q + b * q_blocks + j, 0)),
                  pl.BlockSpec((s_, 4 * LANES), lambda b, j: (ctx_rows // s_ + b, 0)),
                  pl.BlockSpec((n_ctx, 4 * LANES), lambda b, j: (b, 0)),
                  pl.BlockSpec((n_heads, LANES), lambda b, j: (0, 0))],
        out_specs=pl.BlockSpec((ATTN_Q, q_dim), lambda b, j: (ctx_rows // ATTN_Q + b * q_blocks + j, 0)),
        out_shape=o_shape,
        compiler_params=_params("arbitrary", "arbitrary"),
        name="gqa_window_attn",
    )(q, kv, kv, sinks)
    if need_ctx:
        cq_blocks = n_ctx // ATTN_Q
        o = pl.pallas_call(
            functools.partial(_gqa_attn_kernel, windowed=False, **common),
            grid=(b_, cq_blocks),
            in_specs=[pl.BlockSpec((ATTN_Q, q_dim), lambda b, j: (b * cq_blocks + j, 0)),
                      pl.BlockSpec((n_ctx, 4 * LANES), lambda b, j: (b, 0)),
                      pl.BlockSpec((n_heads, LANES), lambda b, j: (0, 0)),
                      pl.BlockSpec(memory_space=pl.ANY)],
            out_specs=pl.BlockSpec((ATTN_Q, q_dim), lambda b, j: (b * cq_blocks + j, 0)),
            out_shape=o_shape,
            input_output_aliases={3: 0},
            compiler_params=_params("arbitrary", "arbitrary"),
            name="gqa_ctx_attn",
        )(q, kv, sinks, o_lat)
    else:
        o = o_lat
    return _out_ln(rows, o, p["w_o"], p["b_o"], x, mod, ln_g, ln_b, alpha=alpha,
                   first_tile=0 if need_ctx else rows.ctx_tiles)


def _mla_proj_kernel(x_ref, mod_ref, wd_ref, qn_ref, kvn_ref, wq_ref, wk_ref, wv_ref, cos_ref, sin_ref,
                     q_ref, k_ref, vt_ref, *, q_lora, kv_lora, scale, n_heads):
    m = mod_ref[0]
    h = (x_ref[...] * (1.0 + m[1:2]) + m[0:1]).astype(BF16)
    p = _bdot(h, wd_ref[...])
    cq, ckv = p[:, :q_lora], p[:, q_lora:q_lora + kv_lora]
    cos, sin = cos_ref[0], sin_ref[0]
    k_rope = _rope_chunk(p[:, q_lora + kv_lora:], cos, sin, QK_ROPE // 2)
    cq = (cq * lax.rsqrt(jnp.mean(cq * cq, axis=-1, keepdims=True) + RMS_EPS) * qn_ref[...]).astype(BF16)
    ckv = (ckv * lax.rsqrt(jnp.mean(ckv * ckv, axis=-1, keepdims=True) + RMS_EPS) * kvn_ref[...]).astype(BF16)
    for hd in range(n_heads):
        cols = slice(hd * LANES, (hd + 1) * LANES)
        qh = _rope_chunk(_bdot(cq, wq_ref[:, cols]), cos, sin, QK_ROPE // 2)
        q_ref[:, cols] = (qh * scale).astype(BF16)
        k_ref[:, cols] = (_bdot(ckv, wk_ref[:, cols]) + k_rope).astype(BF16)
    vt_ref[...] = _bdot(ckv, wv_ref[...]).T.astype(BF16)


def _mla_attn_kernel(*refs, n_heads):
    q_ref, k_ref, vt_ref = refs[:3]
    o_ref, m_scr, l_scr, acc_scr = refs[-4:]
    kt = pl.program_id(2)

    @pl.when(kt == 0)
    def _():
        m_scr[...] = jnp.full_like(m_scr, -jnp.inf)
        l_scr[...] = jnp.zeros_like(l_scr)
        acc_scr[...] = jnp.zeros_like(acc_scr)

    for hd in range(n_heads):
        cols = slice(hd * LANES, (hd + 1) * LANES)
        rws = slice(hd * V_HD, (hd + 1) * V_HD)
        s_t = _bdot_nt(k_ref[:, cols], q_ref[:, cols])
        m_old = m_scr[hd:hd + 1, :]
        m_new = jnp.maximum(m_old, jnp.max(s_t, axis=0, keepdims=True))
        p_t = jnp.exp2(s_t - m_new)
        corr = jnp.exp2(m_old - m_new)
        l_scr[hd:hd + 1, :] = corr * l_scr[hd:hd + 1, :] + jnp.sum(p_t, axis=0, keepdims=True)
        m_scr[hd:hd + 1, :] = m_new
        acc_scr[rws, :] = acc_scr[rws, :] * corr + _bdot(vt_ref[rws, :], p_t)

    @pl.when(kt == pl.num_programs(2) - 1)
    def _():
        for hd in range(n_heads):
            rws = slice(hd * V_HD, (hd + 1) * V_HD)
            acc_scr[rws, :] = acc_scr[rws, :] * (1.0 / l_scr[hd:hd + 1, :])
        o_ref[...] = acc_scr[...].T.astype(BF16)


def _mla_layer(rows, x, mod, p, ln_g, ln_b, alpha, need_ctx):
    t, d = rows.tile, x.shape[1]
    h_ = MLA_HEADS
    w_down, w_uq, w_ukv = p["w_down"], p["w_uq"], p["w_ukv"]
    q_lora = w_uq.shape[0]
    kv_lora = w_ukv.shape[0]
    qk = QK_NOPE + QK_ROPE
    assert QK_NOPE == V_HD == LANES // 2 and h_ % 2 == 0 and q_lora % LANES == 0 and kv_lora % LANES == 0
    pad = LANES - qk
    zc = lambda r, w: jnp.zeros((r, w), F32)
    wd_p = jnp.concatenate([w_down[:, :q_lora + kv_lora], zc(d, QK_NOPE), w_down[:, q_lora + kv_lora:],
                            zc(d, pad)], axis=1).astype(BF16)
    wq_p = jnp.concatenate([w_uq.reshape(q_lora, h_, qk), jnp.zeros((q_lora, h_, pad), F32)],
                           axis=-1).reshape(q_lora, h_ * LANES).astype(BF16)
    ukv = w_ukv.reshape(kv_lora, h_, QK_NOPE + V_HD)
    wk_p = jnp.concatenate([ukv[..., :QK_NOPE], jnp.zeros((kv_lora, h_, LANES - QK_NOPE), F32)],
                           axis=-1).reshape(kv_lora, h_ * LANES).astype(BF16)
    wv_p = ukv[..., QK_NOPE:].reshape(kv_lora, h_ * V_HD).astype(BF16)
    cos, sin = _rope_tables(rows, QK_ROPE, QK_NOPE, pad)
    n_down = wd_p.shape[1]
    q, k, vt = pl.pallas_call(
        functools.partial(_mla_proj_kernel, q_lora=q_lora, kv_lora=kv_lora, scale=qk ** -0.5 * math.log2(math.e),
                          n_heads=h_),
        grid=(rows.tiles,),
        in_specs=[pl.BlockSpec((t, d), lambda i: (i, 0)),
                  pl.BlockSpec((1, 6, d), lambda i: (rows.mod_index(i), 0, 0)),
                  pl.BlockSpec((d, n_down), lambda i: (0, 0)),
                  pl.BlockSpec((1, q_lora), lambda i: (0, 0)),
                  pl.BlockSpec((1, kv_lora), lambda i: (0, 0)),
                  pl.BlockSpec((q_lora, h_ * LANES), lambda i: (0, 0)),
                  pl.BlockSpec((kv_lora, h_ * LANES), lambda i: (0, 0)),
                  pl.BlockSpec((kv_lora, h_ * V_HD), lambda i: (0, 0)),
                  pl.BlockSpec((1, t, LANES), lambda i: (rows.rope_index(i), 0, 0)),
                  pl.BlockSpec((1, t, LANES), lambda i: (rows.rope_index(i), 0, 0))],
        out_specs=[pl.BlockSpec((t, h_ * LANES), lambda i: (i, 0)),
                   pl.BlockSpec((t, h_ * LANES), lambda i: (i, 0)),
                   pl.BlockSpec((h_ * V_HD, t), lambda i: (0, i))],
        out_shape=[jax.ShapeDtypeStruct((rows.rows, h_ * LANES), BF16),
                   jax.ShapeDtypeStruct((rows.rows, h_ * LANES), BF16),
                   jax.ShapeDtypeStruct((h_ * V_HD, rows.rows), BF16)],
        compiler_params=_params("arbitrary"),
        name="mla_proj",
    )(x, mod, wd_p, p["q_norm"].reshape(1, -1), p["kv_norm"].reshape(1, -1), wq_p, wk_p, wv_p, cos, sin)

    b_, s_, n_ctx = rows.batch, rows.seq, rows.n_ctx
    tk = MLA_TK
    tq, tq_ctx = min(MLA_TQ, s_), min(MLA_TQ, n_ctx)
    assert n_ctx % tk == 0 and s_ % tk == 0 and n_ctx % tq_ctx == 0 and s_ % tq == 0
    ck, lk = n_ctx // tk, s_ // tk
    ctx_kblocks = b_ * ck

    def kv_block(b, kt):
        return jnp.where(kt < ck, b * ck + kt, ctx_kblocks + b * lk + (kt - ck))

    def scratch(rows_q):
        return [pltpu.VMEM((h_, rows_q), F32), pltpu.VMEM((h_, rows_q), F32), pltpu.VMEM((h_ * V_HD, rows_q), F32)]

    o_shape = jax.ShapeDtypeStruct((rows.rows, h_ * V_HD), BF16)
    lq = s_ // tq
    ctx_qblocks = b_ * n_ctx // tq
    o_lat = pl.pallas_call(
        functools.partial(_mla_attn_kernel, n_heads=h_),
        grid=(b_, lq, ck + lk),
        in_specs=[pl.BlockSpec((tq, h_ * LANES), lambda b, i, kt: (ctx_qblocks + b * lq + i, 0)),
                  pl.BlockSpec((tk, h_ * LANES), lambda b, i, kt: (kv_block(b, kt), 0)),
                  pl.BlockSpec((h_ * V_HD, tk), lambda b, i, kt: (0, kv_block(b, kt)))],
        out_specs=pl.BlockSpec((tq, h_ * V_HD), lambda b, i, kt: (ctx_qblocks + b * lq + i, 0)),
        out_shape=o_shape,
        scratch_shapes=scratch(tq),
        compiler_params=_params("arbitrary", "arbitrary", "arbitrary"),
        name="mla_attn",
    )(q, k, vt)
    if need_ctx:
        cq = n_ctx // tq_ctx
        o = pl.pallas_call(
            functools.partial(_mla_attn_kernel, n_heads=h_),
            grid=(b_, cq, ck),
            in_specs=[pl.BlockSpec((tq_ctx, h_ * LANES), lambda b, i, kt: (b * cq + i, 0)),
                      pl.BlockSpec((tk, h_ * LANES), lambda b, i, kt: (b * ck + kt, 0)),
                      pl.BlockSpec((h_ * V_HD, tk), lambda b, i, kt: (0, b * ck + kt)),
                      pl.BlockSpec(memory_space=pl.ANY)],
            out_specs=pl.BlockSpec((tq_ctx, h_ * V_HD), lambda b, i, kt: (b * cq + i, 0)),
            out_shape=o_shape,
            scratch_shapes=scratch(tq_ctx),
            input_output_aliases={3: 0},
            compiler_params=_params("arbitrary", "arbitrary", "arbitrary"),
            name="mla_ctx_attn",
        )(q, k, vt, o_lat)
    else:
        o = o_lat
    return _out_ln(rows, o, p["w_o"], jnp.zeros((d,), F32), x, mod, ln_g, ln_b, alpha=alpha,
                   first_tile=0 if need_ctx else rows.ctx_tiles)


def _split_gu_kernel(w_ref, g_ref, u_ref):
    win = 2 * MXU_DIM
    r_i = lax.broadcasted_iota(jnp.int32, (win, MXU_DIM), 0)
    c_i = lax.broadcasted_iota(jnp.int32, (win, MXU_DIM), 1)
    pick_even = (r_i == 2 * c_i).astype(BF16)
    pick_odd = (r_i == 2 * c_i + 1).astype(BF16)
    for j in range(w_ref.shape[2] // win):
        w = w_ref[0, :, j * win:(j + 1) * win].astype(BF16)
        g_ref[0, :, j * MXU_DIM:(j + 1) * MXU_DIM] = _bdot(w, pick_even).astype(BF16)
        u_ref[0, :, j * MXU_DIM:(j + 1) * MXU_DIM] = _bdot(w, pick_odd).astype(BF16)


def _split_gate_up(w_gu):
    depth, n_exp, d, f2 = w_gu.shape
    w = w_gu.reshape(depth * n_exp, d, f2)
    tr = ROW_TILE
    spec_out = pl.BlockSpec((1, tr, f2 // 2), lambda e, r: (e, r, 0))
    shape_out = jax.ShapeDtypeStruct((depth * n_exp, d, f2 // 2), BF16)
    return pl.pallas_call(
        _split_gu_kernel,
        grid=(depth * n_exp, d // tr),
        in_specs=[pl.BlockSpec((1, tr, f2), lambda e, r: (e, r, 0))],
        out_specs=[spec_out, spec_out],
        out_shape=[shape_out, shape_out],
        compiler_params=_params("arbitrary", "arbitrary"),
        name="moe_split_gate_up",
    )(w)


def _router_kernel(x_ref, mod_ref, rw_ref, rb_ref, idx_ref, gate_ref, rank_ref, cnt_ref, cnt_scr, *, top_k):
    @pl.when(pl.program_id(0) == 0)
    def _():
        cnt_scr[...] = jnp.zeros_like(cnt_scr)

    m = mod_ref[0]
    h = x_ref[...] * (1.0 + m[4:5]) + m[3:4]
    logits = jnp.dot(h, rw_ref[...], preferred_element_type=F32, precision=lax.Precision.HIGHEST) + rb_ref[...]
    t, n_exp = logits.shape
    lane = lax.broadcasted_iota(jnp.int32, (t, n_exp), 1).astype(F32)
    work = logits
    sel, val = [], []
    for _ in range(top_k):
        mx = jnp.max(work, axis=1, keepdims=True)
        pick = jnp.min(jnp.where(work == mx, lane, float(n_exp)), axis=1, keepdims=True)
        sel.append(pick)
        val.append(mx)
        work = jnp.where(lane == pick, -jnp.inf, work)
    ex = [jnp.exp(v - val[0]) for v in val]
    tot = ex[0]
    for e in ex[1:]:
        tot = tot + e
    onehot = jnp.zeros((t, n_exp), F32)
    for pick in sel:
        onehot = onehot + (lane == pick).astype(F32)
    r_i = lax.broadcasted_iota(jnp.int32, (t, t), 0)
    c_i = lax.broadcasted_iota(jnp.int32, (t, t), 1)
    before = _bdot((c_i < r_i).astype(BF16), onehot) + cnt_scr[...]
    col = lax.broadcasted_iota(jnp.int32, (t, top_k), 1)
    idx = jnp.zeros((t, top_k), jnp.int32)
    gate = jnp.zeros((t, top_k), F32)
    rank = jnp.zeros((t, top_k), jnp.int32)
    for k in range(top_k):
        rk = jnp.sum(jnp.where(lane == sel[k], before, 0.0), axis=1, keepdims=True).astype(jnp.int32)
        idx = jnp.where(col == k, sel[k].astype(jnp.int32), idx)
        gate = jnp.where(col == k, ex[k] / tot, gate)
        rank = jnp.where(col == k, rk, rank)
    idx_ref[...] = idx
    gate_ref[...] = gate
    rank_ref[...] = rank
    cnt_scr[...] = cnt_scr[...] + jnp.sum(onehot, axis=0, keepdims=True)
    cnt_ref[...] = cnt_scr[...]


def _dispatch_row_copy(h_scr, xs_ref, sem, dest_ref, t, k, top_k):
    d = dest_ref[0, 0, t * top_k + k]
    return pltpu.make_async_copy(h_scr.at[pl.ds(t, 1)], xs_ref.at[pl.ds(d, 1)], sem)


def _dispatch_kernel(dest_ref, x_ref, mod_ref, xs_in_ref, xs_ref, h_scr, sem, *, top_k):
    del xs_in_ref
    m = mod_ref[0]
    h_scr[...] = x_ref[...] * (1.0 + m[4:5]) + m[3:4]
    t_rows = h_scr.shape[0]

    def start(t, c):
        for k in range(top_k):
            _dispatch_row_copy(h_scr, xs_ref, sem, dest_ref, t, k, top_k).start()
        return c

    def wait(t, c):
        for k in range(top_k):
            _dispatch_row_copy(h_scr, xs_ref, sem, dest_ref, t, k, top_k).wait()
        return c

    lax.fori_loop(0, t_rows, start, 0)
    lax.fori_loop(0, t_rows, wait, 0)


def _expert_kernel(be_ref, nu_ref, x_ref, wg_ref, wu_ref, bg_ref, bu_ref, wd_ref, bd_ref, y_ref):
    del be_ref

    @pl.when(pl.program_id(0) < nu_ref[0])
    def _():
        xb = x_ref[...].astype(BF16)
        g = jnp.minimum(_bdot(xb, wg_ref[0]) + bg_ref[0], SWIGLU_LIMIT)
        u = jnp.clip(_bdot(xb, wu_ref[0]) + bu_ref[0], -SWIGLU_LIMIT, SWIGLU_LIMIT)
        act = (u + 1.0) * (g * jax.nn.sigmoid(SWIGLU_ALPHA * g))
        y_ref[...] = _bdot(act, wd_ref[0]) + bd_ref[0]


def _combine_row_copy(y_ref, ybuf, sem, dest_ref, t, k, top_k):
    d = dest_ref[0, 0, t * top_k + k]
    return pltpu.make_async_copy(y_ref.at[pl.ds(d, 1)], ybuf.at[k, pl.ds(t, 1)], sem)


def _combine_kernel(dest_ref, x_ref, mod_ref, gate_ref, g_ref, bt_ref, y_ref, o_ref, ybuf, sem, *, top_k, alpha):
    t_rows = x_ref.shape[0]

    def start(t, c):
        for k in range(top_k):
            _combine_row_copy(y_ref, ybuf, sem, dest_ref, t, k, top_k).start()
        return c

    def wait(t, c):
        for k in range(top_k):
            _combine_row_copy(y_ref, ybuf, sem, dest_ref, t, k, top_k).wait()
        return c

    lax.fori_loop(0, t_rows, start, 0)
    lax.fori_loop(0, t_rows, wait, 0)
    gate = gate_ref[...]
    acc = gate[:, 0:1] * ybuf[0]
    for k in range(1, top_k):
        acc = acc + gate[:, k:k + 1] * ybuf[k]
    m = mod_ref[0]
    r = alpha * x_ref[...] + m[5:6] * acc
    o_ref[...] = _layer_norm(r, g_ref[...], bt_ref[...])


def _moe_layer(rows, x, mod, router_w, router_b, wg, wu, wd, layer, b_gu, b_down, ln_g, ln_b, alpha, need_ctx):
    t, d = rows.tile, x.shape[1]
    n_exp = router_w.shape[1]
    d_ff = wd.shape[1]
    first = 0 if need_ctx else rows.ctx_tiles
    n_tiles = rows.tiles - first
    n_tok = n_tiles * t
    row = lambda i: (i + first, 0)
    modi = lambda i: (rows.mod_index(i + first), 0, 0)
    tok = lambda i: (i, 0)

    idx, gate, rank, cnt = pl.pallas_call(
        functools.partial(_router_kernel, top_k=TOP_K),
        grid=(n_tiles,),
        in_specs=[pl.BlockSpec((t, d), row), pl.BlockSpec((1, 6, d), modi),
                  pl.BlockSpec((d, n_exp), lambda i: (0, 0)), pl.BlockSpec((1, n_exp), lambda i: (0, 0))],
        out_specs=[pl.BlockSpec((t, TOP_K), tok), pl.BlockSpec((t, TOP_K), tok), pl.BlockSpec((t, TOP_K), tok),
                   pl.BlockSpec((1, n_exp), lambda i: (0, 0))],
        out_shape=[jax.ShapeDtypeStruct((n_tok, TOP_K), jnp.int32), jax.ShapeDtypeStruct((n_tok, TOP_K), F32),
                   jax.ShapeDtypeStruct((n_tok, TOP_K), jnp.int32), jax.ShapeDtypeStruct((1, n_exp), F32)],
        scratch_shapes=[pltpu.VMEM((1, n_exp), F32)],
        compiler_params=_params("arbitrary"),
        name="moe_router",
    )(x, mod, router_w, router_b.reshape(1, n_exp))

    count = cnt[0].astype(jnp.int32)
    padded = (count + MOE_BLOCK - 1) // MOE_BLOCK * MOE_BLOCK
    pad_end = jnp.cumsum(padded)
    pad_start = pad_end - padded
    n_asg = n_tok * TOP_K
    n_blocks = (n_asg + n_exp * (MOE_BLOCK - 1) + MOE_BLOCK - 1) // MOE_BLOCK
    n_slots = n_blocks * MOE_BLOCK
    n_used = (pad_end[-1] // MOE_BLOCK).astype(jnp.int32)
    blk = jnp.minimum(jnp.arange(n_blocks, dtype=jnp.int32), n_used - 1) * MOE_BLOCK
    block_expert = jnp.minimum(jnp.searchsorted(pad_end, blk, side="right"), n_exp - 1).astype(jnp.int32)
    dest = (pad_start[idx] + rank).astype(jnp.int32).reshape(n_tiles, 1, t * TOP_K)

    xs = pl.pallas_call(
        functools.partial(_dispatch_kernel, top_k=TOP_K),
        grid_spec=pltpu.PrefetchScalarGridSpec(
            num_scalar_prefetch=0,
            grid=(n_tiles,),
            in_specs=[pl.BlockSpec((1, 1, t * TOP_K), lambda i: (i, 0, 0), memory_space=pltpu.SMEM),
                      pl.BlockSpec((t, d), row), pl.BlockSpec((1, 6, d), modi),
                      pl.BlockSpec(memory_space=pl.ANY)],
            out_specs=pl.BlockSpec(memory_space=pl.ANY),
            scratch_shapes=[pltpu.VMEM((t, d), F32), pltpu.SemaphoreType.DMA(())]),
        out_shape=jax.ShapeDtypeStruct((n_slots, d), F32),
        input_output_aliases={3: 0},
        compiler_params=_params("arbitrary"),
        name="moe_dispatch",
    )(dest, x, mod, jnp.zeros((n_slots, d), F32))

    bg = b_gu[:, 0::2].reshape(n_exp, 1, d_ff)
    bu = b_gu[:, 1::2].reshape(n_exp, 1, d_ff)
    used = lambda i, be, nu: (jnp.minimum(i, nu[0] - 1), 0)
    wmap = lambda i, be, nu: (be[i], 0, 0)
    wmap_all = lambda i, be, nu: (be[i] + layer * n_exp, 0, 0)
    y = pl.pallas_call(
        _expert_kernel,
        grid_spec=pltpu.PrefetchScalarGridSpec(
            num_scalar_prefetch=2,
            grid=(n_blocks,),
            in_specs=[pl.BlockSpec((MOE_BLOCK, d), used),
                      pl.BlockSpec((1, d, d_ff), wmap_all), pl.BlockSpec((1, d, d_ff), wmap_all),
                      pl.BlockSpec((1, 1, d_ff), wmap), pl.BlockSpec((1, 1, d_ff), wmap),
                      pl.BlockSpec((1, d_ff, d), wmap_all), pl.BlockSpec((1, 1, d), wmap)],
            out_specs=pl.BlockSpec((MOE_BLOCK, d), used)),
        out_shape=jax.ShapeDtypeStruct((n_slots, d), F32),
        compiler_params=_params("arbitrary"),
        name="moe_experts",
    )(block_expert, n_used.reshape(1), xs, wg, wu, bg, bu, wd, b_down.reshape(n_exp, 1, d))

    return pl.pallas_call(
        functools.partial(_combine_kernel, top_k=TOP_K, alpha=alpha),
        grid_spec=pltpu.PrefetchScalarGridSpec(
            num_scalar_prefetch=0,
            grid=(n_tiles,),
            in_specs=[pl.BlockSpec((1, 1, t * TOP_K), lambda i: (i, 0, 0), memory_space=pltpu.SMEM),
                      pl.BlockSpec((t, d), row), pl.BlockSpec((1, 6, d), modi),
                      pl.BlockSpec((t, TOP_K), tok),
                      pl.BlockSpec((1, d), lambda i: (0, 0)), pl.BlockSpec((1, d), lambda i: (0, 0)),
                      pl.BlockSpec(memory_space=pl.ANY)],
            out_specs=pl.BlockSpec((t, d), row),
            scratch_shapes=[pltpu.VMEM((TOP_K, t, d), F32), pltpu.SemaphoreType.DMA(())]),
        out_shape=jax.ShapeDtypeStruct(x.shape, F32),
        input_output_aliases={1: 0},
        compiler_params=_params("arbitrary"),
        name="moe_combine_ln",
    )(dest, x, mod, gate, ln_g.reshape(1, d), ln_b.reshape(1, d), y)


def kernel(x, c, ctx, c_ctx, ada_w, ada_b, ln1_g, ln1_b, ln2_g, ln2_b, router_w, router_b, exp_gu_w, exp_gu_b, exp_down_w, exp_down_b, rg_w_in, rg_conv_w, rg_conv_b, rg_gate_a_w, rg_gate_a_b, rg_gate_x_w, rg_gate_x_b, rg_lambda, rg_w_out, gqa_w_qkv, gqa_b_qkv, gqa_sinks, gqa_w_o, gqa_b_o, mla_w_down, mla_q_norm, mla_kv_norm, mla_w_uq, mla_w_ukv, mla_w_o):
    batch, seq, d = x.shape
    n_ctx = ctx.shape[1]
    depth = ada_w.shape[0]
    alpha = (2 * depth) ** 0.25
    rows = _Rows(batch, n_ctx, seq, ROW_TILE)
    xa = jnp.concatenate([ctx.reshape(batch * n_ctx, d), x.reshape(batch * seq, d)], axis=0)
    mods = _ada_table(jnp.concatenate([c_ctx[None, :], c], axis=0), ada_w, ada_b)
    wg, wu = _split_gate_up(exp_gu_w)
    wd = exp_down_w.reshape((-1,) + exp_down_w.shape[2:]).astype(BF16)
    for i in range(depth):
        need_ctx = i < depth - 1
        kind, j = i % 3, i // 3
        mod = mods[i]
        if kind == 0:
            prm = dict(w_in=rg_w_in[j], conv_w=rg_conv_w[j], conv_b=rg_conv_b[j], gate_a_w=rg_gate_a_w[j],
                       gate_a_b=rg_gate_a_b[j], gate_x_w=rg_gate_x_w[j], gate_x_b=rg_gate_x_b[j],
                       lam=rg_lambda[j], w_out=rg_w_out[j])
            xa = _rglru_layer(rows, xa, mod, prm, ln1_g[i], ln1_b[i], alpha, need_ctx)
        elif kind == 1:
            prm = dict(w_qkv=gqa_w_qkv[j], b_qkv=gqa_b_qkv[j], sinks=gqa_sinks[j], w_o=gqa_w_o[j], b_o=gqa_b_o[j])
            xa = _gqa_layer(rows, xa, mod, prm, ln1_g[i], ln1_b[i], alpha, need_ctx)
        else:
            prm = dict(w_down=mla_w_down[j], q_norm=mla_q_norm[j], kv_norm=mla_kv_norm[j], w_uq=mla_w_uq[j],
                       w_ukv=mla_w_ukv[j], w_o=mla_w_o[j])
            xa = _mla_layer(rows, xa, mod, prm, ln1_g[i], ln1_b[i], alpha, need_ctx)
        xa = _moe_layer(rows, xa, mod, router_w[i], router_b[i], wg, wu, wd, i, exp_gu_b[i], exp_down_b[i],
                        ln2_g[i], ln2_b[i], alpha, need_ctx)
    return xa[batch * n_ctx:].reshape(batch, seq, d)
```

```python
import functools
import math

import jax
import jax.numpy as jnp
from jax import lax
from jax.experimental import pallas as pl
from jax.experimental.pallas import tpu as pltpu

F32 = jnp.float32
BF16 = jnp.bfloat16

GRID_W = 64
LN_EPS = 1e-5
RMS_EPS = 1e-6
ROPE_THETA = 10000.0
NEG_INF = -1e30
RG_C = 8.0
GQA_KV = 2
GQA_HD = 64
WINDOW = 128
MLA_HEADS = 16
QK_NOPE = 64
QK_ROPE = 32
V_HD = 64
TOP_K = 4
SWIGLU_LIMIT = 7.0
SWIGLU_ALPHA = 1.702
MOE_BLOCK = 256

LANES = 128
SUBLANES = 8
ROW_TILE = 256
ATTN_Q = 128
MLA_TQ = 512
MLA_TK = 256
MXU_DIM = 256
DMA_QUEUES = 2
VMEM_LIMIT = 56 * 1024 * 1024


def _params(*sem):
    return pltpu.CompilerParams(dimension_semantics=sem, vmem_limit_bytes=VMEM_LIMIT)


def _bdot(a, b):
    return jnp.dot(a.astype(BF16), b.astype(BF16), preferred_element_type=F32)


def _bdot_nt(a, b):
    return lax.dot_general(a.astype(BF16), b.astype(BF16), (((1,), (1,)), ((), ())),
                           preferred_element_type=F32)


def _layer_norm(r, g, b):
    mu = jnp.mean(r, axis=-1, keepdims=True)
    d = r - mu
    var = jnp.mean(d * d, axis=-1, keepdims=True)
    return d * lax.rsqrt(var + LN_EPS) * g + b


def _gelu_tanh(x):
    return 0.5 * x * (1.0 + jnp.tanh(math.sqrt(2.0 / math.pi) * (x + 0.044715 * (x * x * x))))


class _Rows:
    def __init__(self, batch, n_ctx, seq, tile):
        assert n_ctx % tile == 0 and seq % tile == 0
        self.batch, self.n_ctx, self.seq, self.tile = batch, n_ctx, seq, tile
        self.ctx_tiles = batch * n_ctx // tile
        self.lat_tiles = batch * seq // tile
        self.tiles = self.ctx_tiles + self.lat_tiles
        self.lat_per_batch = seq // tile
        self.ctx_per_batch = n_ctx // tile
        self.rows = batch * (n_ctx + seq)

    def mod_index(self, i):
        return jnp.where(i < self.ctx_tiles, 0, 1 + (i - self.ctx_tiles) // self.lat_per_batch)

    def rope_index(self, i):
        return jnp.where(i < self.ctx_tiles, self.lat_per_batch, (i - self.ctx_tiles) % self.lat_per_batch)


def _ada_kernel(c_ref, w_ref, b_ref, o_ref):
    cv = c_ref[...]
    s = cv * jax.nn.sigmoid(cv)
    o_ref[0] = jnp.dot(s, w_ref[0], preferred_element_type=F32,
                       precision=lax.Precision.HIGHEST) + b_ref[0]


def _ada_table(cvec, ada_w, ada_b):
    depth, d, d6 = ada_w.shape
    n = cvec.shape[0]
    chunk = d
    out = pl.pallas_call(
        _ada_kernel,
        grid=(depth, d6 // chunk),
        in_specs=[pl.BlockSpec((n, d), lambda l, j: (0, 0)),
                  pl.BlockSpec((1, d, chunk), lambda l, j: (l, 0, j)),
                  pl.BlockSpec((1, 1, chunk), lambda l, j: (l, 0, j))],
        out_specs=pl.BlockSpec((1, n, chunk), lambda l, j: (l, 0, j)),
        out_shape=jax.ShapeDtypeStruct((depth, n, d6), F32),
        compiler_params=_params("arbitrary", "arbitrary"),
        name="ada_table",
    )(cvec, ada_w, ada_b.reshape(depth, 1, d6))
    return out.reshape(depth, n, 6, d)


def _out_ln_kernel(z_ref, w_ref, b_ref, x_ref, mod_ref, g_ref, bt_ref, o_ref, *, gate_row, alpha):
    y = _bdot(z_ref[...], w_ref[...]) + b_ref[...]
    m = mod_ref[0]
    r = alpha * x_ref[...] + m[gate_row:gate_row + 1] * y
    o_ref[...] = _layer_norm(r, g_ref[...], bt_ref[...])


def _out_ln(rows, z, w, bias, x, mod, ln_g, ln_b, *, alpha, first_tile=0):
    t, d = rows.tile, x.shape[1]
    kdim = z.shape[1]
    n_tiles = rows.tiles - first_tile
    row = lambda i: (i + first_tile, 0)
    return pl.pallas_call(
        functools.partial(_out_ln_kernel, gate_row=2, alpha=alpha),
        grid=(n_tiles,),
        in_specs=[pl.BlockSpec((t, kdim), row),
                  pl.BlockSpec((kdim, d), lambda i: (0, 0)),
                  pl.BlockSpec((1, d), lambda i: (0, 0)),
                  pl.BlockSpec((t, d), row),
                  pl.BlockSpec((1, 6, d), lambda i: (rows.mod_index(i + first_tile), 0, 0)),
                  pl.BlockSpec((1, d), lambda i: (0, 0)),
                  pl.BlockSpec((1, d), lambda i: (0, 0))],
        out_specs=pl.BlockSpec((t, d), row),
        out_shape=jax.ShapeDtypeStruct(x.shape, F32),
        input_output_aliases={3: 0},
        compiler_params=_params("arbitrary"),
        name="out_proj_ln",
    )(z, w.astype(BF16), bias.reshape(1, d), x, mod, ln_g.reshape(1, d), ln_b.reshape(1, d))


def _rg_in_kernel(x_ref, mod_ref, w_ref, gel_ref, rec_ref, *, d_rnn):
    m = mod_ref[0]
    h = (x_ref[...] * (1.0 + m[1:2]) + m[0:1]).astype(BF16)
    gel_ref[...] = _gelu_tanh(_bdot(h, w_ref[:, :d_rnn])).astype(BF16)
    rec_ref[...] = _bdot(h, w_ref[:, d_rnn:])


def _rg_in(rows, x, mod, w_in):
    t, d = rows.tile, x.shape[1]
    d_rnn = w_in.shape[1] // 2
    return pl.pallas_call(
        functools.partial(_rg_in_kernel, d_rnn=d_rnn),
        grid=(rows.tiles,),
        in_specs=[pl.BlockSpec((t, d), lambda i: (i, 0)),
                  pl.BlockSpec((1, 6, d), lambda i: (rows.mod_index(i), 0, 0)),
                  pl.BlockSpec((d, 2 * d_rnn), lambda i: (0, 0))],
        out_specs=[pl.BlockSpec((t, d_rnn), lambda i: (i, 0)),
                   pl.BlockSpec((t, d_rnn), lambda i: (i, 0))],
        out_shape=[jax.ShapeDtypeStruct((rows.rows, d_rnn), BF16),
                   jax.ShapeDtypeStruct((rows.rows, d_rnn), F32)],
        compiler_params=_params("arbitrary"),
        name="rg_in_proj",
    )(x, mod, w_in.astype(BF16))


def _rg_scan_kernel(*refs, reverse, fuse_out, n_blocks, block_w, tile, ctx_tiles, lat_tiles):
    if fuse_out:
        (x_ref, xp_ref, xn_ref, cw_ref, cb_ref, wa_ref, ba_ref, wx_ref, bx_ref, lam_ref,
         hf_ref, gel_ref, out_ref, a_scr, u_scr, h_scr, carry_scr) = refs
    else:
        (x_ref, xp_ref, xn_ref, cw_ref, cb_ref, wa_ref, ba_ref, wx_ref, bx_ref, lam_ref,
         out_ref, a_scr, u_scr, carry_scr) = refs
        h_scr = out_ref
    j = pl.program_id(1)
    is_ctx = j < ctx_tiles
    n_seq = jnp.where(is_ctx, ctx_tiles, lat_tiles)
    step = jnp.where(is_ctx, j, j - ctx_tiles)
    pos = (n_seq - 1 - step) if reverse else step
    prev_ok = (pos > 0).astype(F32)
    next_ok = (pos < n_seq - 1).astype(F32)

    @pl.when(j == 0)
    def _():
        carry_scr[...] = jnp.zeros_like(carry_scr)

    row = lax.broadcasted_iota(jnp.int32, (tile, block_w), 0)
    for n in range(n_blocks):
        cols = slice(n * block_w, (n + 1) * block_w)
        x = x_ref[:, cols]
        prev = xp_ref[SUBLANES - 1:SUBLANES, cols] * prev_ok
        nxt0 = xn_ref[0:1, cols] * next_ok
        nxt1 = xn_ref[1:2, cols] * next_ok
        x_m1 = jnp.where(row == 0, prev, pltpu.roll(x, 1, 0))
        x_p1 = jnp.where(row == tile - 1, nxt0, pltpu.roll(x, tile - 1, 0))
        x_p2 = jnp.where(row == tile - 2, nxt0, jnp.where(row == tile - 1, nxt1, pltpu.roll(x, tile - 2, 0)))
        xc = (cw_ref[0:1, cols] * x_m1 + cw_ref[1:2, cols] * x + cw_ref[2:3, cols] * x_p1
              + cw_ref[3:4, cols] * x_p2 + cb_ref[:, cols])
        xb = xc.astype(BF16)
        r = jax.nn.sigmoid(_bdot(xb, wa_ref[n]) + ba_ref[:, cols])
        gi = jax.nn.sigmoid(_bdot(xb, wx_ref[n]) + bx_ref[:, cols])
        z = -lam_ref[:, cols]
        softplus = jnp.maximum(z, 0.0) + jnp.log1p(jnp.exp(-jnp.abs(z)))
        log_a = (-RG_C) * r * softplus
        a = jnp.exp(log_a)
        a_scr[:, cols] = a
        u_scr[:, cols] = jnp.sqrt(1.0 - a * a) * (gi * xc)

    width = n_blocks * block_w
    sub = lax.broadcasted_iota(jnp.int32, (SUBLANES, width), 0)
    groups = tile // SUBLANES

    def body(g, carry):
        gg = (groups - 1 - g) if reverse else g
        r0 = pl.multiple_of(gg * SUBLANES, SUBLANES)
        a8 = a_scr[pl.ds(r0, SUBLANES), :]
        u8 = u_scr[pl.ds(r0, SUBLANES), :]
        for s in (1, 2, 4):
            if reverse:
                a_sh, u_sh, ok = pltpu.roll(a8, SUBLANES - s, 0), pltpu.roll(u8, SUBLANES - s, 0), sub < SUBLANES - s
            else:
                a_sh, u_sh, ok = pltpu.roll(a8, s, 0), pltpu.roll(u8, s, 0), sub >= s
            u8 = jnp.where(ok, a8 * u_sh + u8, u8)
            a8 = jnp.where(ok, a8 * a_sh, a8)
        h8 = a8 * carry + u8
        h_scr[pl.ds(r0, SUBLANES), :] = h8
        return h8[0:1, :] if reverse else h8[SUBLANES - 1:SUBLANES, :]

    carry_scr[...] = lax.fori_loop(0, groups, body, carry_scr[...])
    if fuse_out:
        out_ref[...] = ((hf_ref[...] + h_scr[...]) * gel_ref[...].astype(F32)).astype(BF16)


def _rg_scan(rows, rec, conv_w, conv_b, wa, ba, wx, bx, lam, *, reverse, h_fwd=None, gel=None):
    t = rows.tile
    c = rec.shape[1]
    n_blocks, block_w = wa.shape[0], wa.shape[1]
    nc, nl = rows.ctx_per_batch, rows.lat_per_batch
    halo = t // SUBLANES
    last_halo = rows.rows // SUBLANES - 1
    fuse_out = h_fwd is not None

    def tile_index(b, j):
        is_ctx = j < nc
        step = jnp.where(is_ctx, j, j - nc)
        n_seq = jnp.where(is_ctx, nc, nl)
        pos = (n_seq - 1 - step) if reverse else step
        return jnp.where(is_ctx, b * nc + pos, rows.ctx_tiles + b * nl + pos)

    cur = lambda b, j: (tile_index(b, j), 0)
    prv = lambda b, j: (jnp.maximum(tile_index(b, j) * halo - 1, 0), 0)
    nxt = lambda b, j: (jnp.minimum((tile_index(b, j) + 1) * halo, last_halo), 0)
    full2 = lambda b, j: (0, 0)
    full3 = lambda b, j: (0, 0, 0)
    in_specs = [pl.BlockSpec((t, c), cur), pl.BlockSpec((SUBLANES, c), prv), pl.BlockSpec((SUBLANES, c), nxt),
                pl.BlockSpec((4, c), full2), pl.BlockSpec((1, c), full2),
                pl.BlockSpec((n_blocks, block_w, block_w), full3), pl.BlockSpec((1, c), full2),
                pl.BlockSpec((n_blocks, block_w, block_w), full3), pl.BlockSpec((1, c), full2),
                pl.BlockSpec((1, c), full2)]
    args = [rec, rec, rec, conv_w, conv_b.reshape(1, c), wa.astype(BF16), ba.reshape(1, c),
            wx.astype(BF16), bx.reshape(1, c), lam.reshape(1, c)]
    scratch = [pltpu.VMEM((t, c), F32), pltpu.VMEM((t, c), F32)]
    if fuse_out:
        in_specs += [pl.BlockSpec((t, c), cur), pl.BlockSpec((t, c), cur)]
        args += [h_fwd, gel]
        scratch.append(pltpu.VMEM((t, c), F32))
    scratch.append(pltpu.VMEM((1, c), F32))
    return pl.pallas_call(
        functools.partial(_rg_scan_kernel, reverse=reverse, fuse_out=fuse_out, n_blocks=n_blocks,
                          block_w=block_w, tile=t, ctx_tiles=nc, lat_tiles=nl),
        grid=(rows.batch, nc + nl),
        in_specs=in_specs,
        out_specs=pl.BlockSpec((t, c), cur),
        out_shape=jax.ShapeDtypeStruct((rows.rows, c), BF16 if fuse_out else F32),
        scratch_shapes=scratch,
        compiler_params=_params("arbitrary", "arbitrary"),
        name="rg_scan_bwd" if reverse else "rg_scan_fwd",
    )(*args)


def _rglru_layer(rows, x, mod, p, ln_g, ln_b, alpha, need_ctx):
    gel, rec = _rg_in(rows, x, mod, p["w_in"])
    h_fwd = _rg_scan(rows, rec, p["conv_w"], p["conv_b"], p["gate_a_w"][0], p["gate_a_b"][0],
                     p["gate_x_w"][0], p["gate_x_b"][0], p["lam"][0], reverse=False)
    z = _rg_scan(rows, rec, p["conv_w"], p["conv_b"], p["gate_a_w"][1], p["gate_a_b"][1],
                 p["gate_x_w"][1], p["gate_x_b"][1], p["lam"][1], reverse=True, h_fwd=h_fwd, gel=gel)
    d = x.shape[1]
    return _out_ln(rows, z, p["w_out"], jnp.zeros((d,), F32), x, mod, ln_g, ln_b, alpha=alpha,
                   first_tile=0 if need_ctx else rows.ctx_tiles)


def _axial_angles(seq, rot_dim):
    pos = jnp.arange(seq, dtype=jnp.int32)
    row = (pos // GRID_W).astype(F32)
    col = (pos % GRID_W).astype(F32)
    n_freq = rot_dim // 4
    inv_freq = ROPE_THETA ** (-jnp.arange(n_freq, dtype=F32) / n_freq)
    return jnp.concatenate([row[:, None] * inv_freq, col[:, None] * inv_freq], axis=-1)


def _rope_tables(rows, rot_dim, lead, trail):
    ang = _axial_angles(rows.seq, rot_dim)
    cos, sin = jnp.cos(ang), jnp.sin(ang)
    ones = lambda w: jnp.ones((rows.seq, w), F32)
    zeros = lambda w: jnp.zeros((rows.seq, w), F32)
    c = jnp.concatenate([ones(lead), cos, cos, ones(trail)], axis=-1)
    s = jnp.concatenate([zeros(lead), -sin, sin, zeros(trail)], axis=-1)
    reps = LANES // c.shape[1]
    c, s = jnp.tile(c, (1, reps)), jnp.tile(s, (1, reps))
    t = rows.tile
    c = jnp.concatenate([c.reshape(rows.lat_per_batch, t, LANES), jnp.ones((1, t, LANES), F32)], axis=0)
    s = jnp.concatenate([s.reshape(rows.lat_per_batch, t, LANES), jnp.zeros((1, t, LANES), F32)], axis=0)
    return c, s


def _rope_chunk(x, cos, sin, half):
    lane = lax.broadcasted_iota(jnp.int32, x.shape, 1)
    partner = jnp.where((lane % (2 * half)) < half, pltpu.roll(x, LANES - half, 1), pltpu.roll(x, half, 1))
    return x * cos + partner * sin


def _gqa_proj_kernel(x_ref, mod_ref, w_ref, b_ref, cos_ref, sin_ref, q_ref, kv_ref, *, q_dim, scale, half):
    m = mod_ref[0]
    h = (x_ref[...] * (1.0 + m[1:2]) + m[0:1]).astype(BF16)
    cos, sin = cos_ref[0], sin_ref[0]
    n_q = q_dim // LANES
    n_all = w_ref.shape[1] // LANES
    for c in range(n_all):
        cols = slice(c * LANES, (c + 1) * LANES)
        p = _bdot(h, w_ref[:, cols]) + b_ref[:, cols]
        is_v = c in (n_q + 1, n_q + 3)
        if not is_v:
            p = _rope_chunk(p, cos, sin, half)
        if c < n_q:
            q_ref[:, cols] = (p * scale).astype(BF16)
        else:
            kv_ref[:, (c - n_q) * LANES:(c - n_q + 1) * LANES] = p.astype(BF16)


def _gqa_attn_kernel(*refs, windowed, window, seq, n_pairs, pairs_per_kv):
    if windowed:
        q_ref, kv_ref, kvc_ref, sink_ref, o_ref = refs
    else:
        q_ref, kvc_ref, sink_ref, _, o_ref = refs
    tq = q_ref.shape[0]
    kvc = kvc_ref[...]
    if windowed:
        span = tq + 2 * window
        qs = pl.program_id(1) * tq
        ws = pl.multiple_of(jnp.clip(qs - window, 0, seq - span), LANES)
        kv = jnp.concatenate([kv_ref[pl.ds(ws, span), :], kvc], axis=0)
        n_keys = kv.shape[0]
        qpos = qs + lax.broadcasted_iota(jnp.int32, (tq, n_keys), 0)
        col = lax.broadcasted_iota(jnp.int32, (tq, n_keys), 1)
        mask = (jnp.abs(ws + col - qpos) <= window) | (col >= span)
    else:
        kv = kvc
        mask = None
    lane = lax.broadcasted_iota(jnp.int32, (kv.shape[0], LANES), 1)
    low = lane < GQA_HD
    zero = jnp.zeros((kv.shape[0], LANES), BF16)
    k_plain, v_plain = kv[:, 0:LANES], kv[:, LANES:2 * LANES]
    k_swap, v_swap = kv[:, 2 * LANES:3 * LANES], kv[:, 3 * LANES:4 * LANES]
    for pr in range(n_pairs):
        g = pr // pairs_per_kv
        k_lo = jnp.where(low, k_plain if g == 0 else k_swap, zero)
        k_hi = jnp.where(low, zero, k_swap if g == 0 else k_plain)
        v_lo = jnp.where(low, v_plain if g == 0 else v_swap, zero)
        v_hi = jnp.where(low, zero, v_swap if g == 0 else v_plain)
        q2 = q_ref[:, pr * LANES:(pr + 1) * LANES]
        acc = jnp.zeros((tq, LANES), F32)
        for hh, (kx, vx) in enumerate(((k_lo, v_lo), (k_hi, v_hi))):
            head = 2 * pr + hh
            s = _bdot_nt(q2, kx)
            if mask is not None:
                s = jnp.where(mask, s, NEG_INF)
            sk = sink_ref[head:head + 1, 0:1]
            mx = jnp.maximum(jnp.max(s, axis=1, keepdims=True), sk)
            p = jnp.exp(s - mx)
            denom = jnp.sum(p, axis=1, keepdims=True) + jnp.exp(sk - mx)
            acc = acc + _bdot(p * (1.0 / denom), vx)
        o_ref[:, pr * LANES:(pr + 1) * LANES] = acc.astype(BF16)


def _gqa_layer(rows, x, mod, p, ln_g, ln_b, alpha, need_ctx):
    t, d = rows.tile, x.shape[1]
    w_qkv, b_qkv = p["w_qkv"], p["b_qkv"]
    kv_dim = GQA_KV * GQA_HD
    q_dim = w_qkv.shape[1] - 2 * kv_dim
    n_heads = q_dim // GQA_HD
    assert kv_dim == LANES and GQA_KV == 2 and q_dim % LANES == 0
    swap = lambda a: jnp.concatenate([a[..., GQA_HD:], a[..., :GQA_HD]], axis=-1)
    wk, wv = w_qkv[:, q_dim:q_dim + kv_dim], w_qkv[:, q_dim + kv_dim:]
    bk, bv = b_qkv[q_dim:q_dim + kv_dim], b_qkv[q_dim + kv_dim:]
    w_ext = jnp.concatenate([w_qkv, swap(wk), swap(wv)], axis=1).astype(BF16)
    b_ext = jnp.concatenate([b_qkv, swap(bk), swap(bv)]).reshape(1, -1)
    n_ext = w_ext.shape[1]
    cos, sin = _rope_tables(rows, GQA_HD, 0, 0)
    q, kv = pl.pallas_call(
        functools.partial(_gqa_proj_kernel, q_dim=q_dim, scale=GQA_HD ** -0.5, half=GQA_HD // 2),
        grid=(rows.tiles,),
        in_specs=[pl.BlockSpec((t, d), lambda i: (i, 0)),
                  pl.BlockSpec((1, 6, d), lambda i: (rows.mod_index(i), 0, 0)),
                  pl.BlockSpec((d, n_ext), lambda i: (0, 0)),
                  pl.BlockSpec((1, n_ext), lambda i: (0, 0)),
                  pl.BlockSpec((1, t, LANES), lambda i: (rows.rope_index(i), 0, 0)),
                  pl.BlockSpec((1, t, LANES), lambda i: (rows.rope_index(i), 0, 0))],
        out_specs=[pl.BlockSpec((t, q_dim), lambda i: (i, 0)),
                   pl.BlockSpec((t, 4 * LANES), lambda i: (i, 0))],
        out_shape=[jax.ShapeDtypeStruct((rows.rows, q_dim), BF16),
                   jax.ShapeDtypeStruct((rows.rows, 4 * LANES), BF16)],
        compiler_params=_params("arbitrary"),
        name="gqa_qkv_proj",
    )(x, mod, w_ext, b_ext, cos, sin)

    sinks = jnp.broadcast_to(p["sinks"].astype(F32)[:, None], (n_heads, LANES))
    b_, s_, n_ctx = rows.batch, rows.seq, rows.n_ctx
    ctx_rows = b_ * n_ctx
    assert ctx_rows % s_ == 0 and s_ >= ATTN_Q + 2 * WINDOW
    q_blocks = s_ // ATTN_Q
    n_pairs = q_dim // LANES
    common = dict(window=WINDOW, seq=s_, n_pairs=n_pairs, pairs_per_kv=n_pairs // GQA_KV)
    o_shape = jax.ShapeDtypeStruct((rows.rows, q_dim), BF16)
    o_lat = pl.pallas_call(
        functools.partial(_gqa_attn_kernel, windowed=True, **common),
        grid=(b_, q_blocks),
        in_specs=[pl.BlockSpec((ATTN_Q, q_dim), lambda b, j: (ctx_rows // ATTN_Q + b * q_blocks + j, 0)),
                  pl.BlockSpec((s_, 4 * LANES), lambda b, j: (ctx_rows // s_ + b, 0)),
                  pl.BlockSpec((n_ctx, 4 * LANES), lambda b, j: (b, 0)),
                  pl.BlockSpec((n_heads, LANES), lambda b, j: (0, 0))],
        out_specs=pl.BlockSpec((ATTN_Q, q_dim), lambda b, j: (ctx_rows // ATTN_Q + b * q_blocks + j, 0)),
        out_shape=o_shape,
        compiler_params=_params("arbitrary", "arbitrary"),
        name="gqa_window_attn",
    )(q, kv, kv, sinks)
    if need_ctx:
        cq_blocks = n_ctx // ATTN_Q
        o = pl.pallas_call(
            functools.partial(_gqa_attn_kernel, windowed=False, **common),
            grid=(b_, cq_blocks),
            in_specs=[pl.BlockSpec((ATTN_Q, q_dim), lambda b, j: (b * cq_blocks + j, 0)),
                      pl.BlockSpec((n_ctx, 4 * LANES), lambda b, j: (b, 0)),
                      pl.BlockSpec((n_heads, LANES), lambda b, j: (0, 0)),
                      pl.BlockSpec(memory_space=pl.ANY)],
            out_specs=pl.BlockSpec((ATTN_Q, q_dim), lambda b, j: (b * cq_blocks + j, 0)),
            out_shape=o_shape,
            input_output_aliases={3: 0},
            compiler_params=_params("arbitrary", "arbitrary"),
            name="gqa_ctx_attn",
        )(q, kv, sinks, o_lat)
    else:
        o = o_lat
    return _out_ln(rows, o, p["w_o"], p["b_o"], x, mod, ln_g, ln_b, alpha=alpha,
                   first_tile=0 if need_ctx else rows.ctx_tiles)


def _mla_proj_kernel(x_ref, mod_ref, wd_ref, qn_ref, kvn_ref, wq_ref, wk_ref, wv_ref, cos_ref, sin_ref,
                     q_ref, k_ref, vt_ref, *, q_lora, kv_lora, scale, n_heads):
    m = mod_ref[0]
    h = (x_ref[...] * (1.0 + m[1:2]) + m[0:1]).astype(BF16)
    p = _bdot(h, wd_ref[...])
    cq, ckv = p[:, :q_lora], p[:, q_lora:q_lora + kv_lora]
    cos, sin = cos_ref[0], sin_ref[0]
    k_rope = _rope_chunk(p[:, q_lora + kv_lora:], cos, sin, QK_ROPE // 2)
    cq = (cq * lax.rsqrt(jnp.mean(cq * cq, axis=-1, keepdims=True) + RMS_EPS) * qn_ref[...]).astype(BF16)
    ckv = (ckv * lax.rsqrt(jnp.mean(ckv * ckv, axis=-1, keepdims=True) + RMS_EPS) * kvn_ref[...]).astype(BF16)
    for hd in range(n_heads):
        cols = slice(hd * LANES, (hd + 1) * LANES)
        qh = _rope_chunk(_bdot(cq, wq_ref[:, cols]), cos, sin, QK_ROPE // 2)
        q_ref[:, cols] = (qh * scale).astype(BF16)
        k_ref[:, cols] = (_bdot(ckv, wk_ref[:, cols]) + k_rope).astype(BF16)
    vt_ref[...] = _bdot(ckv, wv_ref[...]).T.astype(BF16)


def _mla_attn_kernel(*refs, n_heads):
    q_ref, k_ref, vt_ref = refs[:3]
    o_ref, m_scr, l_scr, acc_scr = refs[-4:]
    kt = pl.program_id(2)

    @pl.when(kt == 0)
    def _():
        m_scr[...] = jnp.full_like(m_scr, -jnp.inf)
        l_scr[...] = jnp.zeros_like(l_scr)
        acc_scr[...] = jnp.zeros_like(acc_scr)

    for hd in range(n_heads):
        cols = slice(hd * LANES, (hd + 1) * LANES)
        rws = slice(hd * V_HD, (hd + 1) * V_HD)
        s_t = _bdot_nt(k_ref[:, cols], q_ref[:, cols])
        m_old = m_scr[hd:hd + 1, :]
        m_new = jnp.maximum(m_old, jnp.max(s_t, axis=0, keepdims=True))
        p_t = jnp.exp2(s_t - m_new)
        corr = jnp.exp2(m_old - m_new)
        l_scr[hd:hd + 1, :] = corr * l_scr[hd:hd + 1, :] + jnp.sum(p_t, axis=0, keepdims=True)
        m_scr[hd:hd + 1, :] = m_new
        acc_scr[rws, :] = acc_scr[rws, :] * corr + _bdot(vt_ref[rws, :], p_t)

    @pl.when(kt == pl.num_programs(2) - 1)
    def _():
        for hd in range(n_heads):
            rws = slice(hd * V_HD, (hd + 1) * V_HD)
            acc_scr[rws, :] = acc_scr[rws, :] * (1.0 / l_scr[hd:hd + 1, :])
        o_ref[...] = acc_scr[...].T.astype(BF16)


def _mla_layer(rows, x, mod, p, ln_g, ln_b, alpha, need_ctx):
    t, d = rows.tile, x.shape[1]
    h_ = MLA_HEADS
    w_down, w_uq, w_ukv = p["w_down"], p["w_uq"], p["w_ukv"]
    q_lora = w_uq.shape[0]
    kv_lora = w_ukv.shape[0]
    qk = QK_NOPE + QK_ROPE
    assert QK_NOPE == V_HD == LANES // 2 and h_ % 2 == 0 and q_lora % LANES == 0 and kv_lora % LANES == 0
    pad = LANES - qk
    zc = lambda r, w: jnp.zeros((r, w), F32)
    wd_p = jnp.concatenate([w_down[:, :q_lora + kv_lora], zc(d, QK_NOPE), w_down[:, q_lora + kv_lora:],
                            zc(d, pad)], axis=1).astype(BF16)
    wq_p = jnp.concatenate([w_uq.reshape(q_lora, h_, qk), jnp.zeros((q_lora, h_, pad), F32)],
                           axis=-1).reshape(q_lora, h_ * LANES).astype(BF16)
    ukv = w_ukv.reshape(kv_lora, h_, QK_NOPE + V_HD)
    wk_p = jnp.concatenate([ukv[..., :QK_NOPE], jnp.zeros((kv_lora, h_, LANES - QK_NOPE), F32)],
                           axis=-1).reshape(kv_lora, h_ * LANES).astype(BF16)
    wv_p = ukv[..., QK_NOPE:].reshape(kv_lora, h_ * V_HD).astype(BF16)
    cos, sin = _rope_tables(rows, QK_ROPE, QK_NOPE, pad)
    n_down = wd_p.shape[1]
    q, k, vt = pl.pallas_call(
        functools.partial(_mla_proj_kernel, q_lora=q_lora, kv_lora=kv_lora, scale=qk ** -0.5 * math.log2(math.e),
                          n_heads=h_),
        grid=(rows.tiles,),
        in_specs=[pl.BlockSpec((t, d), lambda i: (i, 0)),
                  pl.BlockSpec((1, 6, d), lambda i: (rows.mod_index(i), 0, 0)),
                  pl.BlockSpec((d, n_down), lambda i: (0, 0)),
                  pl.BlockSpec((1, q_lora), lambda i: (0, 0)),
                  pl.BlockSpec((1, kv_lora), lambda i: (0, 0)),
                  pl.BlockSpec((q_lora, h_ * LANES), lambda i: (0, 0)),
                  pl.BlockSpec((kv_lora, h_ * LANES), lambda i: (0, 0)),
                  pl.BlockSpec((kv_lora, h_ * V_HD), lambda i: (0, 0)),
                  pl.BlockSpec((1, t, LANES), lambda i: (rows.rope_index(i), 0, 0)),
                  pl.BlockSpec((1, t, LANES), lambda i: (rows.rope_index(i), 0, 0))],
        out_specs=[pl.BlockSpec((t, h_ * LANES), lambda i: (i, 0)),
                   pl.BlockSpec((t, h_ * LANES), lambda i: (i, 0)),
                   pl.BlockSpec((h_ * V_HD, t), lambda i: (0, i))],
        out_shape=[jax.ShapeDtypeStruct((rows.rows, h_ * LANES), BF16),
                   jax.ShapeDtypeStruct((rows.rows, h_ * LANES), BF16),
                   jax.ShapeDtypeStruct((h_ * V_HD, rows.rows), BF16)],
        compiler_params=_params("arbitrary"),
        name="mla_proj",
    )(x, mod, wd_p, p["q_norm"].reshape(1, -1), p["kv_norm"].reshape(1, -1), wq_p, wk_p, wv_p, cos, sin)

    b_, s_, n_ctx = rows.batch, rows.seq, rows.n_ctx
    tk = MLA_TK
    tq, tq_ctx = min(MLA_TQ, s_), min(MLA_TQ, n_ctx)
    assert n_ctx % tk == 0 and s_ % tk == 0 and n_ctx % tq_ctx == 0 and s_ % tq == 0
    ck, lk = n_ctx // tk, s_ // tk
    ctx_kblocks = b_ * ck

    def kv_block(b, kt):
        return jnp.where(kt < ck, b * ck + kt, ctx_kblocks + b * lk + (kt - ck))

    def scratch(rows_q):
        return [pltpu.VMEM((h_, rows_q), F32), pltpu.VMEM((h_, rows_q), F32), pltpu.VMEM((h_ * V_HD, rows_q), F32)]

    o_shape = jax.ShapeDtypeStruct((rows.rows, h_ * V_HD), BF16)
    lq = s_ // tq
    ctx_qblocks = b_ * n_ctx // tq
    o_lat = pl.pallas_call(
        functools.partial(_mla_attn_kernel, n_heads=h_),
        grid=(b_, lq, ck + lk),
        in_specs=[pl.BlockSpec((tq, h_ * LANES), lambda b, i, kt: (ctx_qblocks + b * lq + i, 0)),
                  pl.BlockSpec((tk, h_ * LANES), lambda b, i, kt: (kv_block(b, kt), 0)),
                  pl.BlockSpec((h_ * V_HD, tk), lambda b, i, kt: (0, kv_block(b, kt)))],
        out_specs=pl.BlockSpec((tq, h_ * V_HD), lambda b, i, kt: (ctx_qblocks + b * lq + i, 0)),
        out_shape=o_shape,
        scratch_shapes=scratch(tq),
        compiler_params=_params("arbitrary", "arbitrary", "arbitrary"),
        name="mla_attn",
    )(q, k, vt)
    if need_ctx:
        cq = n_ctx // tq_ctx
        o = pl.pallas_call(
            functools.partial(_mla_attn_kernel, n_heads=h_),
            grid=(b_, cq, ck),
            in_specs=[pl.BlockSpec((tq_ctx, h_ * LANES), lambda b, i, kt: (b * cq + i, 0)),
                      pl.BlockSpec((tk, h_ * LANES), lambda b, i, kt: (b * ck + kt, 0)),
                      pl.BlockSpec((h_ * V_HD, tk), lambda b, i, kt: (0, b * ck + kt)),
                      pl.BlockSpec(memory_space=pl.ANY)],
            out_specs=pl.BlockSpec((tq_ctx, h_ * V_HD), lambda b, i, kt: (b * cq + i, 0)),
            out_shape=o_shape,
            scratch_shapes=scratch(tq_ctx),
            input_output_aliases={3: 0},
            compiler_params=_params("arbitrary", "arbitrary", "arbitrary"),
            name="mla_ctx_attn",
        )(q, k, vt, o_lat)
    else:
        o = o_lat
    return _out_ln(rows, o, p["w_o"], jnp.zeros((d,), F32), x, mod, ln_g, ln_b, alpha=alpha,
                   first_tile=0 if need_ctx else rows.ctx_tiles)


def _split_gu_kernel(w_ref, g_ref, u_ref):
    win = 2 * MXU_DIM
    r_i = lax.broadcasted_iota(jnp.int32, (win, MXU_DIM), 0)
    c_i = lax.broadcasted_iota(jnp.int32, (win, MXU_DIM), 1)
    pick_even = (r_i == 2 * c_i).astype(BF16)
    pick_odd = (r_i == 2 * c_i + 1).astype(BF16)
    for j in range(w_ref.shape[2] // win):
        w = w_ref[0, :, j * win:(j + 1) * win].astype(BF16)
        g_ref[0, :, j * MXU_DIM:(j + 1) * MXU_DIM] = _bdot(w, pick_even).astype(BF16)
        u_ref[0, :, j * MXU_DIM:(j + 1) * MXU_DIM] = _bdot(w, pick_odd).astype(BF16)


def _split_gate_up(w_gu):
    depth, n_exp, d, f2 = w_gu.shape
    w = w_gu.reshape(depth * n_exp, d, f2)
    tr = ROW_TILE
    spec_out = pl.BlockSpec((1, tr, f2 // 2), lambda e, r: (e, r, 0))
    shape_out = jax.ShapeDtypeStruct((depth * n_exp, d, f2 // 2), BF16)
    return pl.pallas_call(
        _split_gu_kernel,
        grid=(depth * n_exp, d // tr),
        in_specs=[pl.BlockSpec((1, tr, f2), lambda e, r: (e, r, 0))],
        out_specs=[spec_out, spec_out],
        out_shape=[shape_out, shape_out],
        compiler_params=_params("arbitrary", "arbitrary"),
        name="moe_split_gate_up",
    )(w)


def _router_kernel(x_ref, mod_ref, rw_ref, rb_ref, idx_ref, gate_ref, rank_ref, cnt_ref, cnt_scr, *, top_k):
    @pl.when(pl.program_id(0) == 0)
    def _():
        cnt_scr[...] = jnp.zeros_like(cnt_scr)

    m = mod_ref[0]
    h = x_ref[...] * (1.0 + m[4:5]) + m[3:4]
    logits = jnp.dot(h, rw_ref[...], preferred_element_type=F32, precision=lax.Precision.HIGHEST) + rb_ref[...]
    t, n_exp = logits.shape
    lane = lax.broadcasted_iota(jnp.int32, (t, n_exp), 1).astype(F32)
    work = logits
    sel, val = [], []
    for _ in range(top_k):
        mx = jnp.max(work, axis=1, keepdims=True)
        pick = jnp.min(jnp.where(work == mx, lane, float(n_exp)), axis=1, keepdims=True)
        sel.append(pick)
        val.append(mx)
        work = jnp.where(lane == pick, -jnp.inf, work)
    ex = [jnp.exp(v - val[0]) for v in val]
    tot = ex[0]
    for e in ex[1:]:
        tot = tot + e
    onehot = jnp.zeros((t, n_exp), F32)
    for pick in sel:
        onehot = onehot + (lane == pick).astype(F32)
    r_i = lax.broadcasted_iota(jnp.int32, (t, t), 0)
    c_i = lax.broadcasted_iota(jnp.int32, (t, t), 1)
    before = _bdot((c_i < r_i).astype(BF16), onehot) + cnt_scr[...]
    col = lax.broadcasted_iota(jnp.int32, (t, top_k), 1)
    idx = jnp.zeros((t, top_k), jnp.int32)
    gate = jnp.zeros((t, top_k), F32)
    rank = jnp.zeros((t, top_k), jnp.int32)
    for k in range(top_k):
        rk = jnp.sum(jnp.where(lane == sel[k], before, 0.0), axis=1, keepdims=True).astype(jnp.int32)
        idx = jnp.where(col == k, sel[k].astype(jnp.int32), idx)
        gate = jnp.where(col == k, ex[k] / tot, gate)
        rank = jnp.where(col == k, rk, rank)
    idx_ref[...] = idx
    gate_ref[...] = gate
    rank_ref[...] = rank
    cnt_scr[...] = cnt_scr[...] + jnp.sum(onehot, axis=0, keepdims=True)
    cnt_ref[...] = cnt_scr[...]


def _dispatch_row_copy(h_scr, xs_ref, sem, dest_ref, t, k, top_k):
    d = dest_ref[0, 0, t * top_k + k]
    return pltpu.make_async_copy(h_scr.at[pl.ds(t, 1)], xs_ref.at[pl.ds(d, 1)], sem)


def _pad_row_copy(z_scr, xs_ref, sem, r):
    return pltpu.make_async_copy(z_scr, xs_ref.at[pl.ds(r, 1)], sem)


def _dispatch_kernel(lo_ref, hi_ref, dest_ref, x_ref, mod_ref, xs_ref, h_scr, z_scr, sem, zsem, *, top_k):
    @pl.when(pl.program_id(0) == 0)
    def _():
        z_scr[...] = jnp.zeros_like(z_scr)
        for e in range(lo_ref.shape[0]):
            lax.fori_loop(lo_ref[e], hi_ref[e],
                          lambda r, c: (_pad_row_copy(z_scr, xs_ref, zsem, r).start(), c)[1], 0)
        for e in range(lo_ref.shape[0]):
            lax.fori_loop(lo_ref[e], hi_ref[e],
                          lambda r, c: (_pad_row_copy(z_scr, xs_ref, zsem, r).wait(), c)[1], 0)

    m = mod_ref[0]
    h_scr[...] = x_ref[...] * (1.0 + m[4:5]) + m[3:4]
    t_rows = h_scr.shape[0]

    def start(t, c):
        for k in range(top_k):
            _dispatch_row_copy(h_scr, xs_ref, sem, dest_ref, t, k, top_k).start(priority=k % DMA_QUEUES)
        return c

    def wait(t, c):
        for k in range(top_k):
            _dispatch_row_copy(h_scr, xs_ref, sem, dest_ref, t, k, top_k).wait()
        return c

    lax.fori_loop(0, t_rows, start, 0)
    lax.fori_loop(0, t_rows, wait, 0)


def _expert_kernel(be_ref, nu_ref, x_ref, wg_ref, wu_ref, bg_ref, bu_ref, wd_ref, bd_ref, y_ref):
    del be_ref

    @pl.when(pl.program_id(0) < nu_ref[0])
    def _():
        xb = x_ref[...].astype(BF16)
        g = jnp.minimum(_bdot(xb, wg_ref[0]) + bg_ref[0], SWIGLU_LIMIT)
        u = jnp.clip(_bdot(xb, wu_ref[0]) + bu_ref[0], -SWIGLU_LIMIT, SWIGLU_LIMIT)
        act = (u + 1.0) * (g * jax.nn.sigmoid(SWIGLU_ALPHA * g))
        y_ref[...] = _bdot(act, wd_ref[0]) + bd_ref[0]


def _combine_row_copy(y_ref, ybuf, sem, dest_ref, t, k, top_k):
    d = dest_ref[0, 0, t * top_k + k]
    return pltpu.make_async_copy(y_ref.at[pl.ds(d, 1)], ybuf.at[k, pl.ds(t, 1)], sem)


def _combine_kernel(dest_ref, x_ref, mod_ref, gate_ref, g_ref, bt_ref, y_ref, o_ref, ybuf, sem, *, top_k, alpha):
    t_rows = x_ref.shape[0]

    def start(t, c):
        for k in range(top_k):
            _combine_row_copy(y_ref, ybuf, sem, dest_ref, t, k, top_k).start(priority=k % DMA_QUEUES)
        return c

    def wait(t, c):
        for k in range(top_k):
            _combine_row_copy(y_ref, ybuf, sem, dest_ref, t, k, top_k).wait()
        return c

    lax.fori_loop(0, t_rows, start, 0)
    lax.fori_loop(0, t_rows, wait, 0)
    gate = gate_ref[...]
    acc = gate[:, 0:1] * ybuf[0]
    for k in range(1, top_k):
        acc = acc + gate[:, k:k + 1] * ybuf[k]
    m = mod_ref[0]
    r = alpha * x_ref[...] + m[5:6] * acc
    o_ref[...] = _layer_norm(r, g_ref[...], bt_ref[...])


def _moe_layer(rows, x, mod, router_w, router_b, wg, wu, wd, layer, b_gu, b_down, ln_g, ln_b, alpha, need_ctx):
    t, d = rows.tile, x.shape[1]
    n_exp = router_w.shape[1]
    d_ff = wd.shape[1]
    first = 0 if need_ctx else rows.ctx_tiles
    n_tiles = rows.tiles - first
    n_tok = n_tiles * t
    row = lambda i: (i + first, 0)
    modi = lambda i: (rows.mod_index(i + first), 0, 0)
    tok = lambda i: (i, 0)

    idx, gate, rank, cnt = pl.pallas_call(
        functools.partial(_router_kernel, top_k=TOP_K),
        grid=(n_tiles,),
        in_specs=[pl.BlockSpec((t, d), row), pl.BlockSpec((1, 6, d), modi),
                  pl.BlockSpec((d, n_exp), lambda i: (0, 0)), pl.BlockSpec((1, n_exp), lambda i: (0, 0))],
        out_specs=[pl.BlockSpec((t, TOP_K), tok), pl.BlockSpec((t, TOP_K), tok), pl.BlockSpec((t, TOP_K), tok),
                   pl.BlockSpec((1, n_exp), lambda i: (0, 0))],
        out_shape=[jax.ShapeDtypeStruct((n_tok, TOP_K), jnp.int32), jax.ShapeDtypeStruct((n_tok, TOP_K), F32),
                   jax.ShapeDtypeStruct((n_tok, TOP_K), jnp.int32), jax.ShapeDtypeStruct((1, n_exp), F32)],
        scratch_shapes=[pltpu.VMEM((1, n_exp), F32)],
        compiler_params=_params("arbitrary"),
        name="moe_router",
    )(x, mod, router_w, router_b.reshape(1, n_exp))

    count = cnt[0].astype(jnp.int32)
    padded = (count + MOE_BLOCK - 1) // MOE_BLOCK * MOE_BLOCK
    pad_end = jnp.cumsum(padded)
    pad_start = pad_end - padded
    n_asg = n_tok * TOP_K
    n_blocks = (n_asg + n_exp * (MOE_BLOCK - 1) + MOE_BLOCK - 1) // MOE_BLOCK
    n_slots = n_blocks * MOE_BLOCK
    n_used = (pad_end[-1] // MOE_BLOCK).astype(jnp.int32)
    blk = jnp.minimum(jnp.arange(n_blocks, dtype=jnp.int32), n_used - 1) * MOE_BLOCK
    block_expert = jnp.minimum(jnp.searchsorted(pad_end, blk, side="right"), n_exp - 1).astype(jnp.int32)
    dest = (pad_start[idx] + rank).astype(jnp.int32).reshape(n_tiles, 1, t * TOP_K)

    xs = pl.pallas_call(
        functools.partial(_dispatch_kernel, top_k=TOP_K),
        grid_spec=pltpu.PrefetchScalarGridSpec(
            num_scalar_prefetch=2,
            grid=(n_tiles,),
            in_specs=[pl.BlockSpec((1, 1, t * TOP_K), lambda i, lo, hi: (i, 0, 0), memory_space=pltpu.SMEM),
                      pl.BlockSpec((t, d), lambda i, lo, hi: row(i)),
                      pl.BlockSpec((1, 6, d), lambda i, lo, hi: modi(i))],
            out_specs=pl.BlockSpec(memory_space=pl.ANY),
            scratch_shapes=[pltpu.VMEM((t, d), F32), pltpu.VMEM((1, d), F32),
                            pltpu.SemaphoreType.DMA(()), pltpu.SemaphoreType.DMA(())]),
        out_shape=jax.ShapeDtypeStruct((n_slots, d), F32),
        compiler_params=_params("arbitrary"),
        name="moe_dispatch",
    )((pad_start + count).astype(jnp.int32), pad_end.astype(jnp.int32), dest, x, mod)

    bg = b_gu[:, 0::2].reshape(n_exp, 1, d_ff)
    bu = b_gu[:, 1::2].reshape(n_exp, 1, d_ff)
    used = lambda i, be, nu: (jnp.minimum(i, nu[0] - 1), 0)
    wmap = lambda i, be, nu: (be[i], 0, 0)
    wmap_all = lambda i, be, nu: (be[i] + layer * n_exp, 0, 0)
    y = pl.pallas_call(
        _expert_kernel,
        grid_spec=pltpu.PrefetchScalarGridSpec(
            num_scalar_prefetch=2,
            grid=(n_blocks,),
            in_specs=[pl.BlockSpec((MOE_BLOCK, d), used),
                      pl.BlockSpec((1, d, d_ff), wmap_all), pl.BlockSpec((1, d, d_ff), wmap_all),
                      pl.BlockSpec((1, 1, d_ff), wmap), pl.BlockSpec((1, 1, d_ff), wmap),
                      pl.BlockSpec((1, d_ff, d), wmap_all), pl.BlockSpec((1, 1, d), wmap)],
            out_specs=pl.BlockSpec((MOE_BLOCK, d), used)),
        out_shape=jax.ShapeDtypeStruct((n_slots, d), F32),
        compiler_params=_params("arbitrary"),
        name="moe_experts",
    )(block_expert, n_used.reshape(1), xs, wg, wu, bg, bu, wd, b_down.reshape(n_exp, 1, d))

    return pl.pallas_call(
        functools.partial(_combine_kernel, top_k=TOP_K, alpha=alpha),
        grid_spec=pltpu.PrefetchScalarGridSpec(
            num_scalar_prefetch=0,
            grid=(n_tiles,),
            in_specs=[pl.BlockSpec((1, 1, t * TOP_K), lambda i: (i, 0, 0), memory_space=pltpu.SMEM),
                      pl.BlockSpec((t, d), row), pl.BlockSpec((1, 6, d), modi),
                      pl.BlockSpec((t, TOP_K), tok),
                      pl.BlockSpec((1, d), lambda i: (0, 0)), pl.BlockSpec((1, d), lambda i: (0, 0)),
                      pl.BlockSpec(memory_space=pl.ANY)],
            out_specs=pl.BlockSpec((t, d), row),
            scratch_shapes=[pltpu.VMEM((TOP_K, t, d), F32), pltpu.SemaphoreType.DMA(())]),
        out_shape=jax.ShapeDtypeStruct(x.shape, F32),
        input_output_aliases={1: 0},
        compiler_params=_params("arbitrary"),
        name="moe_combine_ln",
    )(dest, x, mod, gate, ln_g.reshape(1, d), ln_b.reshape(1, d), y)


def kernel(x, c, ctx, c_ctx, ada_w, ada_b, ln1_g, ln1_b, ln2_g, ln2_b, router_w, router_b, exp_gu_w, exp_gu_b, exp_down_w, exp_down_b, rg_w_in, rg_conv_w, rg_conv_b, rg_gate_a_w, rg_gate_a_b, rg_gate_x_w, rg_gate_x_b, rg_lambda, rg_w_out, gqa_w_qkv, gqa_b_qkv, gqa_sinks, gqa_w_o, gqa_b_o, mla_w_down, mla_q_norm, mla_kv_norm, mla_w_uq, mla_w_ukv, mla_w_o):
    batch, seq, d = x.shape
    n_ctx = ctx.shape[1]
    depth = ada_w.shape[0]
    alpha = (2 * depth) ** 0.25
    rows = _Rows(batch, n_ctx, seq, ROW_TILE)
    xa = jnp.concatenate([ctx.reshape(batch * n_ctx, d), x.reshape(batch * seq, d)], axis=0)
    mods = _ada_table(jnp.concatenate([c_ctx[None, :], c], axis=0), ada_w, ada_b)
    wg, wu = _split_gate_up(exp_gu_w)
    wd = exp_down_w.reshape((-1,) + exp_down_w.shape[2:]).astype(BF16)
    for i in range(depth):
        need_ctx = i < depth - 1
        kind, j = i % 3, i // 3
        mod = mods[i]
        if kind == 0:
            prm = dict(w_in=rg_w_in[j], conv_w=rg_conv_w[j], conv_b=rg_conv_b[j], gate_a_w=rg_gate_a_w[j],
                       gate_a_b=rg_gate_a_b[j], gate_x_w=rg_gate_x_w[j], gate_x_b=rg_gate_x_b[j],
                       lam=rg_lambda[j], w_out=rg_w_out[j])
            xa = _rglru_layer(rows, xa, mod, prm, ln1_g[i], ln1_b[i], alpha, need_ctx)
        elif kind == 1:
            prm = dict(w_qkv=gqa_w_qkv[j], b_qkv=gqa_b_qkv[j], sinks=gqa_sinks[j], w_o=gqa_w_o[j], b_o=gqa_b_o[j])
            xa = _gqa_layer(rows, xa, mod, prm, ln1_g[i], ln1_b[i], alpha, need_ctx)
        else:
            prm = dict(w_down=mla_w_down[j], q_norm=mla_q_norm[j], kv_norm=mla_kv_norm[j], w_uq=mla_w_uq[j],
                       w_ukv=mla_w_ukv[j], w_o=mla_w_o[j])
            xa = _mla_layer(rows, xa, mod, prm, ln1_g[i], ln1_b[i], alpha, need_ctx)
        xa = _moe_layer(rows, xa, mod, router_w[i], router_b[i], wg, wu, wd, i, exp_gu_b[i], exp_down_b[i],
                        ln2_g[i], ln2_b[i], alpha, need_ctx)
    return xa[batch * n_ctx:].reshape(batch, seq, d)
```

```python
import functools
import math

import jax
import jax.numpy as jnp
from jax import lax
from jax.experimental import pallas as pl
from jax.experimental.pallas import tpu as pltpu

F32 = jnp.float32
BF16 = jnp.bfloat16

GRID_W = 64
LN_EPS = 1e-5
RMS_EPS = 1e-6
ROPE_THETA = 10000.0
NEG_INF = -1e30
RG_C = 8.0
GQA_KV = 2
GQA_HD = 64
WINDOW = 128
MLA_HEADS = 16
QK_NOPE = 64
QK_ROPE = 32
V_HD = 64
TOP_K = 4
SWIGLU_LIMIT = 7.0
SWIGLU_ALPHA = 1.702
MOE_BLOCK = 256

LANES = 128
SUBLANES = 8
ROW_TILE = 256
ATTN_Q = 128
MLA_TQ = 512
MLA_TK = 256
MXU_DIM = 256
SEG_ALIGN = SUBLANES
SEG_PIECES = (8, 16, 32)
VMEM_LIMIT = 56 * 1024 * 1024


def _params(*sem):
    return pltpu.CompilerParams(dimension_semantics=sem, vmem_limit_bytes=VMEM_LIMIT)


def _bdot(a, b):
    return jnp.dot(a.astype(BF16), b.astype(BF16), preferred_element_type=F32)


def _bdot_nt(a, b):
    return lax.dot_general(a.astype(BF16), b.astype(BF16), (((1,), (1,)), ((), ())),
                           preferred_element_type=F32)


def _layer_norm(r, g, b):
    mu = jnp.mean(r, axis=-1, keepdims=True)
    d = r - mu
    var = jnp.mean(d * d, axis=-1, keepdims=True)
    return d * lax.rsqrt(var + LN_EPS) * g + b


def _gelu_tanh(x):
    return 0.5 * x * (1.0 + jnp.tanh(math.sqrt(2.0 / math.pi) * (x + 0.044715 * (x * x * x))))


class _Rows:
    def __init__(self, batch, n_ctx, seq, tile):
        assert n_ctx % tile == 0 and seq % tile == 0
        self.batch, self.n_ctx, self.seq, self.tile = batch, n_ctx, seq, tile
        self.ctx_tiles = batch * n_ctx // tile
        self.lat_tiles = batch * seq // tile
        self.tiles = self.ctx_tiles + self.lat_tiles
        self.lat_per_batch = seq // tile
        self.ctx_per_batch = n_ctx // tile
        self.rows = batch * (n_ctx + seq)

    def mod_index(self, i):
        return jnp.where(i < self.ctx_tiles, 0, 1 + (i - self.ctx_tiles) // self.lat_per_batch)

    def rope_index(self, i):
        return jnp.where(i < self.ctx_tiles, self.lat_per_batch, (i - self.ctx_tiles) % self.lat_per_batch)


def _ada_kernel(c_ref, w_ref, b_ref, o_ref):
    cv = c_ref[...]
    s = cv * jax.nn.sigmoid(cv)
    o_ref[0] = jnp.dot(s, w_ref[0], preferred_element_type=F32,
                       precision=lax.Precision.HIGHEST) + b_ref[0]


def _ada_table(cvec, ada_w, ada_b):
    depth, d, d6 = ada_w.shape
    n = cvec.shape[0]
    chunk = d
    out = pl.pallas_call(
        _ada_kernel,
        grid=(depth, d6 // chunk),
        in_specs=[pl.BlockSpec((n, d), lambda l, j: (0, 0)),
                  pl.BlockSpec((1, d, chunk), lambda l, j: (l, 0, j)),
                  pl.BlockSpec((1, 1, chunk), lambda l, j: (l, 0, j))],
        out_specs=pl.BlockSpec((1, n, chunk), lambda l, j: (l, 0, j)),
        out_shape=jax.ShapeDtypeStruct((depth, n, d6), F32),
        compiler_params=_params("arbitrary", "arbitrary"),
        name="ada_table",
    )(cvec, ada_w, ada_b.reshape(depth, 1, d6))
    return out.reshape(depth, n, 6, d)


def _out_ln_kernel(z_ref, w_ref, b_ref, x_ref, mod_ref, g_ref, bt_ref, o_ref, *, gate_row, alpha):
    y = _bdot(z_ref[...], w_ref[...]) + b_ref[...]
    m = mod_ref[0]
    r = alpha * x_ref[...] + m[gate_row:gate_row + 1] * y
    o_ref[...] = _layer_norm(r, g_ref[...], bt_ref[...])


def _out_ln(rows, z, w, bias, x, mod, ln_g, ln_b, *, alpha, first_tile=0):
    t, d = rows.tile, x.shape[1]
    kdim = z.shape[1]
    n_tiles = rows.tiles - first_tile
    row = lambda i: (i + first_tile, 0)
    return pl.pallas_call(
        functools.partial(_out_ln_kernel, gate_row=2, alpha=alpha),
        grid=(n_tiles,),
        in_specs=[pl.BlockSpec((t, kdim), row),
                  pl.BlockSpec((kdim, d), lambda i: (0, 0)),
                  pl.BlockSpec((1, d), lambda i: (0, 0)),
                  pl.BlockSpec((t, d), row),
                  pl.BlockSpec((1, 6, d), lambda i: (rows.mod_index(i + first_tile), 0, 0)),
                  pl.BlockSpec((1, d), lambda i: (0, 0)),
                  pl.BlockSpec((1, d), lambda i: (0, 0))],
        out_specs=pl.BlockSpec((t, d), row),
        out_shape=jax.ShapeDtypeStruct(x.shape, F32),
        input_output_aliases={3: 0},
        compiler_params=_params("arbitrary"),
        name="out_proj_ln",
    )(z, w.astype(BF16), bias.reshape(1, d), x, mod, ln_g.reshape(1, d), ln_b.reshape(1, d))


def _rg_in_kernel(x_ref, mod_ref, w_ref, gel_ref, rec_ref, *, d_rnn):
    m = mod_ref[0]
    h = (x_ref[...] * (1.0 + m[1:2]) + m[0:1]).astype(BF16)
    gel_ref[...] = _gelu_tanh(_bdot(h, w_ref[:, :d_rnn])).astype(BF16)
    rec_ref[...] = _bdot(h, w_ref[:, d_rnn:])


def _rg_in(rows, x, mod, w_in):
    t, d = rows.tile, x.shape[1]
    d_rnn = w_in.shape[1] // 2
    return pl.pallas_call(
        functools.partial(_rg_in_kernel, d_rnn=d_rnn),
        grid=(rows.tiles,),
        in_specs=[pl.BlockSpec((t, d), lambda i: (i, 0)),
                  pl.BlockSpec((1, 6, d), lambda i: (rows.mod_index(i), 0, 0)),
                  pl.BlockSpec((d, 2 * d_rnn), lambda i: (0, 0))],
        out_specs=[pl.BlockSpec((t, d_rnn), lambda i: (i, 0)),
                   pl.BlockSpec((t, d_rnn), lambda i: (i, 0))],
        out_shape=[jax.ShapeDtypeStruct((rows.rows, d_rnn), BF16),
                   jax.ShapeDtypeStruct((rows.rows, d_rnn), F32)],
        compiler_params=_params("arbitrary"),
        name="rg_in_proj",
    )(x, mod, w_in.astype(BF16))


def _rg_scan_kernel(*refs, reverse, fuse_out, n_blocks, block_w, tile, ctx_tiles, lat_tiles):
    if fuse_out:
        (x_ref, xp_ref, xn_ref, cw_ref, cb_ref, wa_ref, ba_ref, wx_ref, bx_ref, lam_ref,
         hf_ref, gel_ref, out_ref, a_scr, u_scr, h_scr, carry_scr) = refs
    else:
        (x_ref, xp_ref, xn_ref, cw_ref, cb_ref, wa_ref, ba_ref, wx_ref, bx_ref, lam_ref,
         out_ref, a_scr, u_scr, carry_scr) = refs
        h_scr = out_ref
    j = pl.program_id(1)
    is_ctx = j < ctx_tiles
    n_seq = jnp.where(is_ctx, ctx_tiles, lat_tiles)
    step = jnp.where(is_ctx, j, j - ctx_tiles)
    pos = (n_seq - 1 - step) if reverse else step
    prev_ok = (pos > 0).astype(F32)
    next_ok = (pos < n_seq - 1).astype(F32)

    @pl.when(j == 0)
    def _():
        carry_scr[...] = jnp.zeros_like(carry_scr)

    row = lax.broadcasted_iota(jnp.int32, (tile, block_w), 0)
    for n in range(n_blocks):
        cols = slice(n * block_w, (n + 1) * block_w)
        x = x_ref[:, cols]
        prev = xp_ref[SUBLANES - 1:SUBLANES, cols] * prev_ok
        nxt0 = xn_ref[0:1, cols] * next_ok
        nxt1 = xn_ref[1:2, cols] * next_ok
        x_m1 = jnp.where(row == 0, prev, pltpu.roll(x, 1, 0))
        x_p1 = jnp.where(row == tile - 1, nxt0, pltpu.roll(x, tile - 1, 0))
        x_p2 = jnp.where(row == tile - 2, nxt0, jnp.where(row == tile - 1, nxt1, pltpu.roll(x, tile - 2, 0)))
        xc = (cw_ref[0:1, cols] * x_m1 + cw_ref[1:2, cols] * x + cw_ref[2:3, cols] * x_p1
              + cw_ref[3:4, cols] * x_p2 + cb_ref[:, cols])
        xb = xc.astype(BF16)
        r = jax.nn.sigmoid(_bdot(xb, wa_ref[n]) + ba_ref[:, cols])
        gi = jax.nn.sigmoid(_bdot(xb, wx_ref[n]) + bx_ref[:, cols])
        z = -lam_ref[:, cols]
        softplus = jnp.maximum(z, 0.0) + jnp.log1p(jnp.exp(-jnp.abs(z)))
        log_a = (-RG_C) * r * softplus
        a = jnp.exp(log_a)
        a_scr[:, cols] = a
        u_scr[:, cols] = jnp.sqrt(1.0 - a * a) * (gi * xc)

    width = n_blocks * block_w
    sub = lax.broadcasted_iota(jnp.int32, (SUBLANES, width), 0)
    groups = tile // SUBLANES

    def body(g, carry):
        gg = (groups - 1 - g) if reverse else g
        r0 = pl.multiple_of(gg * SUBLANES, SUBLANES)
        a8 = a_scr[pl.ds(r0, SUBLANES), :]
        u8 = u_scr[pl.ds(r0, SUBLANES), :]
        for s in (1, 2, 4):
            if reverse:
                a_sh, u_sh, ok = pltpu.roll(a8, SUBLANES - s, 0), pltpu.roll(u8, SUBLANES - s, 0), sub < SUBLANES - s
            else:
                a_sh, u_sh, ok = pltpu.roll(a8, s, 0), pltpu.roll(u8, s, 0), sub >= s
            u8 = jnp.where(ok, a8 * u_sh + u8, u8)
            a8 = jnp.where(ok, a8 * a_sh, a8)
        h8 = a8 * carry + u8
        h_scr[pl.ds(r0, SUBLANES), :] = h8
        return h8[0:1, :] if reverse else h8[SUBLANES - 1:SUBLANES, :]

    carry_scr[...] = lax.fori_loop(0, groups, body, carry_scr[...])
    if fuse_out:
        out_ref[...] = ((hf_ref[...] + h_scr[...]) * gel_ref[...].astype(F32)).astype(BF16)


def _rg_scan(rows, rec, conv_w, conv_b, wa, ba, wx, bx, lam, *, reverse, h_fwd=None, gel=None):
    t = rows.tile
    c = rec.shape[1]
    n_blocks, block_w = wa.shape[0], wa.shape[1]
    nc, nl = rows.ctx_per_batch, rows.lat_per_batch
    halo = t // SUBLANES
    last_halo = rows.rows // SUBLANES - 1
    fuse_out = h_fwd is not None

    def tile_index(b, j):
        is_ctx = j < nc
        step = jnp.where(is_ctx, j, j - nc)
        n_seq = jnp.where(is_ctx, nc, nl)
        pos = (n_seq - 1 - step) if reverse else step
        return jnp.where(is_ctx, b * nc + pos, rows.ctx_tiles + b * nl + pos)

    cur = lambda b, j: (tile_index(b, j), 0)
    prv = lambda b, j: (jnp.maximum(tile_index(b, j) * halo - 1, 0), 0)
    nxt = lambda b, j: (jnp.minimum((tile_index(b, j) + 1) * halo, last_halo), 0)
    full2 = lambda b, j: (0, 0)
    full3 = lambda b, j: (0, 0, 0)
    in_specs = [pl.BlockSpec((t, c), cur), pl.BlockSpec((SUBLANES, c), prv), pl.BlockSpec((SUBLANES, c), nxt),
                pl.BlockSpec((4, c), full2), pl.BlockSpec((1, c), full2),
                pl.BlockSpec((n_blocks, block_w, block_w), full3), pl.BlockSpec((1, c), full2),
                pl.BlockSpec((n_blocks, block_w, block_w), full3), pl.BlockSpec((1, c), full2),
                pl.BlockSpec((1, c), full2)]
    args = [rec, rec, rec, conv_w, conv_b.reshape(1, c), wa.astype(BF16), ba.reshape(1, c),
            wx.astype(BF16), bx.reshape(1, c), lam.reshape(1, c)]
    scratch = [pltpu.VMEM((t, c), F32), pltpu.VMEM((t, c), F32)]
    if fuse_out:
        in_specs += [pl.BlockSpec((t, c), cur), pl.BlockSpec((t, c), cur)]
        args += [h_fwd, gel]
        scratch.append(pltpu.VMEM((t, c), F32))
    scratch.append(pltpu.VMEM((1, c), F32))
    return pl.pallas_call(
        functools.partial(_rg_scan_kernel, reverse=reverse, fuse_out=fuse_out, n_blocks=n_blocks,
                          block_w=block_w, tile=t, ctx_tiles=nc, lat_tiles=nl),
        grid=(rows.batch, nc + nl),
        in_specs=in_specs,
        out_specs=pl.BlockSpec((t, c), cur),
        out_shape=jax.ShapeDtypeStruct((rows.rows, c), BF16 if fuse_out else F32),
        scratch_shapes=scratch,
        compiler_params=_params("arbitrary", "arbitrary"),
        name="rg_scan_bwd" if reverse else "rg_scan_fwd",
    )(*args)


def _rglru_layer(rows, x, mod, p, ln_g, ln_b, alpha, need_ctx):
    gel, rec = _rg_in(rows, x, mod, p["w_in"])
    h_fwd = _rg_scan(rows, rec, p["conv_w"], p["conv_b"], p["gate_a_w"][0], p["gate_a_b"][0],
                     p["gate_x_w"][0], p["gate_x_b"][0], p["lam"][0], reverse=False)
    z = _rg_scan(rows, rec, p["conv_w"], p["conv_b"], p["gate_a_w"][1], p["gate_a_b"][1],
                 p["gate_x_w"][1], p["gate_x_b"][1], p["lam"][1], reverse=True, h_fwd=h_fwd, gel=gel)
    d = x.shape[1]
    return _out_ln(rows, z, p["w_out"], jnp.zeros((d,), F32), x, mod, ln_g, ln_b, alpha=alpha,
                   first_tile=0 if need_ctx else rows.ctx_tiles)


def _axial_angles(seq, rot_dim):
    pos = jnp.arange(seq, dtype=jnp.int32)
    row = (pos // GRID_W).astype(F32)
    col = (pos % GRID_W).astype(F32)
    n_freq = rot_dim // 4
    inv_freq = ROPE_THETA ** (-jnp.arange(n_freq, dtype=F32) / n_freq)
    return jnp.concatenate([row[:, None] * inv_freq, col[:, None] * inv_freq], axis=-1)


def _rope_tables(rows, rot_dim, lead, trail):
    ang = _axial_angles(rows.seq, rot_dim)
    cos, sin = jnp.cos(ang), jnp.sin(ang)
    ones = lambda w: jnp.ones((rows.seq, w), F32)
    zeros = lambda w: jnp.zeros((rows.seq, w), F32)
    c = jnp.concatenate([ones(lead), cos, cos, ones(trail)], axis=-1)
    s = jnp.concatenate([zeros(lead), -sin, sin, zeros(trail)], axis=-1)
    reps = LANES // c.shape[1]
    c, s = jnp.tile(c, (1, reps)), jnp.tile(s, (1, reps))
    t = rows.tile
    c = jnp.concatenate([c.reshape(rows.lat_per_batch, t, LANES), jnp.ones((1, t, LANES), F32)], axis=0)
    s = jnp.concatenate([s.reshape(rows.lat_per_batch, t, LANES), jnp.zeros((1, t, LANES), F32)], axis=0)
    return c, s


def _rope_chunk(x, cos, sin, half):
    lane = lax.broadcasted_iota(jnp.int32, x.shape, 1)
    partner = jnp.where((lane % (2 * half)) < half, pltpu.roll(x, LANES - half, 1), pltpu.roll(x, half, 1))
    return x * cos + partner * sin


def _gqa_proj_kernel(x_ref, mod_ref, w_ref, b_ref, cos_ref, sin_ref, q_ref, kv_ref, *, q_dim, scale, half):
    m = mod_ref[0]
    h = (x_ref[...] * (1.0 + m[1:2]) + m[0:1]).astype(BF16)
    cos, sin = cos_ref[0], sin_ref[0]
    n_q = q_dim // LANES
    n_all = w_ref.shape[1] // LANES
    for c in range(n_all):
        cols = slice(c * LANES, (c + 1) * LANES)
        p = _bdot(h, w_ref[:, cols]) + b_ref[:, cols]
        is_v = c in (n_q + 1, n_q + 3)
        if not is_v:
            p = _rope_chunk(p, cos, sin, half)
        if c < n_q:
            q_ref[:, cols] = (p * scale).astype(BF16)
        else:
            kv_ref[:, (c - n_q) * LANES:(c - n_q + 1) * LANES] = p.astype(BF16)


def _gqa_attn_kernel(*refs, windowed, window, seq, n_pairs, pairs_per_kv):
    if windowed:
        q_ref, kv_ref, kvc_ref, sink_ref, o_ref = refs
    else:
        q_ref, kvc_ref, sink_ref, _, o_ref = refs
    tq = q_ref.shape[0]
    kvc = kvc_ref[...]
    if windowed:
        span = tq + 2 * window
        qs = pl.program_id(1) * tq
        ws = pl.multiple_of(jnp.clip(qs - window, 0, seq - span), LANES)
        kv = jnp.concatenate([kv_ref[pl.ds(ws, span), :], kvc], axis=0)
        n_keys = kv.shape[0]
        qpos = qs + lax.broadcasted_iota(jnp.int32, (tq, n_keys), 0)
        col = lax.broadcasted_iota(jnp.int32, (tq, n_keys), 1)
        mask = (jnp.abs(ws + col - qpos) <= window) | (col >= span)
    else:
        kv = kvc
        mask = None
    lane = lax.broadcasted_iota(jnp.int32, (kv.shape[0], LANES), 1)
    low = lane < GQA_HD
    zero = jnp.zeros((kv.shape[0], LANES), BF16)
    k_plain, v_plain = kv[:, 0:LANES], kv[:, LANES:2 * LANES]
    k_swap, v_swap = kv[:, 2 * LANES:3 * LANES], kv[:, 3 * LANES:4 * LANES]
    for pr in range(n_pairs):
        g = pr // pairs_per_kv
        k_lo = jnp.where(low, k_plain if g == 0 else k_swap, zero)
        k_hi = jnp.where(low, zero, k_swap if g == 0 else k_plain)
        v_lo = jnp.where(low, v_plain if g == 0 else v_swap, zero)
        v_hi = jnp.where(low, zero, v_swap if g == 0 else v_plain)
        q2 = q_ref[:, pr * LANES:(pr + 1) * LANES]
        acc = jnp.zeros((tq, LANES), F32)
        for hh, (kx, vx) in enumerate(((k_lo, v_lo), (k_hi, v_hi))):
            head = 2 * pr + hh
            s = _bdot_nt(q2, kx)
            if mask is not None:
                s = jnp.where(mask, s, NEG_INF)
            sk = sink_ref[head:head + 1, 0:1]
            mx = jnp.maximum(jnp.max(s, axis=1, keepdims=True), sk)
            p = jnp.exp(s - mx)
            denom = jnp.sum(p, axis=1, keepdims=True) + jnp.exp(sk - mx)
            acc = acc + _bdot(p * (1.0 / denom), vx)
        o_ref[:, pr * LANES:(pr + 1) * LANES] = acc.astype(BF16)


def _gqa_layer(rows, x, mod, p, ln_g, ln_b, alpha, need_ctx):
    t, d = rows.tile, x.shape[1]
    w_qkv, b_qkv = p["w_qkv"], p["b_qkv"]
    kv_dim = GQA_KV * GQA_HD
    q_dim = w_qkv.shape[1] - 2 * kv_dim
    n_heads = q_dim // GQA_HD
    assert kv_dim == LANES and GQA_KV == 2 and q_dim % LANES == 0
    swap = lambda a: jnp.concatenate([a[..., GQA_HD:], a[..., :GQA_HD]], axis=-1)
    wk, wv = w_qkv[:, q_dim:q_dim + kv_dim], w_qkv[:, q_dim + kv_dim:]
    bk, bv = b_qkv[q_dim:q_dim + kv_dim], b_qkv[q_dim + kv_dim:]
    w_ext = jnp.concatenate([w_qkv, swap(wk), swap(wv)], axis=1).astype(BF16)
    b_ext = jnp.concatenate([b_qkv, swap(bk), swap(bv)]).reshape(1, -1)
    n_ext = w_ext.shape[1]
    cos, sin = _rope_tables(rows, GQA_HD, 0, 0)
    q, kv = pl.pallas_call(
        functools.partial(_gqa_proj_kernel, q_dim=q_dim, scale=GQA_HD ** -0.5, half=GQA_HD // 2),
        grid=(rows.tiles,),
        in_specs=[pl.BlockSpec((t, d), lambda i: (i, 0)),
                  pl.BlockSpec((1, 6, d), lambda i: (rows.mod_index(i), 0, 0)),
                  pl.BlockSpec((d, n_ext), lambda i: (0, 0)),
                  pl.BlockSpec((1, n_ext), lambda i: (0, 0)),
                  pl.BlockSpec((1, t, LANES), lambda i: (rows.rope_index(i), 0, 0)),
                  pl.BlockSpec((1, t, LANES), lambda i: (rows.rope_index(i), 0, 0))],
        out_specs=[pl.BlockSpec((t, q_dim), lambda i: (i, 0)),
                   pl.BlockSpec((t, 4 * LANES), lambda i: (i, 0))],
        out_shape=[jax.ShapeDtypeStruct((rows.rows, q_dim), BF16),
                   jax.ShapeDtypeStruct((rows.rows, 4 * LANES), BF16)],
        compiler_params=_params("arbitrary"),
        name="gqa_qkv_proj",
    )(x, mod, w_ext, b_ext, cos, sin)

    sinks = jnp.broadcast_to(p["sinks"].astype(F32)[:, None], (n_heads, LANES))
    b_, s_, n_ctx = rows.batch, rows.seq, rows.n_ctx
    ctx_rows = b_ * n_ctx
    assert ctx_rows % s_ == 0 and s_ >= ATTN_Q + 2 * WINDOW
    q_blocks = s_ // ATTN_Q
    n_pairs = q_dim // LANES
    common = dict(window=WINDOW, seq=s_, n_pairs=n_pairs, pairs_per_kv=n_pairs // GQA_KV)
    o_shape = jax.ShapeDtypeStruct((rows.rows, q_dim), BF16)
    o_lat = pl.pallas_call(
        functools.partial(_gqa_attn_kernel, windowed=True, **common),
        grid=(b_, q_blocks),
        in_specs=[pl.BlockSpec((ATTN_Q, q_dim), lambda b, j: (ctx_rows // ATTN_Q + b * q_blocks + j, 0)),
                  pl.BlockSpec((s_, 4 * LANES), lambda b, j: (ctx_rows // s_ + b, 0)),
                  pl.BlockSpec((n_ctx, 4 * LANES), lambda b, j: (b, 0)),
                  pl.BlockSpec((n_heads, LANES), lambda b, j: (0, 0))],
        out_specs=pl.BlockSpec((ATTN_Q, q_dim), lambda b, j: (ctx_rows // ATTN_Q + b * q_blocks + j, 0)),
        out_shape=o_shape,
        compiler_params=_params("arbitrary", "arbitrary"),
        name="gqa_window_attn",
    )(q, kv, kv, sinks)
    if need_ctx:
        cq_blocks = n_ctx // ATTN_Q
        o = pl.pallas_call(
            functools.partial(_gqa_attn_kernel, windowed=False, **common),
            grid=(b_, cq_blocks),
            in_specs=[pl.BlockSpec((ATTN_Q, q_dim), lambda b, j: (b * cq_blocks + j, 0)),
                      pl.BlockSpec((n_ctx, 4 * LANES), lambda b, j: (b, 0)),
                      pl.BlockSpec((n_heads, LANES), lambda b, j: (0, 0)),
                      pl.BlockSpec(memory_space=pl.ANY)],
            out_specs=pl.BlockSpec((ATTN_Q, q_dim), lambda b, j: (b * cq_blocks + j, 0)),
            out_shape=o_shape,
            input_output_aliases={3: 0},
            compiler_params=_params("arbitrary", "arbitrary"),
            name="gqa_ctx_attn",
        )(q, kv, sinks, o_lat)
    else:
        o = o_lat
    return _out_ln(rows, o, p["w_o"], p["b_o"], x, mod, ln_g, ln_b, alpha=alpha,
                   first_tile=0 if need_ctx else rows.ctx_tiles)


def _mla_proj_kernel(x_ref, mod_ref, wd_ref, qn_ref, kvn_ref, wq_ref, wk_ref, wv_ref, cos_ref, sin_ref,
                     q_ref, k_ref, vt_ref, *, q_lora, kv_lora, scale, n_heads):
    m = mod_ref[0]
    h = (x_ref[...] * (1.0 + m[1:2]) + m[0:1]).astype(BF16)
    p = _bdot(h, wd_ref[...])
    cq, ckv = p[:, :q_lora], p[:, q_lora:q_lora + kv_lora]
    cos, sin = cos_ref[0], sin_ref[0]
    k_rope = _rope_chunk(p[:, q_lora + kv_lora:], cos, sin, QK_ROPE // 2)
    cq = (cq * lax.rsqrt(jnp.mean(cq * cq, axis=-1, keepdims=True) + RMS_EPS) * qn_ref[...]).astype(BF16)
    ckv = (ckv * lax.rsqrt(jnp.mean(ckv * ckv, axis=-1, keepdims=True) + RMS_EPS) * kvn_ref[...]).astype(BF16)
    for hd in range(n_heads):
        cols = slice(hd * LANES, (hd + 1) * LANES)
        qh = _rope_chunk(_bdot(cq, wq_ref[:, cols]), cos, sin, QK_ROPE // 2)
        q_ref[:, cols] = (qh * scale).astype(BF16)
        k_ref[:, cols] = (_bdot(ckv, wk_ref[:, cols]) + k_rope).astype(BF16)
    vt_ref[...] = _bdot(ckv, wv_ref[...]).T.astype(BF16)


def _mla_attn_kernel(*refs, n_heads):
    q_ref, k_ref, vt_ref = refs[:3]
    o_ref, m_scr, l_scr, acc_scr = refs[-4:]
    kt = pl.program_id(2)

    @pl.when(kt == 0)
    def _():
        m_scr[...] = jnp.full_like(m_scr, -jnp.inf)
        l_scr[...] = jnp.zeros_like(l_scr)
        acc_scr[...] = jnp.zeros_like(acc_scr)

    for hd in range(n_heads):
        cols = slice(hd * LANES, (hd + 1) * LANES)
        rws = slice(hd * V_HD, (hd + 1) * V_HD)
        s_t = _bdot_nt(k_ref[:, cols], q_ref[:, cols])
        m_old = m_scr[hd:hd + 1, :]
        m_new = jnp.maximum(m_old, jnp.max(s_t, axis=0, keepdims=True))
        p_t = jnp.exp2(s_t - m_new)
        corr = jnp.exp2(m_old - m_new)
        l_scr[hd:hd + 1, :] = corr * l_scr[hd:hd + 1, :] + jnp.sum(p_t, axis=0, keepdims=True)
        m_scr[hd:hd + 1, :] = m_new
        acc_scr[rws, :] = acc_scr[rws, :] * corr + _bdot(vt_ref[rws, :], p_t)

    @pl.when(kt == pl.num_programs(2) - 1)
    def _():
        for hd in range(n_heads):
            rws = slice(hd * V_HD, (hd + 1) * V_HD)
            acc_scr[rws, :] = acc_scr[rws, :] * (1.0 / l_scr[hd:hd + 1, :])
        o_ref[...] = acc_scr[...].T.astype(BF16)


def _mla_layer(rows, x, mod, p, ln_g, ln_b, alpha, need_ctx):
    t, d = rows.tile, x.shape[1]
    h_ = MLA_HEADS
    w_down, w_uq, w_ukv = p["w_down"], p["w_uq"], p["w_ukv"]
    q_lora = w_uq.shape[0]
    kv_lora = w_ukv.shape[0]
    qk = QK_NOPE + QK_ROPE
    assert QK_NOPE == V_HD == LANES // 2 and h_ % 2 == 0 and q_lora % LANES == 0 and kv_lora % LANES == 0
    pad = LANES - qk
    zc = lambda r, w: jnp.zeros((r, w), F32)
    wd_p = jnp.concatenate([w_down[:, :q_lora + kv_lora], zc(d, QK_NOPE), w_down[:, q_lora + kv_lora:],
                            zc(d, pad)], axis=1).astype(BF16)
    wq_p = jnp.concatenate([w_uq.reshape(q_lora, h_, qk), jnp.zeros((q_lora, h_, pad), F32)],
                           axis=-1).reshape(q_lora, h_ * LANES).astype(BF16)
    ukv = w_ukv.reshape(kv_lora, h_, QK_NOPE + V_HD)
    wk_p = jnp.concatenate([ukv[..., :QK_NOPE], jnp.zeros((kv_lora, h_, LANES - QK_NOPE), F32)],
                           axis=-1).reshape(kv_lora, h_ * LANES).astype(BF16)
    wv_p = ukv[..., QK_NOPE:].reshape(kv_lora, h_ * V_HD).astype(BF16)
    cos, sin = _rope_tables(rows, QK_ROPE, QK_NOPE, pad)
    n_down = wd_p.shape[1]
    q, k, vt = pl.pallas_call(
        functools.partial(_mla_proj_kernel, q_lora=q_lora, kv_lora=kv_lora, scale=qk ** -0.5 * math.log2(math.e),
                          n_heads=h_),
        grid=(rows.tiles,),
        in_specs=[pl.BlockSpec((t, d), lambda i: (i, 0)),
                  pl.BlockSpec((1, 6, d), lambda i: (rows.mod_index(i), 0, 0)),
                  pl.BlockSpec((d, n_down), lambda i: (0, 0)),
                  pl.BlockSpec((1, q_lora), lambda i: (0, 0)),
                  pl.BlockSpec((1, kv_lora), lambda i: (0, 0)),
                  pl.BlockSpec((q_lora, h_ * LANES), lambda i: (0, 0)),
                  pl.BlockSpec((kv_lora, h_ * LANES), lambda i: (0, 0)),
                  pl.BlockSpec((kv_lora, h_ * V_HD), lambda i: (0, 0)),
                  pl.BlockSpec((1, t, LANES), lambda i: (rows.rope_index(i), 0, 0)),
                  pl.BlockSpec((1, t, LANES), lambda i: (rows.rope_index(i), 0, 0))],
        out_specs=[pl.BlockSpec((t, h_ * LANES), lambda i: (i, 0)),
                   pl.BlockSpec((t, h_ * LANES), lambda i: (i, 0)),
                   pl.BlockSpec((h_ * V_HD, t), lambda i: (0, i))],
        out_shape=[jax.ShapeDtypeStruct((rows.rows, h_ * LANES), BF16),
                   jax.ShapeDtypeStruct((rows.rows, h_ * LANES), BF16),
                   jax.ShapeDtypeStruct((h_ * V_HD, rows.rows), BF16)],
        compiler_params=_params("arbitrary"),
        name="mla_proj",
    )(x, mod, wd_p, p["q_norm"].reshape(1, -1), p["kv_norm"].reshape(1, -1), wq_p, wk_p, wv_p, cos, sin)

    b_, s_, n_ctx = rows.batch, rows.seq, rows.n_ctx
    tk = MLA_TK
    tq, tq_ctx = min(MLA_TQ, s_), min(MLA_TQ, n_ctx)
    assert n_ctx % tk == 0 and s_ % tk == 0 and n_ctx % tq_ctx == 0 and s_ % tq == 0
    ck, lk = n_ctx // tk, s_ // tk
    ctx_kblocks = b_ * ck

    def kv_block(b, kt):
        return jnp.where(kt < ck, b * ck + kt, ctx_kblocks + b * lk + (kt - ck))

    def scratch(rows_q):
        return [pltpu.VMEM((h_, rows_q), F32), pltpu.VMEM((h_, rows_q), F32), pltpu.VMEM((h_ * V_HD, rows_q), F32)]

    o_shape = jax.ShapeDtypeStruct((rows.rows, h_ * V_HD), BF16)
    lq = s_ // tq
    ctx_qblocks = b_ * n_ctx // tq
    o_lat = pl.pallas_call(
        functools.partial(_mla_attn_kernel, n_heads=h_),
        grid=(b_, lq, ck + lk),
        in_specs=[pl.BlockSpec((tq, h_ * LANES), lambda b, i, kt: (ctx_qblocks + b * lq + i, 0)),
                  pl.BlockSpec((tk, h_ * LANES), lambda b, i, kt: (kv_block(b, kt), 0)),
                  pl.BlockSpec((h_ * V_HD, tk), lambda b, i, kt: (0, kv_block(b, kt)))],
        out_specs=pl.BlockSpec((tq, h_ * V_HD), lambda b, i, kt: (ctx_qblocks + b * lq + i, 0)),
        out_shape=o_shape,
        scratch_shapes=scratch(tq),
        compiler_params=_params("arbitrary", "arbitrary", "arbitrary"),
        name="mla_attn",
    )(q, k, vt)
    if need_ctx:
        cq = n_ctx // tq_ctx
        o = pl.pallas_call(
            functools.partial(_mla_attn_kernel, n_heads=h_),
            grid=(b_, cq, ck),
            in_specs=[pl.BlockSpec((tq_ctx, h_ * LANES), lambda b, i, kt: (b * cq + i, 0)),
                      pl.BlockSpec((tk, h_ * LANES), lambda b, i, kt: (b * ck + kt, 0)),
                      pl.BlockSpec((h_ * V_HD, tk), lambda b, i, kt: (0, b * ck + kt)),
                      pl.BlockSpec(memory_space=pl.ANY)],
            out_specs=pl.BlockSpec((tq_ctx, h_ * V_HD), lambda b, i, kt: (b * cq + i, 0)),
            out_shape=o_shape,
            scratch_shapes=scratch(tq_ctx),
            input_output_aliases={3: 0},
            compiler_params=_params("arbitrary", "arbitrary", "arbitrary"),
            name="mla_ctx_attn",
        )(q, k, vt, o_lat)
    else:
        o = o_lat
    return _out_ln(rows, o, p["w_o"], jnp.zeros((d,), F32), x, mod, ln_g, ln_b, alpha=alpha,
                   first_tile=0 if need_ctx else rows.ctx_tiles)


def _split_gu_kernel(w_ref, g_ref, u_ref):
    win = 2 * MXU_DIM
    r_i = lax.broadcasted_iota(jnp.int32, (win, MXU_DIM), 0)
    c_i = lax.broadcasted_iota(jnp.int32, (win, MXU_DIM), 1)
    pick_even = (r_i == 2 * c_i).astype(BF16)
    pick_odd = (r_i == 2 * c_i + 1).astype(BF16)
    for j in range(w_ref.shape[2] // win):
        w = w_ref[0, :, j * win:(j + 1) * win].astype(BF16)
        g_ref[0, :, j * MXU_DIM:(j + 1) * MXU_DIM] = _bdot(w, pick_even).astype(BF16)
        u_ref[0, :, j * MXU_DIM:(j + 1) * MXU_DIM] = _bdot(w, pick_odd).astype(BF16)


def _split_gate_up(w_gu):
    depth, n_exp, d, f2 = w_gu.shape
    w = w_gu.reshape(depth * n_exp, d, f2)
    tr = ROW_TILE
    spec_out = pl.BlockSpec((1, tr, f2 // 2), lambda e, r: (e, r, 0))
    shape_out = jax.ShapeDtypeStruct((depth * n_exp, d, f2 // 2), BF16)
    return pl.pallas_call(
        _split_gu_kernel,
        grid=(depth * n_exp, d // tr),
        in_specs=[pl.BlockSpec((1, tr, f2), lambda e, r: (e, r, 0))],
        out_specs=[spec_out, spec_out],
        out_shape=[shape_out, shape_out],
        compiler_params=_params("arbitrary", "arbitrary"),
        name="moe_split_gate_up",
    )(w)


def _seg_rows(t, n_exp):
    return t * TOP_K + n_exp * SEG_ALIGN


def _router_kernel(x_ref, mod_ref, rw_ref, rb_ref, slot_ref, slot_t_ref, gate_ref, cnt_ref, *, top_k):
    m = mod_ref[0]
    h = x_ref[...] * (1.0 + m[4:5]) + m[3:4]
    logits = jnp.dot(h, rw_ref[...], preferred_element_type=F32, precision=lax.Precision.HIGHEST) + rb_ref[...]
    t, n_exp = logits.shape
    lane = lax.broadcasted_iota(jnp.int32, (t, n_exp), 1).astype(F32)
    work = logits
    sel, val = [], []
    for _ in range(top_k):
        mx = jnp.max(work, axis=1, keepdims=True)
        pick = jnp.min(jnp.where(work == mx, lane, float(n_exp)), axis=1, keepdims=True)
        sel.append(pick)
        val.append(mx)
        work = jnp.where(lane == pick, -jnp.inf, work)
    ex = [jnp.exp(v - val[0]) for v in val]
    tot = ex[0]
    for e in ex[1:]:
        tot = tot + e
    onehot = jnp.zeros((t, n_exp), F32)
    for pick in sel:
        onehot = onehot + (lane == pick).astype(F32)
    r_i = lax.broadcasted_iota(jnp.int32, (t, t), 0)
    c_i = lax.broadcasted_iota(jnp.int32, (t, t), 1)
    before = _bdot((c_i < r_i).astype(BF16), onehot)
    count = jnp.sum(onehot, axis=0, keepdims=True)
    cap = jnp.floor((count + (SEG_ALIGN - 1)) * (1.0 / SEG_ALIGN)) * SEG_ALIGN
    e_r = lax.broadcasted_iota(jnp.int32, (n_exp, n_exp), 0)
    e_c = lax.broadcasted_iota(jnp.int32, (n_exp, n_exp), 1)
    seg_start = _bdot(cap, (e_r < e_c).astype(BF16))
    base = before + seg_start
    col = lax.broadcasted_iota(jnp.int32, (t, top_k), 1)
    wide_lane = lax.broadcasted_iota(jnp.int32, (t, LANES), 1)
    slot = jnp.zeros((t, top_k), jnp.int32)
    gate = jnp.zeros((t, top_k), F32)
    wide = jnp.zeros((t, LANES), F32)
    for k in range(top_k):
        sk = jnp.sum(jnp.where(lane == sel[k], base, 0.0), axis=1, keepdims=True)
        slot = jnp.where(col == k, sk.astype(jnp.int32), slot)
        gate = jnp.where(col == k, ex[k] / tot, gate)
        wide = jnp.where(wide_lane == k, sk, wide)
    slot_ref[...] = slot
    gate_ref[...] = gate
    slot_t_ref[...] = wide.T[:SUBLANES, :]
    cnt_ref[0] = count


def _segment_copies(seg_ref, n_exp, make_copy, act):
    *small, big = SEG_PIECES
    for e in range(n_exp):
        g0, rows, l0 = seg_ref[0, 0, e], seg_ref[0, 0, n_exp + e], seg_ref[0, 0, 2 * n_exp + e]

        def whole(j, carry, g0=g0, l0=l0):
            act(make_copy(pl.multiple_of(l0 + j * big, SEG_ALIGN), pl.multiple_of(g0 + j * big, SEG_ALIGN), big))
            return carry

        lax.fori_loop(0, lax.shift_right_logical(rows, big.bit_length() - 1), whole, 0)
        for size in small:
            done = rows & (-2 * size)

            @pl.when((rows & size) != 0)
            def _(g0=g0, l0=l0, done=done, size=size):
                act(make_copy(pl.multiple_of(l0 + done, SEG_ALIGN), pl.multiple_of(g0 + done, SEG_ALIGN), size))


def _dispatch_kernel(tail_ref, seg_ref, seg_prev_ref, x_ref, mod_ref, slot_t_ref, xs_ref, buf, zbuf, sem, zsem, *,
                     n_exp, top_k):
    i = pl.program_id(0)
    cur = i % 2

    def to_slots(buf_slot, sem_slot):
        return lambda l, g, rows: pltpu.make_async_copy(buf.at[buf_slot, pl.ds(l, rows)], xs_ref.at[pl.ds(g, rows)],
                                                        sem.at[sem_slot])

    @pl.when(i == 0)
    def _():
        zbuf[...] = jnp.zeros_like(zbuf)
        zero_fill = lambda l, g, rows: pltpu.make_async_copy(zbuf.at[pl.ds(l, rows)], xs_ref.at[pl.ds(g, rows)], zsem)
        _segment_copies(tail_ref, n_exp, zero_fill, lambda c: c.start())
        _segment_copies(tail_ref, n_exp, zero_fill, lambda c: c.wait())

    m = mod_ref[0]
    h = (x_ref[...] * (1.0 + m[4:5]) + m[3:4]).astype(BF16)
    n_rows, t = buf.shape[1], x_ref.shape[0]
    row = lax.broadcasted_iota(jnp.int32, (n_rows, t), 0)
    slot_t = slot_t_ref[...].astype(jnp.int32)
    pick = row == slot_t[0:1, :]
    for k in range(1, top_k):
        pick = pick | (row == slot_t[k:k + 1, :])
    buf[cur] = _bdot(pick.astype(BF16), h)
    _segment_copies(seg_ref, n_exp, to_slots(cur, cur), lambda c: c.start())

    @pl.when(i > 0)
    def _():
        _segment_copies(seg_prev_ref, n_exp, to_slots(1 - cur, 1 - cur), lambda c: c.wait())

    @pl.when(i == pl.num_programs(0) - 1)
    def _():
        _segment_copies(seg_ref, n_exp, to_slots(cur, cur), lambda c: c.wait())


def _expert_kernel(be_ref, nu_ref, x_ref, wg_ref, wu_ref, bg_ref, bu_ref, wd_ref, bd_ref, y_ref):
    del be_ref

    @pl.when(pl.program_id(0) < nu_ref[0])
    def _():
        xb = x_ref[...].astype(BF16)
        g = jnp.minimum(_bdot(xb, wg_ref[0]) + bg_ref[0], SWIGLU_LIMIT)
        u = jnp.clip(_bdot(xb, wu_ref[0]) + bu_ref[0], -SWIGLU_LIMIT, SWIGLU_LIMIT)
        act = (u + 1.0) * (g * jax.nn.sigmoid(SWIGLU_ALPHA * g))
        y_ref[...] = _bdot(act, wd_ref[0]) + bd_ref[0]


def _combine_kernel(seg_ref, seg_next_ref, x_ref, mod_ref, gate_ref, slot_ref, g_ref, bt_ref, y_ref, o_ref,
                    ybuf, sem, *, n_exp, top_k, alpha):
    i = pl.program_id(0)
    cur = i % 2

    def from_slots(buf_slot):
        return lambda l, g, rows: pltpu.make_async_copy(y_ref.at[pl.ds(g, rows)], ybuf.at[buf_slot, pl.ds(l, rows)],
                                                        sem.at[buf_slot])

    @pl.when(i == 0)
    def _():
        ybuf[...] = jnp.zeros_like(ybuf)
        _segment_copies(seg_ref, n_exp, from_slots(0), lambda c: c.start())

    @pl.when(i < pl.num_programs(0) - 1)
    def _():
        _segment_copies(seg_next_ref, n_exp, from_slots(1 - cur), lambda c: c.start())

    _segment_copies(seg_ref, n_exp, from_slots(cur), lambda c: c.wait())
    t, n_rows = x_ref.shape[0], ybuf.shape[1]
    lane = lax.broadcasted_iota(jnp.int32, (t, n_rows), 1)
    gate, slot = gate_ref[...], slot_ref[...]
    weights = jnp.zeros((t, n_rows), F32)
    for k in range(top_k):
        weights = jnp.where(lane == slot[:, k:k + 1], gate[:, k:k + 1], weights)
    acc = _bdot(weights, ybuf[cur])
    m = mod_ref[0]
    r = alpha * x_ref[...] + m[5:6] * acc
    o_ref[...] = _layer_norm(r, g_ref[...], bt_ref[...])


def _moe_layer(rows, x, mod, router_w, router_b, wg, wu, wd, layer, b_gu, b_down, ln_g, ln_b, alpha, need_ctx):
    t, d = rows.tile, x.shape[1]
    n_exp = router_w.shape[1]
    d_ff = wd.shape[1]
    first = 0 if need_ctx else rows.ctx_tiles
    n_tiles = rows.tiles - first
    n_tok = n_tiles * t
    row = lambda i: (i + first, 0)
    modi = lambda i: (rows.mod_index(i + first), 0, 0)
    tok = lambda i: (i, 0)

    seg_rows = _seg_rows(t, n_exp)
    assert 4 * n_exp <= LANES and t % SEG_PIECES[-1] == 0 and MOE_BLOCK % SEG_PIECES[-1] == 0
    slot, slot_t, gate, cnt = pl.pallas_call(
        functools.partial(_router_kernel, top_k=TOP_K),
        grid=(n_tiles,),
        in_specs=[pl.BlockSpec((t, d), row), pl.BlockSpec((1, 6, d), modi),
                  pl.BlockSpec((d, n_exp), lambda i: (0, 0)), pl.BlockSpec((1, n_exp), lambda i: (0, 0))],
        out_specs=[pl.BlockSpec((t, TOP_K), tok), pl.BlockSpec((SUBLANES, t), tok), pl.BlockSpec((t, TOP_K), tok),
                   pl.BlockSpec((1, 1, n_exp), lambda i: (i, 0, 0))],
        out_shape=[jax.ShapeDtypeStruct((n_tok, TOP_K), jnp.int32), jax.ShapeDtypeStruct((n_tiles * SUBLANES, t), F32),
                   jax.ShapeDtypeStruct((n_tok, TOP_K), F32), jax.ShapeDtypeStruct((n_tiles, 1, n_exp), F32)],
        compiler_params=_params("arbitrary"),
        name="moe_router",
    )(x, mod, router_w, router_b.reshape(1, n_exp))

    i32 = jnp.int32
    count = cnt.reshape(n_tiles, n_exp).astype(i32)
    cap = (count + SEG_ALIGN - 1) // SEG_ALIGN * SEG_ALIGN
    e_rows = jnp.sum(cap, axis=0)
    e_pad = (e_rows + MOE_BLOCK - 1) // MOE_BLOCK * MOE_BLOCK
    e_end = jnp.cumsum(e_pad)
    e_start = e_end - e_pad
    seg_global = e_start[None, :] + jnp.cumsum(cap, axis=0) - cap
    seg_local = jnp.cumsum(cap, axis=1) - cap
    fill = jnp.zeros((n_tiles, LANES - 3 * n_exp), i32)
    seg = jnp.concatenate([seg_global, cap, seg_local, fill], axis=1).astype(i32).reshape(n_tiles, 1, LANES)
    tail = jnp.concatenate([e_start + e_rows, e_pad - e_rows, jnp.zeros((LANES - 2 * n_exp,), i32)])
    tail = tail.astype(i32).reshape(1, 1, LANES)
    max_slots = n_tok * TOP_K + n_tiles * n_exp * (SEG_ALIGN - 1) + n_exp * (MOE_BLOCK - 1)
    n_blocks = (max_slots + MOE_BLOCK - 1) // MOE_BLOCK
    n_slots = n_blocks * MOE_BLOCK
    n_used = (e_end[-1] // MOE_BLOCK).astype(i32)
    blk = jnp.minimum(jnp.arange(n_blocks, dtype=i32), n_used - 1) * MOE_BLOCK
    block_expert = jnp.minimum(jnp.searchsorted(e_end, blk, side="right"), n_exp - 1).astype(i32)

    smem_tile = lambda index_map: pl.BlockSpec((1, 1, LANES), index_map, memory_space=pltpu.SMEM)
    xs = pl.pallas_call(
        functools.partial(_dispatch_kernel, n_exp=n_exp, top_k=TOP_K),
        grid=(n_tiles,),
        in_specs=[smem_tile(lambda i: (0, 0, 0)), smem_tile(lambda i: (i, 0, 0)),
                  smem_tile(lambda i: (jnp.maximum(i - 1, 0), 0, 0)),
                  pl.BlockSpec((t, d), row), pl.BlockSpec((1, 6, d), modi), pl.BlockSpec((SUBLANES, t), tok)],
        out_specs=pl.BlockSpec(memory_space=pl.ANY),
        out_shape=jax.ShapeDtypeStruct((n_slots, d), F32),
        scratch_shapes=[pltpu.VMEM((2, seg_rows, d), F32), pltpu.VMEM((MOE_BLOCK, d), F32),
                        pltpu.SemaphoreType.DMA((2,)), pltpu.SemaphoreType.DMA(())],
        compiler_params=_params("arbitrary"),
        name="moe_dispatch",
    )(tail, seg, seg, x, mod, slot_t)

    bg = b_gu[:, 0::2].reshape(n_exp, 1, d_ff)
    bu = b_gu[:, 1::2].reshape(n_exp, 1, d_ff)
    used = lambda i, be, nu: (jnp.minimum(i, nu[0] - 1), 0)
    wmap = lambda i, be, nu: (be[i], 0, 0)
    wmap_all = lambda i, be, nu: (be[i] + layer * n_exp, 0, 0)
    y = pl.pallas_call(
        _expert_kernel,
        grid_spec=pltpu.PrefetchScalarGridSpec(
            num_scalar_prefetch=2,
            grid=(n_blocks,),
            in_specs=[pl.BlockSpec((MOE_BLOCK, d), used),
                      pl.BlockSpec((1, d, d_ff), wmap_all), pl.BlockSpec((1, d, d_ff), wmap_all),
                      pl.BlockSpec((1, 1, d_ff), wmap), pl.BlockSpec((1, 1, d_ff), wmap),
                      pl.BlockSpec((1, d_ff, d), wmap_all), pl.BlockSpec((1, 1, d), wmap)],
            out_specs=pl.BlockSpec((MOE_BLOCK, d), used)),
        out_shape=jax.ShapeDtypeStruct((n_slots, d), F32),
        compiler_params=_params("arbitrary"),
        name="moe_experts",
    )(block_expert, n_used.reshape(1), xs, wg, wu, bg, bu, wd, b_down.reshape(n_exp, 1, d))

    return pl.pallas_call(
        functools.partial(_combine_kernel, n_exp=n_exp, top_k=TOP_K, alpha=alpha),
        grid=(n_tiles,),
        in_specs=[smem_tile(lambda i: (i, 0, 0)), smem_tile(lambda i: (jnp.minimum(i + 1, n_tiles - 1), 0, 0)),
                  pl.BlockSpec((t, d), row), pl.BlockSpec((1, 6, d), modi),
                  pl.BlockSpec((t, TOP_K), tok), pl.BlockSpec((t, TOP_K), tok),
                  pl.BlockSpec((1, d), lambda i: (0, 0)), pl.BlockSpec((1, d), lambda i: (0, 0)),
                  pl.BlockSpec(memory_space=pl.ANY)],
        out_specs=pl.BlockSpec((t, d), row),
        out_shape=jax.ShapeDtypeStruct(x.shape, F32),
        scratch_shapes=[pltpu.VMEM((2, seg_rows, d), F32), pltpu.SemaphoreType.DMA((2,))],
        input_output_aliases={2: 0},
        compiler_params=_params("arbitrary"),
        name="moe_combine_ln",
    )(seg, seg, x, mod, gate, slot, ln_g.reshape(1, d), ln_b.reshape(1, d), y)


def kernel(x, c, ctx, c_ctx, ada_w, ada_b, ln1_g, ln1_b, ln2_g, ln2_b, router_w, router_b, exp_gu_w, exp_gu_b, exp_down_w, exp_down_b, rg_w_in, rg_conv_w, rg_conv_b, rg_gate_a_w, rg_gate_a_b, rg_gate_x_w, rg_gate_x_b, rg_lambda, rg_w_out, gqa_w_qkv, gqa_b_qkv, gqa_sinks, gqa_w_o, gqa_b_o, mla_w_down, mla_q_norm, mla_kv_norm, mla_w_uq, mla_w_ukv, mla_w_o):
    batch, seq, d = x.shape
    n_ctx = ctx.shape[1]
    depth = ada_w.shape[0]
    alpha = (2 * depth) ** 0.25
    rows = _Rows(batch, n_ctx, seq, ROW_TILE)
    xa = jnp.concatenate([ctx.reshape(batch * n_ctx, d), x.reshape(batch * seq, d)], axis=0)
    mods = _ada_table(jnp.concatenate([c_ctx[None, :], c], axis=0), ada_w, ada_b)
    wg, wu = _split_gate_up(exp_gu_w)
    wd = exp_down_w.reshape((-1,) + exp_down_w.shape[2:]).astype(BF16)
    for i in range(depth):
        need_ctx = i < depth - 1
        kind, j = i % 3, i // 3
        mod = mods[i]
        if kind == 0:
            prm = dict(w_in=rg_w_in[j], conv_w=rg_conv_w[j], conv_b=rg_conv_b[j], gate_a_w=rg_gate_a_w[j],
                       gate_a_b=rg_gate_a_b[j], gate_x_w=rg_gate_x_w[j], gate_x_b=rg_gate_x_b[j],
                       lam=rg_lambda[j], w_out=rg_w_out[j])
            xa = _rglru_layer(rows, xa, mod, prm, ln1_g[i], ln1_b[i], alpha, need_ctx)
        elif kind == 1:
            prm = dict(w_qkv=gqa_w_qkv[j], b_qkv=gqa_b_qkv[j], sinks=gqa_sinks[j], w_o=gqa_w_o[j], b_o=gqa_b_o[j])
            xa = _gqa_layer(rows, xa, mod, prm, ln1_g[i], ln1_b[i], alpha, need_ctx)
        else:
            prm = dict(w_down=mla_w_down[j], q_norm=mla_q_norm[j], kv_norm=mla_kv_norm[j], w_uq=mla_w_uq[j],
                       w_ukv=mla_w_ukv[j], w_o=mla_w_o[j])
            xa = _mla_layer(rows, xa, mod, prm, ln1_g[i], ln1_b[i], alpha, need_ctx)
        xa = _moe_layer(rows, xa, mod, router_w[i], router_b[i], wg, wu, wd, i, exp_gu_b[i], exp_down_b[i],
                        ln2_g[i], ln2_b[i], alpha, need_ctx)
    return xa[batch * n_ctx:].reshape(batch, seq, d)
```

```python
import functools
import math

import jax
import jax.numpy as jnp
from jax import lax
from jax.experimental import pallas as pl
from jax.experimental.pallas import tpu as pltpu

F32 = jnp.float32
BF16 = jnp.bfloat16

GRID_W = 64
LN_EPS = 1e-5
RMS_EPS = 1e-6
ROPE_THETA = 10000.0
NEG_INF = -1e30
RG_C = 8.0
GQA_KV = 2
GQA_HD = 64
WINDOW = 128
MLA_HEADS = 16
QK_NOPE = 64
QK_ROPE = 32
V_HD = 64
TOP_K = 4
SWIGLU_LIMIT = 7.0
SWIGLU_ALPHA = 1.702
MOE_BLOCK = 256

LANES = 128
SUBLANES = 8
BF16_ROWS = 16
ROW_TILE = 256
ATTN_Q = 128
MLA_TQ = 512
MLA_TK = 256
MLA_SUB_K = 128
MLA_SUB_Q = 256
MLA_LOOKAHEAD = 16
MXU_DIM = 256
SEG_ALIGN = SUBLANES
SEG_PIECES = (8, 16, 32)
VMEM_LIMIT = 56 * 1024 * 1024


def _params(*sem):
    return pltpu.CompilerParams(dimension_semantics=sem, vmem_limit_bytes=VMEM_LIMIT)


def _bdot(a, b):
    return jnp.dot(a.astype(BF16), b.astype(BF16), preferred_element_type=F32)


def _bdot_nt(a, b):
    return lax.dot_general(a.astype(BF16), b.astype(BF16), (((1,), (1,)), ((), ())),
                           preferred_element_type=F32)


def _layer_norm(r, g, b):
    mu = jnp.mean(r, axis=-1, keepdims=True)
    d = r - mu
    var = jnp.mean(d * d, axis=-1, keepdims=True)
    return d * lax.rsqrt(var + LN_EPS) * g + b


def _gelu_tanh(x):
    return 0.5 * x * (1.0 + jnp.tanh(math.sqrt(2.0 / math.pi) * (x + 0.044715 * (x * x * x))))


class _Rows:
    def __init__(self, batch, n_ctx, seq, tile):
        assert n_ctx % tile == 0 and seq % tile == 0
        self.batch, self.n_ctx, self.seq, self.tile = batch, n_ctx, seq, tile
        self.ctx_tiles = batch * n_ctx // tile
        self.lat_tiles = batch * seq // tile
        self.tiles = self.ctx_tiles + self.lat_tiles
        self.lat_per_batch = seq // tile
        self.ctx_per_batch = n_ctx // tile
        self.rows = batch * (n_ctx + seq)

    def mod_index(self, i):
        return jnp.where(i < self.ctx_tiles, 0, 1 + (i - self.ctx_tiles) // self.lat_per_batch)

    def rope_index(self, i):
        return jnp.where(i < self.ctx_tiles, self.lat_per_batch, (i - self.ctx_tiles) % self.lat_per_batch)


def _ada_kernel(c_ref, w_ref, b_ref, o_ref):
    cv = c_ref[...]
    s = cv * jax.nn.sigmoid(cv)
    o_ref[0] = jnp.dot(s, w_ref[0], preferred_element_type=F32,
                       precision=lax.Precision.HIGHEST) + b_ref[0]


def _ada_table(cvec, ada_w, ada_b):
    depth, d, d6 = ada_w.shape
    n = cvec.shape[0]
    chunk = d
    out = pl.pallas_call(
        _ada_kernel,
        grid=(depth, d6 // chunk),
        in_specs=[pl.BlockSpec((n, d), lambda l, j: (0, 0)),
                  pl.BlockSpec((1, d, chunk), lambda l, j: (l, 0, j)),
                  pl.BlockSpec((1, 1, chunk), lambda l, j: (l, 0, j))],
        out_specs=pl.BlockSpec((1, n, chunk), lambda l, j: (l, 0, j)),
        out_shape=jax.ShapeDtypeStruct((depth, n, d6), F32),
        compiler_params=_params("arbitrary", "arbitrary"),
        name="ada_table",
    )(cvec, ada_w, ada_b.reshape(depth, 1, d6))
    return out.reshape(depth, n, 6, d)


def _out_ln_kernel(z_ref, w_ref, b_ref, x_ref, mod_ref, g_ref, bt_ref, o_ref, *, gate_row, alpha):
    y = _bdot(z_ref[...], w_ref[...]) + b_ref[...]
    m = mod_ref[0]
    r = alpha * x_ref[...] + m[gate_row:gate_row + 1] * y
    o_ref[...] = _layer_norm(r, g_ref[...], bt_ref[...])


def _out_ln(rows, z, w, bias, x, mod, ln_g, ln_b, *, alpha, first_tile=0):
    t, d = rows.tile, x.shape[1]
    kdim = z.shape[1]
    n_tiles = rows.tiles - first_tile
    row = lambda i: (i + first_tile, 0)
    return pl.pallas_call(
        functools.partial(_out_ln_kernel, gate_row=2, alpha=alpha),
        grid=(n_tiles,),
        in_specs=[pl.BlockSpec((t, kdim), row),
                  pl.BlockSpec((kdim, d), lambda i: (0, 0)),
                  pl.BlockSpec((1, d), lambda i: (0, 0)),
                  pl.BlockSpec((t, d), row),
                  pl.BlockSpec((1, 6, d), lambda i: (rows.mod_index(i + first_tile), 0, 0)),
                  pl.BlockSpec((1, d), lambda i: (0, 0)),
                  pl.BlockSpec((1, d), lambda i: (0, 0))],
        out_specs=pl.BlockSpec((t, d), row),
        out_shape=jax.ShapeDtypeStruct(x.shape, F32),
        input_output_aliases={3: 0},
        compiler_params=_params("arbitrary"),
        name="out_proj_ln",
    )(z, w.astype(BF16), bias.reshape(1, d), x, mod, ln_g.reshape(1, d), ln_b.reshape(1, d))


def _rg_in_kernel(x_ref, mod_ref, w_ref, gel_ref, rec_ref, *, d_rnn):
    m = mod_ref[0]
    h = (x_ref[...] * (1.0 + m[1:2]) + m[0:1]).astype(BF16)
    gel_ref[...] = _gelu_tanh(_bdot(h, w_ref[:, :d_rnn])).astype(BF16)
    rec_ref[...] = _bdot(h, w_ref[:, d_rnn:])


def _rg_in(rows, x, mod, w_in):
    t, d = rows.tile, x.shape[1]
    d_rnn = w_in.shape[1] // 2
    return pl.pallas_call(
        functools.partial(_rg_in_kernel, d_rnn=d_rnn),
        grid=(rows.tiles,),
        in_specs=[pl.BlockSpec((t, d), lambda i: (i, 0)),
                  pl.BlockSpec((1, 6, d), lambda i: (rows.mod_index(i), 0, 0)),
                  pl.BlockSpec((d, 2 * d_rnn), lambda i: (0, 0))],
        out_specs=[pl.BlockSpec((t, d_rnn), lambda i: (i, 0)),
                   pl.BlockSpec((t, d_rnn), lambda i: (i, 0))],
        out_shape=[jax.ShapeDtypeStruct((rows.rows, d_rnn), BF16),
                   jax.ShapeDtypeStruct((rows.rows, d_rnn), F32)],
        compiler_params=_params("arbitrary"),
        name="rg_in_proj",
    )(x, mod, w_in.astype(BF16))


def _rg_scan_kernel(*refs, reverse, fuse_out, n_blocks, block_w, tile, ctx_tiles, lat_tiles):
    if fuse_out:
        (x_ref, xp_ref, xn_ref, cw_ref, cb_ref, wa_ref, ba_ref, wx_ref, bx_ref, lam_ref,
         hf_ref, gel_ref, out_ref, a_scr, u_scr, h_scr, carry_scr) = refs
    else:
        (x_ref, xp_ref, xn_ref, cw_ref, cb_ref, wa_ref, ba_ref, wx_ref, bx_ref, lam_ref,
         out_ref, a_scr, u_scr, carry_scr) = refs
        h_scr = out_ref
    j = pl.program_id(1)
    is_ctx = j < ctx_tiles
    n_seq = jnp.where(is_ctx, ctx_tiles, lat_tiles)
    step = jnp.where(is_ctx, j, j - ctx_tiles)
    pos = (n_seq - 1 - step) if reverse else step
    prev_ok = (pos > 0).astype(F32)
    next_ok = (pos < n_seq - 1).astype(F32)

    @pl.when(j == 0)
    def _():
        carry_scr[...] = jnp.zeros_like(carry_scr)

    row = lax.broadcasted_iota(jnp.int32, (tile, block_w), 0)
    for n in range(n_blocks):
        cols = slice(n * block_w, (n + 1) * block_w)
        x = x_ref[:, cols]
        prev = xp_ref[SUBLANES - 1:SUBLANES, cols] * prev_ok
        nxt0 = xn_ref[0:1, cols] * next_ok
        nxt1 = xn_ref[1:2, cols] * next_ok
        x_m1 = jnp.where(row == 0, prev, pltpu.roll(x, 1, 0))
        x_p1 = jnp.where(row == tile - 1, nxt0, pltpu.roll(x, tile - 1, 0))
        x_p2 = jnp.where(row == tile - 2, nxt0, jnp.where(row == tile - 1, nxt1, pltpu.roll(x, tile - 2, 0)))
        xc = (cw_ref[0:1, cols] * x_m1 + cw_ref[1:2, cols] * x + cw_ref[2:3, cols] * x_p1
              + cw_ref[3:4, cols] * x_p2 + cb_ref[:, cols])
        xb = xc.astype(BF16)
        r = jax.nn.sigmoid(_bdot(xb, wa_ref[n]) + ba_ref[:, cols])
        gi = jax.nn.sigmoid(_bdot(xb, wx_ref[n]) + bx_ref[:, cols])
        z = -lam_ref[:, cols]
        softplus = jnp.maximum(z, 0.0) + jnp.log1p(jnp.exp(-jnp.abs(z)))
        log_a = (-RG_C) * r * softplus
        a = jnp.exp(log_a)
        a_scr[:, cols] = a
        u_scr[:, cols] = jnp.sqrt(1.0 - a * a) * (gi * xc)

    width = n_blocks * block_w
    sub = lax.broadcasted_iota(jnp.int32, (SUBLANES, width), 0)
    groups = tile // SUBLANES

    def body(g, carry):
        gg = (groups - 1 - g) if reverse else g
        r0 = pl.multiple_of(gg * SUBLANES, SUBLANES)
        a8 = a_scr[pl.ds(r0, SUBLANES), :]
        u8 = u_scr[pl.ds(r0, SUBLANES), :]
        for s in (1, 2, 4):
            if reverse:
                a_sh, u_sh, ok = pltpu.roll(a8, SUBLANES - s, 0), pltpu.roll(u8, SUBLANES - s, 0), sub < SUBLANES - s
            else:
                a_sh, u_sh, ok = pltpu.roll(a8, s, 0), pltpu.roll(u8, s, 0), sub >= s
            u8 = jnp.where(ok, a8 * u_sh + u8, u8)
            a8 = jnp.where(ok, a8 * a_sh, a8)
        h8 = a8 * carry + u8
        h_scr[pl.ds(r0, SUBLANES), :] = h8
        return h8[0:1, :] if reverse else h8[SUBLANES - 1:SUBLANES, :]

    carry_scr[...] = lax.fori_loop(0, groups, body, carry_scr[...])
    if fuse_out:
        out_ref[...] = ((hf_ref[...] + h_scr[...]) * gel_ref[...].astype(F32)).astype(BF16)


def _rg_scan(rows, rec, conv_w, conv_b, wa, ba, wx, bx, lam, *, reverse, h_fwd=None, gel=None):
    t = rows.tile
    c = rec.shape[1]
    n_blocks, block_w = wa.shape[0], wa.shape[1]
    nc, nl = rows.ctx_per_batch, rows.lat_per_batch
    halo = t // SUBLANES
    last_halo = rows.rows // SUBLANES - 1
    fuse_out = h_fwd is not None

    def tile_index(b, j):
        is_ctx = j < nc
        step = jnp.where(is_ctx, j, j - nc)
        n_seq = jnp.where(is_ctx, nc, nl)
        pos = (n_seq - 1 - step) if reverse else step
        return jnp.where(is_ctx, b * nc + pos, rows.ctx_tiles + b * nl + pos)

    cur = lambda b, j: (tile_index(b, j), 0)
    prv = lambda b, j: (jnp.maximum(tile_index(b, j) * halo - 1, 0), 0)
    nxt = lambda b, j: (jnp.minimum((tile_index(b, j) + 1) * halo, last_halo), 0)
    full2 = lambda b, j: (0, 0)
    full3 = lambda b, j: (0, 0, 0)
    in_specs = [pl.BlockSpec((t, c), cur), pl.BlockSpec((SUBLANES, c), prv), pl.BlockSpec((SUBLANES, c), nxt),
                pl.BlockSpec((4, c), full2), pl.BlockSpec((1, c), full2),
                pl.BlockSpec((n_blocks, block_w, block_w), full3), pl.BlockSpec((1, c), full2),
                pl.BlockSpec((n_blocks, block_w, block_w), full3), pl.BlockSpec((1, c), full2),
                pl.BlockSpec((1, c), full2)]
    args = [rec, rec, rec, conv_w, conv_b.reshape(1, c), wa.astype(BF16), ba.reshape(1, c),
            wx.astype(BF16), bx.reshape(1, c), lam.reshape(1, c)]
    scratch = [pltpu.VMEM((t, c), F32), pltpu.VMEM((t, c), F32)]
    if fuse_out:
        in_specs += [pl.BlockSpec((t, c), cur), pl.BlockSpec((t, c), cur)]
        args += [h_fwd, gel]
        scratch.append(pltpu.VMEM((t, c), F32))
    scratch.append(pltpu.VMEM((1, c), F32))
    return pl.pallas_call(
        functools.partial(_rg_scan_kernel, reverse=reverse, fuse_out=fuse_out, n_blocks=n_blocks,
                          block_w=block_w, tile=t, ctx_tiles=nc, lat_tiles=nl),
        grid=(rows.batch, nc + nl),
        in_specs=in_specs,
        out_specs=pl.BlockSpec((t, c), cur),
        out_shape=jax.ShapeDtypeStruct((rows.rows, c), BF16 if fuse_out else F32),
        scratch_shapes=scratch,
        compiler_params=_params("arbitrary", "arbitrary"),
        name="rg_scan_bwd" if reverse else "rg_scan_fwd",
    )(*args)


def _rglru_layer(rows, x, mod, p, ln_g, ln_b, alpha, need_ctx):
    gel, rec = _rg_in(rows, x, mod, p["w_in"])
    h_fwd = _rg_scan(rows, rec, p["conv_w"], p["conv_b"], p["gate_a_w"][0], p["gate_a_b"][0],
                     p["gate_x_w"][0], p["gate_x_b"][0], p["lam"][0], reverse=False)
    z = _rg_scan(rows, rec, p["conv_w"], p["conv_b"], p["gate_a_w"][1], p["gate_a_b"][1],
                 p["gate_x_w"][1], p["gate_x_b"][1], p["lam"][1], reverse=True, h_fwd=h_fwd, gel=gel)
    d = x.shape[1]
    return _out_ln(rows, z, p["w_out"], jnp.zeros((d,), F32), x, mod, ln_g, ln_b, alpha=alpha,
                   first_tile=0 if need_ctx else rows.ctx_tiles)


def _axial_angles(seq, rot_dim):
    pos = jnp.arange(seq, dtype=jnp.int32)
    row = (pos // GRID_W).astype(F32)
    col = (pos % GRID_W).astype(F32)
    n_freq = rot_dim // 4
    inv_freq = ROPE_THETA ** (-jnp.arange(n_freq, dtype=F32) / n_freq)
    return jnp.concatenate([row[:, None] * inv_freq, col[:, None] * inv_freq], axis=-1)


def _rope_tables(rows, rot_dim, lead, trail):
    ang = _axial_angles(rows.seq, rot_dim)
    cos, sin = jnp.cos(ang), jnp.sin(ang)
    ones = lambda w: jnp.ones((rows.seq, w), F32)
    zeros = lambda w: jnp.zeros((rows.seq, w), F32)
    c = jnp.concatenate([ones(lead), cos, cos, ones(trail)], axis=-1)
    s = jnp.concatenate([zeros(lead), -sin, sin, zeros(trail)], axis=-1)
    reps = LANES // c.shape[1]
    c, s = jnp.tile(c, (1, reps)), jnp.tile(s, (1, reps))
    t = rows.tile
    c = jnp.concatenate([c.reshape(rows.lat_per_batch, t, LANES), jnp.ones((1, t, LANES), F32)], axis=0)
    s = jnp.concatenate([s.reshape(rows.lat_per_batch, t, LANES), jnp.zeros((1, t, LANES), F32)], axis=0)
    return c, s


def _rope_chunk(x, cos, sin, half):
    lane = lax.broadcasted_iota(jnp.int32, x.shape, 1)
    partner = jnp.where((lane % (2 * half)) < half, pltpu.roll(x, LANES - half, 1), pltpu.roll(x, half, 1))
    return x * cos + partner * sin


def _gqa_proj_kernel(x_ref, mod_ref, w_ref, b_ref, cos_ref, sin_ref, q_ref, kv_ref, *, q_dim, scale, half):
    m = mod_ref[0]
    h = (x_ref[...] * (1.0 + m[1:2]) + m[0:1]).astype(BF16)
    cos, sin = cos_ref[0], sin_ref[0]
    n_q = q_dim // LANES
    n_all = w_ref.shape[1] // LANES
    for c in range(n_all):
        cols = slice(c * LANES, (c + 1) * LANES)
        p = _bdot(h, w_ref[:, cols]) + b_ref[:, cols]
        is_v = c in (n_q + 1, n_q + 3)
        if not is_v:
            p = _rope_chunk(p, cos, sin, half)
        if c < n_q:
            q_ref[:, cols] = (p * scale).astype(BF16)
        else:
            kv_ref[:, (c - n_q) * LANES:(c - n_q + 1) * LANES] = p.astype(BF16)


def _gqa_attn_kernel(*refs, windowed, window, seq, n_pairs, pairs_per_kv):
    if windowed:
        q_ref, kv_ref, kvc_ref, sink_ref, o_ref = refs
    else:
        q_ref, kvc_ref, sink_ref, _, o_ref = refs
    tq = q_ref.shape[0]
    kvc = kvc_ref[...]
    if windowed:
        span = tq + 2 * window
        qs = pl.program_id(1) * tq
        ws = pl.multiple_of(jnp.clip(qs - window, 0, seq - span), LANES)
        kv = jnp.concatenate([kv_ref[pl.ds(ws, span), :], kvc], axis=0)
        n_keys = kv.shape[0]
        qpos = qs + lax.broadcasted_iota(jnp.int32, (tq, n_keys), 0)
        col = lax.broadcasted_iota(jnp.int32, (tq, n_keys), 1)
        mask = (jnp.abs(ws + col - qpos) <= window) | (col >= span)
    else:
        kv = kvc
        mask = None
    lane = lax.broadcasted_iota(jnp.int32, (kv.shape[0], LANES), 1)
    low = lane < GQA_HD
    zero = jnp.zeros((kv.shape[0], LANES), BF16)
    k_plain, v_plain = kv[:, 0:LANES], kv[:, LANES:2 * LANES]
    k_swap, v_swap = kv[:, 2 * LANES:3 * LANES], kv[:, 3 * LANES:4 * LANES]
    for pr in range(n_pairs):
        g = pr // pairs_per_kv
        k_lo = jnp.where(low, k_plain if g == 0 else k_swap, zero)
        k_hi = jnp.where(low, zero, k_swap if g == 0 else k_plain)
        v_lo = jnp.where(low, v_plain if g == 0 else v_swap, zero)
        v_hi = jnp.where(low, zero, v_swap if g == 0 else v_plain)
        q2 = q_ref[:, pr * LANES:(pr + 1) * LANES]
        acc = jnp.zeros((tq, LANES), F32)
        for hh, (kx, vx) in enumerate(((k_lo, v_lo), (k_hi, v_hi))):
            head = 2 * pr + hh
            s = _bdot_nt(q2, kx)
            if mask is not None:
                s = jnp.where(mask, s, NEG_INF)
            sk = sink_ref[head:head + 1, 0:1]
            mx = jnp.maximum(jnp.max(s, axis=1, keepdims=True), sk)
            p = jnp.exp(s - mx)
            denom = jnp.sum(p, axis=1, keepdims=True) + jnp.exp(sk - mx)
            acc = acc + _bdot(p * (1.0 / denom), vx)
        o_ref[:, pr * LANES:(pr + 1) * LANES] = acc.astype(BF16)


def _gqa_layer(rows, x, mod, p, ln_g, ln_b, alpha, need_ctx):
    t, d = rows.tile, x.shape[1]
    w_qkv, b_qkv = p["w_qkv"], p["b_qkv"]
    kv_dim = GQA_KV * GQA_HD
    q_dim = w_qkv.shape[1] - 2 * kv_dim
    n_heads = q_dim // GQA_HD
    assert kv_dim == LANES and GQA_KV == 2 and q_dim % LANES == 0
    swap = lambda a: jnp.concatenate([a[..., GQA_HD:], a[..., :GQA_HD]], axis=-1)
    wk, wv = w_qkv[:, q_dim:q_dim + kv_dim], w_qkv[:, q_dim + kv_dim:]
    bk, bv = b_qkv[q_dim:q_dim + kv_dim], b_qkv[q_dim + kv_dim:]
    w_ext = jnp.concatenate([w_qkv, swap(wk), swap(wv)], axis=1).astype(BF16)
    b_ext = jnp.concatenate([b_qkv, swap(bk), swap(bv)]).reshape(1, -1)
    n_ext = w_ext.shape[1]
    cos, sin = _rope_tables(rows, GQA_HD, 0, 0)
    q, kv = pl.pallas_call(
        functools.partial(_gqa_proj_kernel, q_dim=q_dim, scale=GQA_HD ** -0.5, half=GQA_HD // 2),
        grid=(rows.tiles,),
        in_specs=[pl.BlockSpec((t, d), lambda i: (i, 0)),
                  pl.BlockSpec((1, 6, d), lambda i: (rows.mod_index(i), 0, 0)),
                  pl.BlockSpec((d, n_ext), lambda i: (0, 0)),
                  pl.BlockSpec((1, n_ext), lambda i: (0, 0)),
                  pl.BlockSpec((1, t, LANES), lambda i: (rows.rope_index(i), 0, 0)),
                  pl.BlockSpec((1, t, LANES), lambda i: (rows.rope_index(i), 0, 0))],
        out_specs=[pl.BlockSpec((t, q_dim), lambda i: (i, 0)),
                   pl.BlockSpec((t, 4 * LANES), lambda i: (i, 0))],
        out_shape=[jax.ShapeDtypeStruct((rows.rows, q_dim), BF16),
                   jax.ShapeDtypeStruct((rows.rows, 4 * LANES), BF16)],
        compiler_params=_params("arbitrary"),
        name="gqa_qkv_proj",
    )(x, mod, w_ext, b_ext, cos, sin)

    sinks = jnp.broadcast_to(p["sinks"].astype(F32)[:, None], (n_heads, LANES))
    b_, s_, n_ctx = rows.batch, rows.seq, rows.n_ctx
    ctx_rows = b_ * n_ctx
    assert ctx_rows % s_ == 0 and s_ >= ATTN_Q + 2 * WINDOW
    q_blocks = s_ // ATTN_Q
    n_pairs = q_dim // LANES
    common = dict(window=WINDOW, seq=s_, n_pairs=n_pairs, pairs_per_kv=n_pairs // GQA_KV)
    o_shape = jax.ShapeDtypeStruct((rows.rows, q_dim), BF16)
    o_lat = pl.pallas_call(
        functools.partial(_gqa_attn_kernel, windowed=True, **common),
        grid=(b_, q_blocks),
        in_specs=[pl.BlockSpec((ATTN_Q, q_dim), lambda b, j: (ctx_rows // ATTN_Q + b * q_blocks + j, 0)),
                  pl.BlockSpec((s_, 4 * LANES), lambda b, j: (ctx_rows // s_ + b, 0)),
                  pl.BlockSpec((n_ctx, 4 * LANES), lambda b, j: (b, 0)),
                  pl.BlockSpec((n_heads, LANES), lambda b, j: (0, 0))],
        out_specs=pl.BlockSpec((ATTN_Q, q_dim), lambda b, j: (ctx_rows // ATTN_Q + b * q_blocks + j, 0)),
        out_shape=o_shape,
        compiler_params=_params("arbitrary", "arbitrary"),
        name="gqa_window_attn",
    )(q, kv, kv, sinks)
    if need_ctx:
        cq_blocks = n_ctx // ATTN_Q
        o = pl.pallas_call(
            functools.partial(_gqa_attn_kernel, windowed=False, **common),
            grid=(b_, cq_blocks),
            in_specs=[pl.BlockSpec((ATTN_Q, q_dim), lambda b, j: (b * cq_blocks + j, 0)),
                      pl.BlockSpec((n_ctx, 4 * LANES), lambda b, j: (b, 0)),
                      pl.BlockSpec((n_heads, LANES), lambda b, j: (0, 0)),
                      pl.BlockSpec(memory_space=pl.ANY)],
            out_specs=pl.BlockSpec((ATTN_Q, q_dim), lambda b, j: (b * cq_blocks + j, 0)),
            out_shape=o_shape,
            input_output_aliases={3: 0},
            compiler_params=_params("arbitrary", "arbitrary"),
            name="gqa_ctx_attn",
        )(q, kv, sinks, o_lat)
    else:
        o = o_lat
    return _out_ln(rows, o, p["w_o"], p["b_o"], x, mod, ln_g, ln_b, alpha=alpha,
                   first_tile=0 if need_ctx else rows.ctx_tiles)


def _mla_proj_kernel(x_ref, mod_ref, wd_ref, qn_ref, kvn_ref, wq_ref, wk_ref, wv_ref, cos_ref, sin_ref,
                     q_ref, k_ref, vt_ref, *, q_lora, kv_lora, scale, n_heads):
    m = mod_ref[0]
    h = (x_ref[...] * (1.0 + m[1:2]) + m[0:1]).astype(BF16)
    p = _bdot(h, wd_ref[...])
    cq, ckv = p[:, :q_lora], p[:, q_lora:q_lora + kv_lora]
    cos, sin = cos_ref[0], sin_ref[0]
    k_rope = _rope_chunk(p[:, q_lora + kv_lora:], cos, sin, QK_ROPE // 2)
    cq = (cq * lax.rsqrt(jnp.mean(cq * cq, axis=-1, keepdims=True) + RMS_EPS) * qn_ref[...]).astype(BF16)
    ckv = (ckv * lax.rsqrt(jnp.mean(ckv * ckv, axis=-1, keepdims=True) + RMS_EPS) * kvn_ref[...]).astype(BF16)
    for hd in range(n_heads):
        cols = slice(hd * LANES, (hd + 1) * LANES)
        qh = _rope_chunk(_bdot(cq, wq_ref[:, cols]), cos, sin, QK_ROPE // 2)
        q_ref[:, cols] = (qh * scale).astype(BF16)
        k_ref[:, cols] = (_bdot(ckv, wk_ref[:, cols]) + k_rope).astype(BF16)
    vt_ref[...] = _bdot(ckv, wv_ref[...]).T.astype(BF16)


def _mla_attn_kernel(*refs, n_heads):
    q_ref, k_ref, vt_ref = refs[:3]
    o_ref, m_scr, l_scr, acc_scr = refs[-4:]
    kt = pl.program_id(2)

    @pl.when(kt == 0)
    def _():
        m_scr[...] = jnp.full_like(m_scr, -jnp.inf)
        l_scr[...] = jnp.zeros_like(l_scr)
        acc_scr[...] = jnp.zeros_like(acc_scr)

    tq, tk = q_ref.shape[0], k_ref.shape[0]
    sub_q, sub_k = min(MLA_SUB_Q, tq), min(MLA_SUB_K, tk)
    n_kh = tk // sub_k
    ones = jnp.ones((BF16_ROWS, sub_k), BF16)
    items = [(hd, kh, qh) for hd in range(n_heads) for kh in range(n_kh) for qh in range(tq // sub_q)]

    def scores(item):
        hd, kh, qh = item
        cols = slice(hd * LANES, (hd + 1) * LANES)
        s = _bdot_nt(k_ref[kh * sub_k:(kh + 1) * sub_k, cols], q_ref[qh * sub_q:(qh + 1) * sub_q, cols])
        return s, jnp.max(s, axis=0, keepdims=True)

    ahead = [scores(it) for it in items[:MLA_LOOKAHEAD]]
    state = {}
    for n, (hd, kh, qh) in enumerate(items):
        rws = slice(hd * V_HD, (hd + 1) * V_HD)
        qcols = slice(qh * sub_q, (qh + 1) * sub_q)
        s_t, s_max = ahead.pop(0)
        if n + MLA_LOOKAHEAD < len(items):
            ahead.append(scores(items[n + MLA_LOOKAHEAD]))
        if kh == 0:
            state[hd, qh] = (m_scr[hd:hd + 1, qcols], l_scr[hd:hd + 1, qcols], acc_scr[rws, qcols])
        m_old, l_old, acc = state[hd, qh]
        m_new = jnp.maximum(m_old, s_max)
        p_t = jnp.exp2((s_t - m_new).astype(BF16))
        corr = jnp.exp2(m_old - m_new)
        vt = vt_ref[rws, kh * sub_k:(kh + 1) * sub_k]
        pv = _bdot(jnp.concatenate([vt, ones], axis=0), p_t)
        state[hd, qh] = (m_new, corr * l_old + pv[V_HD:V_HD + 1, :], acc * corr + pv[:V_HD, :])
        if kh == n_kh - 1:
            m_scr[hd:hd + 1, qcols], l_scr[hd:hd + 1, qcols], acc_scr[rws, qcols] = state.pop((hd, qh))

    @pl.when(kt == pl.num_programs(2) - 1)
    def _():
        for hd in range(n_heads):
            rws = slice(hd * V_HD, (hd + 1) * V_HD)
            acc_scr[rws, :] = acc_scr[rws, :] * (1.0 / l_scr[hd:hd + 1, :])
        o_ref[...] = acc_scr[...].T.astype(BF16)


def _mla_layer(rows, x, mod, p, ln_g, ln_b, alpha, need_ctx):
    t, d = rows.tile, x.shape[1]
    h_ = MLA_HEADS
    w_down, w_uq, w_ukv = p["w_down"], p["w_uq"], p["w_ukv"]
    q_lora = w_uq.shape[0]
    kv_lora = w_ukv.shape[0]
    qk = QK_NOPE + QK_ROPE
    assert QK_NOPE == V_HD == LANES // 2 and h_ % 2 == 0 and q_lora % LANES == 0 and kv_lora % LANES == 0
    pad = LANES - qk
    zc = lambda r, w: jnp.zeros((r, w), F32)
    wd_p = jnp.concatenate([w_down[:, :q_lora + kv_lora], zc(d, QK_NOPE), w_down[:, q_lora + kv_lora:],
                            zc(d, pad)], axis=1).astype(BF16)
    wq_p = jnp.concatenate([w_uq.reshape(q_lora, h_, qk), jnp.zeros((q_lora, h_, pad), F32)],
                           axis=-1).reshape(q_lora, h_ * LANES).astype(BF16)
    ukv = w_ukv.reshape(kv_lora, h_, QK_NOPE + V_HD)
    wk_p = jnp.concatenate([ukv[..., :QK_NOPE], jnp.zeros((kv_lora, h_, LANES - QK_NOPE), F32)],
                           axis=-1).reshape(kv_lora, h_ * LANES).astype(BF16)
    wv_p = ukv[..., QK_NOPE:].reshape(kv_lora, h_ * V_HD).astype(BF16)
    cos, sin = _rope_tables(rows, QK_ROPE, QK_NOPE, pad)
    n_down = wd_p.shape[1]
    q, k, vt = pl.pallas_call(
        functools.partial(_mla_proj_kernel, q_lora=q_lora, kv_lora=kv_lora, scale=qk ** -0.5 * math.log2(math.e),
                          n_heads=h_),
        grid=(rows.tiles,),
        in_specs=[pl.BlockSpec((t, d), lambda i: (i, 0)),
                  pl.BlockSpec((1, 6, d), lambda i: (rows.mod_index(i), 0, 0)),
                  pl.BlockSpec((d, n_down), lambda i: (0, 0)),
                  pl.BlockSpec((1, q_lora), lambda i: (0, 0)),
                  pl.BlockSpec((1, kv_lora), lambda i: (0, 0)),
                  pl.BlockSpec((q_lora, h_ * LANES), lambda i: (0, 0)),
                  pl.BlockSpec((kv_lora, h_ * LANES), lambda i: (0, 0)),
                  pl.BlockSpec((kv_lora, h_ * V_HD), lambda i: (0, 0)),
                  pl.BlockSpec((1, t, LANES), lambda i: (rows.rope_index(i), 0, 0)),
                  pl.BlockSpec((1, t, LANES), lambda i: (rows.rope_index(i), 0, 0))],
        out_specs=[pl.BlockSpec((t, h_ * LANES), lambda i: (i, 0)),
                   pl.BlockSpec((t, h_ * LANES), lambda i: (i, 0)),
                   pl.BlockSpec((h_ * V_HD, t), lambda i: (0, i))],
        out_shape=[jax.ShapeDtypeStruct((rows.rows, h_ * LANES), BF16),
                   jax.ShapeDtypeStruct((rows.rows, h_ * LANES), BF16),
                   jax.ShapeDtypeStruct((h_ * V_HD, rows.rows), BF16)],
        compiler_params=_params("arbitrary"),
        name="mla_proj",
    )(x, mod, wd_p, p["q_norm"].reshape(1, -1), p["kv_norm"].reshape(1, -1), wq_p, wk_p, wv_p, cos, sin)

    b_, s_, n_ctx = rows.batch, rows.seq, rows.n_ctx
    tk = MLA_TK
    tq, tq_ctx = min(MLA_TQ, s_), min(MLA_TQ, n_ctx)
    assert n_ctx % tk == 0 and s_ % tk == 0 and n_ctx % tq_ctx == 0 and s_ % tq == 0
    ck, lk = n_ctx // tk, s_ // tk
    ctx_kblocks = b_ * ck

    def kv_block(b, kt):
        return jnp.where(kt < ck, b * ck + kt, ctx_kblocks + b * lk + (kt - ck))

    def scratch(rows_q):
        return [pltpu.VMEM((h_, rows_q), F32), pltpu.VMEM((h_, rows_q), F32), pltpu.VMEM((h_ * V_HD, rows_q), F32)]

    o_shape = jax.ShapeDtypeStruct((rows.rows, h_ * V_HD), BF16)
    lq = s_ // tq
    ctx_qblocks = b_ * n_ctx // tq
    o_lat = pl.pallas_call(
        functools.partial(_mla_attn_kernel, n_heads=h_),
        grid=(b_, lq, ck + lk),
        in_specs=[pl.BlockSpec((tq, h_ * LANES), lambda b, i, kt: (ctx_qblocks + b * lq + i, 0)),
                  pl.BlockSpec((tk, h_ * LANES), lambda b, i, kt: (kv_block(b, kt), 0)),
                  pl.BlockSpec((h_ * V_HD, tk), lambda b, i, kt: (0, kv_block(b, kt)))],
        out_specs=pl.BlockSpec((tq, h_ * V_HD), lambda b, i, kt: (ctx_qblocks + b * lq + i, 0)),
        out_shape=o_shape,
        scratch_shapes=scratch(tq),
        compiler_params=_params("arbitrary", "arbitrary", "arbitrary"),
        name="mla_attn",
    )(q, k, vt)
    if need_ctx:
        cq = n_ctx // tq_ctx
        o = pl.pallas_call(
            functools.partial(_mla_attn_kernel, n_heads=h_),
            grid=(b_, cq, ck),
            in_specs=[pl.BlockSpec((tq_ctx, h_ * LANES), lambda b, i, kt: (b * cq + i, 0)),
                      pl.BlockSpec((tk, h_ * LANES), lambda b, i, kt: (b * ck + kt, 0)),
                      pl.BlockSpec((h_ * V_HD, tk), lambda b, i, kt: (0, b * ck + kt)),
                      pl.BlockSpec(memory_space=pl.ANY)],
            out_specs=pl.BlockSpec((tq_ctx, h_ * V_HD), lambda b, i, kt: (b * cq + i, 0)),
            out_shape=o_shape,
            scratch_shapes=scratch(tq_ctx),
            input_output_aliases={3: 0},
            compiler_params=_params("arbitrary", "arbitrary", "arbitrary"),
            name="mla_ctx_attn",
        )(q, k, vt, o_lat)
    else:
        o = o_lat
    return _out_ln(rows, o, p["w_o"], jnp.zeros((d,), F32), x, mod, ln_g, ln_b, alpha=alpha,
                   first_tile=0 if need_ctx else rows.ctx_tiles)


def _split_gu_kernel(w_ref, g_ref, u_ref):
    win = 2 * MXU_DIM
    r_i = lax.broadcasted_iota(jnp.int32, (win, MXU_DIM), 0)
    c_i = lax.broadcasted_iota(jnp.int32, (win, MXU_DIM), 1)
    pick_even = (r_i == 2 * c_i).astype(BF16)
    pick_odd = (r_i == 2 * c_i + 1).astype(BF16)
    for j in range(w_ref.shape[2] // win):
        w = w_ref[0, :, j * win:(j + 1) * win].astype(BF16)
        g_ref[0, :, j * MXU_DIM:(j + 1) * MXU_DIM] = _bdot(w, pick_even).astype(BF16)
        u_ref[0, :, j * MXU_DIM:(j + 1) * MXU_DIM] = _bdot(w, pick_odd).astype(BF16)


def _split_gate_up(w_gu):
    depth, n_exp, d, f2 = w_gu.shape
    w = w_gu.reshape(depth * n_exp, d, f2)
    tr = ROW_TILE
    spec_out = pl.BlockSpec((1, tr, f2 // 2), lambda e, r: (e, r, 0))
    shape_out = jax.ShapeDtypeStruct((depth * n_exp, d, f2 // 2), BF16)
    return pl.pallas_call(
        _split_gu_kernel,
        grid=(depth * n_exp, d // tr),
        in_specs=[pl.BlockSpec((1, tr, f2), lambda e, r: (e, r, 0))],
        out_specs=[spec_out, spec_out],
        out_shape=[shape_out, shape_out],
        compiler_params=_params("arbitrary", "arbitrary"),
        name="moe_split_gate_up",
    )(w)


def _seg_rows(t, n_exp):
    return t * TOP_K + n_exp * SEG_ALIGN


def _router_kernel(x_ref, mod_ref, rw_ref, rb_ref, slot_ref, slot_t_ref, gate_ref, cnt_ref, *, top_k):
    m = mod_ref[0]
    h = x_ref[...] * (1.0 + m[4:5]) + m[3:4]
    logits = jnp.dot(h, rw_ref[...], preferred_element_type=F32, precision=lax.Precision.HIGHEST) + rb_ref[...]
    t, n_exp = logits.shape
    lane = lax.broadcasted_iota(jnp.int32, (t, n_exp), 1).astype(F32)
    work = logits
    sel, val = [], []
    for _ in range(top_k):
        mx = jnp.max(work, axis=1, keepdims=True)
        pick = jnp.min(jnp.where(work == mx, lane, float(n_exp)), axis=1, keepdims=True)
        sel.append(pick)
        val.append(mx)
        work = jnp.where(lane == pick, -jnp.inf, work)
    ex = [jnp.exp(v - val[0]) for v in val]
    tot = ex[0]
    for e in ex[1:]:
        tot = tot + e
    onehot = jnp.zeros((t, n_exp), F32)
    for pick in sel:
        onehot = onehot + (lane == pick).astype(F32)
    r_i = lax.broadcasted_iota(jnp.int32, (t, t), 0)
    c_i = lax.broadcasted_iota(jnp.int32, (t, t), 1)
    before = _bdot((c_i < r_i).astype(BF16), onehot)
    count = jnp.sum(onehot, axis=0, keepdims=True)
    cap = jnp.floor((count + (SEG_ALIGN - 1)) * (1.0 / SEG_ALIGN)) * SEG_ALIGN
    e_r = lax.broadcasted_iota(jnp.int32, (n_exp, n_exp), 0)
    e_c = lax.broadcasted_iota(jnp.int32, (n_exp, n_exp), 1)
    seg_start = _bdot(cap, (e_r < e_c).astype(BF16))
    base = before + seg_start
    col = lax.broadcasted_iota(jnp.int32, (t, top_k), 1)
    wide_lane = lax.broadcasted_iota(jnp.int32, (t, LANES), 1)
    slot = jnp.zeros((t, top_k), jnp.int32)
    gate = jnp.zeros((t, top_k), F32)
    wide = jnp.zeros((t, LANES), F32)
    for k in range(top_k):
        sk = jnp.sum(jnp.where(lane == sel[k], base, 0.0), axis=1, keepdims=True)
        slot = jnp.where(col == k, sk.astype(jnp.int32), slot)
        gate = jnp.where(col == k, ex[k] / tot, gate)
        wide = jnp.where(wide_lane == k, sk, wide)
    slot_ref[...] = slot
    gate_ref[...] = gate
    slot_t_ref[...] = wide.T[:SUBLANES, :]
    cnt_ref[0] = count


def _segment_copies(seg_ref, n_exp, make_copy, act):
    *small, big = SEG_PIECES
    for e in range(n_exp):
        g0, rows, l0 = seg_ref[0, 0, e], seg_ref[0, 0, n_exp + e], seg_ref[0, 0, 2 * n_exp + e]

        def whole(j, carry, g0=g0, l0=l0):
            act(make_copy(pl.multiple_of(l0 + j * big, SEG_ALIGN), pl.multiple_of(g0 + j * big, SEG_ALIGN), big))
            return carry

        lax.fori_loop(0, lax.shift_right_logical(rows, big.bit_length() - 1), whole, 0)
        for size in small:
            done = rows & (-2 * size)

            @pl.when((rows & size) != 0)
            def _(g0=g0, l0=l0, done=done, size=size):
                act(make_copy(pl.multiple_of(l0 + done, SEG_ALIGN), pl.multiple_of(g0 + done, SEG_ALIGN), size))


def _dispatch_kernel(tail_ref, seg_ref, seg_prev_ref, x_ref, mod_ref, slot_t_ref, xs_ref, buf, zbuf, sem, zsem, *,
                     n_exp, top_k):
    i = pl.program_id(0)
    cur = i % 2

    def to_slots(buf_slot, sem_slot):
        return lambda l, g, rows: pltpu.make_async_copy(buf.at[buf_slot, pl.ds(l, rows)], xs_ref.at[pl.ds(g, rows)],
                                                        sem.at[sem_slot])

    @pl.when(i == 0)
    def _():
        zbuf[...] = jnp.zeros_like(zbuf)
        zero_fill = lambda l, g, rows: pltpu.make_async_copy(zbuf.at[pl.ds(l, rows)], xs_ref.at[pl.ds(g, rows)], zsem)
        _segment_copies(tail_ref, n_exp, zero_fill, lambda c: c.start())
        _segment_copies(tail_ref, n_exp, zero_fill, lambda c: c.wait())

    m = mod_ref[0]
    h = (x_ref[...] * (1.0 + m[4:5]) + m[3:4]).astype(BF16)
    n_rows, t = buf.shape[1], x_ref.shape[0]
    row = lax.broadcasted_iota(jnp.int32, (n_rows, t), 0)
    slot_t = slot_t_ref[...].astype(jnp.int32)
    pick = row == slot_t[0:1, :]
    for k in range(1, top_k):
        pick = pick | (row == slot_t[k:k + 1, :])
    buf[cur] = _bdot(pick.astype(BF16), h)
    _segment_copies(seg_ref, n_exp, to_slots(cur, cur), lambda c: c.start())

    @pl.when(i > 0)
    def _():
        _segment_copies(seg_prev_ref, n_exp, to_slots(1 - cur, 1 - cur), lambda c: c.wait())

    @pl.when(i == pl.num_programs(0) - 1)
    def _():
        _segment_copies(seg_ref, n_exp, to_slots(cur, cur), lambda c: c.wait())


def _expert_kernel(be_ref, nu_ref, x_ref, wg_ref, wu_ref, bg_ref, bu_ref, wd_ref, bd_ref, y_ref):
    del be_ref

    @pl.when(pl.program_id(0) < nu_ref[0])
    def _():
        xb = x_ref[...].astype(BF16)
        g = jnp.minimum(_bdot(xb, wg_ref[0]) + bg_ref[0], SWIGLU_LIMIT)
        u = jnp.clip(_bdot(xb, wu_ref[0]) + bu_ref[0], -SWIGLU_LIMIT, SWIGLU_LIMIT)
        act = (u + 1.0) * (g * jax.nn.sigmoid(SWIGLU_ALPHA * g))
        y_ref[...] = _bdot(act, wd_ref[0]) + bd_ref[0]


def _combine_kernel(seg_ref, seg_next_ref, x_ref, mod_ref, gate_ref, slot_ref, g_ref, bt_ref, y_ref, o_ref,
                    ybuf, sem, *, n_exp, top_k, alpha):
    i = pl.program_id(0)
    cur = i % 2

    def from_slots(buf_slot):
        return lambda l, g, rows: pltpu.make_async_copy(y_ref.at[pl.ds(g, rows)], ybuf.at[buf_slot, pl.ds(l, rows)],
                                                        sem.at[buf_slot])

    @pl.when(i == 0)
    def _():
        ybuf[...] = jnp.zeros_like(ybuf)
        _segment_copies(seg_ref, n_exp, from_slots(0), lambda c: c.start())

    @pl.when(i < pl.num_programs(0) - 1)
    def _():
        _segment_copies(seg_next_ref, n_exp, from_slots(1 - cur), lambda c: c.start())

    _segment_copies(seg_ref, n_exp, from_slots(cur), lambda c: c.wait())
    t, n_rows = x_ref.shape[0], ybuf.shape[1]
    lane = lax.broadcasted_iota(jnp.int32, (t, n_rows), 1)
    gate, slot = gate_ref[...], slot_ref[...]
    weights = jnp.zeros((t, n_rows), F32)
    for k in range(top_k):
        weights = jnp.where(lane == slot[:, k:k + 1], gate[:, k:k + 1], weights)
    acc = _bdot(weights, ybuf[cur])
    m = mod_ref[0]
    r = alpha * x_ref[...] + m[5:6] * acc
    o_ref[...] = _layer_norm(r, g_ref[...], bt_ref[...])


def _moe_layer(rows, x, mod, router_w, router_b, wg, wu, wd, layer, b_gu, b_down, ln_g, ln_b, alpha, need_ctx):
    t, d = rows.tile, x.shape[1]
    n_exp = router_w.shape[1]
    d_ff = wd.shape[1]
    first = 0 if need_ctx else rows.ctx_tiles
    n_tiles = rows.tiles - first
    n_tok = n_tiles * t
    row = lambda i: (i + first, 0)
    modi = lambda i: (rows.mod_index(i + first), 0, 0)
    tok = lambda i: (i, 0)

    seg_rows = _seg_rows(t, n_exp)
    assert 4 * n_exp <= LANES and t % SEG_PIECES[-1] == 0 and MOE_BLOCK % SEG_PIECES[-1] == 0
    slot, slot_t, gate, cnt = pl.pallas_call(
        functools.partial(_router_kernel, top_k=TOP_K),
        grid=(n_tiles,),
        in_specs=[pl.BlockSpec((t, d), row), pl.BlockSpec((1, 6, d), modi),
                  pl.BlockSpec((d, n_exp), lambda i: (0, 0)), pl.BlockSpec((1, n_exp), lambda i: (0, 0))],
        out_specs=[pl.BlockSpec((t, TOP_K), tok), pl.BlockSpec((SUBLANES, t), tok), pl.BlockSpec((t, TOP_K), tok),
                   pl.BlockSpec((1, 1, n_exp), lambda i: (i, 0, 0))],
        out_shape=[jax.ShapeDtypeStruct((n_tok, TOP_K), jnp.int32), jax.ShapeDtypeStruct((n_tiles * SUBLANES, t), F32),
                   jax.ShapeDtypeStruct((n_tok, TOP_K), F32), jax.ShapeDtypeStruct((n_tiles, 1, n_exp), F32)],
        compiler_params=_params("arbitrary"),
        name="moe_router",
    )(x, mod, router_w, router_b.reshape(1, n_exp))

    i32 = jnp.int32
    count = cnt.reshape(n_tiles, n_exp).astype(i32)
    cap = (count + SEG_ALIGN - 1) // SEG_ALIGN * SEG_ALIGN
    e_rows = jnp.sum(cap, axis=0)
    e_pad = (e_rows + MOE_BLOCK - 1) // MOE_BLOCK * MOE_BLOCK
    e_end = jnp.cumsum(e_pad)
    e_start = e_end - e_pad
    seg_global = e_start[None, :] + jnp.cumsum(cap, axis=0) - cap
    seg_local = jnp.cumsum(cap, axis=1) - cap
    fill = jnp.zeros((n_tiles, LANES - 3 * n_exp), i32)
    seg = jnp.concatenate([seg_global, cap, seg_local, fill], axis=1).astype(i32).reshape(n_tiles, 1, LANES)
    tail = jnp.concatenate([e_start + e_rows, e_pad - e_rows, jnp.zeros((LANES - 2 * n_exp,), i32)])
    tail = tail.astype(i32).reshape(1, 1, LANES)
    max_slots = n_tok * TOP_K + n_tiles * n_exp * (SEG_ALIGN - 1) + n_exp * (MOE_BLOCK - 1)
    n_blocks = (max_slots + MOE_BLOCK - 1) // MOE_BLOCK
    n_slots = n_blocks * MOE_BLOCK
    n_used = (e_end[-1] // MOE_BLOCK).astype(i32)
    blk = jnp.minimum(jnp.arange(n_blocks, dtype=i32), n_used - 1) * MOE_BLOCK
    block_expert = jnp.minimum(jnp.searchsorted(e_end, blk, side="right"), n_exp - 1).astype(i32)

    smem_tile = lambda index_map: pl.BlockSpec((1, 1, LANES), index_map, memory_space=pltpu.SMEM)
    xs = pl.pallas_call(
        functools.partial(_dispatch_kernel, n_exp=n_exp, top_k=TOP_K),
        grid=(n_tiles,),
        in_specs=[smem_tile(lambda i: (0, 0, 0)), smem_tile(lambda i: (i, 0, 0)),
                  smem_tile(lambda i: (jnp.maximum(i - 1, 0), 0, 0)),
                  pl.BlockSpec((t, d), row), pl.BlockSpec((1, 6, d), modi), pl.BlockSpec((SUBLANES, t), tok)],
        out_specs=pl.BlockSpec(memory_space=pl.ANY),
        out_shape=jax.ShapeDtypeStruct((n_slots, d), F32),
        scratch_shapes=[pltpu.VMEM((2, seg_rows, d), F32), pltpu.VMEM((MOE_BLOCK, d), F32),
                        pltpu.SemaphoreType.DMA((2,)), pltpu.SemaphoreType.DMA(())],
        compiler_params=_params("arbitrary"),
        name="moe_dispatch",
    )(tail, seg, seg, x, mod, slot_t)

    bg = b_gu[:, 0::2].reshape(n_exp, 1, d_ff)
    bu = b_gu[:, 1::2].reshape(n_exp, 1, d_ff)
    used = lambda i, be, nu: (jnp.minimum(i, nu[0] - 1), 0)
    wmap = lambda i, be, nu: (be[i], 0, 0)
    wmap_all = lambda i, be, nu: (be[i] + layer * n_exp, 0, 0)
    y = pl.pallas_call(
        _expert_kernel,
        grid_spec=pltpu.PrefetchScalarGridSpec(
            num_scalar_prefetch=2,
            grid=(n_blocks,),
            in_specs=[pl.BlockSpec((MOE_BLOCK, d), used),
                      pl.BlockSpec((1, d, d_ff), wmap_all), pl.BlockSpec((1, d, d_ff), wmap_all),
                      pl.BlockSpec((1, 1, d_ff), wmap), pl.BlockSpec((1, 1, d_ff), wmap),
                      pl.BlockSpec((1, d_ff, d), wmap_all), pl.BlockSpec((1, 1, d), wmap)],
            out_specs=pl.BlockSpec((MOE_BLOCK, d), used)),
        out_shape=jax.ShapeDtypeStruct((n_slots, d), F32),
        compiler_params=_params("arbitrary"),
        name="moe_experts",
    )(block_expert, n_used.reshape(1), xs, wg, wu, bg, bu, wd, b_down.reshape(n_exp, 1, d))

    return pl.pallas_call(
        functools.partial(_combine_kernel, n_exp=n_exp, top_k=TOP_K, alpha=alpha),
        grid=(n_tiles,),
        in_specs=[smem_tile(lambda i: (i, 0, 0)), smem_tile(lambda i: (jnp.minimum(i + 1, n_tiles - 1), 0, 0)),
                  pl.BlockSpec((t, d), row), pl.BlockSpec((1, 6, d), modi),
                  pl.BlockSpec((t, TOP_K), tok), pl.BlockSpec((t, TOP_K), tok),
                  pl.BlockSpec((1, d), lambda i: (0, 0)), pl.BlockSpec((1, d), lambda i: (0, 0)),
                  pl.BlockSpec(memory_space=pl.ANY)],
        out_specs=pl.BlockSpec((t, d), row),
        out_shape=jax.ShapeDtypeStruct(x.shape, F32),
        scratch_shapes=[pltpu.VMEM((2, seg_rows, d), F32), pltpu.SemaphoreType.DMA((2,))],
        input_output_aliases={2: 0},
        compiler_params=_params("arbitrary"),
        name="moe_combine_ln",
    )(seg, seg, x, mod, gate, slot, ln_g.reshape(1, d), ln_b.reshape(1, d), y)


def kernel(x, c, ctx, c_ctx, ada_w, ada_b, ln1_g, ln1_b, ln2_g, ln2_b, router_w, router_b, exp_gu_w, exp_gu_b, exp_down_w, exp_down_b, rg_w_in, rg_conv_w, rg_conv_b, rg_gate_a_w, rg_gate_a_b, rg_gate_x_w, rg_gate_x_b, rg_lambda, rg_w_out, gqa_w_qkv, gqa_b_qkv, gqa_sinks, gqa_w_o, gqa_b_o, mla_w_down, mla_q_norm, mla_kv_norm, mla_w_uq, mla_w_ukv, mla_w_o):
    batch, seq, d = x.shape
    n_ctx = ctx.shape[1]
    depth = ada_w.shape[0]
    alpha = (2 * depth) ** 0.25
    rows = _Rows(batch, n_ctx, seq, ROW_TILE)
    xa = jnp.concatenate([ctx.reshape(batch * n_ctx, d), x.reshape(batch * seq, d)], axis=0)
    mods = _ada_table(jnp.concatenate([c_ctx[None, :], c], axis=0), ada_w, ada_b)
    wg, wu = _split_gate_up(exp_gu_w)
    wd = exp_down_w.reshape((-1,) + exp_down_w.shape[2:]).astype(BF16)
    for i in range(depth):
        need_ctx = i < depth - 1
        kind, j = i % 3, i // 3
        mod = mods[i]
        if kind == 0:
            prm = dict(w_in=rg_w_in[j], conv_w=rg_conv_w[j], conv_b=rg_conv_b[j], gate_a_w=rg_gate_a_w[j],
                       gate_a_b=rg_gate_a_b[j], gate_x_w=rg_gate_x_w[j], gate_x_b=rg_gate_x_b[j],
                       lam=rg_lambda[j], w_out=rg_w_out[j])
            xa = _rglru_layer(rows, xa, mod, prm, ln1_g[i], ln1_b[i], alpha, need_ctx)
        elif kind == 1:
            prm = dict(w_qkv=gqa_w_qkv[j], b_qkv=gqa_b_qkv[j], sinks=gqa_sinks[j], w_o=gqa_w_o[j], b_o=gqa_b_o[j])
            xa = _gqa_layer(rows, xa, mod, prm, ln1_g[i], ln1_b[i], alpha, need_ctx)
        else:
            prm = dict(w_down=mla_w_down[j], q_norm=mla_q_norm[j], kv_norm=mla_kv_norm[j], w_uq=mla_w_uq[j],
                       w_ukv=mla_w_ukv[j], w_o=mla_w_o[j])
            xa = _mla_layer(rows, xa, mod, prm, ln1_g[i], ln1_b[i], alpha, need_ctx)
        xa = _moe_layer(rows, xa, mod, router_w[i], router_b[i], wg, wu, wd, i, exp_gu_b[i], exp_down_b[i],
                        ln2_g[i], ln2_b[i], alpha, need_ctx)
    return xa[batch * n_ctx:].reshape(batch, seq, d)
```

```python
import functools
import math

import jax
import jax.numpy as jnp
from jax import lax
from jax.experimental import pallas as pl
from jax.experimental.pallas import tpu as pltpu

F32 = jnp.float32
BF16 = jnp.bfloat16

GRID_W = 64
LN_EPS = 1e-5
RMS_EPS = 1e-6
ROPE_THETA = 10000.0
NEG_INF = -1e30
RG_C = 8.0
GQA_KV = 2
GQA_HD = 64
WINDOW = 128
MLA_HEADS = 16
QK_NOPE = 64
QK_ROPE = 32
V_HD = 64
TOP_K = 4
SWIGLU_LIMIT = 7.0
SWIGLU_ALPHA = 1.702
MOE_BLOCK = 512
MOE_ROWS = 256

LANES = 128
SUBLANES = 8
BF16_ROWS = 16
ROW_TILE = 256
ATTN_Q = 128
GQA_LOOKAHEAD = 4
MLA_TQ = 512
MLA_TK = 256
MLA_SUB_K = 128
MLA_SUB_Q = 256
MLA_LOOKAHEAD = 16
MXU_DIM = 256
SEG_ALIGN = SUBLANES
SEG_PIECES = (8, 16, 32)
VMEM_LIMIT = 56 * 1024 * 1024


def _params(*sem):
    return pltpu.CompilerParams(dimension_semantics=sem, vmem_limit_bytes=VMEM_LIMIT)


def _bdot(a, b):
    return jnp.dot(a.astype(BF16), b.astype(BF16), preferred_element_type=F32)


def _bdot_nt(a, b):
    return lax.dot_general(a.astype(BF16), b.astype(BF16), (((1,), (1,)), ((), ())),
                           preferred_element_type=F32)


def _layer_norm(r, g, b):
    mu = jnp.mean(r, axis=-1, keepdims=True)
    d = r - mu
    var = jnp.mean(d * d, axis=-1, keepdims=True)
    return d * lax.rsqrt(var + LN_EPS) * g + b


def _gelu_tanh(x):
    return 0.5 * x * (1.0 + jnp.tanh(math.sqrt(2.0 / math.pi) * (x + 0.044715 * (x * x * x))))


class _Rows:
    def __init__(self, batch, n_ctx, seq, tile):
        assert n_ctx % tile == 0 and seq % tile == 0
        self.batch, self.n_ctx, self.seq, self.tile = batch, n_ctx, seq, tile
        self.ctx_tiles = batch * n_ctx // tile
        self.lat_tiles = batch * seq // tile
        self.tiles = self.ctx_tiles + self.lat_tiles
        self.lat_per_batch = seq // tile
        self.ctx_per_batch = n_ctx // tile
        self.rows = batch * (n_ctx + seq)

    def mod_index(self, i):
        return jnp.where(i < self.ctx_tiles, 0, 1 + (i - self.ctx_tiles) // self.lat_per_batch)

    def rope_index(self, i):
        return jnp.where(i < self.ctx_tiles, self.lat_per_batch, (i - self.ctx_tiles) % self.lat_per_batch)


def _ada_kernel(c_ref, w_ref, b_ref, o_ref):
    cv = c_ref[...]
    s = cv * jax.nn.sigmoid(cv)
    o_ref[0] = jnp.dot(s, w_ref[0], preferred_element_type=F32,
                       precision=lax.Precision.HIGHEST) + b_ref[0]


def _ada_table(cvec, ada_w, ada_b):
    depth, d, d6 = ada_w.shape
    n = cvec.shape[0]
    chunk = d
    out = pl.pallas_call(
        _ada_kernel,
        grid=(depth, d6 // chunk),
        in_specs=[pl.BlockSpec((n, d), lambda l, j: (0, 0)),
                  pl.BlockSpec((1, d, chunk), lambda l, j: (l, 0, j)),
                  pl.BlockSpec((1, 1, chunk), lambda l, j: (l, 0, j))],
        out_specs=pl.BlockSpec((1, n, chunk), lambda l, j: (l, 0, j)),
        out_shape=jax.ShapeDtypeStruct((depth, n, d6), F32),
        compiler_params=_params("arbitrary", "arbitrary"),
        name="ada_table",
    )(cvec, ada_w, ada_b.reshape(depth, 1, d6))
    return out.reshape(depth, n, 6, d)


def _out_ln_kernel(z_ref, w_ref, b_ref, x_ref, mod_ref, g_ref, bt_ref, o_ref, *, gate_row, alpha):
    y = _bdot(z_ref[...], w_ref[...]) + b_ref[...]
    m = mod_ref[0]
    r = alpha * x_ref[...] + m[gate_row:gate_row + 1] * y
    o_ref[...] = _layer_norm(r, g_ref[...], bt_ref[...])


def _out_ln(rows, z, w, bias, x, mod, ln_g, ln_b, *, alpha, first_tile=0):
    t, d = rows.tile, x.shape[1]
    kdim = z.shape[1]
    n_tiles = rows.tiles - first_tile
    row = lambda i: (i + first_tile, 0)
    return pl.pallas_call(
        functools.partial(_out_ln_kernel, gate_row=2, alpha=alpha),
        grid=(n_tiles,),
        in_specs=[pl.BlockSpec((t, kdim), row),
                  pl.BlockSpec((kdim, d), lambda i: (0, 0)),
                  pl.BlockSpec((1, d), lambda i: (0, 0)),
                  pl.BlockSpec((t, d), row),
                  pl.BlockSpec((1, 6, d), lambda i: (rows.mod_index(i + first_tile), 0, 0)),
                  pl.BlockSpec((1, d), lambda i: (0, 0)),
                  pl.BlockSpec((1, d), lambda i: (0, 0))],
        out_specs=pl.BlockSpec((t, d), row),
        out_shape=jax.ShapeDtypeStruct(x.shape, F32),
        input_output_aliases={3: 0},
        compiler_params=_params("arbitrary"),
        name="out_proj_ln",
    )(z, w.astype(BF16), bias.reshape(1, d), x, mod, ln_g.reshape(1, d), ln_b.reshape(1, d))


def _rg_in_kernel(x_ref, mod_ref, w_ref, gel_ref, rec_ref, *, d_rnn):
    m = mod_ref[0]
    h = (x_ref[...] * (1.0 + m[1:2]) + m[0:1]).astype(BF16)
    gel_ref[...] = _gelu_tanh(_bdot(h, w_ref[:, :d_rnn])).astype(BF16)
    rec_ref[...] = _bdot(h, w_ref[:, d_rnn:])


def _rg_in(rows, x, mod, w_in):
    t, d = rows.tile, x.shape[1]
    d_rnn = w_in.shape[1] // 2
    return pl.pallas_call(
        functools.partial(_rg_in_kernel, d_rnn=d_rnn),
        grid=(rows.tiles,),
        in_specs=[pl.BlockSpec((t, d), lambda i: (i, 0)),
                  pl.BlockSpec((1, 6, d), lambda i: (rows.mod_index(i), 0, 0)),
                  pl.BlockSpec((d, 2 * d_rnn), lambda i: (0, 0))],
        out_specs=[pl.BlockSpec((t, d_rnn), lambda i: (i, 0)),
                   pl.BlockSpec((t, d_rnn), lambda i: (i, 0))],
        out_shape=[jax.ShapeDtypeStruct((rows.rows, d_rnn), BF16),
                   jax.ShapeDtypeStruct((rows.rows, d_rnn), F32)],
        compiler_params=_params("arbitrary"),
        name="rg_in_proj",
    )(x, mod, w_in.astype(BF16))


def _rg_scan_kernel(*refs, reverse, fuse_out, n_blocks, block_w, tile, ctx_tiles, lat_tiles):
    if fuse_out:
        (x_ref, xp_ref, xn_ref, cw_ref, cb_ref, wa_ref, ba_ref, wx_ref, bx_ref, lam_ref,
         hf_ref, gel_ref, out_ref, a_scr, u_scr, h_scr, carry_scr) = refs
    else:
        (x_ref, xp_ref, xn_ref, cw_ref, cb_ref, wa_ref, ba_ref, wx_ref, bx_ref, lam_ref,
         out_ref, a_scr, u_scr, carry_scr) = refs
        h_scr = out_ref
    j = pl.program_id(1)
    is_ctx = j < ctx_tiles
    n_seq = jnp.where(is_ctx, ctx_tiles, lat_tiles)
    step = jnp.where(is_ctx, j, j - ctx_tiles)
    pos = (n_seq - 1 - step) if reverse else step
    prev_ok = (pos > 0).astype(F32)
    next_ok = (pos < n_seq - 1).astype(F32)

    @pl.when(j == 0)
    def _():
        carry_scr[...] = jnp.zeros_like(carry_scr)

    row = lax.broadcasted_iota(jnp.int32, (tile, block_w), 0)
    for n in range(n_blocks):
        cols = slice(n * block_w, (n + 1) * block_w)
        x = x_ref[:, cols]
        prev = xp_ref[SUBLANES - 1:SUBLANES, cols] * prev_ok
        nxt0 = xn_ref[0:1, cols] * next_ok
        nxt1 = xn_ref[1:2, cols] * next_ok
        x_m1 = jnp.where(row == 0, prev, pltpu.roll(x, 1, 0))
        x_p1 = jnp.where(row == tile - 1, nxt0, pltpu.roll(x, tile - 1, 0))
        x_p2 = jnp.where(row == tile - 2, nxt0, jnp.where(row == tile - 1, nxt1, pltpu.roll(x, tile - 2, 0)))
        xc = (cw_ref[0:1, cols] * x_m1 + cw_ref[1:2, cols] * x + cw_ref[2:3, cols] * x_p1
              + cw_ref[3:4, cols] * x_p2 + cb_ref[:, cols])
        xb = xc.astype(BF16)
        r = jax.nn.sigmoid(_bdot(xb, wa_ref[n]) + ba_ref[:, cols])
        gi = jax.nn.sigmoid(_bdot(xb, wx_ref[n]) + bx_ref[:, cols])
        z = -lam_ref[:, cols]
        softplus = jnp.maximum(z, 0.0) + jnp.log1p(jnp.exp(-jnp.abs(z)))
        log_a = (-RG_C) * r * softplus
        a = jnp.exp(log_a)
        a_scr[:, cols] = a
        u_scr[:, cols] = jnp.sqrt(1.0 - a * a) * (gi * xc)

    width = n_blocks * block_w
    sub = lax.broadcasted_iota(jnp.int32, (SUBLANES, width), 0)
    groups = tile // SUBLANES

    def body(g, carry):
        gg = (groups - 1 - g) if reverse else g
        r0 = pl.multiple_of(gg * SUBLANES, SUBLANES)
        a8 = a_scr[pl.ds(r0, SUBLANES), :]
        u8 = u_scr[pl.ds(r0, SUBLANES), :]
        for s in (1, 2, 4):
            if reverse:
                a_sh, u_sh, ok = pltpu.roll(a8, SUBLANES - s, 0), pltpu.roll(u8, SUBLANES - s, 0), sub < SUBLANES - s
            else:
                a_sh, u_sh, ok = pltpu.roll(a8, s, 0), pltpu.roll(u8, s, 0), sub >= s
            u8 = jnp.where(ok, a8 * u_sh + u8, u8)
            a8 = jnp.where(ok, a8 * a_sh, a8)
        h8 = a8 * carry + u8
        h_scr[pl.ds(r0, SUBLANES), :] = h8
        return h8[0:1, :] if reverse else h8[SUBLANES - 1:SUBLANES, :]

    carry_scr[...] = lax.fori_loop(0, groups, body, carry_scr[...])
    if fuse_out:
        out_ref[...] = ((hf_ref[...] + h_scr[...]) * gel_ref[...].astype(F32)).astype(BF16)


def _rg_scan(rows, rec, conv_w, conv_b, wa, ba, wx, bx, lam, *, reverse, h_fwd=None, gel=None):
    t = rows.tile
    c = rec.shape[1]
    n_blocks, block_w = wa.shape[0], wa.shape[1]
    nc, nl = rows.ctx_per_batch, rows.lat_per_batch
    halo = t // SUBLANES
    last_halo = rows.rows // SUBLANES - 1
    fuse_out = h_fwd is not None

    def tile_index(b, j):
        is_ctx = j < nc
        step = jnp.where(is_ctx, j, j - nc)
        n_seq = jnp.where(is_ctx, nc, nl)
        pos = (n_seq - 1 - step) if reverse else step
        return jnp.where(is_ctx, b * nc + pos, rows.ctx_tiles + b * nl + pos)

    cur = lambda b, j: (tile_index(b, j), 0)
    prv = lambda b, j: (jnp.maximum(tile_index(b, j) * halo - 1, 0), 0)
    nxt = lambda b, j: (jnp.minimum((tile_index(b, j) + 1) * halo, last_halo), 0)
    full2 = lambda b, j: (0, 0)
    full3 = lambda b, j: (0, 0, 0)
    in_specs = [pl.BlockSpec((t, c), cur), pl.BlockSpec((SUBLANES, c), prv), pl.BlockSpec((SUBLANES, c), nxt),
                pl.BlockSpec((4, c), full2), pl.BlockSpec((1, c), full2),
                pl.BlockSpec((n_blocks, block_w, block_w), full3), pl.BlockSpec((1, c), full2),
                pl.BlockSpec((n_blocks, block_w, block_w), full3), pl.BlockSpec((1, c), full2),
                pl.BlockSpec((1, c), full2)]
    args = [rec, rec, rec, conv_w, conv_b.reshape(1, c), wa.astype(BF16), ba.reshape(1, c),
            wx.astype(BF16), bx.reshape(1, c), lam.reshape(1, c)]
    scratch = [pltpu.VMEM((t, c), F32), pltpu.VMEM((t, c), F32)]
    if fuse_out:
        in_specs += [pl.BlockSpec((t, c), cur), pl.BlockSpec((t, c), cur)]
        args += [h_fwd, gel]
        scratch.append(pltpu.VMEM((t, c), F32))
    scratch.append(pltpu.VMEM((1, c), F32))
    return pl.pallas_call(
        functools.partial(_rg_scan_kernel, reverse=reverse, fuse_out=fuse_out, n_blocks=n_blocks,
                          block_w=block_w, tile=t, ctx_tiles=nc, lat_tiles=nl),
        grid=(rows.batch, nc + nl),
        in_specs=in_specs,
        out_specs=pl.BlockSpec((t, c), cur),
        out_shape=jax.ShapeDtypeStruct((rows.rows, c), BF16 if fuse_out else F32),
        scratch_shapes=scratch,
        compiler_params=_params("arbitrary", "arbitrary"),
        name="rg_scan_bwd" if reverse else "rg_scan_fwd",
    )(*args)


def _rglru_layer(rows, x, mod, p, ln_g, ln_b, alpha, need_ctx):
    gel, rec = _rg_in(rows, x, mod, p["w_in"])
    h_fwd = _rg_scan(rows, rec, p["conv_w"], p["conv_b"], p["gate_a_w"][0], p["gate_a_b"][0],
                     p["gate_x_w"][0], p["gate_x_b"][0], p["lam"][0], reverse=False)
    z = _rg_scan(rows, rec, p["conv_w"], p["conv_b"], p["gate_a_w"][1], p["gate_a_b"][1],
                 p["gate_x_w"][1], p["gate_x_b"][1], p["lam"][1], reverse=True, h_fwd=h_fwd, gel=gel)
    d = x.shape[1]
    return _out_ln(rows, z, p["w_out"], jnp.zeros((d,), F32), x, mod, ln_g, ln_b, alpha=alpha,
                   first_tile=0 if need_ctx else rows.ctx_tiles)


def _axial_angles(seq, rot_dim):
    pos = jnp.arange(seq, dtype=jnp.int32)
    row = (pos // GRID_W).astype(F32)
    col = (pos % GRID_W).astype(F32)
    n_freq = rot_dim // 4
    inv_freq = ROPE_THETA ** (-jnp.arange(n_freq, dtype=F32) / n_freq)
    return jnp.concatenate([row[:, None] * inv_freq, col[:, None] * inv_freq], axis=-1)


def _rope_tables(rows, rot_dim, lead, trail):
    ang = _axial_angles(rows.seq, rot_dim)
    cos, sin = jnp.cos(ang), jnp.sin(ang)
    ones = lambda w: jnp.ones((rows.seq, w), F32)
    zeros = lambda w: jnp.zeros((rows.seq, w), F32)
    c = jnp.concatenate([ones(lead), cos, cos, ones(trail)], axis=-1)
    s = jnp.concatenate([zeros(lead), -sin, sin, zeros(trail)], axis=-1)
    reps = LANES // c.shape[1]
    c, s = jnp.tile(c, (1, reps)), jnp.tile(s, (1, reps))
    t = rows.tile
    c = jnp.concatenate([c.reshape(rows.lat_per_batch, t, LANES), jnp.ones((1, t, LANES), F32)], axis=0)
    s = jnp.concatenate([s.reshape(rows.lat_per_batch, t, LANES), jnp.zeros((1, t, LANES), F32)], axis=0)
    return c, s


def _rope_chunk(x, cos, sin, half):
    lane = lax.broadcasted_iota(jnp.int32, x.shape, 1)
    partner = jnp.where((lane % (2 * half)) < half, pltpu.roll(x, LANES - half, 1), pltpu.roll(x, half, 1))
    return x * cos + partner * sin


def _gqa_proj_kernel(x_ref, mod_ref, w_ref, b_ref, cos_ref, sin_ref, q_ref, kv_ref, *, q_dim, scale, half):
    m = mod_ref[0]
    h = (x_ref[...] * (1.0 + m[1:2]) + m[0:1]).astype(BF16)
    cos, sin = cos_ref[0], sin_ref[0]
    n_q = q_dim // LANES
    n_all = w_ref.shape[1] // LANES
    for c in range(n_all):
        cols = slice(c * LANES, (c + 1) * LANES)
        p = _bdot(h, w_ref[:, cols]) + b_ref[:, cols]
        is_v = c in (n_q + 1, n_q + 3)
        if not is_v:
            p = _rope_chunk(p, cos, sin, half)
        if c < n_q:
            q_ref[:, cols] = (p * scale).astype(BF16)
        else:
            kv_ref[:, (c - n_q) * LANES:(c - n_q + 1) * LANES] = p.astype(BF16)


def _gqa_attn_kernel(*refs, windowed, window, seq, n_pairs, pairs_per_kv):
    if windowed:
        q_ref, kv_ref, kvc_ref, sink_ref, o_ref = refs
    else:
        q_ref, kvc_ref, sink_ref, _, o_ref = refs
    tq = q_ref.shape[0]
    kvc = kvc_ref[...]
    if windowed:
        span = tq + 2 * window
        qs = pl.program_id(1) * tq
        ws = pl.multiple_of(jnp.clip(qs - window, 0, seq - span), LANES)
        kv = jnp.concatenate([kv_ref[pl.ds(ws, span), :], kvc], axis=0)
        n_keys = kv.shape[0]
        qpos = qs + lax.broadcasted_iota(jnp.int32, (tq, n_keys), 0)
        col = lax.broadcasted_iota(jnp.int32, (tq, n_keys), 1)
        mask = (jnp.abs(ws + col - qpos) <= window) | (col >= span)
    else:
        kv = kvc
        mask = None
    lane = lax.broadcasted_iota(jnp.int32, (kv.shape[0], LANES), 1)
    low = lane < GQA_HD
    zero = jnp.zeros((kv.shape[0], LANES), BF16)
    one_hi = (lane == GQA_HD).astype(F32).astype(BF16)
    one_lo = (lane == 0).astype(F32).astype(BF16)
    low_q = lax.broadcasted_iota(jnp.int32, (tq, LANES), 1) < GQA_HD
    k_plain, v_plain = kv[:, 0:LANES], kv[:, LANES:2 * LANES]
    k_swap, v_swap = kv[:, 2 * LANES:3 * LANES], kv[:, 3 * LANES:4 * LANES]
    k_half, v_half = {}, {}
    for g in range(GQA_KV):
        k_half[g, 0] = jnp.where(low, k_plain if g == 0 else k_swap, zero)
        k_half[g, 1] = jnp.where(low, zero, k_swap if g == 0 else k_plain)
        v_half[g, 0] = jnp.where(low, v_plain if g == 0 else v_swap, one_hi)
        v_half[g, 1] = jnp.where(low, one_lo, v_swap if g == 0 else v_plain)
    items = [(pr, hh) for pr in range(n_pairs) for hh in range(2)]

    def scores(item):
        pr, hh = item
        return _bdot_nt(q_ref[:, pr * LANES:(pr + 1) * LANES], k_half[pr // pairs_per_kv, hh])

    ahead = [scores(it) for it in items[:GQA_LOOKAHEAD]]
    acc = None
    for n, (pr, hh) in enumerate(items):
        s = ahead.pop(0)
        if n + GQA_LOOKAHEAD < len(items):
            ahead.append(scores(items[n + GQA_LOOKAHEAD]))
        head = 2 * pr + hh
        if mask is not None:
            s = jnp.where(mask, s, NEG_INF)
        sk = sink_ref[head:head + 1, 0:1]
        mx = jnp.maximum(jnp.max(s, axis=1, keepdims=True), sk)
        p = jnp.exp((s - mx).astype(BF16))
        pv = _bdot(p, v_half[pr // pairs_per_kv, hh])
        sum_lane = GQA_HD if hh == 0 else 0
        denom = pv[:, sum_lane:sum_lane + 1] + jnp.exp(sk - mx)
        part = pv * (1.0 / denom)
        if hh == 1:
            o_ref[:, pr * LANES:(pr + 1) * LANES] = jnp.where(low_q, acc, part).astype(BF16)
        acc = part


def _gqa_layer(rows, x, mod, p, ln_g, ln_b, alpha, need_ctx):
    t, d = rows.tile, x.shape[1]
    w_qkv, b_qkv = p["w_qkv"], p["b_qkv"]
    kv_dim = GQA_KV * GQA_HD
    q_dim = w_qkv.shape[1] - 2 * kv_dim
    n_heads = q_dim // GQA_HD
    assert kv_dim == LANES and GQA_KV == 2 and q_dim % LANES == 0
    swap = lambda a: jnp.concatenate([a[..., GQA_HD:], a[..., :GQA_HD]], axis=-1)
    wk, wv = w_qkv[:, q_dim:q_dim + kv_dim], w_qkv[:, q_dim + kv_dim:]
    bk, bv = b_qkv[q_dim:q_dim + kv_dim], b_qkv[q_dim + kv_dim:]
    w_ext = jnp.concatenate([w_qkv, swap(wk), swap(wv)], axis=1).astype(BF16)
    b_ext = jnp.concatenate([b_qkv, swap(bk), swap(bv)]).reshape(1, -1)
    n_ext = w_ext.shape[1]
    cos, sin = _rope_tables(rows, GQA_HD, 0, 0)
    q, kv = pl.pallas_call(
        functools.partial(_gqa_proj_kernel, q_dim=q_dim, scale=GQA_HD ** -0.5, half=GQA_HD // 2),
        grid=(rows.tiles,),
        in_specs=[pl.BlockSpec((t, d), lambda i: (i, 0)),
                  pl.BlockSpec((1, 6, d), lambda i: (rows.mod_index(i), 0, 0)),
                  pl.BlockSpec((d, n_ext), lambda i: (0, 0)),
                  pl.BlockSpec((1, n_ext), lambda i: (0, 0)),
                  pl.BlockSpec((1, t, LANES), lambda i: (rows.rope_index(i), 0, 0)),
                  pl.BlockSpec((1, t, LANES), lambda i: (rows.rope_index(i), 0, 0))],
        out_specs=[pl.BlockSpec((t, q_dim), lambda i: (i, 0)),
                   pl.BlockSpec((t, 4 * LANES), lambda i: (i, 0))],
        out_shape=[jax.ShapeDtypeStruct((rows.rows, q_dim), BF16),
                   jax.ShapeDtypeStruct((rows.rows, 4 * LANES), BF16)],
        compiler_params=_params("arbitrary"),
        name="gqa_qkv_proj",
    )(x, mod, w_ext, b_ext, cos, sin)

    sinks = jnp.broadcast_to(p["sinks"].astype(F32)[:, None], (n_heads, LANES))
    b_, s_, n_ctx = rows.batch, rows.seq, rows.n_ctx
    ctx_rows = b_ * n_ctx
    assert ctx_rows % s_ == 0 and s_ >= ATTN_Q + 2 * WINDOW
    q_blocks = s_ // ATTN_Q
    n_pairs = q_dim // LANES
    common = dict(window=WINDOW, seq=s_, n_pairs=n_pairs, pairs_per_kv=n_pairs // GQA_KV)
    o_shape = jax.ShapeDtypeStruct((rows.rows, q_dim), BF16)
    o_lat = pl.pallas_call(
        functools.partial(_gqa_attn_kernel, windowed=True, **common),
        grid=(b_, q_blocks),
        in_specs=[pl.BlockSpec((ATTN_Q, q_dim), lambda b, j: (ctx_rows // ATTN_Q + b * q_blocks + j, 0)),
                  pl.BlockSpec((s_, 4 * LANES), lambda b, j: (ctx_rows // s_ + b, 0)),
                  pl.BlockSpec((n_ctx, 4 * LANES), lambda b, j: (b, 0)),
                  pl.BlockSpec((n_heads, LANES), lambda b, j: (0, 0))],
        out_specs=pl.BlockSpec((ATTN_Q, q_dim), lambda b, j: (ctx_rows // ATTN_Q + b * q_blocks + j, 0)),
        out_shape=o_shape,
        compiler_params=_params("arbitrary", "arbitrary"),
        name="gqa_window_attn",
    )(q, kv, kv, sinks)
    if need_ctx:
        cq_blocks = n_ctx // ATTN_Q
        o = pl.pallas_call(
            functools.partial(_gqa_attn_kernel, windowed=False, **common),
            grid=(b_, cq_blocks),
            in_specs=[pl.BlockSpec((ATTN_Q, q_dim), lambda b, j: (b * cq_blocks + j, 0)),
                      pl.BlockSpec((n_ctx, 4 * LANES), lambda b, j: (b, 0)),
                      pl.BlockSpec((n_heads, LANES), lambda b, j: (0, 0)),
                      pl.BlockSpec(memory_space=pl.ANY)],
            out_specs=pl.BlockSpec((ATTN_Q, q_dim), lambda b, j: (b * cq_blocks + j, 0)),
            out_shape=o_shape,
            input_output_aliases={3: 0},
            compiler_params=_params("arbitrary", "arbitrary"),
            name="gqa_ctx_attn",
        )(q, kv, sinks, o_lat)
    else:
        o = o_lat
    return _out_ln(rows, o, p["w_o"], p["b_o"], x, mod, ln_g, ln_b, alpha=alpha,
                   first_tile=0 if need_ctx else rows.ctx_tiles)


def _mla_proj_kernel(x_ref, mod_ref, wd_ref, qn_ref, kvn_ref, wq_ref, wk_ref, wv_ref, cos_ref, sin_ref,
                     q_ref, k_ref, vt_ref, *, q_lora, kv_lora, scale, n_heads):
    m = mod_ref[0]
    h = (x_ref[...] * (1.0 + m[1:2]) + m[0:1]).astype(BF16)
    p = _bdot(h, wd_ref[...])
    cq, ckv = p[:, :q_lora], p[:, q_lora:q_lora + kv_lora]
    cos, sin = cos_ref[0], sin_ref[0]
    k_rope = _rope_chunk(p[:, q_lora + kv_lora:], cos, sin, QK_ROPE // 2)
    cq = (cq * lax.rsqrt(jnp.mean(cq * cq, axis=-1, keepdims=True) + RMS_EPS) * qn_ref[...]).astype(BF16)
    ckv = (ckv * lax.rsqrt(jnp.mean(ckv * ckv, axis=-1, keepdims=True) + RMS_EPS) * kvn_ref[...]).astype(BF16)
    for hd in range(n_heads):
        cols = slice(hd * LANES, (hd + 1) * LANES)
        qh = _rope_chunk(_bdot(cq, wq_ref[:, cols]), cos, sin, QK_ROPE // 2)
        q_ref[:, cols] = (qh * scale).astype(BF16)
        k_ref[:, cols] = (_bdot(ckv, wk_ref[:, cols]) + k_rope).astype(BF16)
    vt_ref[...] = _bdot(ckv, wv_ref[...]).T.astype(BF16)


def _mla_attn_kernel(*refs, n_heads):
    q_ref, k_ref, vt_ref = refs[:3]
    o_ref, m_scr, l_scr, acc_scr = refs[-4:]
    kt = pl.program_id(2)

    @pl.when(kt == 0)
    def _():
        m_scr[...] = jnp.full_like(m_scr, -jnp.inf)
        l_scr[...] = jnp.zeros_like(l_scr)
        acc_scr[...] = jnp.zeros_like(acc_scr)

    tq, tk = q_ref.shape[0], k_ref.shape[0]
    sub_q, sub_k = min(MLA_SUB_Q, tq), min(MLA_SUB_K, tk)
    n_kh = tk // sub_k
    ones = jnp.ones((BF16_ROWS, sub_k), BF16)
    items = [(hd, kh, qh) for hd in range(n_heads) for kh in range(n_kh) for qh in range(tq // sub_q)]

    def scores(item):
        hd, kh, qh = item
        cols = slice(hd * LANES, (hd + 1) * LANES)
        s = _bdot_nt(k_ref[kh * sub_k:(kh + 1) * sub_k, cols], q_ref[qh * sub_q:(qh + 1) * sub_q, cols])
        return s, jnp.max(s, axis=0, keepdims=True)

    ahead = [scores(it) for it in items[:MLA_LOOKAHEAD]]
    state = {}
    for n, (hd, kh, qh) in enumerate(items):
        rws = slice(hd * V_HD, (hd + 1) * V_HD)
        qcols = slice(qh * sub_q, (qh + 1) * sub_q)
        s_t, s_max = ahead.pop(0)
        if n + MLA_LOOKAHEAD < len(items):
            ahead.append(scores(items[n + MLA_LOOKAHEAD]))
        if kh == 0:
            state[hd, qh] = (m_scr[hd:hd + 1, qcols], l_scr[hd:hd + 1, qcols], acc_scr[rws, qcols])
        m_old, l_old, acc = state[hd, qh]
        m_new = jnp.maximum(m_old, s_max)
        p_t = jnp.exp2((s_t - m_new).astype(BF16))
        corr = jnp.exp2(m_old - m_new)
        vt = vt_ref[rws, kh * sub_k:(kh + 1) * sub_k]
        pv = _bdot(jnp.concatenate([vt, ones], axis=0), p_t)
        state[hd, qh] = (m_new, corr * l_old + pv[V_HD:V_HD + 1, :], acc * corr + pv[:V_HD, :])
        if kh == n_kh - 1:
            m_scr[hd:hd + 1, qcols], l_scr[hd:hd + 1, qcols], acc_scr[rws, qcols] = state.pop((hd, qh))

    @pl.when(kt == pl.num_programs(2) - 1)
    def _():
        for hd in range(n_heads):
            rws = slice(hd * V_HD, (hd + 1) * V_HD)
            acc_scr[rws, :] = acc_scr[rws, :] * (1.0 / l_scr[hd:hd + 1, :])
        o_ref[...] = acc_scr[...].T.astype(BF16)


def _mla_layer(rows, x, mod, p, ln_g, ln_b, alpha, need_ctx):
    t, d = rows.tile, x.shape[1]
    h_ = MLA_HEADS
    w_down, w_uq, w_ukv = p["w_down"], p["w_uq"], p["w_ukv"]
    q_lora = w_uq.shape[0]
    kv_lora = w_ukv.shape[0]
    qk = QK_NOPE + QK_ROPE
    assert QK_NOPE == V_HD == LANES // 2 and h_ % 2 == 0 and q_lora % LANES == 0 and kv_lora % LANES == 0
    pad = LANES - qk
    zc = lambda r, w: jnp.zeros((r, w), F32)
    wd_p = jnp.concatenate([w_down[:, :q_lora + kv_lora], zc(d, QK_NOPE), w_down[:, q_lora + kv_lora:],
                            zc(d, pad)], axis=1).astype(BF16)
    wq_p = jnp.concatenate([w_uq.reshape(q_lora, h_, qk), jnp.zeros((q_lora, h_, pad), F32)],
                           axis=-1).reshape(q_lora, h_ * LANES).astype(BF16)
    ukv = w_ukv.reshape(kv_lora, h_, QK_NOPE + V_HD)
    wk_p = jnp.concatenate([ukv[..., :QK_NOPE], jnp.zeros((kv_lora, h_, LANES - QK_NOPE), F32)],
                           axis=-1).reshape(kv_lora, h_ * LANES).astype(BF16)
    wv_p = ukv[..., QK_NOPE:].reshape(kv_lora, h_ * V_HD).astype(BF16)
    cos, sin = _rope_tables(rows, QK_ROPE, QK_NOPE, pad)
    n_down = wd_p.shape[1]
    q, k, vt = pl.pallas_call(
        functools.partial(_mla_proj_kernel, q_lora=q_lora, kv_lora=kv_lora, scale=qk ** -0.5 * math.log2(math.e),
                          n_heads=h_),
        grid=(rows.tiles,),
        in_specs=[pl.BlockSpec((t, d), lambda i: (i, 0)),
                  pl.BlockSpec((1, 6, d), lambda i: (rows.mod_index(i), 0, 0)),
                  pl.BlockSpec((d, n_down), lambda i: (0, 0)),
                  pl.BlockSpec((1, q_lora), lambda i: (0, 0)),
                  pl.BlockSpec((1, kv_lora), lambda i: (0, 0)),
                  pl.BlockSpec((q_lora, h_ * LANES), lambda i: (0, 0)),
                  pl.BlockSpec((kv_lora, h_ * LANES), lambda i: (0, 0)),
                  pl.BlockSpec((kv_lora, h_ * V_HD), lambda i: (0, 0)),
                  pl.BlockSpec((1, t, LANES), lambda i: (rows.rope_index(i), 0, 0)),
                  pl.BlockSpec((1, t, LANES), lambda i: (rows.rope_index(i), 0, 0))],
        out_specs=[pl.BlockSpec((t, h_ * LANES), lambda i: (i, 0)),
                   pl.BlockSpec((t, h_ * LANES), lambda i: (i, 0)),
                   pl.BlockSpec((h_ * V_HD, t), lambda i: (0, i))],
        out_shape=[jax.ShapeDtypeStruct((rows.rows, h_ * LANES), BF16),
                   jax.ShapeDtypeStruct((rows.rows, h_ * LANES), BF16),
                   jax.ShapeDtypeStruct((h_ * V_HD, rows.rows), BF16)],
        compiler_params=_params("arbitrary"),
        name="mla_proj",
    )(x, mod, wd_p, p["q_norm"].reshape(1, -1), p["kv_norm"].reshape(1, -1), wq_p, wk_p, wv_p, cos, sin)

    b_, s_, n_ctx = rows.batch, rows.seq, rows.n_ctx
    tk = MLA_TK
    tq, tq_ctx = min(MLA_TQ, s_), min(MLA_TQ, n_ctx)
    assert n_ctx % tk == 0 and s_ % tk == 0 and n_ctx % tq_ctx == 0 and s_ % tq == 0
    ck, lk = n_ctx // tk, s_ // tk
    ctx_kblocks = b_ * ck

    def kv_block(b, kt):
        return jnp.where(kt < ck, b * ck + kt, ctx_kblocks + b * lk + (kt - ck))

    def scratch(rows_q):
        return [pltpu.VMEM((h_, rows_q), F32), pltpu.VMEM((h_, rows_q), F32), pltpu.VMEM((h_ * V_HD, rows_q), F32)]

    o_shape = jax.ShapeDtypeStruct((rows.rows, h_ * V_HD), BF16)
    lq = s_ // tq
    ctx_qblocks = b_ * n_ctx // tq
    o_lat = pl.pallas_call(
        functools.partial(_mla_attn_kernel, n_heads=h_),
        grid=(b_, lq, ck + lk),
        in_specs=[pl.BlockSpec((tq, h_ * LANES), lambda b, i, kt: (ctx_qblocks + b * lq + i, 0)),
                  pl.BlockSpec((tk, h_ * LANES), lambda b, i, kt: (kv_block(b, kt), 0)),
                  pl.BlockSpec((h_ * V_HD, tk), lambda b, i, kt: (0, kv_block(b, kt)))],
        out_specs=pl.BlockSpec((tq, h_ * V_HD), lambda b, i, kt: (ctx_qblocks + b * lq + i, 0)),
        out_shape=o_shape,
        scratch_shapes=scratch(tq),
        compiler_params=_params("arbitrary", "arbitrary", "arbitrary"),
        name="mla_attn",
    )(q, k, vt)
    if need_ctx:
        cq = n_ctx // tq_ctx
        o = pl.pallas_call(
            functools.partial(_mla_attn_kernel, n_heads=h_),
            grid=(b_, cq, ck),
            in_specs=[pl.BlockSpec((tq_ctx, h_ * LANES), lambda b, i, kt: (b * cq + i, 0)),
                      pl.BlockSpec((tk, h_ * LANES), lambda b, i, kt: (b * ck + kt, 0)),
                      pl.BlockSpec((h_ * V_HD, tk), lambda b, i, kt: (0, b * ck + kt)),
                      pl.BlockSpec(memory_space=pl.ANY)],
            out_specs=pl.BlockSpec((tq_ctx, h_ * V_HD), lambda b, i, kt: (b * cq + i, 0)),
            out_shape=o_shape,
            scratch_shapes=scratch(tq_ctx),
            input_output_aliases={3: 0},
            compiler_params=_params("arbitrary", "arbitrary", "arbitrary"),
            name="mla_ctx_attn",
        )(q, k, vt, o_lat)
    else:
        o = o_lat
    return _out_ln(rows, o, p["w_o"], jnp.zeros((d,), F32), x, mod, ln_g, ln_b, alpha=alpha,
                   first_tile=0 if need_ctx else rows.ctx_tiles)


def _split_gu_kernel(w_ref, g_ref, u_ref):
    win = 2 * MXU_DIM
    r_i = lax.broadcasted_iota(jnp.int32, (win, MXU_DIM), 0)
    c_i = lax.broadcasted_iota(jnp.int32, (win, MXU_DIM), 1)
    pick_even = (r_i == 2 * c_i).astype(BF16)
    pick_odd = (r_i == 2 * c_i + 1).astype(BF16)
    for j in range(w_ref.shape[2] // win):
        w = w_ref[0, :, j * win:(j + 1) * win].astype(BF16)
        g_ref[0, :, j * MXU_DIM:(j + 1) * MXU_DIM] = _bdot(w, pick_even).astype(BF16)
        u_ref[0, :, j * MXU_DIM:(j + 1) * MXU_DIM] = _bdot(w, pick_odd).astype(BF16)


def _split_gate_up(w_gu):
    depth, n_exp, d, f2 = w_gu.shape
    w = w_gu.reshape(depth * n_exp, d, f2)
    tr = ROW_TILE
    spec_out = pl.BlockSpec((1, tr, f2 // 2), lambda e, r: (e, r, 0))
    shape_out = jax.ShapeDtypeStruct((depth * n_exp, d, f2 // 2), BF16)
    return pl.pallas_call(
        _split_gu_kernel,
        grid=(depth * n_exp, d // tr),
        in_specs=[pl.BlockSpec((1, tr, f2), lambda e, r: (e, r, 0))],
        out_specs=[spec_out, spec_out],
        out_shape=[shape_out, shape_out],
        compiler_params=_params("arbitrary", "arbitrary"),
        name="moe_split_gate_up",
    )(w)


def _seg_rows(t, n_exp):
    return t * TOP_K + n_exp * SEG_ALIGN


def _router_kernel(x_ref, mod_ref, rw_ref, rb_ref, slot_ref, slot_t_ref, gate_ref, cnt_ref, *, top_k):
    m = mod_ref[0]
    h = x_ref[...] * (1.0 + m[4:5]) + m[3:4]
    w = rw_ref[...]
    h_hi, w_hi = h.astype(BF16), w.astype(BF16)
    h_lo = (h - h_hi.astype(F32)).astype(BF16)
    w_lo = (w - w_hi.astype(F32)).astype(BF16)
    logits = _bdot(h_hi, w_hi) + (_bdot(h_hi, w_lo) + _bdot(h_lo, w_hi)) + rb_ref[...]
    t, n_exp = logits.shape
    lane = lax.broadcasted_iota(jnp.int32, (t, n_exp), 1).astype(F32)
    work = logits
    sel, val = [], []
    for _ in range(top_k):
        mx = jnp.max(work, axis=1, keepdims=True)
        pick = jnp.min(jnp.where(work == mx, lane, float(n_exp)), axis=1, keepdims=True)
        sel.append(pick)
        val.append(mx)
        work = jnp.where(lane == pick, -jnp.inf, work)
    ex = [jnp.exp(v - val[0]) for v in val]
    tot = ex[0]
    for e in ex[1:]:
        tot = tot + e
    onehot = jnp.zeros((t, n_exp), F32)
    for pick in sel:
        onehot = onehot + (lane == pick).astype(F32)
    r_i = lax.broadcasted_iota(jnp.int32, (t, t), 0)
    c_i = lax.broadcasted_iota(jnp.int32, (t, t), 1)
    before = _bdot((c_i < r_i).astype(BF16), onehot)
    count = jnp.sum(onehot, axis=0, keepdims=True)
    cap = jnp.floor((count + (SEG_ALIGN - 1)) * (1.0 / SEG_ALIGN)) * SEG_ALIGN
    e_r = lax.broadcasted_iota(jnp.int32, (n_exp, n_exp), 0)
    e_c = lax.broadcasted_iota(jnp.int32, (n_exp, n_exp), 1)
    seg_start = _bdot(cap, (e_r < e_c).astype(BF16))
    base = before + seg_start
    col = lax.broadcasted_iota(jnp.int32, (t, top_k), 1)
    wide_lane = lax.broadcasted_iota(jnp.int32, (t, LANES), 1)
    slot = jnp.zeros((t, top_k), jnp.int32)
    gate = jnp.zeros((t, top_k), F32)
    wide = jnp.zeros((t, LANES), F32)
    for k in range(top_k):
        sk = jnp.sum(jnp.where(lane == sel[k], base, 0.0), axis=1, keepdims=True)
        slot = jnp.where(col == k, sk.astype(jnp.int32), slot)
        gate = jnp.where(col == k, ex[k] / tot, gate)
        wide = jnp.where(wide_lane == k, sk, wide)
    slot_ref[...] = slot
    gate_ref[...] = gate
    slot_t_ref[...] = wide.T[:SUBLANES, :]
    cnt_ref[0] = count


def _segment_copies(seg_ref, n_exp, make_copy, act):
    *small, big = SEG_PIECES
    for e in range(n_exp):
        g0, rows, l0 = seg_ref[0, 0, e], seg_ref[0, 0, n_exp + e], seg_ref[0, 0, 2 * n_exp + e]

        def whole(j, carry, g0=g0, l0=l0):
            act(make_copy(pl.multiple_of(l0 + j * big, SEG_ALIGN), pl.multiple_of(g0 + j * big, SEG_ALIGN), big))
            return carry

        lax.fori_loop(0, lax.shift_right_logical(rows, big.bit_length() - 1), whole, 0)
        for size in small:
            done = rows & (-2 * size)

            @pl.when((rows & size) != 0)
            def _(g0=g0, l0=l0, done=done, size=size):
                act(make_copy(pl.multiple_of(l0 + done, SEG_ALIGN), pl.multiple_of(g0 + done, SEG_ALIGN), size))


def _dispatch_kernel(tail_ref, seg_ref, seg_prev_ref, x_ref, mod_ref, slot_t_ref, xs_ref, buf, zbuf, sem, zsem, *,
                     n_exp, top_k):
    i = pl.program_id(0)
    cur = i % 2

    def to_slots(buf_slot, sem_slot):
        return lambda l, g, rows: pltpu.make_async_copy(buf.at[buf_slot, pl.ds(l, rows)], xs_ref.at[pl.ds(g, rows)],
                                                        sem.at[sem_slot])

    @pl.when(i == 0)
    def _():
        zbuf[...] = jnp.zeros_like(zbuf)
        zero_fill = lambda l, g, rows: pltpu.make_async_copy(zbuf.at[pl.ds(l, rows)], xs_ref.at[pl.ds(g, rows)], zsem)
        _segment_copies(tail_ref, n_exp, zero_fill, lambda c: c.start())
        _segment_copies(tail_ref, n_exp, zero_fill, lambda c: c.wait())

    m = mod_ref[0]
    h = (x_ref[...] * (1.0 + m[4:5]) + m[3:4]).astype(BF16)
    n_rows, t = buf.shape[1], x_ref.shape[0]
    row = lax.broadcasted_iota(jnp.int32, (n_rows, t), 0)
    slot_t = slot_t_ref[...].astype(jnp.int32)
    pick = row == slot_t[0:1, :]
    for k in range(1, top_k):
        pick = pick | (row == slot_t[k:k + 1, :])
    buf[cur] = _bdot(pick.astype(BF16), h)
    _segment_copies(seg_ref, n_exp, to_slots(cur, cur), lambda c: c.start())

    @pl.when(i > 0)
    def _():
        _segment_copies(seg_prev_ref, n_exp, to_slots(1 - cur, 1 - cur), lambda c: c.wait())

    @pl.when(i == pl.num_programs(0) - 1)
    def _():
        _segment_copies(seg_ref, n_exp, to_slots(cur, cur), lambda c: c.wait())


def _expert_kernel(be_ref, nu_ref, x_ref, wg_ref, wu_ref, bg_ref, bu_ref, wd_ref, bd_ref, y_ref):
    del be_ref

    @pl.when(pl.program_id(0) < nu_ref[0])
    def _():
        def gate_up(part):
            xb = x_ref[part * MOE_ROWS:(part + 1) * MOE_ROWS, :].astype(BF16)
            return _bdot(xb, wg_ref[0]), _bdot(xb, wu_ref[0])

        n_parts = x_ref.shape[0] // MOE_ROWS
        nxt = gate_up(0)
        for part in range(n_parts):
            g, u = nxt
            if part + 1 < n_parts:
                nxt = gate_up(part + 1)
            g = jnp.minimum(g + bg_ref[0], SWIGLU_LIMIT)
            u = jnp.clip(u + bu_ref[0], -SWIGLU_LIMIT, SWIGLU_LIMIT)
            act = (u + 1.0) * (g * jax.nn.sigmoid(SWIGLU_ALPHA * g))
            y_ref[part * MOE_ROWS:(part + 1) * MOE_ROWS, :] = _bdot(act, wd_ref[0]) + bd_ref[0]


def _combine_kernel(seg_ref, seg_next_ref, x_ref, mod_ref, gate_ref, slot_ref, g_ref, bt_ref, y_ref, o_ref,
                    ybuf, sem, *, n_exp, top_k, alpha):
    i = pl.program_id(0)
    cur = i % 2

    def from_slots(buf_slot):
        return lambda l, g, rows: pltpu.make_async_copy(y_ref.at[pl.ds(g, rows)], ybuf.at[buf_slot, pl.ds(l, rows)],
                                                        sem.at[buf_slot])

    @pl.when(i == 0)
    def _():
        ybuf[...] = jnp.zeros_like(ybuf)
        _segment_copies(seg_ref, n_exp, from_slots(0), lambda c: c.start())

    @pl.when(i < pl.num_programs(0) - 1)
    def _():
        _segment_copies(seg_next_ref, n_exp, from_slots(1 - cur), lambda c: c.start())

    _segment_copies(seg_ref, n_exp, from_slots(cur), lambda c: c.wait())
    t, n_rows = x_ref.shape[0], ybuf.shape[1]
    lane = lax.broadcasted_iota(jnp.int32, (t, n_rows), 1)
    gate, slot = gate_ref[...], slot_ref[...]
    weights = jnp.zeros((t, n_rows), F32)
    for k in range(top_k):
        weights = jnp.where(lane == slot[:, k:k + 1], gate[:, k:k + 1], weights)
    acc = _bdot(weights, ybuf[cur])
    m = mod_ref[0]
    r = alpha * x_ref[...] + m[5:6] * acc
    o_ref[...] = _layer_norm(r, g_ref[...], bt_ref[...])


def _moe_layer(rows, x, mod, router_w, router_b, wg, wu, wd, layer, b_gu, b_down, ln_g, ln_b, alpha, need_ctx):
    t, d = rows.tile, x.shape[1]
    n_exp = router_w.shape[1]
    d_ff = wd.shape[1]
    first = 0 if need_ctx else rows.ctx_tiles
    n_tiles = rows.tiles - first
    n_tok = n_tiles * t
    row = lambda i: (i + first, 0)
    modi = lambda i: (rows.mod_index(i + first), 0, 0)
    tok = lambda i: (i, 0)

    seg_rows = _seg_rows(t, n_exp)
    assert 4 * n_exp <= LANES and t % SEG_PIECES[-1] == 0 and MOE_BLOCK % SEG_PIECES[-1] == 0
    slot, slot_t, gate, cnt = pl.pallas_call(
        functools.partial(_router_kernel, top_k=TOP_K),
        grid=(n_tiles,),
        in_specs=[pl.BlockSpec((t, d), row), pl.BlockSpec((1, 6, d), modi),
                  pl.BlockSpec((d, n_exp), lambda i: (0, 0)), pl.BlockSpec((1, n_exp), lambda i: (0, 0))],
        out_specs=[pl.BlockSpec((t, TOP_K), tok), pl.BlockSpec((SUBLANES, t), tok), pl.BlockSpec((t, TOP_K), tok),
                   pl.BlockSpec((1, 1, n_exp), lambda i: (i, 0, 0))],
        out_shape=[jax.ShapeDtypeStruct((n_tok, TOP_K), jnp.int32), jax.ShapeDtypeStruct((n_tiles * SUBLANES, t), F32),
                   jax.ShapeDtypeStruct((n_tok, TOP_K), F32), jax.ShapeDtypeStruct((n_tiles, 1, n_exp), F32)],
        compiler_params=_params("arbitrary"),
        name="moe_router",
    )(x, mod, router_w, router_b.reshape(1, n_exp))

    i32 = jnp.int32
    count = cnt.reshape(n_tiles, n_exp).astype(i32)
    cap = (count + SEG_ALIGN - 1) // SEG_ALIGN * SEG_ALIGN
    e_rows = jnp.sum(cap, axis=0)
    e_pad = (e_rows + MOE_BLOCK - 1) // MOE_BLOCK * MOE_BLOCK
    e_end = jnp.cumsum(e_pad)
    e_start = e_end - e_pad
    seg_global = e_start[None, :] + jnp.cumsum(cap, axis=0) - cap
    seg_local = jnp.cumsum(cap, axis=1) - cap
    fill = jnp.zeros((n_tiles, LANES - 3 * n_exp), i32)
    seg = jnp.concatenate([seg_global, cap, seg_local, fill], axis=1).astype(i32).reshape(n_tiles, 1, LANES)
    tail = jnp.concatenate([e_start + e_rows, e_pad - e_rows, jnp.zeros((LANES - 2 * n_exp,), i32)])
    tail = tail.astype(i32).reshape(1, 1, LANES)
    max_slots = n_tok * TOP_K + n_tiles * n_exp * (SEG_ALIGN - 1) + n_exp * (MOE_BLOCK - 1)
    n_blocks = (max_slots + MOE_BLOCK - 1) // MOE_BLOCK
    n_slots = n_blocks * MOE_BLOCK
    n_used = (e_end[-1] // MOE_BLOCK).astype(i32)
    blk = jnp.minimum(jnp.arange(n_blocks, dtype=i32), n_used - 1) * MOE_BLOCK
    block_expert = jnp.minimum(jnp.sum((e_end[None, :] <= blk[:, None]).astype(i32), axis=1), n_exp - 1)

    smem_tile = lambda index_map: pl.BlockSpec((1, 1, LANES), index_map, memory_space=pltpu.SMEM)
    xs = pl.pallas_call(
        functools.partial(_dispatch_kernel, n_exp=n_exp, top_k=TOP_K),
        grid=(n_tiles,),
        in_specs=[smem_tile(lambda i: (0, 0, 0)), smem_tile(lambda i: (i, 0, 0)),
                  smem_tile(lambda i: (jnp.maximum(i - 1, 0), 0, 0)),
                  pl.BlockSpec((t, d), row), pl.BlockSpec((1, 6, d), modi), pl.BlockSpec((SUBLANES, t), tok)],
        out_specs=pl.BlockSpec(memory_space=pl.ANY),
        out_shape=jax.ShapeDtypeStruct((n_slots, d), F32),
        scratch_shapes=[pltpu.VMEM((2, seg_rows, d), F32), pltpu.VMEM((MOE_BLOCK, d), F32),
                        pltpu.SemaphoreType.DMA((2,)), pltpu.SemaphoreType.DMA(())],
        compiler_params=_params("arbitrary"),
        name="moe_dispatch",
    )(tail, seg, seg, x, mod, slot_t)

    bg = b_gu[:, 0::2].reshape(n_exp, 1, d_ff)
    bu = b_gu[:, 1::2].reshape(n_exp, 1, d_ff)
    used = lambda i, be, nu: (jnp.minimum(i, nu[0] - 1), 0)
    wmap = lambda i, be, nu: (be[i], 0, 0)
    wmap_all = lambda i, be, nu: (be[i] + layer * n_exp, 0, 0)
    y = pl.pallas_call(
        _expert_kernel,
        grid_spec=pltpu.PrefetchScalarGridSpec(
            num_scalar_prefetch=2,
            grid=(n_blocks,),
            in_specs=[pl.BlockSpec((MOE_BLOCK, d), used),
                      pl.BlockSpec((1, d, d_ff), wmap_all), pl.BlockSpec((1, d, d_ff), wmap_all),
                      pl.BlockSpec((1, 1, d_ff), wmap), pl.BlockSpec((1, 1, d_ff), wmap),
                      pl.BlockSpec((1, d_ff, d), wmap_all), pl.BlockSpec((1, 1, d), wmap)],
            out_specs=pl.BlockSpec((MOE_BLOCK, d), used)),
        out_shape=jax.ShapeDtypeStruct((n_slots, d), F32),
        compiler_params=_params("arbitrary"),
        name="moe_experts",
    )(block_expert, n_used.reshape(1), xs, wg, wu, bg, bu, wd, b_down.reshape(n_exp, 1, d))

    return pl.pallas_call(
        functools.partial(_combine_kernel, n_exp=n_exp, top_k=TOP_K, alpha=alpha),
        grid=(n_tiles,),
        in_specs=[smem_tile(lambda i: (i, 0, 0)), smem_tile(lambda i: (jnp.minimum(i + 1, n_tiles - 1), 0, 0)),
                  pl.BlockSpec((t, d), row), pl.BlockSpec((1, 6, d), modi),
                  pl.BlockSpec((t, TOP_K), tok), pl.BlockSpec((t, TOP_K), tok),
                  pl.BlockSpec((1, d), lambda i: (0, 0)), pl.BlockSpec((1, d), lambda i: (0, 0)),
                  pl.BlockSpec(memory_space=pl.ANY)],
        out_specs=pl.BlockSpec((t, d), row),
        out_shape=jax.ShapeDtypeStruct(x.shape, F32),
        scratch_shapes=[pltpu.VMEM((2, seg_rows, d), F32), pltpu.SemaphoreType.DMA((2,))],
        input_output_aliases={2: 0},
        compiler_params=_params("arbitrary"),
        name="moe_combine_ln",
    )(seg, seg, x, mod, gate, slot, ln_g.reshape(1, d), ln_b.reshape(1, d), y)


def kernel(x, c, ctx, c_ctx, ada_w, ada_b, ln1_g, ln1_b, ln2_g, ln2_b, router_w, router_b, exp_gu_w, exp_gu_b, exp_down_w, exp_down_b, rg_w_in, rg_conv_w, rg_conv_b, rg_gate_a_w, rg_gate_a_b, rg_gate_x_w, rg_gate_x_b, rg_lambda, rg_w_out, gqa_w_qkv, gqa_b_qkv, gqa_sinks, gqa_w_o, gqa_b_o, mla_w_down, mla_q_norm, mla_kv_norm, mla_w_uq, mla_w_ukv, mla_w_o):
    batch, seq, d = x.shape
    n_ctx = ctx.shape[1]
    depth = ada_w.shape[0]
    alpha = (2 * depth) ** 0.25
    rows = _Rows(batch, n_ctx, seq, ROW_TILE)
    xa = jnp.concatenate([ctx.reshape(batch * n_ctx, d), x.reshape(batch * seq, d)], axis=0)
    mods = _ada_table(jnp.concatenate([c_ctx[None, :], c], axis=0), ada_w, ada_b)
    wg, wu = _split_gate_up(exp_gu_w)
    wd = exp_down_w.reshape((-1,) + exp_down_w.shape[2:]).astype(BF16)
    for i in range(depth):
        need_ctx = i < depth - 1
        kind, j = i % 3, i // 3
        mod = mods[i]
        if kind == 0:
            prm = dict(w_in=rg_w_in[j], conv_w=rg_conv_w[j], conv_b=rg_conv_b[j], gate_a_w=rg_gate_a_w[j],
                       gate_a_b=rg_gate_a_b[j], gate_x_w=rg_gate_x_w[j], gate_x_b=rg_gate_x_b[j],
                       lam=rg_lambda[j], w_out=rg_w_out[j])
            xa = _rglru_layer(rows, xa, mod, prm, ln1_g[i], ln1_b[i], alpha, need_ctx)
        elif kind == 1:
            prm = dict(w_qkv=gqa_w_qkv[j], b_qkv=gqa_b_qkv[j], sinks=gqa_sinks[j], w_o=gqa_w_o[j], b_o=gqa_b_o[j])
            xa = _gqa_layer(rows, xa, mod, prm, ln1_g[i], ln1_b[i], alpha, need_ctx)
        else:
            prm = dict(w_down=mla_w_down[j], q_norm=mla_q_norm[j], kv_norm=mla_kv_norm[j], w_uq=mla_w_uq[j],
                       w_ukv=mla_w_ukv[j], w_o=mla_w_o[j])
            xa = _mla_layer(rows, xa, mod, prm, ln1_g[i], ln1_b[i], alpha, need_ctx)
        xa = _moe_layer(rows, xa, mod, router_w[i], router_b[i], wg, wu, wd, i, exp_gu_b[i], exp_down_b[i],
                        ln2_g[i], ln2_b[i], alpha, need_ctx)
    return xa[batch * n_ctx:].reshape(batch, seq, d)
```

```python
import functools
import math

import jax
import jax.numpy as jnp
from jax import lax
from jax.experimental import pallas as pl
from jax.experimental.pallas import tpu as pltpu

F32 = jnp.float32
BF16 = jnp.bfloat16

GRID_W = 64
LN_EPS = 1e-5
RMS_EPS = 1e-6
ROPE_THETA = 10000.0
NEG_INF = -1e30
RG_C = 8.0
SQRT_FLOOR = 1e-30
GQA_KV = 2
GQA_HD = 64
WINDOW = 128
MLA_HEADS = 16
QK_NOPE = 64
QK_ROPE = 32
V_HD = 64
TOP_K = 4
SWIGLU_LIMIT = 7.0
SWIGLU_ALPHA = 1.702
MOE_BLOCK = 512
MOE_ROWS = 256

LANES = 128
SUBLANES = 8
BF16_ROWS = 16
ROW_TILE = 256
PROJ_TILE = 512
ATTN_Q = 128
GQA_LOOKAHEAD = 4
MLA_TQ = 1024
MLA_TK = 256
MLA_SUB_K = 128
MLA_SUB_Q = 256
MLA_LOOKAHEAD = 16
MXU_DIM = 256
SEG_ALIGN = SUBLANES
SEG_PIECES = (8, 16, 32)
VMEM_LIMIT = 56 * 1024 * 1024


def _params(*sem):
    return pltpu.CompilerParams(dimension_semantics=sem, vmem_limit_bytes=VMEM_LIMIT)


def _bdot(a, b):
    return jnp.dot(a.astype(BF16), b.astype(BF16), preferred_element_type=F32)


def _bdot_nt(a, b):
    return lax.dot_general(a.astype(BF16), b.astype(BF16), (((1,), (1,)), ((), ())),
                           preferred_element_type=F32)


def _layer_norm(r, g, b):
    mu = jnp.mean(r, axis=-1, keepdims=True)
    d = r - mu
    var = jnp.mean(d * d, axis=-1, keepdims=True)
    return d * lax.rsqrt(var + LN_EPS) * g + b


def _sigmoid(x):
    return 0.5 * jnp.tanh(0.5 * x) + 0.5


def _gelu_tanh(x):
    return 0.5 * x * (1.0 + jnp.tanh(math.sqrt(2.0 / math.pi) * (x + 0.044715 * (x * x * x))))


class _Rows:
    def __init__(self, batch, n_ctx, seq, tile):
        assert (batch * n_ctx) % tile == 0 and seq % tile == 0
        self.batch, self.n_ctx, self.seq, self.tile = batch, n_ctx, seq, tile
        self.ctx_tiles = batch * n_ctx // tile
        self.lat_tiles = batch * seq // tile
        self.tiles = self.ctx_tiles + self.lat_tiles
        self.lat_per_batch = seq // tile
        self.ctx_per_batch = n_ctx // tile
        self.rows = batch * (n_ctx + seq)

    def mod_index(self, i):
        return jnp.where(i < self.ctx_tiles, 0, 1 + (i - self.ctx_tiles) // self.lat_per_batch)

    def rope_index(self, i):
        return jnp.where(i < self.ctx_tiles, self.lat_per_batch, (i - self.ctx_tiles) % self.lat_per_batch)


def _ada_kernel(c_ref, w_ref, b_ref, o_ref):
    cv = c_ref[...]
    s = cv * jax.nn.sigmoid(cv)
    o_ref[0] = jnp.dot(s, w_ref[0], preferred_element_type=F32,
                       precision=lax.Precision.HIGHEST) + b_ref[0]


def _ada_table(cvec, ada_w, ada_b):
    depth, d, d6 = ada_w.shape
    n = cvec.shape[0]
    chunk = d
    out = pl.pallas_call(
        _ada_kernel,
        grid=(depth, d6 // chunk),
        in_specs=[pl.BlockSpec((n, d), lambda l, j: (0, 0)),
                  pl.BlockSpec((1, d, chunk), lambda l, j: (l, 0, j)),
                  pl.BlockSpec((1, 1, chunk), lambda l, j: (l, 0, j))],
        out_specs=pl.BlockSpec((1, n, chunk), lambda l, j: (l, 0, j)),
        out_shape=jax.ShapeDtypeStruct((depth, n, d6), F32),
        compiler_params=_params("arbitrary", "arbitrary"),
        name="ada_table",
    )(cvec, ada_w, ada_b.reshape(depth, 1, d6))
    return out.reshape(depth, n, 6, d)


def _out_ln_kernel(z_ref, w_ref, b_ref, x_ref, mod_ref, g_ref, bt_ref, o_ref, *, gate_row, alpha):
    y = _bdot(z_ref[...], w_ref[...]) + b_ref[...]
    m = mod_ref[0]
    r = alpha * x_ref[...] + m[gate_row:gate_row + 1] * y
    o_ref[...] = _layer_norm(r, g_ref[...], bt_ref[...])


def _out_ln(rows, z, w, bias, x, mod, ln_g, ln_b, *, alpha, first_tile=0):
    t, d = rows.tile, x.shape[1]
    kdim = z.shape[1]
    n_tiles = rows.tiles - first_tile
    row = lambda i: (i + first_tile, 0)
    return pl.pallas_call(
        functools.partial(_out_ln_kernel, gate_row=2, alpha=alpha),
        grid=(n_tiles,),
        in_specs=[pl.BlockSpec((t, kdim), row),
                  pl.BlockSpec((kdim, d), lambda i: (0, 0)),
                  pl.BlockSpec((1, d), lambda i: (0, 0)),
                  pl.BlockSpec((t, d), row),
                  pl.BlockSpec((1, 6, d), lambda i: (rows.mod_index(i + first_tile), 0, 0)),
                  pl.BlockSpec((1, d), lambda i: (0, 0)),
                  pl.BlockSpec((1, d), lambda i: (0, 0))],
        out_specs=pl.BlockSpec((t, d), row),
        out_shape=jax.ShapeDtypeStruct(x.shape, F32),
        input_output_aliases={3: 0},
        compiler_params=_params("arbitrary"),
        name="out_proj_ln",
    )(z, w.astype(BF16), bias.reshape(1, d), x, mod, ln_g.reshape(1, d), ln_b.reshape(1, d))


def _rg_in_kernel(x_ref, mod_ref, w_ref, gel_ref, rec_ref, *, d_rnn):
    m = mod_ref[0]
    h = (x_ref[...] * (1.0 + m[1:2]) + m[0:1]).astype(BF16)
    gel_ref[...] = _gelu_tanh(_bdot(h, w_ref[:, :d_rnn])).astype(BF16)
    rec_ref[...] = _bdot(h, w_ref[:, d_rnn:])


def _rg_in(rows, x, mod, w_in):
    t, d = rows.tile, x.shape[1]
    d_rnn = w_in.shape[1] // 2
    return pl.pallas_call(
        functools.partial(_rg_in_kernel, d_rnn=d_rnn),
        grid=(rows.tiles,),
        in_specs=[pl.BlockSpec((t, d), lambda i: (i, 0)),
                  pl.BlockSpec((1, 6, d), lambda i: (rows.mod_index(i), 0, 0)),
                  pl.BlockSpec((d, 2 * d_rnn), lambda i: (0, 0))],
        out_specs=[pl.BlockSpec((t, d_rnn), lambda i: (i, 0)),
                   pl.BlockSpec((t, d_rnn), lambda i: (i, 0))],
        out_shape=[jax.ShapeDtypeStruct((rows.rows, d_rnn), BF16),
                   jax.ShapeDtypeStruct((rows.rows, d_rnn), F32)],
        compiler_params=_params("arbitrary"),
        name="rg_in_proj",
    )(x, mod, w_in.astype(BF16))


def _rg_scan_kernel(*refs, reverse, fuse_out, n_blocks, block_w, tile, ctx_tiles, lat_tiles):
    if fuse_out:
        (x_ref, xp_ref, xn_ref, cw_ref, cb_ref, wa_ref, ba_ref, wx_ref, bx_ref, lam_ref,
         hf_ref, gel_ref, out_ref, a_scr, u_scr, h_scr, carry_scr) = refs
    else:
        (x_ref, xp_ref, xn_ref, cw_ref, cb_ref, wa_ref, ba_ref, wx_ref, bx_ref, lam_ref,
         out_ref, a_scr, u_scr, carry_scr) = refs
        h_scr = out_ref
    j = pl.program_id(1)
    is_ctx = j < ctx_tiles
    n_seq = jnp.where(is_ctx, ctx_tiles, lat_tiles)
    step = jnp.where(is_ctx, j, j - ctx_tiles)
    pos = (n_seq - 1 - step) if reverse else step
    prev_ok = (pos > 0).astype(F32)
    next_ok = (pos < n_seq - 1).astype(F32)

    @pl.when(j == 0)
    def _():
        carry_scr[...] = jnp.zeros_like(carry_scr)

    row8 = lax.broadcasted_iota(jnp.int32, (SUBLANES, block_w), 0)
    for n in range(n_blocks):
        cols = slice(n * block_w, (n + 1) * block_w)
        x = x_ref[:, cols]
        prev = xp_ref[SUBLANES - 1:SUBLANES, cols] * prev_ok
        nxt0 = xn_ref[0:1, cols] * next_ok
        nxt1 = xn_ref[1:2, cols] * next_ok
        cw = [cw_ref[k:k + 1, cols] for k in range(4)]
        xc = (cw[0] * pltpu.roll(x, 1, 0) + cw[1] * x + cw[2] * pltpu.roll(x, tile - 1, 0)
              + cw[3] * pltpu.roll(x, tile - 2, 0) + cb_ref[:, cols])
        first, last = x[0:1, :], x[tile - 1:tile, :]
        head = xc[:SUBLANES] + jnp.where(row8 == 0, cw[0] * (prev - last), 0.0)
        tail = xc[tile - SUBLANES:] + jnp.where(
            row8 == SUBLANES - 2, cw[3] * (nxt0 - first),
            jnp.where(row8 == SUBLANES - 1, cw[2] * (nxt0 - first) + cw[3] * (nxt1 - x[1:2, :]), 0.0))
        xc = jnp.concatenate([head, xc[SUBLANES:tile - SUBLANES], tail], axis=0)
        xb = xc.astype(BF16)
        r = _sigmoid(_bdot(xb, wa_ref[n]) + ba_ref[:, cols])
        gi = _sigmoid(_bdot(xb, wx_ref[n]) + bx_ref[:, cols])
        z = -lam_ref[:, cols]
        softplus = jnp.maximum(z, 0.0) + jnp.log1p(jnp.exp(-jnp.abs(z)))
        a = jnp.exp2(r * ((-RG_C * math.log2(math.e)) * softplus))
        a_scr[:, cols] = a
        v = 1.0 - a * a
        u_scr[:, cols] = (v * lax.rsqrt(jnp.maximum(v, SQRT_FLOOR))) * (gi * xc)

    width = n_blocks * block_w
    sub = lax.broadcasted_iota(jnp.int32, (SUBLANES, width), 0)
    groups = tile // SUBLANES

    def body(g, carry):
        gg = (groups - 1 - g) if reverse else g
        r0 = pl.multiple_of(gg * SUBLANES, SUBLANES)
        a8 = a_scr[pl.ds(r0, SUBLANES), :]
        u8 = u_scr[pl.ds(r0, SUBLANES), :]
        for s in (1, 2, 4):
            if reverse:
                a_sh, u_sh, ok = pltpu.roll(a8, SUBLANES - s, 0), pltpu.roll(u8, SUBLANES - s, 0), sub < SUBLANES - s
            else:
                a_sh, u_sh, ok = pltpu.roll(a8, s, 0), pltpu.roll(u8, s, 0), sub >= s
            u8 = jnp.where(ok, a8 * u_sh + u8, u8)
            a8 = jnp.where(ok, a8 * a_sh, a8)
        h8 = a8 * carry + u8
        h_scr[pl.ds(r0, SUBLANES), :] = h8
        return h8[0:1, :] if reverse else h8[SUBLANES - 1:SUBLANES, :]

    carry_scr[...] = lax.fori_loop(0, groups, body, carry_scr[...])
    if fuse_out:
        out_ref[...] = ((hf_ref[...] + h_scr[...]) * gel_ref[...].astype(F32)).astype(BF16)


def _rg_scan(rows, rec, conv_w, conv_b, wa, ba, wx, bx, lam, *, reverse, h_fwd=None, gel=None):
    t = rows.tile
    c = rec.shape[1]
    n_blocks, block_w = wa.shape[0], wa.shape[1]
    nc, nl = rows.ctx_per_batch, rows.lat_per_batch
    halo = t // SUBLANES
    last_halo = rows.rows // SUBLANES - 1
    fuse_out = h_fwd is not None

    def tile_index(b, j):
        is_ctx = j < nc
        step = jnp.where(is_ctx, j, j - nc)
        n_seq = jnp.where(is_ctx, nc, nl)
        pos = (n_seq - 1 - step) if reverse else step
        return jnp.where(is_ctx, b * nc + pos, rows.ctx_tiles + b * nl + pos)

    cur = lambda b, j: (tile_index(b, j), 0)
    prv = lambda b, j: (jnp.maximum(tile_index(b, j) * halo - 1, 0), 0)
    nxt = lambda b, j: (jnp.minimum((tile_index(b, j) + 1) * halo, last_halo), 0)
    full2 = lambda b, j: (0, 0)
    full3 = lambda b, j: (0, 0, 0)
    in_specs = [pl.BlockSpec((t, c), cur), pl.BlockSpec((SUBLANES, c), prv), pl.BlockSpec((SUBLANES, c), nxt),
                pl.BlockSpec((4, c), full2), pl.BlockSpec((1, c), full2),
                pl.BlockSpec((n_blocks, block_w, block_w), full3), pl.BlockSpec((1, c), full2),
                pl.BlockSpec((n_blocks, block_w, block_w), full3), pl.BlockSpec((1, c), full2),
                pl.BlockSpec((1, c), full2)]
    args = [rec, rec, rec, conv_w, conv_b.reshape(1, c), wa.astype(BF16), ba.reshape(1, c),
            wx.astype(BF16), bx.reshape(1, c), lam.reshape(1, c)]
    scratch = [pltpu.VMEM((t, c), F32), pltpu.VMEM((t, c), F32)]
    if fuse_out:
        in_specs += [pl.BlockSpec((t, c), cur), pl.BlockSpec((t, c), cur)]
        args += [h_fwd, gel]
        scratch.append(pltpu.VMEM((t, c), F32))
    scratch.append(pltpu.VMEM((1, c), F32))
    return pl.pallas_call(
        functools.partial(_rg_scan_kernel, reverse=reverse, fuse_out=fuse_out, n_blocks=n_blocks,
                          block_w=block_w, tile=t, ctx_tiles=nc, lat_tiles=nl),
        grid=(rows.batch, nc + nl),
        in_specs=in_specs,
        out_specs=pl.BlockSpec((t, c), cur),
        out_shape=jax.ShapeDtypeStruct((rows.rows, c), BF16 if fuse_out else F32),
        scratch_shapes=scratch,
        compiler_params=_params("arbitrary", "arbitrary"),
        name="rg_scan_bwd" if reverse else "rg_scan_fwd",
    )(*args)


def _rglru_layer(rows, prows, x, mod, p, ln_g, ln_b, alpha, need_ctx):
    gel, rec = _rg_in(prows, x, mod, p["w_in"])
    h_fwd = _rg_scan(rows, rec, p["conv_w"], p["conv_b"], p["gate_a_w"][0], p["gate_a_b"][0],
                     p["gate_x_w"][0], p["gate_x_b"][0], p["lam"][0], reverse=False)
    z = _rg_scan(rows, rec, p["conv_w"], p["conv_b"], p["gate_a_w"][1], p["gate_a_b"][1],
                 p["gate_x_w"][1], p["gate_x_b"][1], p["lam"][1], reverse=True, h_fwd=h_fwd, gel=gel)
    d = x.shape[1]
    return _out_ln(prows, z, p["w_out"], jnp.zeros((d,), F32), x, mod, ln_g, ln_b, alpha=alpha,
                   first_tile=0 if need_ctx else prows.ctx_tiles)


def _axial_angles(seq, rot_dim):
    pos = jnp.arange(seq, dtype=jnp.int32)
    row = (pos // GRID_W).astype(F32)
    col = (pos % GRID_W).astype(F32)
    n_freq = rot_dim // 4
    inv_freq = ROPE_THETA ** (-jnp.arange(n_freq, dtype=F32) / n_freq)
    return jnp.concatenate([row[:, None] * inv_freq, col[:, None] * inv_freq], axis=-1)


def _rope_tables(rows, rot_dim, lead, trail):
    ang = _axial_angles(rows.seq, rot_dim)
    cos, sin = jnp.cos(ang), jnp.sin(ang)
    ones = lambda w: jnp.ones((rows.seq, w), F32)
    zeros = lambda w: jnp.zeros((rows.seq, w), F32)
    c = jnp.concatenate([ones(lead), cos, cos, ones(trail)], axis=-1)
    s = jnp.concatenate([zeros(lead), -sin, sin, zeros(trail)], axis=-1)
    reps = LANES // c.shape[1]
    c, s = jnp.tile(c, (1, reps)), jnp.tile(s, (1, reps))
    t = rows.tile
    c = jnp.concatenate([c.reshape(rows.lat_per_batch, t, LANES), jnp.ones((1, t, LANES), F32)], axis=0)
    s = jnp.concatenate([s.reshape(rows.lat_per_batch, t, LANES), jnp.zeros((1, t, LANES), F32)], axis=0)
    return c, s


def _rope_chunk(x, cos, sin, half):
    lane = lax.broadcasted_iota(jnp.int32, x.shape, 1)
    partner = jnp.where((lane % (2 * half)) < half, pltpu.roll(x, LANES - half, 1), pltpu.roll(x, half, 1))
    return x * cos + partner * sin


def _gqa_proj_kernel(x_ref, mod_ref, w_ref, b_ref, cos_ref, sin_ref, q_ref, kv_ref, *, q_dim, scale, half):
    m = mod_ref[0]
    h = (x_ref[...] * (1.0 + m[1:2]) + m[0:1]).astype(BF16)
    cos, sin = cos_ref[0], sin_ref[0]
    n_q = q_dim // LANES
    n_all = w_ref.shape[1] // LANES
    for c in range(n_all):
        cols = slice(c * LANES, (c + 1) * LANES)
        p = _bdot(h, w_ref[:, cols]) + b_ref[:, cols]
        is_v = c in (n_q + 1, n_q + 3)
        if not is_v:
            p = _rope_chunk(p, cos, sin, half)
        if c < n_q:
            q_ref[:, cols] = (p * scale).astype(BF16)
        else:
            kv_ref[:, (c - n_q) * LANES:(c - n_q + 1) * LANES] = p.astype(BF16)


def _gqa_attn_kernel(*refs, windowed, window, seq, n_pairs, pairs_per_kv):
    if windowed:
        q_ref, kv_ref, kvc_ref, sink_ref, o_ref = refs
    else:
        q_ref, kvc_ref, sink_ref, _, o_ref = refs
    tq = q_ref.shape[0]
    kvc = kvc_ref[...]
    if windowed:
        span = tq + 2 * window
        qs = pl.program_id(1) * tq
        ws = pl.multiple_of(jnp.clip(qs - window, 0, seq - span), LANES)
        kv = jnp.concatenate([kv_ref[pl.ds(ws, span), :], kvc], axis=0)
        n_keys = kv.shape[0]
        qpos = qs + lax.broadcasted_iota(jnp.int32, (tq, n_keys), 0)
        col = lax.broadcasted_iota(jnp.int32, (tq, n_keys), 1)
        mask = (jnp.abs(ws + col - qpos) <= window) | (col >= span)
    else:
        kv = kvc
        mask = None
    lane = lax.broadcasted_iota(jnp.int32, (kv.shape[0], LANES), 1)
    low = lane < GQA_HD
    zero = jnp.zeros((kv.shape[0], LANES), BF16)
    one_hi = (lane == GQA_HD).astype(F32).astype(BF16)
    one_lo = (lane == 0).astype(F32).astype(BF16)
    low_q = lax.broadcasted_iota(jnp.int32, (tq, LANES), 1) < GQA_HD
    k_plain, v_plain = kv[:, 0:LANES], kv[:, LANES:2 * LANES]
    k_swap, v_swap = kv[:, 2 * LANES:3 * LANES], kv[:, 3 * LANES:4 * LANES]
    k_half, v_half = {}, {}
    for g in range(GQA_KV):
        k_half[g, 0] = jnp.where(low, k_plain if g == 0 else k_swap, zero)
        k_half[g, 1] = jnp.where(low, zero, k_swap if g == 0 else k_plain)
        v_half[g, 0] = jnp.where(low, v_plain if g == 0 else v_swap, one_hi)
        v_half[g, 1] = jnp.where(low, one_lo, v_swap if g == 0 else v_plain)
    items = [(pr, hh) for pr in range(n_pairs) for hh in range(2)]

    def scores(item):
        pr, hh = item
        return _bdot_nt(q_ref[:, pr * LANES:(pr + 1) * LANES], k_half[pr // pairs_per_kv, hh])

    ahead = [scores(it) for it in items[:GQA_LOOKAHEAD]]
    acc = None
    for n, (pr, hh) in enumerate(items):
        s = ahead.pop(0)
        if n + GQA_LOOKAHEAD < len(items):
            ahead.append(scores(items[n + GQA_LOOKAHEAD]))
        head = 2 * pr + hh
        if mask is not None:
            s = jnp.where(mask, s, NEG_INF)
        sk = sink_ref[head:head + 1, 0:1]
        mx = jnp.maximum(jnp.max(s, axis=1, keepdims=True), sk)
        p = jnp.exp((s - mx).astype(BF16))
        pv = _bdot(p, v_half[pr // pairs_per_kv, hh])
        sum_lane = GQA_HD if hh == 0 else 0
        denom = pv[:, sum_lane:sum_lane + 1] + jnp.exp(sk - mx)
        part = pv * (1.0 / denom)
        if hh == 1:
            o_ref[:, pr * LANES:(pr + 1) * LANES] = jnp.where(low_q, acc, part).astype(BF16)
        acc = part


def _gqa_layer(rows, x, mod, p, ln_g, ln_b, alpha, need_ctx):
    t, d = rows.tile, x.shape[1]
    w_qkv, b_qkv = p["w_qkv"], p["b_qkv"]
    kv_dim = GQA_KV * GQA_HD
    q_dim = w_qkv.shape[1] - 2 * kv_dim
    n_heads = q_dim // GQA_HD
    assert kv_dim == LANES and GQA_KV == 2 and q_dim % LANES == 0
    swap = lambda a: jnp.concatenate([a[..., GQA_HD:], a[..., :GQA_HD]], axis=-1)
    wk, wv = w_qkv[:, q_dim:q_dim + kv_dim], w_qkv[:, q_dim + kv_dim:]
    bk, bv = b_qkv[q_dim:q_dim + kv_dim], b_qkv[q_dim + kv_dim:]
    w_ext = jnp.concatenate([w_qkv, swap(wk), swap(wv)], axis=1).astype(BF16)
    b_ext = jnp.concatenate([b_qkv, swap(bk), swap(bv)]).reshape(1, -1)
    n_ext = w_ext.shape[1]
    cos, sin = _rope_tables(rows, GQA_HD, 0, 0)
    q, kv = pl.pallas_call(
        functools.partial(_gqa_proj_kernel, q_dim=q_dim, scale=GQA_HD ** -0.5, half=GQA_HD // 2),
        grid=(rows.tiles,),
        in_specs=[pl.BlockSpec((t, d), lambda i: (i, 0)),
                  pl.BlockSpec((1, 6, d), lambda i: (rows.mod_index(i), 0, 0)),
                  pl.BlockSpec((d, n_ext), lambda i: (0, 0)),
                  pl.BlockSpec((1, n_ext), lambda i: (0, 0)),
                  pl.BlockSpec((1, t, LANES), lambda i: (rows.rope_index(i), 0, 0)),
                  pl.BlockSpec((1, t, LANES), lambda i: (rows.rope_index(i), 0, 0))],
        out_specs=[pl.BlockSpec((t, q_dim), lambda i: (i, 0)),
                   pl.BlockSpec((t, 4 * LANES), lambda i: (i, 0))],
        out_shape=[jax.ShapeDtypeStruct((rows.rows, q_dim), BF16),
                   jax.ShapeDtypeStruct((rows.rows, 4 * LANES), BF16)],
        compiler_params=_params("arbitrary"),
        name="gqa_qkv_proj",
    )(x, mod, w_ext, b_ext, cos, sin)

    sinks = jnp.broadcast_to(p["sinks"].astype(F32)[:, None], (n_heads, LANES))
    b_, s_, n_ctx = rows.batch, rows.seq, rows.n_ctx
    ctx_rows = b_ * n_ctx
    assert ctx_rows % s_ == 0 and s_ >= ATTN_Q + 2 * WINDOW
    q_blocks = s_ // ATTN_Q
    n_pairs = q_dim // LANES
    common = dict(window=WINDOW, seq=s_, n_pairs=n_pairs, pairs_per_kv=n_pairs // GQA_KV)
    o_shape = jax.ShapeDtypeStruct((rows.rows, q_dim), BF16)
    o_lat = pl.pallas_call(
        functools.partial(_gqa_attn_kernel, windowed=True, **common),
        grid=(b_, q_blocks),
        in_specs=[pl.BlockSpec((ATTN_Q, q_dim), lambda b, j: (ctx_rows // ATTN_Q + b * q_blocks + j, 0)),
                  pl.BlockSpec((s_, 4 * LANES), lambda b, j: (ctx_rows // s_ + b, 0)),
                  pl.BlockSpec((n_ctx, 4 * LANES), lambda b, j: (b, 0)),
                  pl.BlockSpec((n_heads, LANES), lambda b, j: (0, 0))],
        out_specs=pl.BlockSpec((ATTN_Q, q_dim), lambda b, j: (ctx_rows // ATTN_Q + b * q_blocks + j, 0)),
        out_shape=o_shape,
        compiler_params=_params("arbitrary", "arbitrary"),
        name="gqa_window_attn",
    )(q, kv, kv, sinks)
    if need_ctx:
        cq_blocks = n_ctx // ATTN_Q
        o = pl.pallas_call(
            functools.partial(_gqa_attn_kernel, windowed=False, **common),
            grid=(b_, cq_blocks),
            in_specs=[pl.BlockSpec((ATTN_Q, q_dim), lambda b, j: (b * cq_blocks + j, 0)),
                      pl.BlockSpec((n_ctx, 4 * LANES), lambda b, j: (b, 0)),
                      pl.BlockSpec((n_heads, LANES), lambda b, j: (0, 0)),
                      pl.BlockSpec(memory_space=pl.ANY)],
            out_specs=pl.BlockSpec((ATTN_Q, q_dim), lambda b, j: (b * cq_blocks + j, 0)),
            out_shape=o_shape,
            input_output_aliases={3: 0},
            compiler_params=_params("arbitrary", "arbitrary"),
            name="gqa_ctx_attn",
        )(q, kv, sinks, o_lat)
    else:
        o = o_lat
    return _out_ln(rows, o, p["w_o"], p["b_o"], x, mod, ln_g, ln_b, alpha=alpha,
                   first_tile=0 if need_ctx else rows.ctx_tiles)


def _mla_proj_kernel(x_ref, mod_ref, wd_ref, qn_ref, kvn_ref, wq_ref, wk_ref, wv_ref, cos_ref, sin_ref,
                     q_ref, k_ref, vt_ref, *, q_lora, kv_lora, scale, n_heads):
    m = mod_ref[0]
    h = (x_ref[...] * (1.0 + m[1:2]) + m[0:1]).astype(BF16)
    p = _bdot(h, wd_ref[...])
    cq, ckv = p[:, :q_lora], p[:, q_lora:q_lora + kv_lora]
    cos, sin = cos_ref[0], sin_ref[0]
    k_rope = _rope_chunk(p[:, q_lora + kv_lora:], cos, sin, QK_ROPE // 2)
    cq = (cq * lax.rsqrt(jnp.mean(cq * cq, axis=-1, keepdims=True) + RMS_EPS) * qn_ref[...]).astype(BF16)
    ckv = (ckv * lax.rsqrt(jnp.mean(ckv * ckv, axis=-1, keepdims=True) + RMS_EPS) * kvn_ref[...]).astype(BF16)
    for hd in range(n_heads):
        cols = slice(hd * LANES, (hd + 1) * LANES)
        qh = _rope_chunk(_bdot(cq, wq_ref[:, cols]), cos, sin, QK_ROPE // 2)
        q_ref[:, cols] = (qh * scale).astype(BF16)
        k_ref[:, cols] = (_bdot(ckv, wk_ref[:, cols]) + k_rope).astype(BF16)
    vt_ref[...] = _bdot(ckv, wv_ref[...]).T.astype(BF16)


def _mla_attn_kernel(*refs, n_heads):
    q_ref, k_ref, vt_ref = refs[:3]
    o_ref, m_scr, l_scr, acc_scr = refs[-4:]
    kt = pl.program_id(2)

    @pl.when(kt == 0)
    def _():
        m_scr[...] = jnp.full_like(m_scr, -jnp.inf)
        l_scr[...] = jnp.zeros_like(l_scr)
        acc_scr[...] = jnp.zeros_like(acc_scr)

    tq, tk = q_ref.shape[0], k_ref.shape[0]
    sub_q, sub_k = min(MLA_SUB_Q, tq), min(MLA_SUB_K, tk)
    n_kh = tk // sub_k
    ones = jnp.ones((BF16_ROWS, sub_k), BF16)
    items = [(hd, kh, qh) for hd in range(n_heads) for kh in range(n_kh) for qh in range(tq // sub_q)]

    def scores(item):
        hd, kh, qh = item
        cols = slice(hd * LANES, (hd + 1) * LANES)
        s = _bdot_nt(k_ref[kh * sub_k:(kh + 1) * sub_k, cols], q_ref[qh * sub_q:(qh + 1) * sub_q, cols])
        return s, jnp.max(s, axis=0, keepdims=True)

    ahead = [scores(it) for it in items[:MLA_LOOKAHEAD]]
    state = {}
    for n, (hd, kh, qh) in enumerate(items):
        rws = slice(hd * V_HD, (hd + 1) * V_HD)
        qcols = slice(qh * sub_q, (qh + 1) * sub_q)
        s_t, s_max = ahead.pop(0)
        if n + MLA_LOOKAHEAD < len(items):
            ahead.append(scores(items[n + MLA_LOOKAHEAD]))
        if kh == 0:
            state[hd, qh] = (m_scr[hd:hd + 1, qcols], l_scr[hd:hd + 1, qcols], acc_scr[rws, qcols])
        m_old, l_old, acc = state[hd, qh]
        m_new = jnp.maximum(m_old, s_max)
        p_t = jnp.exp2((s_t - m_new).astype(BF16))
        corr = jnp.exp2(m_old - m_new)
        vt = vt_ref[rws, kh * sub_k:(kh + 1) * sub_k]
        pv = _bdot(jnp.concatenate([vt, ones], axis=0), p_t)
        state[hd, qh] = (m_new, corr * l_old + pv[V_HD:V_HD + 1, :], acc * corr + pv[:V_HD, :])
        if kh == n_kh - 1:
            m_scr[hd:hd + 1, qcols], l_scr[hd:hd + 1, qcols], acc_scr[rws, qcols] = state.pop((hd, qh))

    @pl.when(kt == pl.num_programs(2) - 1)
    def _():
        for hd in range(n_heads):
            rws = slice(hd * V_HD, (hd + 1) * V_HD)
            acc_scr[rws, :] = acc_scr[rws, :] * (1.0 / l_scr[hd:hd + 1, :])
        o_ref[...] = acc_scr[...].T.astype(BF16)


def _mla_layer(rows, x, mod, p, ln_g, ln_b, alpha, need_ctx):
    t, d = rows.tile, x.shape[1]
    h_ = MLA_HEADS
    w_down, w_uq, w_ukv = p["w_down"], p["w_uq"], p["w_ukv"]
    q_lora = w_uq.shape[0]
    kv_lora = w_ukv.shape[0]
    qk = QK_NOPE + QK_ROPE
    assert QK_NOPE == V_HD == LANES // 2 and h_ % 2 == 0 and q_lora % LANES == 0 and kv_lora % LANES == 0
    pad = LANES - qk
    zc = lambda r, w: jnp.zeros((r, w), F32)
    wd_p = jnp.concatenate([w_down[:, :q_lora + kv_lora], zc(d, QK_NOPE), w_down[:, q_lora + kv_lora:],
                            zc(d, pad)], axis=1).astype(BF16)
    wq_p = jnp.concatenate([w_uq.reshape(q_lora, h_, qk), jnp.zeros((q_lora, h_, pad), F32)],
                           axis=-1).reshape(q_lora, h_ * LANES).astype(BF16)
    ukv = w_ukv.reshape(kv_lora, h_, QK_NOPE + V_HD)
    wk_p = jnp.concatenate([ukv[..., :QK_NOPE], jnp.zeros((kv_lora, h_, LANES - QK_NOPE), F32)],
                           axis=-1).reshape(kv_lora, h_ * LANES).astype(BF16)
    wv_p = ukv[..., QK_NOPE:].reshape(kv_lora, h_ * V_HD).astype(BF16)
    cos, sin = _rope_tables(rows, QK_ROPE, QK_NOPE, pad)
    n_down = wd_p.shape[1]
    q, k, vt = pl.pallas_call(
        functools.partial(_mla_proj_kernel, q_lora=q_lora, kv_lora=kv_lora, scale=qk ** -0.5 * math.log2(math.e),
                          n_heads=h_),
        grid=(rows.tiles,),
        in_specs=[pl.BlockSpec((t, d), lambda i: (i, 0)),
                  pl.BlockSpec((1, 6, d), lambda i: (rows.mod_index(i), 0, 0)),
                  pl.BlockSpec((d, n_down), lambda i: (0, 0)),
                  pl.BlockSpec((1, q_lora), lambda i: (0, 0)),
                  pl.BlockSpec((1, kv_lora), lambda i: (0, 0)),
                  pl.BlockSpec((q_lora, h_ * LANES), lambda i: (0, 0)),
                  pl.BlockSpec((kv_lora, h_ * LANES), lambda i: (0, 0)),
                  pl.BlockSpec((kv_lora, h_ * V_HD), lambda i: (0, 0)),
                  pl.BlockSpec((1, t, LANES), lambda i: (rows.rope_index(i), 0, 0)),
                  pl.BlockSpec((1, t, LANES), lambda i: (rows.rope_index(i), 0, 0))],
        out_specs=[pl.BlockSpec((t, h_ * LANES), lambda i: (i, 0)),
                   pl.BlockSpec((t, h_ * LANES), lambda i: (i, 0)),
                   pl.BlockSpec((h_ * V_HD, t), lambda i: (0, i))],
        out_shape=[jax.ShapeDtypeStruct((rows.rows, h_ * LANES), BF16),
                   jax.ShapeDtypeStruct((rows.rows, h_ * LANES), BF16),
                   jax.ShapeDtypeStruct((h_ * V_HD, rows.rows), BF16)],
        compiler_params=_params("arbitrary"),
        name="mla_proj",
    )(x, mod, wd_p, p["q_norm"].reshape(1, -1), p["kv_norm"].reshape(1, -1), wq_p, wk_p, wv_p, cos, sin)

    b_, s_, n_ctx = rows.batch, rows.seq, rows.n_ctx
    tk = MLA_TK
    tq, tq_ctx = min(MLA_TQ, s_), min(MLA_TQ, n_ctx)
    assert n_ctx % tk == 0 and s_ % tk == 0 and n_ctx % tq_ctx == 0 and s_ % tq == 0
    ck, lk = n_ctx // tk, s_ // tk
    ctx_kblocks = b_ * ck

    def kv_block(b, kt):
        return jnp.where(kt < ck, b * ck + kt, ctx_kblocks + b * lk + (kt - ck))

    def scratch(rows_q):
        return [pltpu.VMEM((h_, rows_q), F32), pltpu.VMEM((h_, rows_q), F32), pltpu.VMEM((h_ * V_HD, rows_q), F32)]

    o_shape = jax.ShapeDtypeStruct((rows.rows, h_ * V_HD), BF16)
    lq = s_ // tq
    ctx_qblocks = b_ * n_ctx // tq
    o_lat = pl.pallas_call(
        functools.partial(_mla_attn_kernel, n_heads=h_),
        grid=(b_, lq, ck + lk),
        in_specs=[pl.BlockSpec((tq, h_ * LANES), lambda b, i, kt: (ctx_qblocks + b * lq + i, 0)),
                  pl.BlockSpec((tk, h_ * LANES), lambda b, i, kt: (kv_block(b, kt), 0)),
                  pl.BlockSpec((h_ * V_HD, tk), lambda b, i, kt: (0, kv_block(b, kt)))],
        out_specs=pl.BlockSpec((tq, h_ * V_HD), lambda b, i, kt: (ctx_qblocks + b * lq + i, 0)),
        out_shape=o_shape,
        scratch_shapes=scratch(tq),
        compiler_params=_params("arbitrary", "arbitrary", "arbitrary"),
        name="mla_attn",
    )(q, k, vt)
    if need_ctx:
        cq = n_ctx // tq_ctx
        o = pl.pallas_call(
            functools.partial(_mla_attn_kernel, n_heads=h_),
            grid=(b_, cq, ck),
            in_specs=[pl.BlockSpec((tq_ctx, h_ * LANES), lambda b, i, kt: (b * cq + i, 0)),
                      pl.BlockSpec((tk, h_ * LANES), lambda b, i, kt: (b * ck + kt, 0)),
                      pl.BlockSpec((h_ * V_HD, tk), lambda b, i, kt: (0, b * ck + kt)),
                      pl.BlockSpec(memory_space=pl.ANY)],
            out_specs=pl.BlockSpec((tq_ctx, h_ * V_HD), lambda b, i, kt: (b * cq + i, 0)),
            out_shape=o_shape,
            scratch_shapes=scratch(tq_ctx),
            input_output_aliases={3: 0},
            compiler_params=_params("arbitrary", "arbitrary", "arbitrary"),
            name="mla_ctx_attn",
        )(q, k, vt, o_lat)
    else:
        o = o_lat
    return _out_ln(rows, o, p["w_o"], jnp.zeros((d,), F32), x, mod, ln_g, ln_b, alpha=alpha,
                   first_tile=0 if need_ctx else rows.ctx_tiles)


def _split_gu_kernel(w_ref, g_ref, u_ref):
    win = 2 * MXU_DIM
    r_i = lax.broadcasted_iota(jnp.int32, (win, MXU_DIM), 0)
    c_i = lax.broadcasted_iota(jnp.int32, (win, MXU_DIM), 1)
    pick_even = (r_i == 2 * c_i).astype(BF16)
    pick_odd = (r_i == 2 * c_i + 1).astype(BF16)
    for j in range(w_ref.shape[2] // win):
        w = w_ref[0, :, j * win:(j + 1) * win].astype(BF16)
        g_ref[0, :, j * MXU_DIM:(j + 1) * MXU_DIM] = _bdot(w, pick_even).astype(BF16)
        u_ref[0, :, j * MXU_DIM:(j + 1) * MXU_DIM] = _bdot(w, pick_odd).astype(BF16)


def _split_gate_up(w_gu):
    depth, n_exp, d, f2 = w_gu.shape
    w = w_gu.reshape(depth * n_exp, d, f2)
    tr = PROJ_TILE
    spec_out = pl.BlockSpec((1, tr, f2 // 2), lambda e, r: (e, r, 0))
    shape_out = jax.ShapeDtypeStruct((depth * n_exp, d, f2 // 2), BF16)
    return pl.pallas_call(
        _split_gu_kernel,
        grid=(depth * n_exp, d // tr),
        in_specs=[pl.BlockSpec((1, tr, f2), lambda e, r: (e, r, 0))],
        out_specs=[spec_out, spec_out],
        out_shape=[shape_out, shape_out],
        compiler_params=_params("arbitrary", "arbitrary"),
        name="moe_split_gate_up",
    )(w)


def _seg_rows(t, n_exp):
    return t * TOP_K + n_exp * SEG_ALIGN


def _router_kernel(x_ref, mod_ref, rw_ref, rb_ref, slot_ref, slot_t_ref, gate_ref, cnt_ref, *, top_k):
    m = mod_ref[0]
    h = x_ref[...] * (1.0 + m[4:5]) + m[3:4]
    w = rw_ref[...]
    h_hi, w_hi = h.astype(BF16), w.astype(BF16)
    h_lo = (h - h_hi.astype(F32)).astype(BF16)
    w_lo = (w - w_hi.astype(F32)).astype(BF16)
    logits = _bdot(h_hi, w_hi) + (_bdot(h_hi, w_lo) + _bdot(h_lo, w_hi)) + rb_ref[...]
    t, n_exp = logits.shape
    lane = lax.broadcasted_iota(jnp.int32, (t, n_exp), 1).astype(F32)
    work = logits
    sel, val = [], []
    for _ in range(top_k):
        mx = jnp.max(work, axis=1, keepdims=True)
        pick = jnp.min(jnp.where(work == mx, lane, float(n_exp)), axis=1, keepdims=True)
        sel.append(pick)
        val.append(mx)
        work = jnp.where(lane == pick, -jnp.inf, work)
    ex = [jnp.exp(v - val[0]) for v in val]
    tot = ex[0]
    for e in ex[1:]:
        tot = tot + e
    onehot = jnp.zeros((t, n_exp), F32)
    for pick in sel:
        onehot = onehot + (lane == pick).astype(F32)
    r_i = lax.broadcasted_iota(jnp.int32, (t, t), 0)
    c_i = lax.broadcasted_iota(jnp.int32, (t, t), 1)
    before = _bdot((c_i < r_i).astype(BF16), onehot)
    count = jnp.sum(onehot, axis=0, keepdims=True)
    cap = jnp.floor((count + (SEG_ALIGN - 1)) * (1.0 / SEG_ALIGN)) * SEG_ALIGN
    e_r = lax.broadcasted_iota(jnp.int32, (n_exp, n_exp), 0)
    e_c = lax.broadcasted_iota(jnp.int32, (n_exp, n_exp), 1)
    seg_start = _bdot(cap, (e_r < e_c).astype(BF16))
    base = before + seg_start
    col = lax.broadcasted_iota(jnp.int32, (t, top_k), 1)
    wide_lane = lax.broadcasted_iota(jnp.int32, (t, LANES), 1)
    slot = jnp.zeros((t, top_k), jnp.int32)
    gate = jnp.zeros((t, top_k), F32)
    wide = jnp.zeros((t, LANES), F32)
    for k in range(top_k):
        sk = jnp.sum(jnp.where(lane == sel[k], base, 0.0), axis=1, keepdims=True)
        slot = jnp.where(col == k, sk.astype(jnp.int32), slot)
        gate = jnp.where(col == k, ex[k] / tot, gate)
        wide = jnp.where(wide_lane == k, sk, wide)
    slot_ref[...] = slot
    gate_ref[...] = gate
    slot_t_ref[...] = wide.T[:SUBLANES, :]
    cnt_ref[0] = count


def _segment_copies(seg_ref, n_exp, make_copy, act):
    *small, big = SEG_PIECES
    for e in range(n_exp):
        g0, rows, l0 = seg_ref[0, 0, e], seg_ref[0, 0, n_exp + e], seg_ref[0, 0, 2 * n_exp + e]

        def whole(j, carry, g0=g0, l0=l0):
            act(make_copy(pl.multiple_of(l0 + j * big, SEG_ALIGN), pl.multiple_of(g0 + j * big, SEG_ALIGN), big))
            return carry

        lax.fori_loop(0, lax.shift_right_logical(rows, big.bit_length() - 1), whole, 0)
        for size in small:
            done = rows & (-2 * size)

            @pl.when((rows & size) != 0)
            def _(g0=g0, l0=l0, done=done, size=size):
                act(make_copy(pl.multiple_of(l0 + done, SEG_ALIGN), pl.multiple_of(g0 + done, SEG_ALIGN), size))


def _dispatch_kernel(tail_ref, seg_ref, seg_prev_ref, x_ref, mod_ref, slot_t_ref, xs_ref, buf, zbuf, sem, zsem, *,
                     n_exp, top_k):
    i = pl.program_id(0)
    cur = i % 2

    def to_slots(buf_slot, sem_slot):
        return lambda l, g, rows: pltpu.make_async_copy(buf.at[buf_slot, pl.ds(l, rows)], xs_ref.at[pl.ds(g, rows)],
                                                        sem.at[sem_slot])

    @pl.when(i == 0)
    def _():
        zbuf[...] = jnp.zeros_like(zbuf)
        zero_fill = lambda l, g, rows: pltpu.make_async_copy(zbuf.at[pl.ds(l, rows)], xs_ref.at[pl.ds(g, rows)], zsem)
        _segment_copies(tail_ref, n_exp, zero_fill, lambda c: c.start())
        _segment_copies(tail_ref, n_exp, zero_fill, lambda c: c.wait())

    m = mod_ref[0]
    h = (x_ref[...] * (1.0 + m[4:5]) + m[3:4]).astype(BF16)
    n_rows, t = buf.shape[1], x_ref.shape[0]
    row = lax.broadcasted_iota(jnp.int32, (n_rows, t), 0)
    slot_t = slot_t_ref[...].astype(jnp.int32)
    pick = row == slot_t[0:1, :]
    for k in range(1, top_k):
        pick = pick | (row == slot_t[k:k + 1, :])
    buf[cur] = _bdot(pick.astype(BF16), h)
    _segment_copies(seg_ref, n_exp, to_slots(cur, cur), lambda c: c.start())

    @pl.when(i > 0)
    def _():
        _segment_copies(seg_prev_ref, n_exp, to_slots(1 - cur, 1 - cur), lambda c: c.wait())

    @pl.when(i == pl.num_programs(0) - 1)
    def _():
        _segment_copies(seg_ref, n_exp, to_slots(cur, cur), lambda c: c.wait())


def _expert_kernel(be_ref, nu_ref, x_ref, wg_ref, wu_ref, bg_ref, bu_ref, wd_ref, bd_ref, y_ref):
    del be_ref

    @pl.when(pl.program_id(0) < nu_ref[0])
    def _():
        def gate_up(part):
            xb = x_ref[part * MOE_ROWS:(part + 1) * MOE_ROWS, :].astype(BF16)
            return _bdot(xb, wg_ref[0]), _bdot(xb, wu_ref[0])

        n_parts = x_ref.shape[0] // MOE_ROWS
        nxt = gate_up(0)
        for part in range(n_parts):
            g, u = nxt
            if part + 1 < n_parts:
                nxt = gate_up(part + 1)
            g = jnp.minimum(g + bg_ref[0], SWIGLU_LIMIT)
            u = jnp.clip(u + bu_ref[0], -SWIGLU_LIMIT, SWIGLU_LIMIT)
            act = (u + 1.0) * (g * jax.nn.sigmoid(SWIGLU_ALPHA * g))
            y_ref[part * MOE_ROWS:(part + 1) * MOE_ROWS, :] = _bdot(act, wd_ref[0]) + bd_ref[0]


def _combine_kernel(seg_ref, seg_next_ref, x_ref, mod_ref, gate_ref, slot_ref, g_ref, bt_ref, y_ref, o_ref,
                    ybuf, sem, *, n_exp, top_k, alpha):
    i = pl.program_id(0)
    cur = i % 2

    def from_slots(buf_slot):
        return lambda l, g, rows: pltpu.make_async_copy(y_ref.at[pl.ds(g, rows)], ybuf.at[buf_slot, pl.ds(l, rows)],
                                                        sem.at[buf_slot])

    @pl.when(i == 0)
    def _():
        ybuf[...] = jnp.zeros_like(ybuf)
        _segment_copies(seg_ref, n_exp, from_slots(0), lambda c: c.start())

    @pl.when(i < pl.num_programs(0) - 1)
    def _():
        _segment_copies(seg_next_ref, n_exp, from_slots(1 - cur), lambda c: c.start())

    _segment_copies(seg_ref, n_exp, from_slots(cur), lambda c: c.wait())
    t, n_rows = x_ref.shape[0], ybuf.shape[1]
    lane = lax.broadcasted_iota(jnp.int32, (t, n_rows), 1)
    gate, slot = gate_ref[...], slot_ref[...]
    weights = jnp.zeros((t, n_rows), F32)
    for k in range(top_k):
        weights = jnp.where(lane == slot[:, k:k + 1], gate[:, k:k + 1], weights)
    acc = _bdot(weights, ybuf[cur])
    m = mod_ref[0]
    r = alpha * x_ref[...] + m[5:6] * acc
    o_ref[...] = _layer_norm(r, g_ref[...], bt_ref[...])


def _moe_layer(rows, x, mod, router_w, router_b, wg, wu, wd, layer, b_gu, b_down, ln_g, ln_b, alpha, need_ctx):
    t, d = rows.tile, x.shape[1]
    n_exp = router_w.shape[1]
    d_ff = wd.shape[1]
    first = 0 if need_ctx else rows.ctx_tiles
    n_tiles = rows.tiles - first
    n_tok = n_tiles * t
    row = lambda i: (i + first, 0)
    modi = lambda i: (rows.mod_index(i + first), 0, 0)
    tok = lambda i: (i, 0)

    seg_rows = _seg_rows(t, n_exp)
    assert 4 * n_exp <= LANES and t % SEG_PIECES[-1] == 0 and MOE_BLOCK % SEG_PIECES[-1] == 0
    slot, slot_t, gate, cnt = pl.pallas_call(
        functools.partial(_router_kernel, top_k=TOP_K),
        grid=(n_tiles,),
        in_specs=[pl.BlockSpec((t, d), row), pl.BlockSpec((1, 6, d), modi),
                  pl.BlockSpec((d, n_exp), lambda i: (0, 0)), pl.BlockSpec((1, n_exp), lambda i: (0, 0))],
        out_specs=[pl.BlockSpec((t, TOP_K), tok), pl.BlockSpec((SUBLANES, t), tok), pl.BlockSpec((t, TOP_K), tok),
                   pl.BlockSpec((1, 1, n_exp), lambda i: (i, 0, 0))],
        out_shape=[jax.ShapeDtypeStruct((n_tok, TOP_K), jnp.int32), jax.ShapeDtypeStruct((n_tiles * SUBLANES, t), F32),
                   jax.ShapeDtypeStruct((n_tok, TOP_K), F32), jax.ShapeDtypeStruct((n_tiles, 1, n_exp), F32)],
        compiler_params=_params("arbitrary"),
        name="moe_router",
    )(x, mod, router_w, router_b.reshape(1, n_exp))

    i32 = jnp.int32
    count = cnt.reshape(n_tiles, n_exp).astype(i32)
    cap = (count + SEG_ALIGN - 1) // SEG_ALIGN * SEG_ALIGN
    e_rows = jnp.sum(cap, axis=0)
    e_pad = (e_rows + MOE_BLOCK - 1) // MOE_BLOCK * MOE_BLOCK
    e_end = jnp.cumsum(e_pad)
    e_start = e_end - e_pad
    seg_global = e_start[None, :] + jnp.cumsum(cap, axis=0) - cap
    seg_local = jnp.cumsum(cap, axis=1) - cap
    fill = jnp.zeros((n_tiles, LANES - 3 * n_exp), i32)
    seg = jnp.concatenate([seg_global, cap, seg_local, fill], axis=1).astype(i32).reshape(n_tiles, 1, LANES)
    tail = jnp.concatenate([e_start + e_rows, e_pad - e_rows, jnp.zeros((LANES - 2 * n_exp,), i32)])
    tail = tail.astype(i32).reshape(1, 1, LANES)
    max_slots = n_tok * TOP_K + n_tiles * n_exp * (SEG_ALIGN - 1) + n_exp * (MOE_BLOCK - 1)
    n_blocks = (max_slots + MOE_BLOCK - 1) // MOE_BLOCK
    n_slots = n_blocks * MOE_BLOCK
    n_used = (e_end[-1] // MOE_BLOCK).astype(i32)
    blk = jnp.minimum(jnp.arange(n_blocks, dtype=i32), n_used - 1) * MOE_BLOCK
    block_expert = jnp.minimum(jnp.sum((e_end[None, :] <= blk[:, None]).astype(i32), axis=1), n_exp - 1)

    smem_tile = lambda index_map: pl.BlockSpec((1, 1, LANES), index_map, memory_space=pltpu.SMEM)
    xs = pl.pallas_call(
        functools.partial(_dispatch_kernel, n_exp=n_exp, top_k=TOP_K),
        grid=(n_tiles,),
        in_specs=[smem_tile(lambda i: (0, 0, 0)), smem_tile(lambda i: (i, 0, 0)),
                  smem_tile(lambda i: (jnp.maximum(i - 1, 0), 0, 0)),
                  pl.BlockSpec((t, d), row), pl.BlockSpec((1, 6, d), modi), pl.BlockSpec((SUBLANES, t), tok)],
        out_specs=pl.BlockSpec(memory_space=pl.ANY),
        out_shape=jax.ShapeDtypeStruct((n_slots, d), F32),
        scratch_shapes=[pltpu.VMEM((2, seg_rows, d), F32), pltpu.VMEM((MOE_BLOCK, d), F32),
                        pltpu.SemaphoreType.DMA((2,)), pltpu.SemaphoreType.DMA(())],
        compiler_params=_params("arbitrary"),
        name="moe_dispatch",
    )(tail, seg, seg, x, mod, slot_t)

    bg = b_gu[:, 0::2].reshape(n_exp, 1, d_ff)
    bu = b_gu[:, 1::2].reshape(n_exp, 1, d_ff)
    used = lambda i, be, nu: (jnp.minimum(i, nu[0] - 1), 0)
    wmap = lambda i, be, nu: (be[i], 0, 0)
    wmap_all = lambda i, be, nu: (be[i] + layer * n_exp, 0, 0)
    y = pl.pallas_call(
        _expert_kernel,
        grid_spec=pltpu.PrefetchScalarGridSpec(
            num_scalar_prefetch=2,
            grid=(n_blocks,),
            in_specs=[pl.BlockSpec((MOE_BLOCK, d), used),
                      pl.BlockSpec((1, d, d_ff), wmap_all), pl.BlockSpec((1, d, d_ff), wmap_all),
                      pl.BlockSpec((1, 1, d_ff), wmap), pl.BlockSpec((1, 1, d_ff), wmap),
                      pl.BlockSpec((1, d_ff, d), wmap_all), pl.BlockSpec((1, 1, d), wmap)],
            out_specs=pl.BlockSpec((MOE_BLOCK, d), used)),
        out_shape=jax.ShapeDtypeStruct((n_slots, d), F32),
        compiler_params=_params("arbitrary"),
        name="moe_experts",
    )(block_expert, n_used.reshape(1), xs, wg, wu, bg, bu, wd, b_down.reshape(n_exp, 1, d))

    return pl.pallas_call(
        functools.partial(_combine_kernel, n_exp=n_exp, top_k=TOP_K, alpha=alpha),
        grid=(n_tiles,),
        in_specs=[smem_tile(lambda i: (i, 0, 0)), smem_tile(lambda i: (jnp.minimum(i + 1, n_tiles - 1), 0, 0)),
                  pl.BlockSpec((t, d), row), pl.BlockSpec((1, 6, d), modi),
                  pl.BlockSpec((t, TOP_K), tok), pl.BlockSpec((t, TOP_K), tok),
                  pl.BlockSpec((1, d), lambda i: (0, 0)), pl.BlockSpec((1, d), lambda i: (0, 0)),
                  pl.BlockSpec(memory_space=pl.ANY)],
        out_specs=pl.BlockSpec((t, d), row if need_ctx else tok),
        out_shape=jax.ShapeDtypeStruct(x.shape if need_ctx else (n_tok, d), F32),
        scratch_shapes=[pltpu.VMEM((2, seg_rows, d), F32), pltpu.SemaphoreType.DMA((2,))],
        input_output_aliases={2: 0} if need_ctx else {},
        compiler_params=_params("arbitrary"),
        name="moe_combine_ln",
    )(seg, seg, x, mod, gate, slot, ln_g.reshape(1, d), ln_b.reshape(1, d), y)


def kernel(x, c, ctx, c_ctx, ada_w, ada_b, ln1_g, ln1_b, ln2_g, ln2_b, router_w, router_b, exp_gu_w, exp_gu_b, exp_down_w, exp_down_b, rg_w_in, rg_conv_w, rg_conv_b, rg_gate_a_w, rg_gate_a_b, rg_gate_x_w, rg_gate_x_b, rg_lambda, rg_w_out, gqa_w_qkv, gqa_b_qkv, gqa_sinks, gqa_w_o, gqa_b_o, mla_w_down, mla_q_norm, mla_kv_norm, mla_w_uq, mla_w_ukv, mla_w_o):
    batch, seq, d = x.shape
    n_ctx = ctx.shape[1]
    depth = ada_w.shape[0]
    alpha = (2 * depth) ** 0.25
    rows = _Rows(batch, n_ctx, seq, ROW_TILE)
    prows = _Rows(batch, n_ctx, seq, PROJ_TILE) if (batch * n_ctx) % PROJ_TILE == 0 and seq % PROJ_TILE == 0 else rows
    xa = jnp.concatenate([ctx.reshape(batch * n_ctx, d), x.reshape(batch * seq, d)], axis=0)
    mods = _ada_table(jnp.concatenate([c_ctx[None, :], c], axis=0), ada_w, ada_b)
    wg, wu = _split_gate_up(exp_gu_w)
    wd = exp_down_w.reshape((-1,) + exp_down_w.shape[2:]).astype(BF16)
    for i in range(depth):
        need_ctx = i < depth - 1
        kind, j = i % 3, i // 3
        mod = mods[i]
        if kind == 0:
            prm = dict(w_in=rg_w_in[j], conv_w=rg_conv_w[j], conv_b=rg_conv_b[j], gate_a_w=rg_gate_a_w[j],
                       gate_a_b=rg_gate_a_b[j], gate_x_w=rg_gate_x_w[j], gate_x_b=rg_gate_x_b[j],
                       lam=rg_lambda[j], w_out=rg_w_out[j])
            xa = _rglru_layer(rows, prows, xa, mod, prm, ln1_g[i], ln1_b[i], alpha, need_ctx)
        elif kind == 1:
            prm = dict(w_qkv=gqa_w_qkv[j], b_qkv=gqa_b_qkv[j], sinks=gqa_sinks[j], w_o=gqa_w_o[j], b_o=gqa_b_o[j])
            xa = _gqa_layer(prows, xa, mod, prm, ln1_g[i], ln1_b[i], alpha, need_ctx)
        else:
            prm = dict(w_down=mla_w_down[j], q_norm=mla_q_norm[j], kv_norm=mla_kv_norm[j], w_uq=mla_w_uq[j],
                       w_ukv=mla_w_ukv[j], w_o=mla_w_o[j])
            xa = _mla_layer(prows, xa, mod, prm, ln1_g[i], ln1_b[i], alpha, need_ctx)
        xa = _moe_layer(rows, xa, mod, router_w[i], router_b[i], wg, wu, wd, i, exp_gu_b[i], exp_down_b[i],
                        ln2_g[i], ln2_b[i], alpha, need_ctx)
    return xa.reshape(batch, seq, d)
```

```python
import functools
import math

import jax
import jax.numpy as jnp
from jax import lax
from jax.experimental import pallas as pl
from jax.experimental.pallas import tpu as pltpu

F32 = jnp.float32
BF16 = jnp.bfloat16

GRID_W = 64
LN_EPS = 1e-5
RMS_EPS = 1e-6
ROPE_THETA = 10000.0
NEG_INF = -1e30
RG_C = 8.0
SQRT_FLOOR = 1e-30
GQA_KV = 2
GQA_HD = 64
WINDOW = 128
MLA_HEADS = 16
QK_NOPE = 64
QK_ROPE = 32
V_HD = 64
TOP_K = 4
SWIGLU_LIMIT = 7.0
SWIGLU_ALPHA = 1.702
MOE_BLOCK = 512
MOE_ROWS = 256

LANES = 128
SUBLANES = 8
BF16_ROWS = 16
ROW_TILE = 256
PROJ_TILE = 512
ATTN_Q = 128
GQA_LOOKAHEAD = 4
MLA_TQ = 1024
MLA_TK = 256
MLA_SUB_K = 128
MLA_SUB_Q = 256
MLA_LOOKAHEAD = 12
MXU_DIM = 256
SEG_ALIGN = SUBLANES
SEG_PIECES = (8, 16, 32)
WAIT_PIECES = (8, 16, 32, 64, 128, 256)
VMEM_LIMIT = 56 * 1024 * 1024


def _params(*sem):
    return pltpu.CompilerParams(dimension_semantics=sem, vmem_limit_bytes=VMEM_LIMIT)


def _bdot(a, b):
    return jnp.dot(a.astype(BF16), b.astype(BF16), preferred_element_type=F32)


def _bdot_nt(a, b):
    return lax.dot_general(a.astype(BF16), b.astype(BF16), (((1,), (1,)), ((), ())),
                           preferred_element_type=F32)


def _layer_norm(r, g, b):
    mu = jnp.mean(r, axis=-1, keepdims=True)
    d = r - mu
    var = jnp.mean(d * d, axis=-1, keepdims=True)
    return d * lax.rsqrt(var + LN_EPS) * g + b


def _sigmoid(x):
    return 0.5 * jnp.tanh(0.5 * x) + 0.5


def _gelu_tanh(x):
    return 0.5 * x * (1.0 + jnp.tanh(math.sqrt(2.0 / math.pi) * (x + 0.044715 * (x * x * x))))


class _Rows:
    def __init__(self, batch, n_ctx, seq, tile):
        assert (batch * n_ctx) % tile == 0 and seq % tile == 0
        self.batch, self.n_ctx, self.seq, self.tile = batch, n_ctx, seq, tile
        self.ctx_tiles = batch * n_ctx // tile
        self.lat_tiles = batch * seq // tile
        self.tiles = self.ctx_tiles + self.lat_tiles
        self.lat_per_batch = seq // tile
        self.ctx_per_batch = n_ctx // tile
        self.rows = batch * (n_ctx + seq)

    def mod_index(self, i):
        return jnp.where(i < self.ctx_tiles, 0, 1 + (i - self.ctx_tiles) // self.lat_per_batch)

    def rope_index(self, i):
        return jnp.where(i < self.ctx_tiles, self.lat_per_batch, (i - self.ctx_tiles) % self.lat_per_batch)


def _ada_kernel(c_ref, w_ref, b_ref, o_ref):
    cv = c_ref[...]
    s = cv * jax.nn.sigmoid(cv)
    o_ref[0] = jnp.dot(s, w_ref[0], preferred_element_type=F32,
                       precision=lax.Precision.HIGHEST) + b_ref[0]


def _ada_table(cvec, ada_w, ada_b):
    depth, d, d6 = ada_w.shape
    n = cvec.shape[0]
    chunk = d
    out = pl.pallas_call(
        _ada_kernel,
        grid=(depth, d6 // chunk),
        in_specs=[pl.BlockSpec((n, d), lambda l, j: (0, 0)),
                  pl.BlockSpec((1, d, chunk), lambda l, j: (l, 0, j)),
                  pl.BlockSpec((1, 1, chunk), lambda l, j: (l, 0, j))],
        out_specs=pl.BlockSpec((1, n, chunk), lambda l, j: (l, 0, j)),
        out_shape=jax.ShapeDtypeStruct((depth, n, d6), F32),
        compiler_params=_params("arbitrary", "arbitrary"),
        name="ada_table",
    )(cvec, ada_w, ada_b.reshape(depth, 1, d6))
    return out.reshape(depth, n, 6, d)


def _out_ln_kernel(z_ref, w_ref, b_ref, x_ref, mod_ref, g_ref, bt_ref, o_ref, *, gate_row, alpha):
    y = _bdot(z_ref[...], w_ref[...]) + b_ref[...]
    m = mod_ref[0]
    r = alpha * x_ref[...] + m[gate_row:gate_row + 1] * y
    o_ref[...] = _layer_norm(r, g_ref[...], bt_ref[...])


def _out_ln(rows, z, w, bias, x, mod, ln_g, ln_b, *, alpha, first_tile=0):
    t, d = rows.tile, x.shape[1]
    kdim = z.shape[1]
    n_tiles = rows.tiles - first_tile
    row = lambda i: (i + first_tile, 0)
    return pl.pallas_call(
        functools.partial(_out_ln_kernel, gate_row=2, alpha=alpha),
        grid=(n_tiles,),
        in_specs=[pl.BlockSpec((t, kdim), row),
                  pl.BlockSpec((kdim, d), lambda i: (0, 0)),
                  pl.BlockSpec((1, d), lambda i: (0, 0)),
                  pl.BlockSpec((t, d), row),
                  pl.BlockSpec((1, 6, d), lambda i: (rows.mod_index(i + first_tile), 0, 0)),
                  pl.BlockSpec((1, d), lambda i: (0, 0)),
                  pl.BlockSpec((1, d), lambda i: (0, 0))],
        out_specs=pl.BlockSpec((t, d), row),
        out_shape=jax.ShapeDtypeStruct(x.shape, F32),
        input_output_aliases={3: 0},
        compiler_params=_params("arbitrary"),
        name="out_proj_ln",
    )(z, w.astype(BF16), bias.reshape(1, d), x, mod, ln_g.reshape(1, d), ln_b.reshape(1, d))


def _rg_in_kernel(x_ref, mod_ref, w_ref, gel_ref, rec_ref, *, d_rnn):
    m = mod_ref[0]
    h = (x_ref[...] * (1.0 + m[1:2]) + m[0:1]).astype(BF16)
    gel_ref[...] = _gelu_tanh(_bdot(h, w_ref[:, :d_rnn])).astype(BF16)
    rec_ref[...] = _bdot(h, w_ref[:, d_rnn:])


def _rg_in(rows, x, mod, w_in):
    t, d = rows.tile, x.shape[1]
    d_rnn = w_in.shape[1] // 2
    return pl.pallas_call(
        functools.partial(_rg_in_kernel, d_rnn=d_rnn),
        grid=(rows.tiles,),
        in_specs=[pl.BlockSpec((t, d), lambda i: (i, 0)),
                  pl.BlockSpec((1, 6, d), lambda i: (rows.mod_index(i), 0, 0)),
                  pl.BlockSpec((d, 2 * d_rnn), lambda i: (0, 0))],
        out_specs=[pl.BlockSpec((t, d_rnn), lambda i: (i, 0)),
                   pl.BlockSpec((t, d_rnn), lambda i: (i, 0))],
        out_shape=[jax.ShapeDtypeStruct((rows.rows, d_rnn), BF16),
                   jax.ShapeDtypeStruct((rows.rows, d_rnn), F32)],
        compiler_params=_params("arbitrary"),
        name="rg_in_proj",
    )(x, mod, w_in.astype(BF16))


def _rg_scan_kernel(*refs, reverse, fuse_out, n_blocks, block_w, tile, ctx_tiles, lat_tiles):
    if fuse_out:
        (x_ref, xp_ref, xn_ref, cw_ref, cb_ref, wa_ref, ba_ref, wx_ref, bx_ref, lam_ref,
         hf_ref, gel_ref, out_ref, a_scr, u_scr, h_scr, carry_scr) = refs
    else:
        (x_ref, xp_ref, xn_ref, cw_ref, cb_ref, wa_ref, ba_ref, wx_ref, bx_ref, lam_ref,
         out_ref, a_scr, u_scr, carry_scr) = refs
        h_scr = out_ref
    j = pl.program_id(1)
    is_ctx = j < ctx_tiles
    n_seq = jnp.where(is_ctx, ctx_tiles, lat_tiles)
    step = jnp.where(is_ctx, j, j - ctx_tiles)
    pos = (n_seq - 1 - step) if reverse else step
    prev_ok = (pos > 0).astype(F32)
    next_ok = (pos < n_seq - 1).astype(F32)

    @pl.when(j == 0)
    def _():
        carry_scr[...] = jnp.zeros_like(carry_scr)

    row8 = lax.broadcasted_iota(jnp.int32, (SUBLANES, block_w), 0)
    for n in range(n_blocks):
        cols = slice(n * block_w, (n + 1) * block_w)
        x = x_ref[:, cols]
        prev = xp_ref[SUBLANES - 1:SUBLANES, cols] * prev_ok
        nxt0 = xn_ref[0:1, cols] * next_ok
        nxt1 = xn_ref[1:2, cols] * next_ok
        cw = [cw_ref[k:k + 1, cols] for k in range(4)]
        xc = (cw[0] * pltpu.roll(x, 1, 0) + cw[1] * x + cw[2] * pltpu.roll(x, tile - 1, 0)
              + cw[3] * pltpu.roll(x, tile - 2, 0) + cb_ref[:, cols])
        first, last = x[0:1, :], x[tile - 1:tile, :]
        head = xc[:SUBLANES] + jnp.where(row8 == 0, cw[0] * (prev - last), 0.0)
        tail = xc[tile - SUBLANES:] + jnp.where(
            row8 == SUBLANES - 2, cw[3] * (nxt0 - first),
            jnp.where(row8 == SUBLANES - 1, cw[2] * (nxt0 - first) + cw[3] * (nxt1 - x[1:2, :]), 0.0))
        xc = jnp.concatenate([head, xc[SUBLANES:tile - SUBLANES], tail], axis=0)
        xb = xc.astype(BF16)
        r = _sigmoid(_bdot(xb, wa_ref[n]) + ba_ref[:, cols])
        gi = _sigmoid(_bdot(xb, wx_ref[n]) + bx_ref[:, cols])
        z = -lam_ref[:, cols]
        softplus = jnp.maximum(z, 0.0) + jnp.log1p(jnp.exp(-jnp.abs(z)))
        a = jnp.exp2(r * ((-RG_C * math.log2(math.e)) * softplus))
        a_scr[:, cols] = a
        v = 1.0 - a * a
        u_scr[:, cols] = (v * lax.rsqrt(jnp.maximum(v, SQRT_FLOOR))) * (gi * xc)

    width = n_blocks * block_w
    sub = lax.broadcasted_iota(jnp.int32, (SUBLANES, width), 0)
    groups = tile // SUBLANES

    def body(g, carry):
        gg = (groups - 1 - g) if reverse else g
        r0 = pl.multiple_of(gg * SUBLANES, SUBLANES)
        a8 = a_scr[pl.ds(r0, SUBLANES), :]
        u8 = u_scr[pl.ds(r0, SUBLANES), :]
        for s in (1, 2, 4):
            if reverse:
                a_sh, u_sh, ok = pltpu.roll(a8, SUBLANES - s, 0), pltpu.roll(u8, SUBLANES - s, 0), sub < SUBLANES - s
            else:
                a_sh, u_sh, ok = pltpu.roll(a8, s, 0), pltpu.roll(u8, s, 0), sub >= s
            u8 = jnp.where(ok, a8 * u_sh + u8, u8)
            a8 = jnp.where(ok, a8 * a_sh, a8)
        h8 = a8 * carry + u8
        h_scr[pl.ds(r0, SUBLANES), :] = h8
        return h8[0:1, :] if reverse else h8[SUBLANES - 1:SUBLANES, :]

    carry_scr[...] = lax.fori_loop(0, groups, body, carry_scr[...])
    if fuse_out:
        out_ref[...] = ((hf_ref[...] + h_scr[...]) * gel_ref[...].astype(F32)).astype(BF16)


def _rg_scan(rows, rec, conv_w, conv_b, wa, ba, wx, bx, lam, *, reverse, h_fwd=None, gel=None):
    t = rows.tile
    c = rec.shape[1]
    n_blocks, block_w = wa.shape[0], wa.shape[1]
    nc, nl = rows.ctx_per_batch, rows.lat_per_batch
    halo = t // SUBLANES
    last_halo = rows.rows // SUBLANES - 1
    fuse_out = h_fwd is not None

    def tile_index(b, j):
        is_ctx = j < nc
        step = jnp.where(is_ctx, j, j - nc)
        n_seq = jnp.where(is_ctx, nc, nl)
        pos = (n_seq - 1 - step) if reverse else step
        return jnp.where(is_ctx, b * nc + pos, rows.ctx_tiles + b * nl + pos)

    cur = lambda b, j: (tile_index(b, j), 0)
    prv = lambda b, j: (jnp.maximum(tile_index(b, j) * halo - 1, 0), 0)
    nxt = lambda b, j: (jnp.minimum((tile_index(b, j) + 1) * halo, last_halo), 0)
    full2 = lambda b, j: (0, 0)
    full3 = lambda b, j: (0, 0, 0)
    in_specs = [pl.BlockSpec((t, c), cur), pl.BlockSpec((SUBLANES, c), prv), pl.BlockSpec((SUBLANES, c), nxt),
                pl.BlockSpec((4, c), full2), pl.BlockSpec((1, c), full2),
                pl.BlockSpec((n_blocks, block_w, block_w), full3), pl.BlockSpec((1, c), full2),
                pl.BlockSpec((n_blocks, block_w, block_w), full3), pl.BlockSpec((1, c), full2),
                pl.BlockSpec((1, c), full2)]
    args = [rec, rec, rec, conv_w, conv_b.reshape(1, c), wa.astype(BF16), ba.reshape(1, c),
            wx.astype(BF16), bx.reshape(1, c), lam.reshape(1, c)]
    scratch = [pltpu.VMEM((t, c), F32), pltpu.VMEM((t, c), F32)]
    if fuse_out:
        in_specs += [pl.BlockSpec((t, c), cur), pl.BlockSpec((t, c), cur)]
        args += [h_fwd, gel]
        scratch.append(pltpu.VMEM((t, c), F32))
    scratch.append(pltpu.VMEM((1, c), F32))
    return pl.pallas_call(
        functools.partial(_rg_scan_kernel, reverse=reverse, fuse_out=fuse_out, n_blocks=n_blocks,
                          block_w=block_w, tile=t, ctx_tiles=nc, lat_tiles=nl),
        grid=(rows.batch, nc + nl),
        in_specs=in_specs,
        out_specs=pl.BlockSpec((t, c), cur),
        out_shape=jax.ShapeDtypeStruct((rows.rows, c), BF16 if fuse_out else F32),
        scratch_shapes=scratch,
        compiler_params=_params("arbitrary", "arbitrary"),
        name="rg_scan_bwd" if reverse else "rg_scan_fwd",
    )(*args)


def _rglru_layer(rows, prows, x, mod, p, ln_g, ln_b, alpha, need_ctx):
    gel, rec = _rg_in(prows, x, mod, p["w_in"])
    h_fwd = _rg_scan(rows, rec, p["conv_w"], p["conv_b"], p["gate_a_w"][0], p["gate_a_b"][0],
                     p["gate_x_w"][0], p["gate_x_b"][0], p["lam"][0], reverse=False)
    z = _rg_scan(rows, rec, p["conv_w"], p["conv_b"], p["gate_a_w"][1], p["gate_a_b"][1],
                 p["gate_x_w"][1], p["gate_x_b"][1], p["lam"][1], reverse=True, h_fwd=h_fwd, gel=gel)
    d = x.shape[1]
    return _out_ln(prows, z, p["w_out"], jnp.zeros((d,), F32), x, mod, ln_g, ln_b, alpha=alpha,
                   first_tile=0 if need_ctx else prows.ctx_tiles)


def _axial_angles(seq, rot_dim):
    pos = jnp.arange(seq, dtype=jnp.int32)
    row = (pos // GRID_W).astype(F32)
    col = (pos % GRID_W).astype(F32)
    n_freq = rot_dim // 4
    inv_freq = ROPE_THETA ** (-jnp.arange(n_freq, dtype=F32) / n_freq)
    return jnp.concatenate([row[:, None] * inv_freq, col[:, None] * inv_freq], axis=-1)


def _rope_tables(rows, rot_dim, lead, trail):
    ang = _axial_angles(rows.seq, rot_dim)
    cos, sin = jnp.cos(ang), jnp.sin(ang)
    ones = lambda w: jnp.ones((rows.seq, w), F32)
    zeros = lambda w: jnp.zeros((rows.seq, w), F32)
    c = jnp.concatenate([ones(lead), cos, cos, ones(trail)], axis=-1)
    s = jnp.concatenate([zeros(lead), -sin, sin, zeros(trail)], axis=-1)
    reps = LANES // c.shape[1]
    c, s = jnp.tile(c, (1, reps)), jnp.tile(s, (1, reps))
    t = rows.tile
    c = jnp.concatenate([c.reshape(rows.lat_per_batch, t, LANES), jnp.ones((1, t, LANES), F32)], axis=0)
    s = jnp.concatenate([s.reshape(rows.lat_per_batch, t, LANES), jnp.zeros((1, t, LANES), F32)], axis=0)
    return c, s


def _rope_chunk(x, cos, sin, half):
    lane = lax.broadcasted_iota(jnp.int32, x.shape, 1)
    partner = jnp.where((lane % (2 * half)) < half, pltpu.roll(x, LANES - half, 1), pltpu.roll(x, half, 1))
    return x * cos + partner * sin


def _gqa_proj_kernel(x_ref, mod_ref, w_ref, b_ref, cos_ref, sin_ref, q_ref, kv_ref, *, q_dim, scale, half):
    m = mod_ref[0]
    h = (x_ref[...] * (1.0 + m[1:2]) + m[0:1]).astype(BF16)
    cos, sin = cos_ref[0], sin_ref[0]
    n_q = q_dim // LANES
    n_all = w_ref.shape[1] // LANES
    for c in range(n_all):
        cols = slice(c * LANES, (c + 1) * LANES)
        p = _bdot(h, w_ref[:, cols]) + b_ref[:, cols]
        is_v = c in (n_q + 1, n_q + 3)
        if not is_v:
            p = _rope_chunk(p, cos, sin, half)
        if c < n_q:
            q_ref[:, cols] = (p * scale).astype(BF16)
        else:
            kv_ref[:, (c - n_q) * LANES:(c - n_q + 1) * LANES] = p.astype(BF16)


def _gqa_attn_kernel(*refs, windowed, window, seq, n_pairs, pairs_per_kv):
    if windowed:
        q_ref, kv_ref, kvc_ref, sink_ref, o_ref = refs
    else:
        q_ref, kvc_ref, sink_ref, _, o_ref = refs
    tq = q_ref.shape[0]
    kvc = kvc_ref[...]
    if windowed:
        span = tq + 2 * window
        qs = pl.program_id(1) * tq
        ws = pl.multiple_of(jnp.clip(qs - window, 0, seq - span), LANES)
        kv = jnp.concatenate([kv_ref[pl.ds(ws, span), :], kvc], axis=0)
        n_keys = kv.shape[0]
        qpos = qs + lax.broadcasted_iota(jnp.int32, (tq, n_keys), 0)
        col = lax.broadcasted_iota(jnp.int32, (tq, n_keys), 1)
        mask = (jnp.abs(ws + col - qpos) <= window) | (col >= span)
    else:
        kv = kvc
        mask = None
    lane = lax.broadcasted_iota(jnp.int32, (kv.shape[0], LANES), 1)
    low = lane < GQA_HD
    zero = jnp.zeros((kv.shape[0], LANES), BF16)
    one_hi = (lane == GQA_HD).astype(F32).astype(BF16)
    one_lo = (lane == 0).astype(F32).astype(BF16)
    low_q = lax.broadcasted_iota(jnp.int32, (tq, LANES), 1) < GQA_HD
    k_plain, v_plain = kv[:, 0:LANES], kv[:, LANES:2 * LANES]
    k_swap, v_swap = kv[:, 2 * LANES:3 * LANES], kv[:, 3 * LANES:4 * LANES]
    k_half, v_half = {}, {}
    for g in range(GQA_KV):
        k_half[g, 0] = jnp.where(low, k_plain if g == 0 else k_swap, zero)
        k_half[g, 1] = jnp.where(low, zero, k_swap if g == 0 else k_plain)
        v_half[g, 0] = jnp.where(low, v_plain if g == 0 else v_swap, one_hi)
        v_half[g, 1] = jnp.where(low, one_lo, v_swap if g == 0 else v_plain)
    items = [(pr, hh) for pr in range(n_pairs) for hh in range(2)]

    def scores(item):
        pr, hh = item
        return _bdot_nt(q_ref[:, pr * LANES:(pr + 1) * LANES], k_half[pr // pairs_per_kv, hh])

    ahead = [scores(it) for it in items[:GQA_LOOKAHEAD]]
    acc = None
    for n, (pr, hh) in enumerate(items):
        s = ahead.pop(0)
        if n + GQA_LOOKAHEAD < len(items):
            ahead.append(scores(items[n + GQA_LOOKAHEAD]))
        head = 2 * pr + hh
        if mask is not None:
            s = jnp.where(mask, s, NEG_INF)
        sk = sink_ref[head:head + 1, 0:1]
        mx = jnp.maximum(jnp.max(s, axis=1, keepdims=True), sk)
        p = jnp.exp((s - mx).astype(BF16))
        pv = _bdot(p, v_half[pr // pairs_per_kv, hh])
        sum_lane = GQA_HD if hh == 0 else 0
        denom = pv[:, sum_lane:sum_lane + 1] + jnp.exp(sk - mx)
        part = pv * (1.0 / denom)
        if hh == 1:
            o_ref[:, pr * LANES:(pr + 1) * LANES] = jnp.where(low_q, acc, part).astype(BF16)
        acc = part


def _gqa_layer(rows, x, mod, p, ln_g, ln_b, alpha, need_ctx):
    t, d = rows.tile, x.shape[1]
    w_qkv, b_qkv = p["w_qkv"], p["b_qkv"]
    kv_dim = GQA_KV * GQA_HD
    q_dim = w_qkv.shape[1] - 2 * kv_dim
    n_heads = q_dim // GQA_HD
    assert kv_dim == LANES and GQA_KV == 2 and q_dim % LANES == 0
    swap = lambda a: jnp.concatenate([a[..., GQA_HD:], a[..., :GQA_HD]], axis=-1)
    wk, wv = w_qkv[:, q_dim:q_dim + kv_dim], w_qkv[:, q_dim + kv_dim:]
    bk, bv = b_qkv[q_dim:q_dim + kv_dim], b_qkv[q_dim + kv_dim:]
    w_ext = jnp.concatenate([w_qkv, swap(wk), swap(wv)], axis=1).astype(BF16)
    b_ext = jnp.concatenate([b_qkv, swap(bk), swap(bv)]).reshape(1, -1)
    n_ext = w_ext.shape[1]
    cos, sin = _rope_tables(rows, GQA_HD, 0, 0)
    q, kv = pl.pallas_call(
        functools.partial(_gqa_proj_kernel, q_dim=q_dim, scale=GQA_HD ** -0.5, half=GQA_HD // 2),
        grid=(rows.tiles,),
        in_specs=[pl.BlockSpec((t, d), lambda i: (i, 0)),
                  pl.BlockSpec((1, 6, d), lambda i: (rows.mod_index(i), 0, 0)),
                  pl.BlockSpec((d, n_ext), lambda i: (0, 0)),
                  pl.BlockSpec((1, n_ext), lambda i: (0, 0)),
                  pl.BlockSpec((1, t, LANES), lambda i: (rows.rope_index(i), 0, 0)),
                  pl.BlockSpec((1, t, LANES), lambda i: (rows.rope_index(i), 0, 0))],
        out_specs=[pl.BlockSpec((t, q_dim), lambda i: (i, 0)),
                   pl.BlockSpec((t, 4 * LANES), lambda i: (i, 0))],
        out_shape=[jax.ShapeDtypeStruct((rows.rows, q_dim), BF16),
                   jax.ShapeDtypeStruct((rows.rows, 4 * LANES), BF16)],
        compiler_params=_params("arbitrary"),
        name="gqa_qkv_proj",
    )(x, mod, w_ext, b_ext, cos, sin)

    sinks = jnp.broadcast_to(p["sinks"].astype(F32)[:, None], (n_heads, LANES))
    b_, s_, n_ctx = rows.batch, rows.seq, rows.n_ctx
    ctx_rows = b_ * n_ctx
    assert ctx_rows % s_ == 0 and s_ >= ATTN_Q + 2 * WINDOW
    q_blocks = s_ // ATTN_Q
    n_pairs = q_dim // LANES
    common = dict(window=WINDOW, seq=s_, n_pairs=n_pairs, pairs_per_kv=n_pairs // GQA_KV)
    o_shape = jax.ShapeDtypeStruct((rows.rows, q_dim), BF16)
    o_lat = pl.pallas_call(
        functools.partial(_gqa_attn_kernel, windowed=True, **common),
        grid=(b_, q_blocks),
        in_specs=[pl.BlockSpec((ATTN_Q, q_dim), lambda b, j: (ctx_rows // ATTN_Q + b * q_blocks + j, 0)),
                  pl.BlockSpec((s_, 4 * LANES), lambda b, j: (ctx_rows // s_ + b, 0)),
                  pl.BlockSpec((n_ctx, 4 * LANES), lambda b, j: (b, 0)),
                  pl.BlockSpec((n_heads, LANES), lambda b, j: (0, 0))],
        out_specs=pl.BlockSpec((ATTN_Q, q_dim), lambda b, j: (ctx_rows // ATTN_Q + b * q_blocks + j, 0)),
        out_shape=o_shape,
        compiler_params=_params("arbitrary", "arbitrary"),
        name="gqa_window_attn",
    )(q, kv, kv, sinks)
    if need_ctx:
        cq_blocks = n_ctx // ATTN_Q
        o = pl.pallas_call(
            functools.partial(_gqa_attn_kernel, windowed=False, **common),
            grid=(b_, cq_blocks),
            in_specs=[pl.BlockSpec((ATTN_Q, q_dim), lambda b, j: (b * cq_blocks + j, 0)),
                      pl.BlockSpec((n_ctx, 4 * LANES), lambda b, j: (b, 0)),
                      pl.BlockSpec((n_heads, LANES), lambda b, j: (0, 0)),
                      pl.BlockSpec(memory_space=pl.ANY)],
            out_specs=pl.BlockSpec((ATTN_Q, q_dim), lambda b, j: (b * cq_blocks + j, 0)),
            out_shape=o_shape,
            input_output_aliases={3: 0},
            compiler_params=_params("arbitrary", "arbitrary"),
            name="gqa_ctx_attn",
        )(q, kv, sinks, o_lat)
    else:
        o = o_lat
    return _out_ln(rows, o, p["w_o"], p["b_o"], x, mod, ln_g, ln_b, alpha=alpha,
                   first_tile=0 if need_ctx else rows.ctx_tiles)


def _mla_proj_kernel(x_ref, mod_ref, wd_ref, qn_ref, kvn_ref, wq_ref, wk_ref, wv_ref, cos_ref, sin_ref,
                     q_ref, k_ref, vt_ref, *, q_lora, kv_lora, scale, n_heads):
    m = mod_ref[0]
    h = (x_ref[...] * (1.0 + m[1:2]) + m[0:1]).astype(BF16)
    p = _bdot(h, wd_ref[...])
    cq, ckv = p[:, :q_lora], p[:, q_lora:q_lora + kv_lora]
    cos, sin = cos_ref[0], sin_ref[0]
    k_rope = _rope_chunk(p[:, q_lora + kv_lora:], cos, sin, QK_ROPE // 2)
    cq = (cq * lax.rsqrt(jnp.mean(cq * cq, axis=-1, keepdims=True) + RMS_EPS) * qn_ref[...]).astype(BF16)
    ckv = (ckv * lax.rsqrt(jnp.mean(ckv * ckv, axis=-1, keepdims=True) + RMS_EPS) * kvn_ref[...]).astype(BF16)
    for hd in range(n_heads):
        cols = slice(hd * LANES, (hd + 1) * LANES)
        qh = _rope_chunk(_bdot(cq, wq_ref[:, cols]), cos, sin, QK_ROPE // 2)
        q_ref[:, cols] = (qh * scale).astype(BF16)
        k_ref[:, cols] = (_bdot(ckv, wk_ref[:, cols]) + k_rope).astype(BF16)
    vt_ref[...] = _bdot(ckv, wv_ref[...]).T.astype(BF16)


def _mla_attn_kernel(*refs, n_heads):
    q_ref, k_ref, vt_ref = refs[:3]
    o_ref, m_scr, l_scr, acc_scr = refs[-4:]
    kt = pl.program_id(2)

    @pl.when(kt == 0)
    def _():
        m_scr[...] = jnp.full_like(m_scr, -jnp.inf)
        l_scr[...] = jnp.zeros_like(l_scr)
        acc_scr[...] = jnp.zeros_like(acc_scr)

    tq, tk = q_ref.shape[0], k_ref.shape[0]
    sub_q, sub_k = min(MLA_SUB_Q, tq), min(MLA_SUB_K, tk)
    n_kh = tk // sub_k
    ones = jnp.ones((BF16_ROWS, sub_k), BF16)
    items = [(hd, kh, qh) for hd in range(n_heads) for qh in range(tq // sub_q) for kh in range(n_kh)]

    def scores(item):
        hd, kh, qh = item
        cols = slice(hd * LANES, (hd + 1) * LANES)
        s = _bdot_nt(k_ref[kh * sub_k:(kh + 1) * sub_k, cols], q_ref[qh * sub_q:(qh + 1) * sub_q, cols])
        return s, jnp.max(s, axis=0, keepdims=True)

    ahead = [scores(it) for it in items[:MLA_LOOKAHEAD]]
    state = {}
    for n, (hd, kh, qh) in enumerate(items):
        rws = slice(hd * V_HD, (hd + 1) * V_HD)
        qcols = slice(qh * sub_q, (qh + 1) * sub_q)
        s_t, s_max = ahead.pop(0)
        if n + MLA_LOOKAHEAD < len(items):
            ahead.append(scores(items[n + MLA_LOOKAHEAD]))
        if kh == 0:
            state[hd, qh] = (m_scr[hd:hd + 1, qcols], l_scr[hd:hd + 1, qcols], acc_scr[rws, qcols])
        m_old, l_old, acc = state[hd, qh]
        m_new = jnp.maximum(m_old, s_max)
        p_t = jnp.exp2((s_t - m_new).astype(BF16))
        corr = jnp.exp2(m_old - m_new)
        vt = vt_ref[rws, kh * sub_k:(kh + 1) * sub_k]
        pv = _bdot(jnp.concatenate([vt, ones], axis=0), p_t)
        state[hd, qh] = (m_new, corr * l_old + pv[V_HD:V_HD + 1, :], acc * corr + pv[:V_HD, :])
        if kh == n_kh - 1:
            m_scr[hd:hd + 1, qcols], l_scr[hd:hd + 1, qcols], acc_scr[rws, qcols] = state.pop((hd, qh))

    @pl.when(kt == pl.num_programs(2) - 1)
    def _():
        for hd in range(n_heads):
            rws = slice(hd * V_HD, (hd + 1) * V_HD)
            acc_scr[rws, :] = acc_scr[rws, :] * (1.0 / l_scr[hd:hd + 1, :])
        o_ref[...] = acc_scr[...].T.astype(BF16)


def _mla_layer(rows, x, mod, p, ln_g, ln_b, alpha, need_ctx):
    t, d = rows.tile, x.shape[1]
    h_ = MLA_HEADS
    w_down, w_uq, w_ukv = p["w_down"], p["w_uq"], p["w_ukv"]
    q_lora = w_uq.shape[0]
    kv_lora = w_ukv.shape[0]
    qk = QK_NOPE + QK_ROPE
    assert QK_NOPE == V_HD == LANES // 2 and h_ % 2 == 0 and q_lora % LANES == 0 and kv_lora % LANES == 0
    pad = LANES - qk
    zc = lambda r, w: jnp.zeros((r, w), F32)
    wd_p = jnp.concatenate([w_down[:, :q_lora + kv_lora], zc(d, QK_NOPE), w_down[:, q_lora + kv_lora:],
                            zc(d, pad)], axis=1).astype(BF16)
    wq_p = jnp.concatenate([w_uq.reshape(q_lora, h_, qk), jnp.zeros((q_lora, h_, pad), F32)],
                           axis=-1).reshape(q_lora, h_ * LANES).astype(BF16)
    ukv = w_ukv.reshape(kv_lora, h_, QK_NOPE + V_HD)
    wk_p = jnp.concatenate([ukv[..., :QK_NOPE], jnp.zeros((kv_lora, h_, LANES - QK_NOPE), F32)],
                           axis=-1).reshape(kv_lora, h_ * LANES).astype(BF16)
    wv_p = ukv[..., QK_NOPE:].reshape(kv_lora, h_ * V_HD).astype(BF16)
    cos, sin = _rope_tables(rows, QK_ROPE, QK_NOPE, pad)
    n_down = wd_p.shape[1]
    q, k, vt = pl.pallas_call(
        functools.partial(_mla_proj_kernel, q_lora=q_lora, kv_lora=kv_lora, scale=qk ** -0.5 * math.log2(math.e),
                          n_heads=h_),
        grid=(rows.tiles,),
        in_specs=[pl.BlockSpec((t, d), lambda i: (i, 0)),
                  pl.BlockSpec((1, 6, d), lambda i: (rows.mod_index(i), 0, 0)),
                  pl.BlockSpec((d, n_down), lambda i: (0, 0)),
                  pl.BlockSpec((1, q_lora), lambda i: (0, 0)),
                  pl.BlockSpec((1, kv_lora), lambda i: (0, 0)),
                  pl.BlockSpec((q_lora, h_ * LANES), lambda i: (0, 0)),
                  pl.BlockSpec((kv_lora, h_ * LANES), lambda i: (0, 0)),
                  pl.BlockSpec((kv_lora, h_ * V_HD), lambda i: (0, 0)),
                  pl.BlockSpec((1, t, LANES), lambda i: (rows.rope_index(i), 0, 0)),
                  pl.BlockSpec((1, t, LANES), lambda i: (rows.rope_index(i), 0, 0))],
        out_specs=[pl.BlockSpec((t, h_ * LANES), lambda i: (i, 0)),
                   pl.BlockSpec((t, h_ * LANES), lambda i: (i, 0)),
                   pl.BlockSpec((h_ * V_HD, t), lambda i: (0, i))],
        out_shape=[jax.ShapeDtypeStruct((rows.rows, h_ * LANES), BF16),
                   jax.ShapeDtypeStruct((rows.rows, h_ * LANES), BF16),
                   jax.ShapeDtypeStruct((h_ * V_HD, rows.rows), BF16)],
        compiler_params=_params("arbitrary"),
        name="mla_proj",
    )(x, mod, wd_p, p["q_norm"].reshape(1, -1), p["kv_norm"].reshape(1, -1), wq_p, wk_p, wv_p, cos, sin)

    b_, s_, n_ctx = rows.batch, rows.seq, rows.n_ctx
    tk = MLA_TK
    tq, tq_ctx = min(MLA_TQ, s_), min(MLA_TQ, n_ctx)
    assert n_ctx % tk == 0 and s_ % tk == 0 and n_ctx % tq_ctx == 0 and s_ % tq == 0
    ck, lk = n_ctx // tk, s_ // tk
    ctx_kblocks = b_ * ck

    def kv_block(b, kt):
        return jnp.where(kt < ck, b * ck + kt, ctx_kblocks + b * lk + (kt - ck))

    def scratch(rows_q):
        return [pltpu.VMEM((h_, rows_q), F32), pltpu.VMEM((h_, rows_q), F32), pltpu.VMEM((h_ * V_HD, rows_q), F32)]

    o_shape = jax.ShapeDtypeStruct((rows.rows, h_ * V_HD), BF16)
    lq = s_ // tq
    ctx_qblocks = b_ * n_ctx // tq
    o_lat = pl.pallas_call(
        functools.partial(_mla_attn_kernel, n_heads=h_),
        grid=(b_, lq, ck + lk),
        in_specs=[pl.BlockSpec((tq, h_ * LANES), lambda b, i, kt: (ctx_qblocks + b * lq + i, 0)),
                  pl.BlockSpec((tk, h_ * LANES), lambda b, i, kt: (kv_block(b, kt), 0)),
                  pl.BlockSpec((h_ * V_HD, tk), lambda b, i, kt: (0, kv_block(b, kt)))],
        out_specs=pl.BlockSpec((tq, h_ * V_HD), lambda b, i, kt: (ctx_qblocks + b * lq + i, 0)),
        out_shape=o_shape,
        scratch_shapes=scratch(tq),
        compiler_params=_params("arbitrary", "arbitrary", "arbitrary"),
        name="mla_attn",
    )(q, k, vt)
    if need_ctx:
        cq = n_ctx // tq_ctx
        o = pl.pallas_call(
            functools.partial(_mla_attn_kernel, n_heads=h_),
            grid=(b_, cq, ck),
            in_specs=[pl.BlockSpec((tq_ctx, h_ * LANES), lambda b, i, kt: (b * cq + i, 0)),
                      pl.BlockSpec((tk, h_ * LANES), lambda b, i, kt: (b * ck + kt, 0)),
                      pl.BlockSpec((h_ * V_HD, tk), lambda b, i, kt: (0, b * ck + kt)),
                      pl.BlockSpec(memory_space=pl.ANY)],
            out_specs=pl.BlockSpec((tq_ctx, h_ * V_HD), lambda b, i, kt: (b * cq + i, 0)),
            out_shape=o_shape,
            scratch_shapes=scratch(tq_ctx),
            input_output_aliases={3: 0},
            compiler_params=_params("arbitrary", "arbitrary", "arbitrary"),
            name="mla_ctx_attn",
        )(q, k, vt, o_lat)
    else:
        o = o_lat
    return _out_ln(rows, o, p["w_o"], jnp.zeros((d,), F32), x, mod, ln_g, ln_b, alpha=alpha,
                   first_tile=0 if need_ctx else rows.ctx_tiles)


def _split_gu_kernel(w_ref, g_ref, u_ref):
    win = 2 * MXU_DIM
    r_i = lax.broadcasted_iota(jnp.int32, (win, MXU_DIM), 0)
    c_i = lax.broadcasted_iota(jnp.int32, (win, MXU_DIM), 1)
    pick_even = (r_i == 2 * c_i).astype(BF16)
    pick_odd = (r_i == 2 * c_i + 1).astype(BF16)
    for j in range(w_ref.shape[2] // win):
        w = w_ref[0, :, j * win:(j + 1) * win].astype(BF16)
        g_ref[0, :, j * MXU_DIM:(j + 1) * MXU_DIM] = _bdot(w, pick_even).astype(BF16)
        u_ref[0, :, j * MXU_DIM:(j + 1) * MXU_DIM] = _bdot(w, pick_odd).astype(BF16)


def _split_gate_up(w_gu):
    depth, n_exp, d, f2 = w_gu.shape
    w = w_gu.reshape(depth * n_exp, d, f2)
    tr = PROJ_TILE
    spec_out = pl.BlockSpec((1, tr, f2 // 2), lambda e, r: (e, r, 0))
    shape_out = jax.ShapeDtypeStruct((depth * n_exp, d, f2 // 2), BF16)
    return pl.pallas_call(
        _split_gu_kernel,
        grid=(depth * n_exp, d // tr),
        in_specs=[pl.BlockSpec((1, tr, f2), lambda e, r: (e, r, 0))],
        out_specs=[spec_out, spec_out],
        out_shape=[shape_out, shape_out],
        compiler_params=_params("arbitrary", "arbitrary"),
        name="moe_split_gate_up",
    )(w)


def _seg_rows(t, n_exp):
    return t * TOP_K + n_exp * SEG_ALIGN


def _router_kernel(x_ref, mod_ref, rw_ref, rb_ref, by_token_ref, by_col_ref, cnt_ref, *, top_k):
    m = mod_ref[0]
    h = x_ref[...] * (1.0 + m[4:5]) + m[3:4]
    w_t = rw_ref[...]
    h_hi, w_hi = h.astype(BF16), w_t.astype(BF16)
    h_lo = (h - h_hi.astype(F32)).astype(BF16)
    w_lo = (w_t - w_hi.astype(F32)).astype(BF16)
    logits = _bdot_nt(w_hi, h_hi) + (_bdot_nt(w_lo, h_hi) + _bdot_nt(w_hi, h_lo)) + rb_ref[...]
    n_exp, t = logits.shape
    e_idx = lax.broadcasted_iota(jnp.int32, (n_exp, t), 0).astype(F32)
    work = logits
    sel, val = [], []
    for _ in range(top_k):
        mx = jnp.max(work, axis=0, keepdims=True)
        pick = jnp.min(jnp.where(work == mx, e_idx, float(n_exp)), axis=0, keepdims=True)
        sel.append(pick)
        val.append(mx)
        work = jnp.where(e_idx == pick, -jnp.inf, work)
    ex = [jnp.exp(v - val[0]) for v in val]
    tot = ex[0]
    for e in ex[1:]:
        tot = tot + e
    onehot = jnp.zeros((n_exp, t), F32)
    for pick in sel:
        onehot = onehot + (e_idx == pick).astype(F32)
    r_i = lax.broadcasted_iota(jnp.int32, (t, t), 0)
    c_i = lax.broadcasted_iota(jnp.int32, (t, t), 1)
    before = _bdot(onehot, (r_i < c_i).astype(BF16))
    count = jnp.sum(onehot, axis=1, keepdims=True)
    cap = jnp.floor((count + (SEG_ALIGN - 1)) * (1.0 / SEG_ALIGN)) * SEG_ALIGN
    e_r = lax.broadcasted_iota(jnp.int32, (n_exp, n_exp), 0)
    e_c = lax.broadcasted_iota(jnp.int32, (n_exp, n_exp), 1)
    seg_start = _bdot((e_c < e_r).astype(BF16), jnp.broadcast_to(cap, (n_exp, LANES)))[:, 0:1]
    base = before + seg_start
    rows_out = [jnp.sum(jnp.where(e_idx == sel[k], base, 0.0), axis=0, keepdims=True) for k in range(top_k)]
    rows_out += [ex[k] / tot for k in range(top_k)]
    r8 = lax.broadcasted_iota(jnp.int32, (SUBLANES, t), 0)
    slab = jnp.zeros((SUBLANES, t), F32)
    for k, v in enumerate(rows_out):
        slab = jnp.where(r8 == k, v, slab)
    by_token_ref[...] = slab
    by_col_ref[...] = jnp.concatenate([slab, jnp.zeros((LANES - SUBLANES, t), F32)], axis=0).T
    cnt_ref[0] = count


def _segment_copies(seg_ref, n_exp, make_copy, act):
    *small, big = SEG_PIECES
    for e in range(n_exp):
        g0, rows, l0 = seg_ref[0, 0, e], seg_ref[0, 0, n_exp + e], seg_ref[0, 0, 2 * n_exp + e]

        def whole(j, carry, g0=g0, l0=l0):
            act(make_copy(pl.multiple_of(l0 + j * big, SEG_ALIGN), pl.multiple_of(g0 + j * big, SEG_ALIGN), big))
            return carry

        lax.fori_loop(0, lax.shift_right_logical(rows, big.bit_length() - 1), whole, 0)
        for size in small:
            done = rows & (-2 * size)

            @pl.when((rows & size) != 0)
            def _(g0=g0, l0=l0, done=done, size=size):
                act(make_copy(pl.multiple_of(l0 + done, SEG_ALIGN), pl.multiple_of(g0 + done, SEG_ALIGN), size))


def _segment_waits(seg_ref, n_exp, make_copy):
    total = seg_ref[0, 0, 3 * n_exp]
    *small, big = WAIT_PIECES

    def whole(j, carry):
        make_copy(0, 0, big).wait()
        return carry

    lax.fori_loop(0, lax.shift_right_logical(total, big.bit_length() - 1), whole, 0)
    for size in small:
        @pl.when((total & size) != 0)
        def _(size=size):
            make_copy(0, 0, size).wait()


def _dispatch_kernel(tail_ref, seg_ref, seg_prev_ref, x_ref, mod_ref, slot_t_ref, xs_ref, buf, zbuf, sem, zsem, *,
                     n_exp, top_k):
    i = pl.program_id(0)
    cur = i % 2

    def to_slots(buf_slot, sem_slot):
        return lambda l, g, rows: pltpu.make_async_copy(buf.at[buf_slot, pl.ds(l, rows)], xs_ref.at[pl.ds(g, rows)],
                                                        sem.at[sem_slot])

    @pl.when(i == 0)
    def _():
        zbuf[...] = jnp.zeros_like(zbuf)
        zero_fill = lambda l, g, rows: pltpu.make_async_copy(zbuf.at[pl.ds(l, rows)], xs_ref.at[pl.ds(g, rows)], zsem)
        _segment_copies(tail_ref, n_exp, zero_fill, lambda c: c.start())
        _segment_copies(tail_ref, n_exp, zero_fill, lambda c: c.wait())

    m = mod_ref[0]
    h = (x_ref[...] * (1.0 + m[4:5]) + m[3:4]).astype(BF16)
    n_rows, t = buf.shape[1], x_ref.shape[0]
    row = lax.broadcasted_iota(jnp.int32, (n_rows, t), 0)
    slot_t = slot_t_ref[...].astype(jnp.int32)
    pick = row == slot_t[0:1, :]
    for k in range(1, top_k):
        pick = pick | (row == slot_t[k:k + 1, :])
    buf[cur] = _bdot(pick.astype(BF16), h)
    _segment_copies(seg_ref, n_exp, to_slots(cur, cur), lambda c: c.start())

    @pl.when(i > 0)
    def _():
        _segment_waits(seg_prev_ref, n_exp, to_slots(1 - cur, 1 - cur))

    @pl.when(i == pl.num_programs(0) - 1)
    def _():
        _segment_waits(seg_ref, n_exp, to_slots(cur, cur))


def _expert_kernel(be_ref, nu_ref, x_ref, wg_ref, wu_ref, bg_ref, bu_ref, wd_ref, bd_ref, y_ref):
    del be_ref

    @pl.when(pl.program_id(0) < nu_ref[0])
    def _():
        def gate_up(part):
            xb = x_ref[part * MOE_ROWS:(part + 1) * MOE_ROWS, :].astype(BF16)
            return _bdot(xb, wg_ref[0]), _bdot(xb, wu_ref[0])

        n_parts = x_ref.shape[0] // MOE_ROWS
        nxt = gate_up(0)
        for part in range(n_parts):
            g, u = nxt
            if part + 1 < n_parts:
                nxt = gate_up(part + 1)
            g = jnp.minimum(g + bg_ref[0], SWIGLU_LIMIT)
            u = jnp.clip(u + bu_ref[0], -SWIGLU_LIMIT, SWIGLU_LIMIT)
            act = (u + 1.0) * (g * jax.nn.sigmoid(SWIGLU_ALPHA * g))
            y_ref[part * MOE_ROWS:(part + 1) * MOE_ROWS, :] = _bdot(act, wd_ref[0]) + bd_ref[0]


def _combine_kernel(seg_ref, seg_next_ref, x_ref, mod_ref, route_ref, g_ref, bt_ref, y_ref, o_ref,
                    ybuf, sem, *, n_exp, top_k, alpha):
    i = pl.program_id(0)
    cur = i % 2

    def from_slots(buf_slot):
        return lambda l, g, rows: pltpu.make_async_copy(y_ref.at[pl.ds(g, rows)], ybuf.at[buf_slot, pl.ds(l, rows)],
                                                        sem.at[buf_slot])

    @pl.when(i == 0)
    def _():
        ybuf[...] = jnp.zeros_like(ybuf)
        _segment_copies(seg_ref, n_exp, from_slots(0), lambda c: c.start())

    @pl.when(i < pl.num_programs(0) - 1)
    def _():
        _segment_copies(seg_next_ref, n_exp, from_slots(1 - cur), lambda c: c.start())

    _segment_waits(seg_ref, n_exp, from_slots(cur))
    t, n_rows = x_ref.shape[0], ybuf.shape[1]
    lane = lax.broadcasted_iota(jnp.int32, (t, n_rows), 1)
    route = route_ref[...]
    slot = route[:, :top_k].astype(jnp.int32)
    weights = jnp.zeros((t, n_rows), F32)
    for k in range(top_k):
        weights = jnp.where(lane == slot[:, k:k + 1], route[:, top_k + k:top_k + k + 1], weights)
    acc = _bdot(weights, ybuf[cur])
    m = mod_ref[0]
    r = alpha * x_ref[...] + m[5:6] * acc
    o_ref[...] = _layer_norm(r, g_ref[...], bt_ref[...])


def _moe_layer(rows, x, mod, router_w, router_b, wg, wu, wd, layer, b_gu, b_down, ln_g, ln_b, alpha, need_ctx):
    t, d = rows.tile, x.shape[1]
    n_exp = router_w.shape[1]
    d_ff = wd.shape[1]
    first = 0 if need_ctx else rows.ctx_tiles
    n_tiles = rows.tiles - first
    n_tok = n_tiles * t
    row = lambda i: (i + first, 0)
    modi = lambda i: (rows.mod_index(i + first), 0, 0)
    tok = lambda i: (i, 0)

    seg_rows = _seg_rows(t, n_exp)
    assert 4 * n_exp <= LANES and t % SEG_PIECES[-1] == 0 and MOE_BLOCK % SEG_PIECES[-1] == 0
    assert 2 * TOP_K <= SUBLANES
    slot_t, route, cnt = pl.pallas_call(
        functools.partial(_router_kernel, top_k=TOP_K),
        grid=(n_tiles,),
        in_specs=[pl.BlockSpec((t, d), row), pl.BlockSpec((1, 6, d), modi),
                  pl.BlockSpec((n_exp, d), lambda i: (0, 0)), pl.BlockSpec((n_exp, 1), lambda i: (0, 0))],
        out_specs=[pl.BlockSpec((SUBLANES, t), tok), pl.BlockSpec((t, LANES), tok),
                   pl.BlockSpec((1, n_exp, 1), lambda i: (i, 0, 0))],
        out_shape=[jax.ShapeDtypeStruct((n_tiles * SUBLANES, t), F32), jax.ShapeDtypeStruct((n_tok, LANES), F32),
                   jax.ShapeDtypeStruct((n_tiles, n_exp, 1), F32)],
        compiler_params=_params("arbitrary"),
        name="moe_router",
    )(x, mod, router_w.T, router_b.reshape(n_exp, 1))

    i32 = jnp.int32
    count = cnt.reshape(n_tiles, n_exp).astype(i32)
    cap = (count + SEG_ALIGN - 1) // SEG_ALIGN * SEG_ALIGN
    e_rows = jnp.sum(cap, axis=0)
    e_pad = (e_rows + MOE_BLOCK - 1) // MOE_BLOCK * MOE_BLOCK
    e_end = jnp.cumsum(e_pad)
    e_start = e_end - e_pad
    seg_global = e_start[None, :] + jnp.cumsum(cap, axis=0) - cap
    seg_local = jnp.cumsum(cap, axis=1) - cap
    fill = jnp.zeros((n_tiles, LANES - 3 * n_exp - 1), i32)
    seg = jnp.concatenate([seg_global, cap, seg_local, jnp.sum(cap, axis=1, keepdims=True), fill], axis=1)
    seg = seg.astype(i32).reshape(n_tiles, 1, LANES)
    tail = jnp.concatenate([e_start + e_rows, e_pad - e_rows, jnp.zeros((LANES - 2 * n_exp,), i32)])
    tail = tail.astype(i32).reshape(1, 1, LANES)
    max_slots = n_tok * TOP_K + n_tiles * n_exp * (SEG_ALIGN - 1) + n_exp * (MOE_BLOCK - 1)
    n_blocks = (max_slots + MOE_BLOCK - 1) // MOE_BLOCK
    n_slots = n_blocks * MOE_BLOCK
    n_used = (e_end[-1] // MOE_BLOCK).astype(i32)
    blk = jnp.minimum(jnp.arange(n_blocks, dtype=i32), n_used - 1) * MOE_BLOCK
    block_expert = jnp.minimum(jnp.sum((e_end[None, :] <= blk[:, None]).astype(i32), axis=1), n_exp - 1)

    smem_tile = lambda index_map: pl.BlockSpec((1, 1, LANES), index_map, memory_space=pltpu.SMEM)
    xs = pl.pallas_call(
        functools.partial(_dispatch_kernel, n_exp=n_exp, top_k=TOP_K),
        grid=(n_tiles,),
        in_specs=[smem_tile(lambda i: (0, 0, 0)), smem_tile(lambda i: (i, 0, 0)),
                  smem_tile(lambda i: (jnp.maximum(i - 1, 0), 0, 0)),
                  pl.BlockSpec((t, d), row), pl.BlockSpec((1, 6, d), modi), pl.BlockSpec((SUBLANES, t), tok)],
        out_specs=pl.BlockSpec(memory_space=pl.ANY),
        out_shape=jax.ShapeDtypeStruct((n_slots, d), F32),
        scratch_shapes=[pltpu.VMEM((2, seg_rows, d), F32), pltpu.VMEM((MOE_BLOCK, d), F32),
                        pltpu.SemaphoreType.DMA((2,)), pltpu.SemaphoreType.DMA(())],
        compiler_params=_params("arbitrary"),
        name="moe_dispatch",
    )(tail, seg, seg, x, mod, slot_t)

    bg = b_gu[:, 0::2].reshape(n_exp, 1, d_ff)
    bu = b_gu[:, 1::2].reshape(n_exp, 1, d_ff)
    used = lambda i, be, nu: (jnp.minimum(i, nu[0] - 1), 0)
    wmap = lambda i, be, nu: (be[i], 0, 0)
    wmap_all = lambda i, be, nu: (be[i] + layer * n_exp, 0, 0)
    y = pl.pallas_call(
        _expert_kernel,
        grid_spec=pltpu.PrefetchScalarGridSpec(
            num_scalar_prefetch=2,
            grid=(n_blocks,),
            in_specs=[pl.BlockSpec((MOE_BLOCK, d), used),
                      pl.BlockSpec((1, d, d_ff), wmap_all), pl.BlockSpec((1, d, d_ff), wmap_all),
                      pl.BlockSpec((1, 1, d_ff), wmap), pl.BlockSpec((1, 1, d_ff), wmap),
                      pl.BlockSpec((1, d_ff, d), wmap_all), pl.BlockSpec((1, 1, d), wmap)],
            out_specs=pl.BlockSpec((MOE_BLOCK, d), used)),
        out_shape=jax.ShapeDtypeStruct((n_slots, d), F32),
        compiler_params=_params("arbitrary"),
        name="moe_experts",
    )(block_expert, n_used.reshape(1), xs, wg, wu, bg, bu, wd, b_down.reshape(n_exp, 1, d))

    return pl.pallas_call(
        functools.partial(_combine_kernel, n_exp=n_exp, top_k=TOP_K, alpha=alpha),
        grid=(n_tiles,),
        in_specs=[smem_tile(lambda i: (i, 0, 0)), smem_tile(lambda i: (jnp.minimum(i + 1, n_tiles - 1), 0, 0)),
                  pl.BlockSpec((t, d), row), pl.BlockSpec((1, 6, d), modi),
                  pl.BlockSpec((t, LANES), tok),
                  pl.BlockSpec((1, d), lambda i: (0, 0)), pl.BlockSpec((1, d), lambda i: (0, 0)),
                  pl.BlockSpec(memory_space=pl.ANY)],
        out_specs=pl.BlockSpec((t, d), row if need_ctx else tok),
        out_shape=jax.ShapeDtypeStruct(x.shape if need_ctx else (n_tok, d), F32),
        scratch_shapes=[pltpu.VMEM((2, seg_rows, d), F32), pltpu.SemaphoreType.DMA((2,))],
        input_output_aliases={2: 0} if need_ctx else {},
        compiler_params=_params("arbitrary"),
        name="moe_combine_ln",
    )(seg, seg, x, mod, route, ln_g.reshape(1, d), ln_b.reshape(1, d), y)


def kernel(x, c, ctx, c_ctx, ada_w, ada_b, ln1_g, ln1_b, ln2_g, ln2_b, router_w, router_b, exp_gu_w, exp_gu_b, exp_down_w, exp_down_b, rg_w_in, rg_conv_w, rg_conv_b, rg_gate_a_w, rg_gate_a_b, rg_gate_x_w, rg_gate_x_b, rg_lambda, rg_w_out, gqa_w_qkv, gqa_b_qkv, gqa_sinks, gqa_w_o, gqa_b_o, mla_w_down, mla_q_norm, mla_kv_norm, mla_w_uq, mla_w_ukv, mla_w_o):
    batch, seq, d = x.shape
    n_ctx = ctx.shape[1]
    depth = ada_w.shape[0]
    alpha = (2 * depth) ** 0.25
    rows = _Rows(batch, n_ctx, seq, ROW_TILE)
    prows = _Rows(batch, n_ctx, seq, PROJ_TILE) if (batch * n_ctx) % PROJ_TILE == 0 and seq % PROJ_TILE == 0 else rows
    xa = jnp.concatenate([ctx.reshape(batch * n_ctx, d), x.reshape(batch * seq, d)], axis=0)
    mods = _ada_table(jnp.concatenate([c_ctx[None, :], c], axis=0), ada_w, ada_b)
    wg, wu = _split_gate_up(exp_gu_w)
    wd = exp_down_w.reshape((-1,) + exp_down_w.shape[2:]).astype(BF16)
    for i in range(depth):
        need_ctx = i < depth - 1
        kind, j = i % 3, i // 3
        mod = mods[i]
        if kind == 0:
            prm = dict(w_in=rg_w_in[j], conv_w=rg_conv_w[j], conv_b=rg_conv_b[j], gate_a_w=rg_gate_a_w[j],
                       gate_a_b=rg_gate_a_b[j], gate_x_w=rg_gate_x_w[j], gate_x_b=rg_gate_x_b[j],
                       lam=rg_lambda[j], w_out=rg_w_out[j])
            xa = _rglru_layer(rows, prows, xa, mod, prm, ln1_g[i], ln1_b[i], alpha, need_ctx)
        elif kind == 1:
            prm = dict(w_qkv=gqa_w_qkv[j], b_qkv=gqa_b_qkv[j], sinks=gqa_sinks[j], w_o=gqa_w_o[j], b_o=gqa_b_o[j])
            xa = _gqa_layer(prows, xa, mod, prm, ln1_g[i], ln1_b[i], alpha, need_ctx)
        else:
            prm = dict(w_down=mla_w_down[j], q_norm=mla_q_norm[j], kv_norm=mla_kv_norm[j], w_uq=mla_w_uq[j],
                       w_ukv=mla_w_ukv[j], w_o=mla_w_o[j])
            xa = _mla_layer(prows, xa, mod, prm, ln1_g[i], ln1_b[i], alpha, need_ctx)
        xa = _moe_layer(rows, xa, mod, router_w[i], router_b[i], wg, wu, wd, i, exp_gu_b[i], exp_down_b[i],
                        ln2_g[i], ln2_b[i], alpha, need_ctx)
    return xa.reshape(batch, seq, d)
```

```python
import functools
import math

import jax
import jax.numpy as jnp
from jax import lax
from jax.experimental import pallas as pl
from jax.experimental.pallas import tpu as pltpu

F32 = jnp.float32
BF16 = jnp.bfloat16

GRID_W = 64
LN_EPS = 1e-5
RMS_EPS = 1e-6
ROPE_THETA = 10000.0
NEG_INF = -1e30
RG_C = 8.0
SQRT_FLOOR = 1e-30
GQA_KV = 2
GQA_HD = 64
WINDOW = 128
MLA_HEADS = 16
QK_NOPE = 64
QK_ROPE = 32
V_HD = 64
TOP_K = 4
SWIGLU_LIMIT = 7.0
SWIGLU_ALPHA = 1.702
MOE_BLOCK = 512
MOE_ROWS = 256

LANES = 128
SUBLANES = 8
BF16_ROWS = 16
ROW_TILE = 256
PROJ_TILE = 512
ATTN_Q = 128
GQA_LOOKAHEAD = 4
MLA_TQ = 1024
MLA_TK = 256
MLA_SUB_K = 128
MLA_SUB_Q = 256
MLA_LOOKAHEAD = 12
MXU_DIM = 256
SEG_ALIGN = SUBLANES
SEG_PIECES = (8, 16, 32)
WAIT_PIECES = (8, 16, 32, 64, 128, 256)
VMEM_LIMIT = 56 * 1024 * 1024


def _params(*sem):
    return pltpu.CompilerParams(dimension_semantics=sem, vmem_limit_bytes=VMEM_LIMIT)


def _bdot(a, b):
    return jnp.dot(a.astype(BF16), b.astype(BF16), preferred_element_type=F32)


def _bdot_nt(a, b):
    return lax.dot_general(a.astype(BF16), b.astype(BF16), (((1,), (1,)), ((), ())),
                           preferred_element_type=F32)


def _layer_norm(r, g, b):
    mu = jnp.mean(r, axis=-1, keepdims=True)
    d = r - mu
    var = jnp.mean(d * d, axis=-1, keepdims=True)
    return d * lax.rsqrt(var + LN_EPS) * g + b


def _sigmoid(x):
    return 0.5 * jnp.tanh(0.5 * x) + 0.5


def _gelu_tanh(x):
    return 0.5 * x * (1.0 + jnp.tanh(math.sqrt(2.0 / math.pi) * (x + 0.044715 * (x * x * x))))


class _Rows:
    def __init__(self, batch, n_ctx, seq, tile):
        assert (batch * n_ctx) % tile == 0 and seq % tile == 0
        self.batch, self.n_ctx, self.seq, self.tile = batch, n_ctx, seq, tile
        self.ctx_tiles = batch * n_ctx // tile
        self.lat_tiles = batch * seq // tile
        self.tiles = self.ctx_tiles + self.lat_tiles
        self.lat_per_batch = seq // tile
        self.ctx_per_batch = n_ctx // tile
        self.rows = batch * (n_ctx + seq)

    def mod_index(self, i):
        return jnp.where(i < self.ctx_tiles, 0, 1 + (i - self.ctx_tiles) // self.lat_per_batch)

    def rope_index(self, i):
        return jnp.where(i < self.ctx_tiles, self.lat_per_batch, (i - self.ctx_tiles) % self.lat_per_batch)


def _ada_kernel(c_ref, w_ref, b_ref, o_ref):
    cv = c_ref[...]
    s = cv * jax.nn.sigmoid(cv)
    o_ref[0] = jnp.dot(s, w_ref[0], preferred_element_type=F32,
                       precision=lax.Precision.HIGHEST) + b_ref[0]


def _ada_table(cvec, ada_w, ada_b):
    depth, d, d6 = ada_w.shape
    n = cvec.shape[0]
    chunk = d
    out = pl.pallas_call(
        _ada_kernel,
        grid=(depth, d6 // chunk),
        in_specs=[pl.BlockSpec((n, d), lambda l, j: (0, 0)),
                  pl.BlockSpec((1, d, chunk), lambda l, j: (l, 0, j)),
                  pl.BlockSpec((1, 1, chunk), lambda l, j: (l, 0, j))],
        out_specs=pl.BlockSpec((1, n, chunk), lambda l, j: (l, 0, j)),
        out_shape=jax.ShapeDtypeStruct((depth, n, d6), F32),
        compiler_params=_params("arbitrary", "arbitrary"),
        name="ada_table",
    )(cvec, ada_w, ada_b.reshape(depth, 1, d6))
    return out.reshape(depth, n, 6, d)


def _out_ln_kernel(z_ref, w_ref, b_ref, x_ref, mod_ref, g_ref, bt_ref, o_ref, *, gate_row, alpha):
    y = _bdot(z_ref[...], w_ref[...]) + b_ref[...]
    m = mod_ref[0]
    r = alpha * x_ref[...] + m[gate_row:gate_row + 1] * y
    o_ref[...] = _layer_norm(r, g_ref[...], bt_ref[...])


def _out_ln(rows, z, w, bias, x, mod, ln_g, ln_b, *, alpha, first_tile=0):
    t, d = rows.tile, x.shape[1]
    kdim = z.shape[1]
    n_tiles = rows.tiles - first_tile
    row = lambda i: (i + first_tile, 0)
    return pl.pallas_call(
        functools.partial(_out_ln_kernel, gate_row=2, alpha=alpha),
        grid=(n_tiles,),
        in_specs=[pl.BlockSpec((t, kdim), row),
                  pl.BlockSpec((kdim, d), lambda i: (0, 0)),
                  pl.BlockSpec((1, d), lambda i: (0, 0)),
                  pl.BlockSpec((t, d), row),
                  pl.BlockSpec((1, 6, d), lambda i: (rows.mod_index(i + first_tile), 0, 0)),
                  pl.BlockSpec((1, d), lambda i: (0, 0)),
                  pl.BlockSpec((1, d), lambda i: (0, 0))],
        out_specs=pl.BlockSpec((t, d), row),
        out_shape=jax.ShapeDtypeStruct(x.shape, F32),
        input_output_aliases={3: 0},
        compiler_params=_params("arbitrary"),
        name="out_proj_ln",
    )(z, w.astype(BF16), bias.reshape(1, d), x, mod, ln_g.reshape(1, d), ln_b.reshape(1, d))


def _rg_in_kernel(x_ref, mod_ref, w_ref, gel_ref, rec_ref, *, d_rnn):
    m = mod_ref[0]
    h = (x_ref[...] * (1.0 + m[1:2]) + m[0:1]).astype(BF16)
    gel_ref[...] = _gelu_tanh(_bdot(h, w_ref[:, :d_rnn])).astype(BF16)
    rec_ref[...] = _bdot(h, w_ref[:, d_rnn:])


def _rg_in(rows, x, mod, w_in):
    t, d = rows.tile, x.shape[1]
    d_rnn = w_in.shape[1] // 2
    return pl.pallas_call(
        functools.partial(_rg_in_kernel, d_rnn=d_rnn),
        grid=(rows.tiles,),
        in_specs=[pl.BlockSpec((t, d), lambda i: (i, 0)),
                  pl.BlockSpec((1, 6, d), lambda i: (rows.mod_index(i), 0, 0)),
                  pl.BlockSpec((d, 2 * d_rnn), lambda i: (0, 0))],
        out_specs=[pl.BlockSpec((t, d_rnn), lambda i: (i, 0)),
                   pl.BlockSpec((t, d_rnn), lambda i: (i, 0))],
        out_shape=[jax.ShapeDtypeStruct((rows.rows, d_rnn), BF16),
                   jax.ShapeDtypeStruct((rows.rows, d_rnn), F32)],
        compiler_params=_params("arbitrary"),
        name="rg_in_proj",
    )(x, mod, w_in.astype(BF16))


def _rg_scan_kernel(*refs, reverse, fuse_out, n_blocks, block_w, tile, ctx_tiles, lat_tiles):
    if fuse_out:
        (x_ref, xp_ref, xn_ref, cw_ref, cb_ref, wa_ref, ba_ref, wx_ref, bx_ref, lam_ref,
         hf_ref, gel_ref, out_ref, a_scr, u_scr, h_scr, carry_scr) = refs
    else:
        (x_ref, xp_ref, xn_ref, cw_ref, cb_ref, wa_ref, ba_ref, wx_ref, bx_ref, lam_ref,
         out_ref, a_scr, u_scr, carry_scr) = refs
        h_scr = out_ref
    j = pl.program_id(1)
    is_ctx = j < ctx_tiles
    n_seq = jnp.where(is_ctx, ctx_tiles, lat_tiles)
    step = jnp.where(is_ctx, j, j - ctx_tiles)
    pos = (n_seq - 1 - step) if reverse else step
    prev_ok = (pos > 0).astype(F32)
    next_ok = (pos < n_seq - 1).astype(F32)

    @pl.when(j == 0)
    def _():
        carry_scr[...] = jnp.zeros_like(carry_scr)

    row8 = lax.broadcasted_iota(jnp.int32, (SUBLANES, block_w), 0)
    for n in range(n_blocks):
        cols = slice(n * block_w, (n + 1) * block_w)
        x = x_ref[:, cols]
        prev = xp_ref[SUBLANES - 1:SUBLANES, cols] * prev_ok
        nxt0 = xn_ref[0:1, cols] * next_ok
        nxt1 = xn_ref[1:2, cols] * next_ok
        cw = [cw_ref[k:k + 1, cols] for k in range(4)]
        xc = (cw[0] * pltpu.roll(x, 1, 0) + cw[1] * x + cw[2] * pltpu.roll(x, tile - 1, 0)
              + cw[3] * pltpu.roll(x, tile - 2, 0) + cb_ref[:, cols])
        first, last = x[0:1, :], x[tile - 1:tile, :]
        head = xc[:SUBLANES] + jnp.where(row8 == 0, cw[0] * (prev - last), 0.0)
        tail = xc[tile - SUBLANES:] + jnp.where(
            row8 == SUBLANES - 2, cw[3] * (nxt0 - first),
            jnp.where(row8 == SUBLANES - 1, cw[2] * (nxt0 - first) + cw[3] * (nxt1 - x[1:2, :]), 0.0))
        xc = jnp.concatenate([head, xc[SUBLANES:tile - SUBLANES], tail], axis=0)
        xb = xc.astype(BF16)
        r = _sigmoid(_bdot(xb, wa_ref[n]) + ba_ref[:, cols])
        gi = _sigmoid(_bdot(xb, wx_ref[n]) + bx_ref[:, cols])
        z = -lam_ref[:, cols]
        softplus = jnp.maximum(z, 0.0) + jnp.log1p(jnp.exp(-jnp.abs(z)))
        a = jnp.exp2(r * ((-RG_C * math.log2(math.e)) * softplus))
        a_scr[:, cols] = a
        v = 1.0 - a * a
        u_scr[:, cols] = (v * lax.rsqrt(jnp.maximum(v, SQRT_FLOOR))) * (gi * xc)

    width = n_blocks * block_w
    sub = lax.broadcasted_iota(jnp.int32, (SUBLANES, width), 0)
    groups = tile // SUBLANES

    def body(g, carry):
        gg = (groups - 1 - g) if reverse else g
        r0 = pl.multiple_of(gg * SUBLANES, SUBLANES)
        a8 = a_scr[pl.ds(r0, SUBLANES), :]
        u8 = u_scr[pl.ds(r0, SUBLANES), :]
        for s in (1, 2, 4):
            if reverse:
                a_sh, u_sh, ok = pltpu.roll(a8, SUBLANES - s, 0), pltpu.roll(u8, SUBLANES - s, 0), sub < SUBLANES - s
            else:
                a_sh, u_sh, ok = pltpu.roll(a8, s, 0), pltpu.roll(u8, s, 0), sub >= s
            u8 = jnp.where(ok, a8 * u_sh + u8, u8)
            a8 = jnp.where(ok, a8 * a_sh, a8)
        h8 = a8 * carry + u8
        h_scr[pl.ds(r0, SUBLANES), :] = h8
        return h8[0:1, :] if reverse else h8[SUBLANES - 1:SUBLANES, :]

    carry_scr[...] = lax.fori_loop(0, groups, body, carry_scr[...])
    if fuse_out:
        out_ref[...] = ((hf_ref[...] + h_scr[...]) * gel_ref[...].astype(F32)).astype(BF16)


def _rg_scan(rows, rec, conv_w, conv_b, wa, ba, wx, bx, lam, *, reverse, h_fwd=None, gel=None):
    t = rows.tile
    c = rec.shape[1]
    n_blocks, block_w = wa.shape[0], wa.shape[1]
    nc, nl = rows.ctx_per_batch, rows.lat_per_batch
    halo = t // SUBLANES
    last_halo = rows.rows // SUBLANES - 1
    fuse_out = h_fwd is not None

    def tile_index(b, j):
        is_ctx = j < nc
        step = jnp.where(is_ctx, j, j - nc)
        n_seq = jnp.where(is_ctx, nc, nl)
        pos = (n_seq - 1 - step) if reverse else step
        return jnp.where(is_ctx, b * nc + pos, rows.ctx_tiles + b * nl + pos)

    cur = lambda b, j: (tile_index(b, j), 0)
    prv = lambda b, j: (jnp.maximum(tile_index(b, j) * halo - 1, 0), 0)
    nxt = lambda b, j: (jnp.minimum((tile_index(b, j) + 1) * halo, last_halo), 0)
    full2 = lambda b, j: (0, 0)
    full3 = lambda b, j: (0, 0, 0)
    in_specs = [pl.BlockSpec((t, c), cur), pl.BlockSpec((SUBLANES, c), prv), pl.BlockSpec((SUBLANES, c), nxt),
                pl.BlockSpec((4, c), full2), pl.BlockSpec((1, c), full2),
                pl.BlockSpec((n_blocks, block_w, block_w), full3), pl.BlockSpec((1, c), full2),
                pl.BlockSpec((n_blocks, block_w, block_w), full3), pl.BlockSpec((1, c), full2),
                pl.BlockSpec((1, c), full2)]
    args = [rec, rec, rec, conv_w, conv_b.reshape(1, c), wa.astype(BF16), ba.reshape(1, c),
            wx.astype(BF16), bx.reshape(1, c), lam.reshape(1, c)]
    scratch = [pltpu.VMEM((t, c), F32), pltpu.VMEM((t, c), F32)]
    if fuse_out:
        in_specs += [pl.BlockSpec((t, c), cur), pl.BlockSpec((t, c), cur)]
        args += [h_fwd, gel]
        scratch.append(pltpu.VMEM((t, c), F32))
    scratch.append(pltpu.VMEM((1, c), F32))
    return pl.pallas_call(
        functools.partial(_rg_scan_kernel, reverse=reverse, fuse_out=fuse_out, n_blocks=n_blocks,
                          block_w=block_w, tile=t, ctx_tiles=nc, lat_tiles=nl),
        grid=(rows.batch, nc + nl),
        in_specs=in_specs,
        out_specs=pl.BlockSpec((t, c), cur),
        out_shape=jax.ShapeDtypeStruct((rows.rows, c), BF16 if fuse_out else F32),
        scratch_shapes=scratch,
        compiler_params=_params("arbitrary", "arbitrary"),
        name="rg_scan_bwd" if reverse else "rg_scan_fwd",
    )(*args)


def _rglru_layer(rows, prows, x, mod, p, ln_g, ln_b, alpha, need_ctx):
    gel, rec = _rg_in(prows, x, mod, p["w_in"])
    h_fwd = _rg_scan(rows, rec, p["conv_w"], p["conv_b"], p["gate_a_w"][0], p["gate_a_b"][0],
                     p["gate_x_w"][0], p["gate_x_b"][0], p["lam"][0], reverse=False)
    z = _rg_scan(rows, rec, p["conv_w"], p["conv_b"], p["gate_a_w"][1], p["gate_a_b"][1],
                 p["gate_x_w"][1], p["gate_x_b"][1], p["lam"][1], reverse=True, h_fwd=h_fwd, gel=gel)
    d = x.shape[1]
    return _out_ln(prows, z, p["w_out"], jnp.zeros((d,), F32), x, mod, ln_g, ln_b, alpha=alpha,
                   first_tile=0 if need_ctx else prows.ctx_tiles)


def _axial_angles(seq, rot_dim):
    pos = jnp.arange(seq, dtype=jnp.int32)
    row = (pos // GRID_W).astype(F32)
    col = (pos % GRID_W).astype(F32)
    n_freq = rot_dim // 4
    inv_freq = ROPE_THETA ** (-jnp.arange(n_freq, dtype=F32) / n_freq)
    return jnp.concatenate([row[:, None] * inv_freq, col[:, None] * inv_freq], axis=-1)


def _rope_tables(rows, rot_dim, lead, trail):
    ang = _axial_angles(rows.seq, rot_dim)
    cos, sin = jnp.cos(ang), jnp.sin(ang)
    ones = lambda w: jnp.ones((rows.seq, w), F32)
    zeros = lambda w: jnp.zeros((rows.seq, w), F32)
    c = jnp.concatenate([ones(lead), cos, cos, ones(trail)], axis=-1)
    s = jnp.concatenate([zeros(lead), -sin, sin, zeros(trail)], axis=-1)
    reps = LANES // c.shape[1]
    c, s = jnp.tile(c, (1, reps)), jnp.tile(s, (1, reps))
    t = rows.tile
    c = jnp.concatenate([c.reshape(rows.lat_per_batch, t, LANES), jnp.ones((1, t, LANES), F32)], axis=0)
    s = jnp.concatenate([s.reshape(rows.lat_per_batch, t, LANES), jnp.zeros((1, t, LANES), F32)], axis=0)
    return c, s


def _rope_chunk(x, cos, sin, half):
    lane = lax.broadcasted_iota(jnp.int32, x.shape, 1)
    partner = jnp.where((lane % (2 * half)) < half, pltpu.roll(x, LANES - half, 1), pltpu.roll(x, half, 1))
    return x * cos + partner * sin


def _gqa_proj_kernel(x_ref, mod_ref, w_ref, b_ref, cos_ref, sin_ref, q_ref, kv_ref, *, q_dim, scale, half):
    m = mod_ref[0]
    h = (x_ref[...] * (1.0 + m[1:2]) + m[0:1]).astype(BF16)
    cos, sin = cos_ref[0], sin_ref[0]
    n_q = q_dim // LANES
    n_all = w_ref.shape[1] // LANES
    for c in range(n_all):
        cols = slice(c * LANES, (c + 1) * LANES)
        p = _bdot(h, w_ref[:, cols]) + b_ref[:, cols]
        is_v = c in (n_q + 1, n_q + 3)
        if not is_v:
            p = _rope_chunk(p, cos, sin, half)
        if c < n_q:
            q_ref[:, cols] = (p * scale).astype(BF16)
        else:
            kv_ref[:, (c - n_q) * LANES:(c - n_q + 1) * LANES] = p.astype(BF16)


def _gqa_attn_kernel(*refs, windowed, window, seq, n_pairs, pairs_per_kv):
    if windowed:
        q_ref, kv_ref, kvc_ref, sink_ref, o_ref = refs
    else:
        q_ref, kvc_ref, sink_ref, _, o_ref = refs
    tq = q_ref.shape[0]
    kvc = kvc_ref[...]
    if windowed:
        span = tq + 2 * window
        qs = pl.program_id(1) * tq
        ws = pl.multiple_of(jnp.clip(qs - window, 0, seq - span), LANES)
        kv = jnp.concatenate([kv_ref[pl.ds(ws, span), :], kvc], axis=0)
        n_keys = kv.shape[0]
        qpos = qs + lax.broadcasted_iota(jnp.int32, (tq, n_keys), 0)
        col = lax.broadcasted_iota(jnp.int32, (tq, n_keys), 1)
        mask = (jnp.abs(ws + col - qpos) <= window) | (col >= span)
    else:
        kv = kvc
        mask = None
    lane = lax.broadcasted_iota(jnp.int32, (kv.shape[0], LANES), 1)
    low = lane < GQA_HD
    zero = jnp.zeros((kv.shape[0], LANES), BF16)
    one_hi = (lane == GQA_HD).astype(F32).astype(BF16)
    one_lo = (lane == 0).astype(F32).astype(BF16)
    low_q = lax.broadcasted_iota(jnp.int32, (tq, LANES), 1) < GQA_HD
    k_plain, v_plain = kv[:, 0:LANES], kv[:, LANES:2 * LANES]
    k_swap, v_swap = kv[:, 2 * LANES:3 * LANES], kv[:, 3 * LANES:4 * LANES]
    k_half, v_half = {}, {}
    for g in range(GQA_KV):
        k_half[g, 0] = jnp.where(low, k_plain if g == 0 else k_swap, zero)
        k_half[g, 1] = jnp.where(low, zero, k_swap if g == 0 else k_plain)
        v_half[g, 0] = jnp.where(low, v_plain if g == 0 else v_swap, one_hi)
        v_half[g, 1] = jnp.where(low, one_lo, v_swap if g == 0 else v_plain)
    items = [(pr, hh) for pr in range(n_pairs) for hh in range(2)]

    def scores(item):
        pr, hh = item
        return _bdot_nt(q_ref[:, pr * LANES:(pr + 1) * LANES], k_half[pr // pairs_per_kv, hh])

    ahead = [scores(it) for it in items[:GQA_LOOKAHEAD]]
    acc = None
    for n, (pr, hh) in enumerate(items):
        s = ahead.pop(0)
        if n + GQA_LOOKAHEAD < len(items):
            ahead.append(scores(items[n + GQA_LOOKAHEAD]))
        head = 2 * pr + hh
        if mask is not None:
            s = jnp.where(mask, s, NEG_INF)
        sk = sink_ref[head:head + 1, 0:1]
        mx = jnp.maximum(jnp.max(s, axis=1, keepdims=True), sk)
        p = jnp.exp((s - mx).astype(BF16))
        pv = _bdot(p, v_half[pr // pairs_per_kv, hh])
        sum_lane = GQA_HD if hh == 0 else 0
        denom = pv[:, sum_lane:sum_lane + 1] + jnp.exp(sk - mx)
        part = pv * (1.0 / denom)
        if hh == 1:
            o_ref[:, pr * LANES:(pr + 1) * LANES] = jnp.where(low_q, acc, part).astype(BF16)
        acc = part


def _gqa_layer(rows, x, mod, p, ln_g, ln_b, alpha, need_ctx):
    t, d = rows.tile, x.shape[1]
    w_qkv, b_qkv = p["w_qkv"], p["b_qkv"]
    kv_dim = GQA_KV * GQA_HD
    q_dim = w_qkv.shape[1] - 2 * kv_dim
    n_heads = q_dim // GQA_HD
    assert kv_dim == LANES and GQA_KV == 2 and q_dim % LANES == 0
    swap = lambda a: jnp.concatenate([a[..., GQA_HD:], a[..., :GQA_HD]], axis=-1)
    wk, wv = w_qkv[:, q_dim:q_dim + kv_dim], w_qkv[:, q_dim + kv_dim:]
    bk, bv = b_qkv[q_dim:q_dim + kv_dim], b_qkv[q_dim + kv_dim:]
    w_ext = jnp.concatenate([w_qkv, swap(wk), swap(wv)], axis=1).astype(BF16)
    b_ext = jnp.concatenate([b_qkv, swap(bk), swap(bv)]).reshape(1, -1)
    n_ext = w_ext.shape[1]
    cos, sin = _rope_tables(rows, GQA_HD, 0, 0)
    q, kv = pl.pallas_call(
        functools.partial(_gqa_proj_kernel, q_dim=q_dim, scale=GQA_HD ** -0.5, half=GQA_HD // 2),
        grid=(rows.tiles,),
        in_specs=[pl.BlockSpec((t, d), lambda i: (i, 0)),
                  pl.BlockSpec((1, 6, d), lambda i: (rows.mod_index(i), 0, 0)),
                  pl.BlockSpec((d, n_ext), lambda i: (0, 0)),
                  pl.BlockSpec((1, n_ext), lambda i: (0, 0)),
                  pl.BlockSpec((1, t, LANES), lambda i: (rows.rope_index(i), 0, 0)),
                  pl.BlockSpec((1, t, LANES), lambda i: (rows.rope_index(i), 0, 0))],
        out_specs=[pl.BlockSpec((t, q_dim), lambda i: (i, 0)),
                   pl.BlockSpec((t, 4 * LANES), lambda i: (i, 0))],
        out_shape=[jax.ShapeDtypeStruct((rows.rows, q_dim), BF16),
                   jax.ShapeDtypeStruct((rows.rows, 4 * LANES), BF16)],
        compiler_params=_params("arbitrary"),
        name="gqa_qkv_proj",
    )(x, mod, w_ext, b_ext, cos, sin)

    sinks = jnp.broadcast_to(p["sinks"].astype(F32)[:, None], (n_heads, LANES))
    b_, s_, n_ctx = rows.batch, rows.seq, rows.n_ctx
    ctx_rows = b_ * n_ctx
    assert ctx_rows % s_ == 0 and s_ >= ATTN_Q + 2 * WINDOW
    q_blocks = s_ // ATTN_Q
    n_pairs = q_dim // LANES
    common = dict(window=WINDOW, seq=s_, n_pairs=n_pairs, pairs_per_kv=n_pairs // GQA_KV)
    o_shape = jax.ShapeDtypeStruct((rows.rows, q_dim), BF16)
    o_lat = pl.pallas_call(
        functools.partial(_gqa_attn_kernel, windowed=True, **common),
        grid=(b_, q_blocks),
        in_specs=[pl.BlockSpec((ATTN_Q, q_dim), lambda b, j: (ctx_rows // ATTN_Q + b * q_blocks + j, 0)),
                  pl.BlockSpec((s_, 4 * LANES), lambda b, j: (ctx_rows // s_ + b, 0)),
                  pl.BlockSpec((n_ctx, 4 * LANES), lambda b, j: (b, 0)),
                  pl.BlockSpec((n_heads, LANES), lambda b, j: (0, 0))],
        out_specs=pl.BlockSpec((ATTN_Q, q_dim), lambda b, j: (ctx_rows // ATTN_Q + b * q_blocks + j, 0)),
        out_shape=o_shape,
        compiler_params=_params("arbitrary", "arbitrary"),
        name="gqa_window_attn",
    )(q, kv, kv, sinks)
    if need_ctx:
        cq_blocks = n_ctx // ATTN_Q
        o = pl.pallas_call(
            functools.partial(_gqa_attn_kernel, windowed=False, **common),
            grid=(b_, cq_blocks),
            in_specs=[pl.BlockSpec((ATTN_Q, q_dim), lambda b, j: (b * cq_blocks + j, 0)),
                      pl.BlockSpec((n_ctx, 4 * LANES), lambda b, j: (b, 0)),
                      pl.BlockSpec((n_heads, LANES), lambda b, j: (0, 0)),
                      pl.BlockSpec(memory_space=pl.ANY)],
            out_specs=pl.BlockSpec((ATTN_Q, q_dim), lambda b, j: (b * cq_blocks + j, 0)),
            out_shape=o_shape,
            input_output_aliases={3: 0},
            compiler_params=_params("arbitrary", "arbitrary"),
            name="gqa_ctx_attn",
        )(q, kv, sinks, o_lat)
    else:
        o = o_lat
    return _out_ln(rows, o, p["w_o"], p["b_o"], x, mod, ln_g, ln_b, alpha=alpha,
                   first_tile=0 if need_ctx else rows.ctx_tiles)


def _mla_proj_kernel(x_ref, mod_ref, wd_ref, qn_ref, kvn_ref, wq_ref, wk_ref, wv_ref, cos_ref, sin_ref,
                     q_ref, k_ref, vt_ref, *, q_lora, kv_lora, scale, n_heads):
    m = mod_ref[0]
    h = (x_ref[...] * (1.0 + m[1:2]) + m[0:1]).astype(BF16)
    p = _bdot(h, wd_ref[...])
    cq, ckv = p[:, :q_lora], p[:, q_lora:q_lora + kv_lora]
    cos, sin = cos_ref[0], sin_ref[0]
    k_rope = _rope_chunk(p[:, q_lora + kv_lora:], cos, sin, QK_ROPE // 2)
    cq = (cq * lax.rsqrt(jnp.mean(cq * cq, axis=-1, keepdims=True) + RMS_EPS) * qn_ref[...]).astype(BF16)
    ckv = (ckv * lax.rsqrt(jnp.mean(ckv * ckv, axis=-1, keepdims=True) + RMS_EPS) * kvn_ref[...]).astype(BF16)
    for hd in range(n_heads):
        cols = slice(hd * LANES, (hd + 1) * LANES)
        qh = _rope_chunk(_bdot(cq, wq_ref[:, cols]), cos, sin, QK_ROPE // 2)
        q_ref[:, cols] = (qh * scale).astype(BF16)
        k_ref[:, cols] = (_bdot(ckv, wk_ref[:, cols]) + k_rope).astype(BF16)
    vt_ref[...] = _bdot(ckv, wv_ref[...]).T.astype(BF16)


def _mla_attn_kernel(*refs, n_heads):
    q_ref, k_ref, vt_ref = refs[:3]
    o_ref, m_scr, l_scr, acc_scr = refs[-4:]
    kt = pl.program_id(2)

    @pl.when(kt == 0)
    def _():
        m_scr[...] = jnp.full_like(m_scr, -jnp.inf)
        l_scr[...] = jnp.zeros_like(l_scr)
        acc_scr[...] = jnp.zeros_like(acc_scr)

    tq, tk = q_ref.shape[0], k_ref.shape[0]
    sub_q, sub_k = min(MLA_SUB_Q, tq), min(MLA_SUB_K, tk)
    n_kh = tk // sub_k
    ones = jnp.ones((BF16_ROWS, sub_k), BF16)
    items = [(hd, kh, qh) for hd in range(n_heads) for qh in range(tq // sub_q) for kh in range(n_kh)]

    def scores(item):
        hd, kh, qh = item
        cols = slice(hd * LANES, (hd + 1) * LANES)
        s = _bdot_nt(k_ref[kh * sub_k:(kh + 1) * sub_k, cols], q_ref[qh * sub_q:(qh + 1) * sub_q, cols])
        return s, jnp.max(s, axis=0, keepdims=True)

    ahead = [scores(it) for it in items[:MLA_LOOKAHEAD]]
    state = {}
    for n, (hd, kh, qh) in enumerate(items):
        rws = slice(hd * V_HD, (hd + 1) * V_HD)
        qcols = slice(qh * sub_q, (qh + 1) * sub_q)
        s_t, s_max = ahead.pop(0)
        if n + MLA_LOOKAHEAD < len(items):
            ahead.append(scores(items[n + MLA_LOOKAHEAD]))
        if kh == 0:
            state[hd, qh] = (m_scr[hd:hd + 1, qcols], l_scr[hd:hd + 1, qcols], acc_scr[rws, qcols])
        m_old, l_old, acc = state[hd, qh]
        m_new = jnp.maximum(m_old, s_max)
        p_t = jnp.exp2((s_t - m_new).astype(BF16))
        corr = jnp.exp2(m_old - m_new)
        vt = vt_ref[rws, kh * sub_k:(kh + 1) * sub_k]
        pv = _bdot(jnp.concatenate([vt, ones], axis=0), p_t)
        state[hd, qh] = (m_new, corr * l_old + pv[V_HD:V_HD + 1, :], acc * corr + pv[:V_HD, :])
        if kh == n_kh - 1:
            m_scr[hd:hd + 1, qcols], l_scr[hd:hd + 1, qcols], acc_scr[rws, qcols] = state.pop((hd, qh))

    @pl.when(kt == pl.num_programs(2) - 1)
    def _():
        for hd in range(n_heads):
            rws = slice(hd * V_HD, (hd + 1) * V_HD)
            acc_scr[rws, :] = acc_scr[rws, :] * (1.0 / l_scr[hd:hd + 1, :])
        o_ref[...] = acc_scr[...].T.astype(BF16)


def _mla_layer(rows, x, mod, p, ln_g, ln_b, alpha, need_ctx):
    t, d = rows.tile, x.shape[1]
    h_ = MLA_HEADS
    w_down, w_uq, w_ukv = p["w_down"], p["w_uq"], p["w_ukv"]
    q_lora = w_uq.shape[0]
    kv_lora = w_ukv.shape[0]
    qk = QK_NOPE + QK_ROPE
    assert QK_NOPE == V_HD == LANES // 2 and h_ % 2 == 0 and q_lora % LANES == 0 and kv_lora % LANES == 0
    pad = LANES - qk
    zc = lambda r, w: jnp.zeros((r, w), F32)
    wd_p = jnp.concatenate([w_down[:, :q_lora + kv_lora], zc(d, QK_NOPE), w_down[:, q_lora + kv_lora:],
                            zc(d, pad)], axis=1).astype(BF16)
    wq_p = jnp.concatenate([w_uq.reshape(q_lora, h_, qk), jnp.zeros((q_lora, h_, pad), F32)],
                           axis=-1).reshape(q_lora, h_ * LANES).astype(BF16)
    ukv = w_ukv.reshape(kv_lora, h_, QK_NOPE + V_HD)
    wk_p = jnp.concatenate([ukv[..., :QK_NOPE], jnp.zeros((kv_lora, h_, LANES - QK_NOPE), F32)],
                           axis=-1).reshape(kv_lora, h_ * LANES).astype(BF16)
    wv_p = ukv[..., QK_NOPE:].reshape(kv_lora, h_ * V_HD).astype(BF16)
    cos, sin = _rope_tables(rows, QK_ROPE, QK_NOPE, pad)
    n_down = wd_p.shape[1]
    q, k, vt = pl.pallas_call(
        functools.partial(_mla_proj_kernel, q_lora=q_lora, kv_lora=kv_lora, scale=qk ** -0.5 * math.log2(math.e),
                          n_heads=h_),
        grid=(rows.tiles,),
        in_specs=[pl.BlockSpec((t, d), lambda i: (i, 0)),
                  pl.BlockSpec((1, 6, d), lambda i: (rows.mod_index(i), 0, 0)),
                  pl.BlockSpec((d, n_down), lambda i: (0, 0)),
                  pl.BlockSpec((1, q_lora), lambda i: (0, 0)),
                  pl.BlockSpec((1, kv_lora), lambda i: (0, 0)),
                  pl.BlockSpec((q_lora, h_ * LANES), lambda i: (0, 0)),
                  pl.BlockSpec((kv_lora, h_ * LANES), lambda i: (0, 0)),
                  pl.BlockSpec((kv_lora, h_ * V_HD), lambda i: (0, 0)),
                  pl.BlockSpec((1, t, LANES), lambda i: (rows.rope_index(i), 0, 0)),
                  pl.BlockSpec((1, t, LANES), lambda i: (rows.rope_index(i), 0, 0))],
        out_specs=[pl.BlockSpec((t, h_ * LANES), lambda i: (i, 0)),
                   pl.BlockSpec((t, h_ * LANES), lambda i: (i, 0)),
                   pl.BlockSpec((h_ * V_HD, t), lambda i: (0, i))],
        out_shape=[jax.ShapeDtypeStruct((rows.rows, h_ * LANES), BF16),
                   jax.ShapeDtypeStruct((rows.rows, h_ * LANES), BF16),
                   jax.ShapeDtypeStruct((h_ * V_HD, rows.rows), BF16)],
        compiler_params=_params("arbitrary"),
        name="mla_proj",
    )(x, mod, wd_p, p["q_norm"].reshape(1, -1), p["kv_norm"].reshape(1, -1), wq_p, wk_p, wv_p, cos, sin)

    b_, s_, n_ctx = rows.batch, rows.seq, rows.n_ctx
    tk = MLA_TK
    tq, tq_ctx = min(MLA_TQ, s_), min(MLA_TQ, n_ctx)
    assert n_ctx % tk == 0 and s_ % tk == 0 and n_ctx % tq_ctx == 0 and s_ % tq == 0
    ck, lk = n_ctx // tk, s_ // tk
    ctx_kblocks = b_ * ck

    def kv_block(b, kt):
        return jnp.where(kt < ck, b * ck + kt, ctx_kblocks + b * lk + (kt - ck))

    def scratch(rows_q):
        return [pltpu.VMEM((h_, rows_q), F32), pltpu.VMEM((h_, rows_q), F32), pltpu.VMEM((h_ * V_HD, rows_q), F32)]

    o_shape = jax.ShapeDtypeStruct((rows.rows, h_ * V_HD), BF16)
    lq = s_ // tq
    ctx_qblocks = b_ * n_ctx // tq
    o_lat = pl.pallas_call(
        functools.partial(_mla_attn_kernel, n_heads=h_),
        grid=(b_, lq, ck + lk),
        in_specs=[pl.BlockSpec((tq, h_ * LANES), lambda b, i, kt: (ctx_qblocks + b * lq + i, 0)),
                  pl.BlockSpec((tk, h_ * LANES), lambda b, i, kt: (kv_block(b, kt), 0)),
                  pl.BlockSpec((h_ * V_HD, tk), lambda b, i, kt: (0, kv_block(b, kt)))],
        out_specs=pl.BlockSpec((tq, h_ * V_HD), lambda b, i, kt: (ctx_qblocks + b * lq + i, 0)),
        out_shape=o_shape,
        scratch_shapes=scratch(tq),
        compiler_params=_params("arbitrary", "arbitrary", "arbitrary"),
        name="mla_attn",
    )(q, k, vt)
    if need_ctx:
        cq = n_ctx // tq_ctx
        o = pl.pallas_call(
            functools.partial(_mla_attn_kernel, n_heads=h_),
            grid=(b_, cq, ck),
            in_specs=[pl.BlockSpec((tq_ctx, h_ * LANES), lambda b, i, kt: (b * cq + i, 0)),
                      pl.BlockSpec((tk, h_ * LANES), lambda b, i, kt: (b * ck + kt, 0)),
                      pl.BlockSpec((h_ * V_HD, tk), lambda b, i, kt: (0, b * ck + kt)),
                      pl.BlockSpec(memory_space=pl.ANY)],
            out_specs=pl.BlockSpec((tq_ctx, h_ * V_HD), lambda b, i, kt: (b * cq + i, 0)),
            out_shape=o_shape,
            scratch_shapes=scratch(tq_ctx),
            input_output_aliases={3: 0},
            compiler_params=_params("arbitrary", "arbitrary", "arbitrary"),
            name="mla_ctx_attn",
        )(q, k, vt, o_lat)
    else:
        o = o_lat
    return _out_ln(rows, o, p["w_o"], jnp.zeros((d,), F32), x, mod, ln_g, ln_b, alpha=alpha,
                   first_tile=0 if need_ctx else rows.ctx_tiles)


def _split_gu_kernel(w_ref, g_ref, u_ref):
    win = 2 * MXU_DIM
    r_i = lax.broadcasted_iota(jnp.int32, (win, MXU_DIM), 0)
    c_i = lax.broadcasted_iota(jnp.int32, (win, MXU_DIM), 1)
    pick_even = (r_i == 2 * c_i).astype(BF16)
    pick_odd = (r_i == 2 * c_i + 1).astype(BF16)
    for j in range(w_ref.shape[2] // win):
        w = w_ref[0, :, j * win:(j + 1) * win].astype(BF16)
        g_ref[0, :, j * MXU_DIM:(j + 1) * MXU_DIM] = _bdot(w, pick_even).astype(BF16)
        u_ref[0, :, j * MXU_DIM:(j + 1) * MXU_DIM] = _bdot(w, pick_odd).astype(BF16)


def _split_gate_up(w_gu):
    depth, n_exp, d, f2 = w_gu.shape
    w = w_gu.reshape(depth * n_exp, d, f2)
    tr = PROJ_TILE
    spec_out = pl.BlockSpec((1, tr, f2 // 2), lambda e, r: (e, r, 0))
    shape_out = jax.ShapeDtypeStruct((depth * n_exp, d, f2 // 2), BF16)
    return pl.pallas_call(
        _split_gu_kernel,
        grid=(depth * n_exp, d // tr),
        in_specs=[pl.BlockSpec((1, tr, f2), lambda e, r: (e, r, 0))],
        out_specs=[spec_out, spec_out],
        out_shape=[shape_out, shape_out],
        compiler_params=_params("arbitrary", "arbitrary"),
        name="moe_split_gate_up",
    )(w)


def _pack_pairs(a):
    w = a.shape[1] // 2
    r = a.astype(BF16).astype(F32)
    lo = lax.bitcast_convert_type(r[:, :w], jnp.uint32)
    hi = lax.bitcast_convert_type(r[:, w:], jnp.uint32)
    return (lo >> 16) | (hi & jnp.uint32(0xFFFF0000))


def _unpack_pairs(u):
    lo = lax.bitcast_convert_type(u << 16, F32)
    hi = lax.bitcast_convert_type(u & jnp.uint32(0xFFFF0000), F32)
    return jnp.concatenate([lo, hi], axis=1).astype(BF16)


def _seg_rows(t, n_exp):
    return t * TOP_K + n_exp * SEG_ALIGN


def _router_kernel(x_ref, mod_ref, rw_ref, rb_ref, by_token_ref, by_col_ref, cnt_ref, *, top_k):
    m = mod_ref[0]
    h = x_ref[...] * (1.0 + m[4:5]) + m[3:4]
    w_t = rw_ref[...]
    h_hi, w_hi = h.astype(BF16), w_t.astype(BF16)
    h_lo = (h - h_hi.astype(F32)).astype(BF16)
    w_lo = (w_t - w_hi.astype(F32)).astype(BF16)
    logits = _bdot_nt(w_hi, h_hi) + (_bdot_nt(w_lo, h_hi) + _bdot_nt(w_hi, h_lo)) + rb_ref[...]
    n_exp, t = logits.shape
    e_idx = lax.broadcasted_iota(jnp.int32, (n_exp, t), 0).astype(F32)
    work = logits
    sel, val = [], []
    for _ in range(top_k):
        mx = jnp.max(work, axis=0, keepdims=True)
        pick = jnp.min(jnp.where(work == mx, e_idx, float(n_exp)), axis=0, keepdims=True)
        sel.append(pick)
        val.append(mx)
        work = jnp.where(e_idx == pick, -jnp.inf, work)
    ex = [jnp.exp(v - val[0]) for v in val]
    tot = ex[0]
    for e in ex[1:]:
        tot = tot + e
    onehot = jnp.zeros((n_exp, t), F32)
    for pick in sel:
        onehot = onehot + (e_idx == pick).astype(F32)
    r_i = lax.broadcasted_iota(jnp.int32, (t, t), 0)
    c_i = lax.broadcasted_iota(jnp.int32, (t, t), 1)
    before = _bdot(onehot, (r_i < c_i).astype(BF16))
    count = jnp.sum(onehot, axis=1, keepdims=True)
    cap = jnp.floor((count + (SEG_ALIGN - 1)) * (1.0 / SEG_ALIGN)) * SEG_ALIGN
    e_r = lax.broadcasted_iota(jnp.int32, (n_exp, n_exp), 0)
    e_c = lax.broadcasted_iota(jnp.int32, (n_exp, n_exp), 1)
    seg_start = _bdot((e_c < e_r).astype(BF16), jnp.broadcast_to(cap, (n_exp, LANES)))[:, 0:1]
    base = before + seg_start
    rows_out = [jnp.sum(jnp.where(e_idx == sel[k], base, 0.0), axis=0, keepdims=True) for k in range(top_k)]
    rows_out += [ex[k] / tot for k in range(top_k)]
    r8 = lax.broadcasted_iota(jnp.int32, (SUBLANES, t), 0)
    slab = jnp.zeros((SUBLANES, t), F32)
    for k, v in enumerate(rows_out):
        slab = jnp.where(r8 == k, v, slab)
    by_token_ref[...] = slab
    by_col_ref[...] = jnp.concatenate([slab, jnp.zeros((LANES - SUBLANES, t), F32)], axis=0).T
    cnt_ref[0] = count


def _segment_copies(seg_ref, n_exp, make_copy, act):
    *small, big = SEG_PIECES
    for e in range(n_exp):
        g0, rows, l0 = seg_ref[0, 0, e], seg_ref[0, 0, n_exp + e], seg_ref[0, 0, 2 * n_exp + e]

        def whole(j, carry, g0=g0, l0=l0):
            act(make_copy(pl.multiple_of(l0 + j * big, SEG_ALIGN), pl.multiple_of(g0 + j * big, SEG_ALIGN), big))
            return carry

        lax.fori_loop(0, lax.shift_right_logical(rows, big.bit_length() - 1), whole, 0)
        for size in small:
            done = rows & (-2 * size)

            @pl.when((rows & size) != 0)
            def _(g0=g0, l0=l0, done=done, size=size):
                act(make_copy(pl.multiple_of(l0 + done, SEG_ALIGN), pl.multiple_of(g0 + done, SEG_ALIGN), size))


def _segment_waits(seg_ref, n_exp, make_copy):
    total = seg_ref[0, 0, 3 * n_exp]
    *small, big = WAIT_PIECES

    def whole(j, carry):
        make_copy(0, 0, big).wait()
        return carry

    lax.fori_loop(0, lax.shift_right_logical(total, big.bit_length() - 1), whole, 0)
    for size in small:
        @pl.when((total & size) != 0)
        def _(size=size):
            make_copy(0, 0, size).wait()


def _dispatch_kernel(tail_ref, seg_ref, seg_prev_ref, x_ref, mod_ref, slot_t_ref, xs_ref, buf, zbuf, sem, zsem, *,
                     n_exp, top_k):
    i = pl.program_id(0)
    cur = i % 2

    def to_slots(buf_slot, sem_slot):
        return lambda l, g, rows: pltpu.make_async_copy(buf.at[buf_slot, pl.ds(l, rows)], xs_ref.at[pl.ds(g, rows)],
                                                        sem.at[sem_slot])

    @pl.when(i == 0)
    def _():
        zbuf[...] = jnp.zeros_like(zbuf)
        zero_fill = lambda l, g, rows: pltpu.make_async_copy(zbuf.at[pl.ds(l, rows)], xs_ref.at[pl.ds(g, rows)], zsem)
        _segment_copies(tail_ref, n_exp, zero_fill, lambda c: c.start())
        _segment_copies(tail_ref, n_exp, zero_fill, lambda c: c.wait())

    m = mod_ref[0]
    h = (x_ref[...] * (1.0 + m[4:5]) + m[3:4]).astype(BF16)
    n_rows, t = buf.shape[1], x_ref.shape[0]
    row = lax.broadcasted_iota(jnp.int32, (n_rows, t), 0)
    slot_t = slot_t_ref[...].astype(jnp.int32)
    pick = row == slot_t[0:1, :]
    for k in range(1, top_k):
        pick = pick | (row == slot_t[k:k + 1, :])
    buf[cur] = _pack_pairs(_bdot(pick.astype(BF16), h))
    _segment_copies(seg_ref, n_exp, to_slots(cur, cur), lambda c: c.start())

    @pl.when(i > 0)
    def _():
        _segment_waits(seg_prev_ref, n_exp, to_slots(1 - cur, 1 - cur))

    @pl.when(i == pl.num_programs(0) - 1)
    def _():
        _segment_waits(seg_ref, n_exp, to_slots(cur, cur))


def _expert_kernel(be_ref, nu_ref, x_ref, wg_ref, wu_ref, bg_ref, bu_ref, wd_ref, bd_ref, y_ref):
    del be_ref

    @pl.when(pl.program_id(0) < nu_ref[0])
    def _():
        def gate_up(part):
            xb = _unpack_pairs(x_ref[part * MOE_ROWS:(part + 1) * MOE_ROWS, :])
            return _bdot(xb, wg_ref[0]), _bdot(xb, wu_ref[0])

        n_parts = x_ref.shape[0] // MOE_ROWS
        nxt = gate_up(0)
        for part in range(n_parts):
            g, u = nxt
            if part + 1 < n_parts:
                nxt = gate_up(part + 1)
            g = jnp.minimum(g + bg_ref[0], SWIGLU_LIMIT)
            u = jnp.clip(u + bu_ref[0], -SWIGLU_LIMIT, SWIGLU_LIMIT)
            act = (u + 1.0) * (g * jax.nn.sigmoid(SWIGLU_ALPHA * g))
            y_ref[part * MOE_ROWS:(part + 1) * MOE_ROWS, :] = _pack_pairs(_bdot(act, wd_ref[0]) + bd_ref[0])


def _combine_kernel(seg_ref, seg_next_ref, x_ref, mod_ref, route_ref, g_ref, bt_ref, y_ref, o_ref,
                    ybuf, sem, *, n_exp, top_k, alpha):
    i = pl.program_id(0)
    cur = i % 2

    def from_slots(buf_slot):
        return lambda l, g, rows: pltpu.make_async_copy(y_ref.at[pl.ds(g, rows)], ybuf.at[buf_slot, pl.ds(l, rows)],
                                                        sem.at[buf_slot])

    @pl.when(i == 0)
    def _():
        ybuf[...] = jnp.zeros_like(ybuf)
        _segment_copies(seg_ref, n_exp, from_slots(0), lambda c: c.start())

    @pl.when(i < pl.num_programs(0) - 1)
    def _():
        _segment_copies(seg_next_ref, n_exp, from_slots(1 - cur), lambda c: c.start())

    _segment_waits(seg_ref, n_exp, from_slots(cur))
    t, n_rows = x_ref.shape[0], ybuf.shape[1]
    lane = lax.broadcasted_iota(jnp.int32, (t, n_rows), 1)
    route = route_ref[...]
    slot = route[:, :top_k].astype(jnp.int32)
    weights = jnp.zeros((t, n_rows), F32)
    for k in range(top_k):
        weights = jnp.where(lane == slot[:, k:k + 1], route[:, top_k + k:top_k + k + 1], weights)
    acc = _bdot(weights, _unpack_pairs(ybuf[cur]))
    m = mod_ref[0]
    r = alpha * x_ref[...] + m[5:6] * acc
    o_ref[...] = _layer_norm(r, g_ref[...], bt_ref[...])


def _moe_layer(rows, x, mod, router_w, router_b, wg, wu, wd, layer, b_gu, b_down, ln_g, ln_b, alpha, need_ctx):
    t, d = rows.tile, x.shape[1]
    n_exp = router_w.shape[1]
    d_ff = wd.shape[1]
    first = 0 if need_ctx else rows.ctx_tiles
    n_tiles = rows.tiles - first
    n_tok = n_tiles * t
    row = lambda i: (i + first, 0)
    modi = lambda i: (rows.mod_index(i + first), 0, 0)
    tok = lambda i: (i, 0)

    seg_rows = _seg_rows(t, n_exp)
    dp = d // 2
    assert 4 * n_exp <= LANES and t % SEG_PIECES[-1] == 0 and MOE_BLOCK % SEG_PIECES[-1] == 0
    assert 2 * TOP_K <= SUBLANES
    slot_t, route, cnt = pl.pallas_call(
        functools.partial(_router_kernel, top_k=TOP_K),
        grid=(n_tiles,),
        in_specs=[pl.BlockSpec((t, d), row), pl.BlockSpec((1, 6, d), modi),
                  pl.BlockSpec((n_exp, d), lambda i: (0, 0)), pl.BlockSpec((n_exp, 1), lambda i: (0, 0))],
        out_specs=[pl.BlockSpec((SUBLANES, t), tok), pl.BlockSpec((t, LANES), tok),
                   pl.BlockSpec((1, n_exp, 1), lambda i: (i, 0, 0))],
        out_shape=[jax.ShapeDtypeStruct((n_tiles * SUBLANES, t), F32), jax.ShapeDtypeStruct((n_tok, LANES), F32),
                   jax.ShapeDtypeStruct((n_tiles, n_exp, 1), F32)],
        compiler_params=_params("arbitrary"),
        name="moe_router",
    )(x, mod, router_w.T, router_b.reshape(n_exp, 1))

    i32 = jnp.int32
    count = cnt.reshape(n_tiles, n_exp).astype(i32)
    cap = (count + SEG_ALIGN - 1) // SEG_ALIGN * SEG_ALIGN
    e_rows = jnp.sum(cap, axis=0)
    e_pad = (e_rows + MOE_BLOCK - 1) // MOE_BLOCK * MOE_BLOCK
    e_end = jnp.cumsum(e_pad)
    e_start = e_end - e_pad
    seg_global = e_start[None, :] + jnp.cumsum(cap, axis=0) - cap
    seg_local = jnp.cumsum(cap, axis=1) - cap
    fill = jnp.zeros((n_tiles, LANES - 3 * n_exp - 1), i32)
    seg = jnp.concatenate([seg_global, cap, seg_local, jnp.sum(cap, axis=1, keepdims=True), fill], axis=1)
    seg = seg.astype(i32).reshape(n_tiles, 1, LANES)
    tail = jnp.concatenate([e_start + e_rows, e_pad - e_rows, jnp.zeros((LANES - 2 * n_exp,), i32)])
    tail = tail.astype(i32).reshape(1, 1, LANES)
    max_slots = n_tok * TOP_K + n_tiles * n_exp * (SEG_ALIGN - 1) + n_exp * (MOE_BLOCK - 1)
    n_blocks = (max_slots + MOE_BLOCK - 1) // MOE_BLOCK
    n_slots = n_blocks * MOE_BLOCK
    n_used = (e_end[-1] // MOE_BLOCK).astype(i32)
    blk = jnp.minimum(jnp.arange(n_blocks, dtype=i32), n_used - 1) * MOE_BLOCK
    block_expert = jnp.minimum(jnp.sum((e_end[None, :] <= blk[:, None]).astype(i32), axis=1), n_exp - 1)

    smem_tile = lambda index_map: pl.BlockSpec((1, 1, LANES), index_map, memory_space=pltpu.SMEM)
    xs = pl.pallas_call(
        functools.partial(_dispatch_kernel, n_exp=n_exp, top_k=TOP_K),
        grid=(n_tiles,),
        in_specs=[smem_tile(lambda i: (0, 0, 0)), smem_tile(lambda i: (i, 0, 0)),
                  smem_tile(lambda i: (jnp.maximum(i - 1, 0), 0, 0)),
                  pl.BlockSpec((t, d), row), pl.BlockSpec((1, 6, d), modi), pl.BlockSpec((SUBLANES, t), tok)],
        out_specs=pl.BlockSpec(memory_space=pl.ANY),
        out_shape=jax.ShapeDtypeStruct((n_slots, dp), jnp.uint32),
        scratch_shapes=[pltpu.VMEM((2, seg_rows, dp), jnp.uint32), pltpu.VMEM((MOE_BLOCK, dp), jnp.uint32),
                        pltpu.SemaphoreType.DMA((2,)), pltpu.SemaphoreType.DMA(())],
        compiler_params=_params("arbitrary"),
        name="moe_dispatch",
    )(tail, seg, seg, x, mod, slot_t)

    bg = b_gu[:, 0::2].reshape(n_exp, 1, d_ff)
    bu = b_gu[:, 1::2].reshape(n_exp, 1, d_ff)
    used = lambda i, be, nu: (jnp.minimum(i, nu[0] - 1), 0)
    wmap = lambda i, be, nu: (be[i], 0, 0)
    wmap_all = lambda i, be, nu: (be[i] + layer * n_exp, 0, 0)
    y = pl.pallas_call(
        _expert_kernel,
        grid_spec=pltpu.PrefetchScalarGridSpec(
            num_scalar_prefetch=2,
            grid=(n_blocks,),
            in_specs=[pl.BlockSpec((MOE_BLOCK, dp), used),
                      pl.BlockSpec((1, d, d_ff), wmap_all), pl.BlockSpec((1, d, d_ff), wmap_all),
                      pl.BlockSpec((1, 1, d_ff), wmap), pl.BlockSpec((1, 1, d_ff), wmap),
                      pl.BlockSpec((1, d_ff, d), wmap_all), pl.BlockSpec((1, 1, d), wmap)],
            out_specs=pl.BlockSpec((MOE_BLOCK, dp), used)),
        out_shape=jax.ShapeDtypeStruct((n_slots, dp), jnp.uint32),
        compiler_params=_params("arbitrary"),
        name="moe_experts",
    )(block_expert, n_used.reshape(1), xs, wg, wu, bg, bu, wd, b_down.reshape(n_exp, 1, d))

    return pl.pallas_call(
        functools.partial(_combine_kernel, n_exp=n_exp, top_k=TOP_K, alpha=alpha),
        grid=(n_tiles,),
        in_specs=[smem_tile(lambda i: (i, 0, 0)), smem_tile(lambda i: (jnp.minimum(i + 1, n_tiles - 1), 0, 0)),
                  pl.BlockSpec((t, d), row), pl.BlockSpec((1, 6, d), modi),
                  pl.BlockSpec((t, LANES), tok),
                  pl.BlockSpec((1, d), lambda i: (0, 0)), pl.BlockSpec((1, d), lambda i: (0, 0)),
                  pl.BlockSpec(memory_space=pl.ANY)],
        out_specs=pl.BlockSpec((t, d), row if need_ctx else tok),
        out_shape=jax.ShapeDtypeStruct(x.shape if need_ctx else (n_tok, d), F32),
        scratch_shapes=[pltpu.VMEM((2, seg_rows, dp), jnp.uint32), pltpu.SemaphoreType.DMA((2,))],
        input_output_aliases={2: 0} if need_ctx else {},
        compiler_params=_params("arbitrary"),
        name="moe_combine_ln",
    )(seg, seg, x, mod, route, ln_g.reshape(1, d), ln_b.reshape(1, d), y)


def kernel(x, c, ctx, c_ctx, ada_w, ada_b, ln1_g, ln1_b, ln2_g, ln2_b, router_w, router_b, exp_gu_w, exp_gu_b, exp_down_w, exp_down_b, rg_w_in, rg_conv_w, rg_conv_b, rg_gate_a_w, rg_gate_a_b, rg_gate_x_w, rg_gate_x_b, rg_lambda, rg_w_out, gqa_w_qkv, gqa_b_qkv, gqa_sinks, gqa_w_o, gqa_b_o, mla_w_down, mla_q_norm, mla_kv_norm, mla_w_uq, mla_w_ukv, mla_w_o):
    batch, seq, d = x.shape
    n_ctx = ctx.shape[1]
    depth = ada_w.shape[0]
    alpha = (2 * depth) ** 0.25
    rows = _Rows(batch, n_ctx, seq, ROW_TILE)
    prows = _Rows(batch, n_ctx, seq, PROJ_TILE) if (batch * n_ctx) % PROJ_TILE == 0 and seq % PROJ_TILE == 0 else rows
    xa = jnp.concatenate([ctx.reshape(batch * n_ctx, d), x.reshape(batch * seq, d)], axis=0)
    mods = _ada_table(jnp.concatenate([c_ctx[None, :], c], axis=0), ada_w, ada_b)
    wg, wu = _split_gate_up(exp_gu_w)
    wd = exp_down_w.reshape((-1,) + exp_down_w.shape[2:]).astype(BF16)
    for i in range(depth):
        need_ctx = i < depth - 1
        kind, j = i % 3, i // 3
        mod = mods[i]
        if kind == 0:
            prm = dict(w_in=rg_w_in[j], conv_w=rg_conv_w[j], conv_b=rg_conv_b[j], gate_a_w=rg_gate_a_w[j],
                       gate_a_b=rg_gate_a_b[j], gate_x_w=rg_gate_x_w[j], gate_x_b=rg_gate_x_b[j],
                       lam=rg_lambda[j], w_out=rg_w_out[j])
            xa = _rglru_layer(rows, prows, xa, mod, prm, ln1_g[i], ln1_b[i], alpha, need_ctx)
        elif kind == 1:
            prm = dict(w_qkv=gqa_w_qkv[j], b_qkv=gqa_b_qkv[j], sinks=gqa_sinks[j], w_o=gqa_w_o[j], b_o=gqa_b_o[j])
            xa = _gqa_layer(prows, xa, mod, prm, ln1_g[i], ln1_b[i], alpha, need_ctx)
        else:
            prm = dict(w_down=mla_w_down[j], q_norm=mla_q_norm[j], kv_norm=mla_kv_norm[j], w_uq=mla_w_uq[j],
                       w_ukv=mla_w_ukv[j], w_o=mla_w_o[j])
            xa = _mla_layer(prows, xa, mod, prm, ln1_g[i], ln1_b[i], alpha, need_ctx)
        xa = _moe_layer(rows, xa, mod, router_w[i], router_b[i], wg, wu, wd, i, exp_gu_b[i], exp_down_b[i],
                        ln2_g[i], ln2_b[i], alpha, need_ctx)
    return xa.reshape(batch, seq, d)
```

```python
import functools
import math

import jax
import jax.numpy as jnp
from jax import lax
from jax.experimental import pallas as pl
from jax.experimental.pallas import tpu as pltpu

F32 = jnp.float32
BF16 = jnp.bfloat16

GRID_W = 64
LN_EPS = 1e-5
RMS_EPS = 1e-6
ROPE_THETA = 10000.0
NEG_INF = -1e30
RG_C = 8.0
SQRT_FLOOR = 1e-30
GQA_KV = 2
GQA_HD = 64
WINDOW = 128
MLA_HEADS = 16
QK_NOPE = 64
QK_ROPE = 32
V_HD = 64
TOP_K = 4
SWIGLU_LIMIT = 7.0
SWIGLU_ALPHA = 1.702
MOE_BLOCK = 1024
MOE_ROWS = 512

LANES = 128
SUBLANES = 8
BF16_ROWS = 16
ROW_TILE = 256
PROJ_TILE = 512
ATTN_Q = 128
GQA_LOOKAHEAD = 4
MLA_TQ = 1024
MLA_TK = 256
MLA_SUB_K = 128
MLA_SUB_Q = 256
MLA_LOOKAHEAD = 12
MXU_DIM = 256
SEG_ALIGN = SUBLANES
SEG_PIECES = (8, 16, 32)
WAIT_PIECES = (8, 16, 32, 64, 128, 256)
VMEM_LIMIT = 56 * 1024 * 1024


def _params(*sem):
    return pltpu.CompilerParams(dimension_semantics=sem, vmem_limit_bytes=VMEM_LIMIT)


def _bdot(a, b):
    return jnp.dot(a.astype(BF16), b.astype(BF16), preferred_element_type=F32)


def _bdot_nt(a, b):
    return lax.dot_general(a.astype(BF16), b.astype(BF16), (((1,), (1,)), ((), ())),
                           preferred_element_type=F32)


def _layer_norm(r, g, b):
    mu = jnp.mean(r, axis=-1, keepdims=True)
    d = r - mu
    var = jnp.mean(d * d, axis=-1, keepdims=True)
    return d * lax.rsqrt(var + LN_EPS) * g + b


def _sigmoid(x):
    return 0.5 * jnp.tanh(0.5 * x) + 0.5


def _gelu_tanh(x):
    return 0.5 * x * (1.0 + jnp.tanh(math.sqrt(2.0 / math.pi) * (x + 0.044715 * (x * x * x))))


class _Rows:
    def __init__(self, batch, n_ctx, seq, tile):
        assert (batch * n_ctx) % tile == 0 and seq % tile == 0
        self.batch, self.n_ctx, self.seq, self.tile = batch, n_ctx, seq, tile
        self.ctx_tiles = batch * n_ctx // tile
        self.lat_tiles = batch * seq // tile
        self.tiles = self.ctx_tiles + self.lat_tiles
        self.lat_per_batch = seq // tile
        self.ctx_per_batch = n_ctx // tile
        self.rows = batch * (n_ctx + seq)

    def mod_index(self, i):
        return jnp.where(i < self.ctx_tiles, 0, 1 + (i - self.ctx_tiles) // self.lat_per_batch)

    def rope_index(self, i):
        return jnp.where(i < self.ctx_tiles, self.lat_per_batch, (i - self.ctx_tiles) % self.lat_per_batch)


def _ada_kernel(c_ref, w_ref, b_ref, o_ref):
    cv = c_ref[...]
    s = cv * jax.nn.sigmoid(cv)
    o_ref[0] = jnp.dot(s, w_ref[0], preferred_element_type=F32,
                       precision=lax.Precision.HIGHEST) + b_ref[0]


def _ada_table(cvec, ada_w, ada_b):
    depth, d, d6 = ada_w.shape
    n = cvec.shape[0]
    chunk = d
    out = pl.pallas_call(
        _ada_kernel,
        grid=(depth, d6 // chunk),
        in_specs=[pl.BlockSpec((n, d), lambda l, j: (0, 0)),
                  pl.BlockSpec((1, d, chunk), lambda l, j: (l, 0, j)),
                  pl.BlockSpec((1, 1, chunk), lambda l, j: (l, 0, j))],
        out_specs=pl.BlockSpec((1, n, chunk), lambda l, j: (l, 0, j)),
        out_shape=jax.ShapeDtypeStruct((depth, n, d6), F32),
        compiler_params=_params("arbitrary", "arbitrary"),
        name="ada_table",
    )(cvec, ada_w, ada_b.reshape(depth, 1, d6))
    return out.reshape(depth, n, 6, d)


def _out_ln_kernel(z_ref, w_ref, b_ref, x_ref, mod_ref, g_ref, bt_ref, o_ref, *, gate_row, alpha):
    y = _bdot(z_ref[...], w_ref[...]) + b_ref[...]
    m = mod_ref[0]
    r = alpha * x_ref[...] + m[gate_row:gate_row + 1] * y
    o_ref[...] = _layer_norm(r, g_ref[...], bt_ref[...])


def _out_ln(rows, z, w, bias, x, mod, ln_g, ln_b, *, alpha, first_tile=0):
    t, d = rows.tile, x.shape[1]
    kdim = z.shape[1]
    n_tiles = rows.tiles - first_tile
    row = lambda i: (i + first_tile, 0)
    return pl.pallas_call(
        functools.partial(_out_ln_kernel, gate_row=2, alpha=alpha),
        grid=(n_tiles,),
        in_specs=[pl.BlockSpec((t, kdim), row),
                  pl.BlockSpec((kdim, d), lambda i: (0, 0)),
                  pl.BlockSpec((1, d), lambda i: (0, 0)),
                  pl.BlockSpec((t, d), row),
                  pl.BlockSpec((1, 6, d), lambda i: (rows.mod_index(i + first_tile), 0, 0)),
                  pl.BlockSpec((1, d), lambda i: (0, 0)),
                  pl.BlockSpec((1, d), lambda i: (0, 0))],
        out_specs=pl.BlockSpec((t, d), row),
        out_shape=jax.ShapeDtypeStruct(x.shape, F32),
        input_output_aliases={3: 0},
        compiler_params=_params("arbitrary"),
        name="out_proj_ln",
    )(z, w.astype(BF16), bias.reshape(1, d), x, mod, ln_g.reshape(1, d), ln_b.reshape(1, d))


def _rg_in_kernel(x_ref, mod_ref, w_ref, gel_ref, rec_ref, *, d_rnn):
    m = mod_ref[0]
    h = (x_ref[...] * (1.0 + m[1:2]) + m[0:1]).astype(BF16)
    gel_ref[...] = _gelu_tanh(_bdot(h, w_ref[:, :d_rnn])).astype(BF16)
    rec_ref[...] = _bdot(h, w_ref[:, d_rnn:])


def _rg_in(rows, x, mod, w_in):
    t, d = rows.tile, x.shape[1]
    d_rnn = w_in.shape[1] // 2
    return pl.pallas_call(
        functools.partial(_rg_in_kernel, d_rnn=d_rnn),
        grid=(rows.tiles,),
        in_specs=[pl.BlockSpec((t, d), lambda i: (i, 0)),
                  pl.BlockSpec((1, 6, d), lambda i: (rows.mod_index(i), 0, 0)),
                  pl.BlockSpec((d, 2 * d_rnn), lambda i: (0, 0))],
        out_specs=[pl.BlockSpec((t, d_rnn), lambda i: (i, 0)),
                   pl.BlockSpec((t, d_rnn), lambda i: (i, 0))],
        out_shape=[jax.ShapeDtypeStruct((rows.rows, d_rnn), BF16),
                   jax.ShapeDtypeStruct((rows.rows, d_rnn), F32)],
        compiler_params=_params("arbitrary"),
        name="rg_in_proj",
    )(x, mod, w_in.astype(BF16))


def _rg_scan_kernel(*refs, reverse, fuse_out, n_blocks, block_w, tile, ctx_tiles, lat_tiles):
    if fuse_out:
        (x_ref, xp_ref, xn_ref, cw_ref, cb_ref, wa_ref, ba_ref, wx_ref, bx_ref, lam_ref,
         hf_ref, gel_ref, out_ref, a_scr, u_scr, h_scr, carry_scr) = refs
    else:
        (x_ref, xp_ref, xn_ref, cw_ref, cb_ref, wa_ref, ba_ref, wx_ref, bx_ref, lam_ref,
         out_ref, a_scr, u_scr, carry_scr) = refs
        h_scr = out_ref
    j = pl.program_id(1)
    is_ctx = j < ctx_tiles
    n_seq = jnp.where(is_ctx, ctx_tiles, lat_tiles)
    step = jnp.where(is_ctx, j, j - ctx_tiles)
    pos = (n_seq - 1 - step) if reverse else step
    prev_ok = (pos > 0).astype(F32)
    next_ok = (pos < n_seq - 1).astype(F32)

    @pl.when(j == 0)
    def _():
        carry_scr[...] = jnp.zeros_like(carry_scr)

    row8 = lax.broadcasted_iota(jnp.int32, (SUBLANES, block_w), 0)
    for n in range(n_blocks):
        cols = slice(n * block_w, (n + 1) * block_w)
        x = x_ref[:, cols]
        prev = xp_ref[SUBLANES - 1:SUBLANES, cols] * prev_ok
        nxt0 = xn_ref[0:1, cols] * next_ok
        nxt1 = xn_ref[1:2, cols] * next_ok
        cw = [cw_ref[k:k + 1, cols] for k in range(4)]
        xc = (cw[0] * pltpu.roll(x, 1, 0) + cw[1] * x + cw[2] * pltpu.roll(x, tile - 1, 0)
              + cw[3] * pltpu.roll(x, tile - 2, 0) + cb_ref[:, cols])
        first, last = x[0:1, :], x[tile - 1:tile, :]
        head = xc[:SUBLANES] + jnp.where(row8 == 0, cw[0] * (prev - last), 0.0)
        tail = xc[tile - SUBLANES:] + jnp.where(
            row8 == SUBLANES - 2, cw[3] * (nxt0 - first),
            jnp.where(row8 == SUBLANES - 1, cw[2] * (nxt0 - first) + cw[3] * (nxt1 - x[1:2, :]), 0.0))
        xc = jnp.concatenate([head, xc[SUBLANES:tile - SUBLANES], tail], axis=0)
        xb = xc.astype(BF16)
        r = _sigmoid(_bdot(xb, wa_ref[n]) + ba_ref[:, cols])
        gi = _sigmoid(_bdot(xb, wx_ref[n]) + bx_ref[:, cols])
        z = -lam_ref[:, cols]
        softplus = jnp.maximum(z, 0.0) + jnp.log1p(jnp.exp(-jnp.abs(z)))
        a = jnp.exp2(r * ((-RG_C * math.log2(math.e)) * softplus))
        a_scr[:, cols] = a
        v = 1.0 - a * a
        u_scr[:, cols] = (v * lax.rsqrt(jnp.maximum(v, SQRT_FLOOR))) * (gi * xc)

    width = n_blocks * block_w
    sub = lax.broadcasted_iota(jnp.int32, (SUBLANES, width), 0)
    groups = tile // SUBLANES

    def body(g, carry):
        gg = (groups - 1 - g) if reverse else g
        r0 = pl.multiple_of(gg * SUBLANES, SUBLANES)
        a8 = a_scr[pl.ds(r0, SUBLANES), :]
        u8 = u_scr[pl.ds(r0, SUBLANES), :]
        for s in (1, 2, 4):
            if reverse:
                a_sh, u_sh, ok = pltpu.roll(a8, SUBLANES - s, 0), pltpu.roll(u8, SUBLANES - s, 0), sub < SUBLANES - s
            else:
                a_sh, u_sh, ok = pltpu.roll(a8, s, 0), pltpu.roll(u8, s, 0), sub >= s
            u8 = jnp.where(ok, a8 * u_sh + u8, u8)
            a8 = jnp.where(ok, a8 * a_sh, a8)
        h8 = a8 * carry + u8
        h_scr[pl.ds(r0, SUBLANES), :] = h8
        return h8[0:1, :] if reverse else h8[SUBLANES - 1:SUBLANES, :]

    carry_scr[...] = lax.fori_loop(0, groups, body, carry_scr[...], unroll=2)
    if fuse_out:
        out_ref[...] = ((hf_ref[...] + h_scr[...]) * gel_ref[...].astype(F32)).astype(BF16)


def _rg_scan(rows, rec, conv_w, conv_b, wa, ba, wx, bx, lam, *, reverse, h_fwd=None, gel=None):
    t = rows.tile
    c = rec.shape[1]
    n_blocks, block_w = wa.shape[0], wa.shape[1]
    nc, nl = rows.ctx_per_batch, rows.lat_per_batch
    halo = t // SUBLANES
    last_halo = rows.rows // SUBLANES - 1
    fuse_out = h_fwd is not None

    def tile_index(b, j):
        is_ctx = j < nc
        step = jnp.where(is_ctx, j, j - nc)
        n_seq = jnp.where(is_ctx, nc, nl)
        pos = (n_seq - 1 - step) if reverse else step
        return jnp.where(is_ctx, b * nc + pos, rows.ctx_tiles + b * nl + pos)

    cur = lambda b, j: (tile_index(b, j), 0)
    prv = lambda b, j: (jnp.maximum(tile_index(b, j) * halo - 1, 0), 0)
    nxt = lambda b, j: (jnp.minimum((tile_index(b, j) + 1) * halo, last_halo), 0)
    full2 = lambda b, j: (0, 0)
    full3 = lambda b, j: (0, 0, 0)
    in_specs = [pl.BlockSpec((t, c), cur), pl.BlockSpec((SUBLANES, c), prv), pl.BlockSpec((SUBLANES, c), nxt),
                pl.BlockSpec((4, c), full2), pl.BlockSpec((1, c), full2),
                pl.BlockSpec((n_blocks, block_w, block_w), full3), pl.BlockSpec((1, c), full2),
                pl.BlockSpec((n_blocks, block_w, block_w), full3), pl.BlockSpec((1, c), full2),
                pl.BlockSpec((1, c), full2)]
    args = [rec, rec, rec, conv_w, conv_b.reshape(1, c), wa.astype(BF16), ba.reshape(1, c),
            wx.astype(BF16), bx.reshape(1, c), lam.reshape(1, c)]
    scratch = [pltpu.VMEM((t, c), F32), pltpu.VMEM((t, c), F32)]
    if fuse_out:
        in_specs += [pl.BlockSpec((t, c), cur), pl.BlockSpec((t, c), cur)]
        args += [h_fwd, gel]
        scratch.append(pltpu.VMEM((t, c), F32))
    scratch.append(pltpu.VMEM((1, c), F32))
    return pl.pallas_call(
        functools.partial(_rg_scan_kernel, reverse=reverse, fuse_out=fuse_out, n_blocks=n_blocks,
                          block_w=block_w, tile=t, ctx_tiles=nc, lat_tiles=nl),
        grid=(rows.batch, nc + nl),
        in_specs=in_specs,
        out_specs=pl.BlockSpec((t, c), cur),
        out_shape=jax.ShapeDtypeStruct((rows.rows, c), BF16 if fuse_out else F32),
        scratch_shapes=scratch,
        compiler_params=_params("arbitrary", "arbitrary"),
        name="rg_scan_bwd" if reverse else "rg_scan_fwd",
    )(*args)


def _rglru_layer(rows, prows, x, mod, p, ln_g, ln_b, alpha, need_ctx):
    gel, rec = _rg_in(prows, x, mod, p["w_in"])
    h_fwd = _rg_scan(rows, rec, p["conv_w"], p["conv_b"], p["gate_a_w"][0], p["gate_a_b"][0],
                     p["gate_x_w"][0], p["gate_x_b"][0], p["lam"][0], reverse=False)
    z = _rg_scan(rows, rec, p["conv_w"], p["conv_b"], p["gate_a_w"][1], p["gate_a_b"][1],
                 p["gate_x_w"][1], p["gate_x_b"][1], p["lam"][1], reverse=True, h_fwd=h_fwd, gel=gel)
    d = x.shape[1]
    return _out_ln(prows, z, p["w_out"], jnp.zeros((d,), F32), x, mod, ln_g, ln_b, alpha=alpha,
                   first_tile=0 if need_ctx else prows.ctx_tiles)


def _axial_angles(seq, rot_dim):
    pos = jnp.arange(seq, dtype=jnp.int32)
    row = (pos // GRID_W).astype(F32)
    col = (pos % GRID_W).astype(F32)
    n_freq = rot_dim // 4
    inv_freq = ROPE_THETA ** (-jnp.arange(n_freq, dtype=F32) / n_freq)
    return jnp.concatenate([row[:, None] * inv_freq, col[:, None] * inv_freq], axis=-1)


def _rope_tables(rows, rot_dim, lead, trail):
    ang = _axial_angles(rows.seq, rot_dim)
    cos, sin = jnp.cos(ang), jnp.sin(ang)
    ones = lambda w: jnp.ones((rows.seq, w), F32)
    zeros = lambda w: jnp.zeros((rows.seq, w), F32)
    c = jnp.concatenate([ones(lead), cos, cos, ones(trail)], axis=-1)
    s = jnp.concatenate([zeros(lead), -sin, sin, zeros(trail)], axis=-1)
    reps = LANES // c.shape[1]
    c, s = jnp.tile(c, (1, reps)), jnp.tile(s, (1, reps))
    t = rows.tile
    c = jnp.concatenate([c.reshape(rows.lat_per_batch, t, LANES), jnp.ones((1, t, LANES), F32)], axis=0)
    s = jnp.concatenate([s.reshape(rows.lat_per_batch, t, LANES), jnp.zeros((1, t, LANES), F32)], axis=0)
    return c, s


def _rope_chunk(x, cos, sin, half):
    lane = lax.broadcasted_iota(jnp.int32, x.shape, 1)
    partner = jnp.where((lane % (2 * half)) < half, pltpu.roll(x, LANES - half, 1), pltpu.roll(x, half, 1))
    return x * cos + partner * sin


def _gqa_proj_kernel(x_ref, mod_ref, w_ref, b_ref, cos_ref, sin_ref, q_ref, kv_ref, *, q_dim, scale, half):
    m = mod_ref[0]
    h = (x_ref[...] * (1.0 + m[1:2]) + m[0:1]).astype(BF16)
    cos, sin = cos_ref[0], sin_ref[0]
    n_q = q_dim // LANES
    n_all = w_ref.shape[1] // LANES
    for c in range(n_all):
        cols = slice(c * LANES, (c + 1) * LANES)
        p = _bdot(h, w_ref[:, cols]) + b_ref[:, cols]
        is_v = c in (n_q + 1, n_q + 3)
        if not is_v:
            p = _rope_chunk(p, cos, sin, half)
        if c < n_q:
            q_ref[:, cols] = (p * scale).astype(BF16)
        else:
            kv_ref[:, (c - n_q) * LANES:(c - n_q + 1) * LANES] = p.astype(BF16)


def _gqa_attn_kernel(*refs, windowed, window, seq, n_pairs, pairs_per_kv):
    if windowed:
        q_ref, kv_ref, kvc_ref, sink_ref, o_ref = refs
    else:
        q_ref, kvc_ref, sink_ref, _, o_ref = refs
    tq = q_ref.shape[0]
    kvc = kvc_ref[...]
    if windowed:
        span = tq + 2 * window
        qs = pl.program_id(1) * tq
        ws = pl.multiple_of(jnp.clip(qs - window, 0, seq - span), LANES)
        kv = jnp.concatenate([kv_ref[pl.ds(ws, span), :], kvc], axis=0)
        n_keys = kv.shape[0]
        qpos = qs + lax.broadcasted_iota(jnp.int32, (tq, n_keys), 0)
        col = lax.broadcasted_iota(jnp.int32, (tq, n_keys), 1)
        mask = (jnp.abs(ws + col - qpos) <= window) | (col >= span)
    else:
        kv = kvc
        mask = None
    lane = lax.broadcasted_iota(jnp.int32, (kv.shape[0], LANES), 1)
    low = lane < GQA_HD
    zero = jnp.zeros((kv.shape[0], LANES), BF16)
    one_hi = (lane == GQA_HD).astype(F32).astype(BF16)
    one_lo = (lane == 0).astype(F32).astype(BF16)
    low_q = lax.broadcasted_iota(jnp.int32, (tq, LANES), 1) < GQA_HD
    k_plain, v_plain = kv[:, 0:LANES], kv[:, LANES:2 * LANES]
    k_swap, v_swap = kv[:, 2 * LANES:3 * LANES], kv[:, 3 * LANES:4 * LANES]
    k_half, v_half = {}, {}
    for g in range(GQA_KV):
        k_half[g, 0] = jnp.where(low, k_plain if g == 0 else k_swap, zero)
        k_half[g, 1] = jnp.where(low, zero, k_swap if g == 0 else k_plain)
        v_half[g, 0] = jnp.where(low, v_plain if g == 0 else v_swap, one_hi)
        v_half[g, 1] = jnp.where(low, one_lo, v_swap if g == 0 else v_plain)
    items = [(pr, hh) for pr in range(n_pairs) for hh in range(2)]

    def scores(item):
        pr, hh = item
        return _bdot_nt(q_ref[:, pr * LANES:(pr + 1) * LANES], k_half[pr // pairs_per_kv, hh])

    ahead = [scores(it) for it in items[:GQA_LOOKAHEAD]]
    acc = None
    for n, (pr, hh) in enumerate(items):
        s = ahead.pop(0)
        if n + GQA_LOOKAHEAD < len(items):
            ahead.append(scores(items[n + GQA_LOOKAHEAD]))
        head = 2 * pr + hh
        if mask is not None:
            s = jnp.where(mask, s, NEG_INF)
        sk = sink_ref[head:head + 1, 0:1]
        mx = jnp.maximum(jnp.max(s, axis=1, keepdims=True), sk)
        p = jnp.exp((s - mx).astype(BF16))
        pv = _bdot(p, v_half[pr // pairs_per_kv, hh])
        sum_lane = GQA_HD if hh == 0 else 0
        denom = pv[:, sum_lane:sum_lane + 1] + jnp.exp(sk - mx)
        part = pv * (1.0 / denom)
        if hh == 1:
            o_ref[:, pr * LANES:(pr + 1) * LANES] = jnp.where(low_q, acc, part).astype(BF16)
        acc = part


def _gqa_layer(rows, x, mod, p, ln_g, ln_b, alpha, need_ctx):
    t, d = rows.tile, x.shape[1]
    w_qkv, b_qkv = p["w_qkv"], p["b_qkv"]
    kv_dim = GQA_KV * GQA_HD
    q_dim = w_qkv.shape[1] - 2 * kv_dim
    n_heads = q_dim // GQA_HD
    assert kv_dim == LANES and GQA_KV == 2 and q_dim % LANES == 0
    swap = lambda a: jnp.concatenate([a[..., GQA_HD:], a[..., :GQA_HD]], axis=-1)
    wk, wv = w_qkv[:, q_dim:q_dim + kv_dim], w_qkv[:, q_dim + kv_dim:]
    bk, bv = b_qkv[q_dim:q_dim + kv_dim], b_qkv[q_dim + kv_dim:]
    w_ext = jnp.concatenate([w_qkv, swap(wk), swap(wv)], axis=1).astype(BF16)
    b_ext = jnp.concatenate([b_qkv, swap(bk), swap(bv)]).reshape(1, -1)
    n_ext = w_ext.shape[1]
    cos, sin = _rope_tables(rows, GQA_HD, 0, 0)
    q, kv = pl.pallas_call(
        functools.partial(_gqa_proj_kernel, q_dim=q_dim, scale=GQA_HD ** -0.5, half=GQA_HD // 2),
        grid=(rows.tiles,),
        in_specs=[pl.BlockSpec((t, d), lambda i: (i, 0)),
                  pl.BlockSpec((1, 6, d), lambda i: (rows.mod_index(i), 0, 0)),
                  pl.BlockSpec((d, n_ext), lambda i: (0, 0)),
                  pl.BlockSpec((1, n_ext), lambda i: (0, 0)),
                  pl.BlockSpec((1, t, LANES), lambda i: (rows.rope_index(i), 0, 0)),
                  pl.BlockSpec((1, t, LANES), lambda i: (rows.rope_index(i), 0, 0))],
        out_specs=[pl.BlockSpec((t, q_dim), lambda i: (i, 0)),
                   pl.BlockSpec((t, 4 * LANES), lambda i: (i, 0))],
        out_shape=[jax.ShapeDtypeStruct((rows.rows, q_dim), BF16),
                   jax.ShapeDtypeStruct((rows.rows, 4 * LANES), BF16)],
        compiler_params=_params("arbitrary"),
        name="gqa_qkv_proj",
    )(x, mod, w_ext, b_ext, cos, sin)

    sinks = jnp.broadcast_to(p["sinks"].astype(F32)[:, None], (n_heads, LANES))
    b_, s_, n_ctx = rows.batch, rows.seq, rows.n_ctx
    ctx_rows = b_ * n_ctx
    assert ctx_rows % s_ == 0 and s_ >= ATTN_Q + 2 * WINDOW
    q_blocks = s_ // ATTN_Q
    n_pairs = q_dim // LANES
    common = dict(window=WINDOW, seq=s_, n_pairs=n_pairs, pairs_per_kv=n_pairs // GQA_KV)
    o_shape = jax.ShapeDtypeStruct((rows.rows, q_dim), BF16)
    o_lat = pl.pallas_call(
        functools.partial(_gqa_attn_kernel, windowed=True, **common),
        grid=(b_, q_blocks),
        in_specs=[pl.BlockSpec((ATTN_Q, q_dim), lambda b, j: (ctx_rows // ATTN_Q + b * q_blocks + j, 0)),
                  pl.BlockSpec((s_, 4 * LANES), lambda b, j: (ctx_rows // s_ + b, 0)),
                  pl.BlockSpec((n_ctx, 4 * LANES), lambda b, j: (b, 0)),
                  pl.BlockSpec((n_heads, LANES), lambda b, j: (0, 0))],
        out_specs=pl.BlockSpec((ATTN_Q, q_dim), lambda b, j: (ctx_rows // ATTN_Q + b * q_blocks + j, 0)),
        out_shape=o_shape,
        compiler_params=_params("arbitrary", "arbitrary"),
        name="gqa_window_attn",
    )(q, kv, kv, sinks)
    if need_ctx:
        cq_blocks = n_ctx // ATTN_Q
        o = pl.pallas_call(
            functools.partial(_gqa_attn_kernel, windowed=False, **common),
            grid=(b_, cq_blocks),
            in_specs=[pl.BlockSpec((ATTN_Q, q_dim), lambda b, j: (b * cq_blocks + j, 0)),
                      pl.BlockSpec((n_ctx, 4 * LANES), lambda b, j: (b, 0)),
                      pl.BlockSpec((n_heads, LANES), lambda b, j: (0, 0)),
                      pl.BlockSpec(memory_space=pl.ANY)],
            out_specs=pl.BlockSpec((ATTN_Q, q_dim), lambda b, j: (b * cq_blocks + j, 0)),
            out_shape=o_shape,
            input_output_aliases={3: 0},
            compiler_params=_params("arbitrary", "arbitrary"),
            name="gqa_ctx_attn",
        )(q, kv, sinks, o_lat)
    else:
        o = o_lat
    return _out_ln(rows, o, p["w_o"], p["b_o"], x, mod, ln_g, ln_b, alpha=alpha,
                   first_tile=0 if need_ctx else rows.ctx_tiles)


def _mla_proj_kernel(x_ref, mod_ref, wd_ref, qn_ref, kvn_ref, wq_ref, wk_ref, wv_ref, cos_ref, sin_ref,
                     q_ref, k_ref, vt_ref, *, q_lora, kv_lora, scale, n_heads):
    m = mod_ref[0]
    h = (x_ref[...] * (1.0 + m[1:2]) + m[0:1]).astype(BF16)
    p = _bdot(h, wd_ref[...])
    cq, ckv = p[:, :q_lora], p[:, q_lora:q_lora + kv_lora]
    cos, sin = cos_ref[0], sin_ref[0]
    k_rope = _rope_chunk(p[:, q_lora + kv_lora:], cos, sin, QK_ROPE // 2)
    cq = (cq * lax.rsqrt(jnp.mean(cq * cq, axis=-1, keepdims=True) + RMS_EPS) * qn_ref[...]).astype(BF16)
    ckv = (ckv * lax.rsqrt(jnp.mean(ckv * ckv, axis=-1, keepdims=True) + RMS_EPS) * kvn_ref[...]).astype(BF16)
    for hd in range(n_heads):
        cols = slice(hd * LANES, (hd + 1) * LANES)
        qh = _rope_chunk(_bdot(cq, wq_ref[:, cols]), cos, sin, QK_ROPE // 2)
        q_ref[:, cols] = (qh * scale).astype(BF16)
        k_ref[:, cols] = (_bdot(ckv, wk_ref[:, cols]) + k_rope).astype(BF16)
    vt_ref[...] = _bdot(ckv, wv_ref[...]).T.astype(BF16)


def _mla_attn_kernel(*refs, n_heads):
    q_ref, k_ref, vt_ref = refs[:3]
    o_ref, m_scr, l_scr, acc_scr = refs[-4:]
    kt = pl.program_id(2)

    @pl.when(kt == 0)
    def _():
        m_scr[...] = jnp.full_like(m_scr, -jnp.inf)
        l_scr[...] = jnp.zeros_like(l_scr)
        acc_scr[...] = jnp.zeros_like(acc_scr)

    tq, tk = q_ref.shape[0], k_ref.shape[0]
    sub_q, sub_k = min(MLA_SUB_Q, tq), min(MLA_SUB_K, tk)
    n_kh = tk // sub_k
    ones = jnp.ones((BF16_ROWS, sub_k), BF16)
    items = [(hd, kh, qh) for hd in range(n_heads) for qh in range(tq // sub_q) for kh in range(n_kh)]

    def scores(item):
        hd, kh, qh = item
        cols = slice(hd * LANES, (hd + 1) * LANES)
        s = _bdot_nt(k_ref[kh * sub_k:(kh + 1) * sub_k, cols], q_ref[qh * sub_q:(qh + 1) * sub_q, cols])
        return s, jnp.max(s, axis=0, keepdims=True)

    ahead = [scores(it) for it in items[:MLA_LOOKAHEAD]]
    state = {}
    for n, (hd, kh, qh) in enumerate(items):
        rws = slice(hd * V_HD, (hd + 1) * V_HD)
        qcols = slice(qh * sub_q, (qh + 1) * sub_q)
        s_t, s_max = ahead.pop(0)
        if n + MLA_LOOKAHEAD < len(items):
            ahead.append(scores(items[n + MLA_LOOKAHEAD]))
        if kh == 0:
            state[hd, qh] = (m_scr[hd:hd + 1, qcols], l_scr[hd:hd + 1, qcols], acc_scr[rws, qcols])
        m_old, l_old, acc = state[hd, qh]
        m_new = jnp.maximum(m_old, s_max)
        p_t = jnp.exp2((s_t - m_new).astype(BF16))
        corr = jnp.exp2(m_old - m_new)
        vt = vt_ref[rws, kh * sub_k:(kh + 1) * sub_k]
        pv = _bdot(jnp.concatenate([vt, ones], axis=0), p_t)
        state[hd, qh] = (m_new, corr * l_old + pv[V_HD:V_HD + 1, :], acc * corr + pv[:V_HD, :])
        if kh == n_kh - 1:
            m_scr[hd:hd + 1, qcols], l_scr[hd:hd + 1, qcols], acc_scr[rws, qcols] = state.pop((hd, qh))

    @pl.when(kt == pl.num_programs(2) - 1)
    def _():
        for hd in range(n_heads):
            rws = slice(hd * V_HD, (hd + 1) * V_HD)
            acc_scr[rws, :] = acc_scr[rws, :] * (1.0 / l_scr[hd:hd + 1, :])
        o_ref[...] = acc_scr[...].T.astype(BF16)


def _mla_layer(rows, x, mod, p, ln_g, ln_b, alpha, need_ctx):
    t, d = rows.tile, x.shape[1]
    h_ = MLA_HEADS
    w_down, w_uq, w_ukv = p["w_down"], p["w_uq"], p["w_ukv"]
    q_lora = w_uq.shape[0]
    kv_lora = w_ukv.shape[0]
    qk = QK_NOPE + QK_ROPE
    assert QK_NOPE == V_HD == LANES // 2 and h_ % 2 == 0 and q_lora % LANES == 0 and kv_lora % LANES == 0
    pad = LANES - qk
    zc = lambda r, w: jnp.zeros((r, w), F32)
    wd_p = jnp.concatenate([w_down[:, :q_lora + kv_lora], zc(d, QK_NOPE), w_down[:, q_lora + kv_lora:],
                            zc(d, pad)], axis=1).astype(BF16)
    wq_p = jnp.concatenate([w_uq.reshape(q_lora, h_, qk), jnp.zeros((q_lora, h_, pad), F32)],
                           axis=-1).reshape(q_lora, h_ * LANES).astype(BF16)
    ukv = w_ukv.reshape(kv_lora, h_, QK_NOPE + V_HD)
    wk_p = jnp.concatenate([ukv[..., :QK_NOPE], jnp.zeros((kv_lora, h_, LANES - QK_NOPE), F32)],
                           axis=-1).reshape(kv_lora, h_ * LANES).astype(BF16)
    wv_p = ukv[..., QK_NOPE:].reshape(kv_lora, h_ * V_HD).astype(BF16)
    cos, sin = _rope_tables(rows, QK_ROPE, QK_NOPE, pad)
    n_down = wd_p.shape[1]
    q, k, vt = pl.pallas_call(
        functools.partial(_mla_proj_kernel, q_lora=q_lora, kv_lora=kv_lora, scale=qk ** -0.5 * math.log2(math.e),
                          n_heads=h_),
        grid=(rows.tiles,),
        in_specs=[pl.BlockSpec((t, d), lambda i: (i, 0)),
                  pl.BlockSpec((1, 6, d), lambda i: (rows.mod_index(i), 0, 0)),
                  pl.BlockSpec((d, n_down), lambda i: (0, 0)),
                  pl.BlockSpec((1, q_lora), lambda i: (0, 0)),
                  pl.BlockSpec((1, kv_lora), lambda i: (0, 0)),
                  pl.BlockSpec((q_lora, h_ * LANES), lambda i: (0, 0)),
                  pl.BlockSpec((kv_lora, h_ * LANES), lambda i: (0, 0)),
                  pl.BlockSpec((kv_lora, h_ * V_HD), lambda i: (0, 0)),
                  pl.BlockSpec((1, t, LANES), lambda i: (rows.rope_index(i), 0, 0)),
                  pl.BlockSpec((1, t, LANES), lambda i: (rows.rope_index(i), 0, 0))],
        out_specs=[pl.BlockSpec((t, h_ * LANES), lambda i: (i, 0)),
                   pl.BlockSpec((t, h_ * LANES), lambda i: (i, 0)),
                   pl.BlockSpec((h_ * V_HD, t), lambda i: (0, i))],
        out_shape=[jax.ShapeDtypeStruct((rows.rows, h_ * LANES), BF16),
                   jax.ShapeDtypeStruct((rows.rows, h_ * LANES), BF16),
                   jax.ShapeDtypeStruct((h_ * V_HD, rows.rows), BF16)],
        compiler_params=_params("arbitrary"),
        name="mla_proj",
    )(x, mod, wd_p, p["q_norm"].reshape(1, -1), p["kv_norm"].reshape(1, -1), wq_p, wk_p, wv_p, cos, sin)

    b_, s_, n_ctx = rows.batch, rows.seq, rows.n_ctx
    tk = MLA_TK
    tq, tq_ctx = min(MLA_TQ, s_), min(MLA_TQ, n_ctx)
    assert n_ctx % tk == 0 and s_ % tk == 0 and n_ctx % tq_ctx == 0 and s_ % tq == 0
    ck, lk = n_ctx // tk, s_ // tk
    ctx_kblocks = b_ * ck

    def kv_block(b, kt):
        return jnp.where(kt < ck, b * ck + kt, ctx_kblocks + b * lk + (kt - ck))

    def scratch(rows_q):
        return [pltpu.VMEM((h_, rows_q), F32), pltpu.VMEM((h_, rows_q), F32), pltpu.VMEM((h_ * V_HD, rows_q), F32)]

    o_shape = jax.ShapeDtypeStruct((rows.rows, h_ * V_HD), BF16)
    lq = s_ // tq
    ctx_qblocks = b_ * n_ctx // tq
    o_lat = pl.pallas_call(
        functools.partial(_mla_attn_kernel, n_heads=h_),
        grid=(b_, lq, ck + lk),
        in_specs=[pl.BlockSpec((tq, h_ * LANES), lambda b, i, kt: (ctx_qblocks + b * lq + i, 0)),
                  pl.BlockSpec((tk, h_ * LANES), lambda b, i, kt: (kv_block(b, kt), 0)),
                  pl.BlockSpec((h_ * V_HD, tk), lambda b, i, kt: (0, kv_block(b, kt)))],
        out_specs=pl.BlockSpec((tq, h_ * V_HD), lambda b, i, kt: (ctx_qblocks + b * lq + i, 0)),
        out_shape=o_shape,
        scratch_shapes=scratch(tq),
        compiler_params=_params("arbitrary", "arbitrary", "arbitrary"),
        name="mla_attn",
    )(q, k, vt)
    if need_ctx:
        cq = n_ctx // tq_ctx
        o = pl.pallas_call(
            functools.partial(_mla_attn_kernel, n_heads=h_),
            grid=(b_, cq, ck),
            in_specs=[pl.BlockSpec((tq_ctx, h_ * LANES), lambda b, i, kt: (b * cq + i, 0)),
                      pl.BlockSpec((tk, h_ * LANES), lambda b, i, kt: (b * ck + kt, 0)),
                      pl.BlockSpec((h_ * V_HD, tk), lambda b, i, kt: (0, b * ck + kt)),
                      pl.BlockSpec(memory_space=pl.ANY)],
            out_specs=pl.BlockSpec((tq_ctx, h_ * V_HD), lambda b, i, kt: (b * cq + i, 0)),
            out_shape=o_shape,
            scratch_shapes=scratch(tq_ctx),
            input_output_aliases={3: 0},
            compiler_params=_params("arbitrary", "arbitrary", "arbitrary"),
            name="mla_ctx_attn",
        )(q, k, vt, o_lat)
    else:
        o = o_lat
    return _out_ln(rows, o, p["w_o"], jnp.zeros((d,), F32), x, mod, ln_g, ln_b, alpha=alpha,
                   first_tile=0 if need_ctx else rows.ctx_tiles)


def _split_gu_kernel(w_ref, g_ref, u_ref):
    win = 2 * MXU_DIM
    r_i = lax.broadcasted_iota(jnp.int32, (win, MXU_DIM), 0)
    c_i = lax.broadcasted_iota(jnp.int32, (win, MXU_DIM), 1)
    pick_even = (r_i == 2 * c_i).astype(BF16)
    pick_odd = (r_i == 2 * c_i + 1).astype(BF16)
    for j in range(w_ref.shape[2] // win):
        w = w_ref[0, :, j * win:(j + 1) * win].astype(BF16)
        g_ref[0, :, j * MXU_DIM:(j + 1) * MXU_DIM] = _bdot(w, pick_even).astype(BF16)
        u_ref[0, :, j * MXU_DIM:(j + 1) * MXU_DIM] = _bdot(w, pick_odd).astype(BF16)


def _split_gate_up(w_gu):
    depth, n_exp, d, f2 = w_gu.shape
    w = w_gu.reshape(depth * n_exp, d, f2)
    tr = PROJ_TILE
    spec_out = pl.BlockSpec((1, tr, f2 // 2), lambda e, r: (e, r, 0))
    shape_out = jax.ShapeDtypeStruct((depth * n_exp, d, f2 // 2), BF16)
    return pl.pallas_call(
        _split_gu_kernel,
        grid=(depth * n_exp, d // tr),
        in_specs=[pl.BlockSpec((1, tr, f2), lambda e, r: (e, r, 0))],
        out_specs=[spec_out, spec_out],
        out_shape=[shape_out, shape_out],
        compiler_params=_params("arbitrary", "arbitrary"),
        name="moe_split_gate_up",
    )(w)


def _pack_pairs(a):
    w = a.shape[1] // 2
    r = a.astype(BF16).astype(F32)
    lo = lax.bitcast_convert_type(r[:, :w], jnp.uint32)
    hi = lax.bitcast_convert_type(r[:, w:], jnp.uint32)
    return (lo >> 16) | (hi & jnp.uint32(0xFFFF0000))


def _unpack_pairs(u):
    lo = lax.bitcast_convert_type(u << 16, F32)
    hi = lax.bitcast_convert_type(u & jnp.uint32(0xFFFF0000), F32)
    return jnp.concatenate([lo, hi], axis=1).astype(BF16)


def _seg_rows(t, n_exp):
    return t * TOP_K + n_exp * SEG_ALIGN


def _router_kernel(x_ref, mod_ref, rw_ref, rb_ref, by_token_ref, by_col_ref, cnt_ref, *, top_k):
    m = mod_ref[0]
    h = x_ref[...] * (1.0 + m[4:5]) + m[3:4]
    w_t = rw_ref[...]
    h_hi, w_hi = h.astype(BF16), w_t.astype(BF16)
    h_lo = (h - h_hi.astype(F32)).astype(BF16)
    w_lo = (w_t - w_hi.astype(F32)).astype(BF16)
    logits = _bdot_nt(w_hi, h_hi) + (_bdot_nt(w_lo, h_hi) + _bdot_nt(w_hi, h_lo)) + rb_ref[...]
    n_exp, t = logits.shape
    e_idx = lax.broadcasted_iota(jnp.int32, (n_exp, t), 0).astype(F32)
    work = logits
    sel, val = [], []
    for _ in range(top_k):
        mx = jnp.max(work, axis=0, keepdims=True)
        pick = jnp.min(jnp.where(work == mx, e_idx, float(n_exp)), axis=0, keepdims=True)
        sel.append(pick)
        val.append(mx)
        work = jnp.where(e_idx == pick, -jnp.inf, work)
    ex = [jnp.exp(v - val[0]) for v in val]
    tot = ex[0]
    for e in ex[1:]:
        tot = tot + e
    onehot = jnp.zeros((n_exp, t), F32)
    for pick in sel:
        onehot = onehot + (e_idx == pick).astype(F32)
    r_i = lax.broadcasted_iota(jnp.int32, (t, t), 0)
    c_i = lax.broadcasted_iota(jnp.int32, (t, t), 1)
    before = _bdot(onehot, (r_i < c_i).astype(BF16))
    count = jnp.sum(onehot, axis=1, keepdims=True)
    cap = jnp.floor((count + (SEG_ALIGN - 1)) * (1.0 / SEG_ALIGN)) * SEG_ALIGN
    e_r = lax.broadcasted_iota(jnp.int32, (n_exp, n_exp), 0)
    e_c = lax.broadcasted_iota(jnp.int32, (n_exp, n_exp), 1)
    seg_start = _bdot((e_c < e_r).astype(BF16), jnp.broadcast_to(cap, (n_exp, LANES)))[:, 0:1]
    base = before + seg_start
    rows_out = [jnp.sum(jnp.where(e_idx == sel[k], base, 0.0), axis=0, keepdims=True) for k in range(top_k)]
    rows_out += [ex[k] / tot for k in range(top_k)]
    r8 = lax.broadcasted_iota(jnp.int32, (SUBLANES, t), 0)
    slab = jnp.zeros((SUBLANES, t), F32)
    for k, v in enumerate(rows_out):
        slab = jnp.where(r8 == k, v, slab)
    by_token_ref[...] = slab
    by_col_ref[...] = jnp.concatenate([slab, jnp.zeros((LANES - SUBLANES, t), F32)], axis=0).T
    cnt_ref[0] = count


def _segment_copies(seg_ref, n_exp, make_copy, act):
    *small, big = SEG_PIECES
    for e in range(n_exp):
        g0, rows, l0 = seg_ref[0, 0, e], seg_ref[0, 0, n_exp + e], seg_ref[0, 0, 2 * n_exp + e]

        def whole(j, carry, g0=g0, l0=l0):
            act(make_copy(pl.multiple_of(l0 + j * big, SEG_ALIGN), pl.multiple_of(g0 + j * big, SEG_ALIGN), big))
            return carry

        lax.fori_loop(0, lax.shift_right_logical(rows, big.bit_length() - 1), whole, 0)
        for size in small:
            done = rows & (-2 * size)

            @pl.when((rows & size) != 0)
            def _(g0=g0, l0=l0, done=done, size=size):
                act(make_copy(pl.multiple_of(l0 + done, SEG_ALIGN), pl.multiple_of(g0 + done, SEG_ALIGN), size))


def _segment_waits(seg_ref, n_exp, make_copy):
    total = seg_ref[0, 0, 3 * n_exp]
    *small, big = WAIT_PIECES

    def whole(j, carry):
        make_copy(0, 0, big).wait()
        return carry

    lax.fori_loop(0, lax.shift_right_logical(total, big.bit_length() - 1), whole, 0)
    for size in small:
        @pl.when((total & size) != 0)
        def _(size=size):
            make_copy(0, 0, size).wait()


def _dispatch_kernel(tail_ref, seg_ref, seg_prev_ref, x_ref, mod_ref, slot_t_ref, xs_ref, buf, zbuf, sem, zsem, *,
                     n_exp, top_k):
    i = pl.program_id(0)
    cur = i % 2

    def to_slots(buf_slot, sem_slot):
        return lambda l, g, rows: pltpu.make_async_copy(buf.at[buf_slot, pl.ds(l, rows)], xs_ref.at[pl.ds(g, rows)],
                                                        sem.at[sem_slot])

    @pl.when(i == 0)
    def _():
        zbuf[...] = jnp.zeros_like(zbuf)
        zero_fill = lambda l, g, rows: pltpu.make_async_copy(zbuf.at[pl.ds(l, rows)], xs_ref.at[pl.ds(g, rows)], zsem)
        _segment_copies(tail_ref, n_exp, zero_fill, lambda c: c.start())
        _segment_copies(tail_ref, n_exp, zero_fill, lambda c: c.wait())

    m = mod_ref[0]
    h = (x_ref[...] * (1.0 + m[4:5]) + m[3:4]).astype(BF16)
    n_rows, t = buf.shape[1], x_ref.shape[0]
    row = lax.broadcasted_iota(jnp.int32, (n_rows, t), 0)
    slot_t = slot_t_ref[...].astype(jnp.int32)
    pick = row == slot_t[0:1, :]
    for k in range(1, top_k):
        pick = pick | (row == slot_t[k:k + 1, :])
    buf[cur] = _pack_pairs(_bdot(pick.astype(BF16), h))
    _segment_copies(seg_ref, n_exp, to_slots(cur, cur), lambda c: c.start())

    @pl.when(i > 0)
    def _():
        _segment_waits(seg_prev_ref, n_exp, to_slots(1 - cur, 1 - cur))

    @pl.when(i == pl.num_programs(0) - 1)
    def _():
        _segment_waits(seg_ref, n_exp, to_slots(cur, cur))


def _expert_kernel(be_ref, nu_ref, x_ref, wg_ref, wu_ref, bg_ref, bu_ref, wd_ref, bd_ref, y_ref):
    del be_ref

    @pl.when(pl.program_id(0) < nu_ref[0])
    def _():
        def gate_up(part):
            xb = _unpack_pairs(x_ref[part * MOE_ROWS:(part + 1) * MOE_ROWS, :])
            return _bdot(xb, wg_ref[0]), _bdot(xb, wu_ref[0])

        n_parts = x_ref.shape[0] // MOE_ROWS
        nxt = gate_up(0)
        for part in range(n_parts):
            g, u = nxt
            if part + 1 < n_parts:
                nxt = gate_up(part + 1)
            g = jnp.minimum(g + bg_ref[0], SWIGLU_LIMIT)
            u = jnp.clip(u + bu_ref[0], -SWIGLU_LIMIT, SWIGLU_LIMIT)
            act = (u + 1.0) * (g * jax.nn.sigmoid(SWIGLU_ALPHA * g))
            y_ref[part * MOE_ROWS:(part + 1) * MOE_ROWS, :] = _pack_pairs(_bdot(act, wd_ref[0]) + bd_ref[0])


def _combine_kernel(seg_ref, seg_next_ref, x_ref, mod_ref, route_ref, g_ref, bt_ref, y_ref, o_ref,
                    ybuf, sem, *, n_exp, top_k, alpha):
    i = pl.program_id(0)
    cur = i % 2

    def from_slots(buf_slot):
        return lambda l, g, rows: pltpu.make_async_copy(y_ref.at[pl.ds(g, rows)], ybuf.at[buf_slot, pl.ds(l, rows)],
                                                        sem.at[buf_slot])

    @pl.when(i == 0)
    def _():
        ybuf[...] = jnp.zeros_like(ybuf)
        _segment_copies(seg_ref, n_exp, from_slots(0), lambda c: c.start())

    @pl.when(i < pl.num_programs(0) - 1)
    def _():
        _segment_copies(seg_next_ref, n_exp, from_slots(1 - cur), lambda c: c.start())

    _segment_waits(seg_ref, n_exp, from_slots(cur))
    t, n_rows = x_ref.shape[0], ybuf.shape[1]
    lane = lax.broadcasted_iota(jnp.int32, (t, n_rows), 1)
    route = route_ref[...]
    slot = route[:, :top_k].astype(jnp.int32)
    weights = jnp.zeros((t, n_rows), F32)
    for k in range(top_k):
        weights = jnp.where(lane == slot[:, k:k + 1], route[:, top_k + k:top_k + k + 1], weights)
    acc = _bdot(weights, _unpack_pairs(ybuf[cur]))
    m = mod_ref[0]
    r = alpha * x_ref[...] + m[5:6] * acc
    o_ref[...] = _layer_norm(r, g_ref[...], bt_ref[...])


def _moe_layer(rows, x, mod, router_w, router_b, wg, wu, wd, layer, b_gu, b_down, ln_g, ln_b, alpha, need_ctx):
    t, d = rows.tile, x.shape[1]
    n_exp = router_w.shape[1]
    d_ff = wd.shape[1]
    first = 0 if need_ctx else rows.ctx_tiles
    n_tiles = rows.tiles - first
    n_tok = n_tiles * t
    row = lambda i: (i + first, 0)
    modi = lambda i: (rows.mod_index(i + first), 0, 0)
    tok = lambda i: (i, 0)

    seg_rows = _seg_rows(t, n_exp)
    dp = d // 2
    assert 4 * n_exp <= LANES and t % SEG_PIECES[-1] == 0 and MOE_BLOCK % SEG_PIECES[-1] == 0
    assert 2 * TOP_K <= SUBLANES
    slot_t, route, cnt = pl.pallas_call(
        functools.partial(_router_kernel, top_k=TOP_K),
        grid=(n_tiles,),
        in_specs=[pl.BlockSpec((t, d), row), pl.BlockSpec((1, 6, d), modi),
                  pl.BlockSpec((n_exp, d), lambda i: (0, 0)), pl.BlockSpec((n_exp, 1), lambda i: (0, 0))],
        out_specs=[pl.BlockSpec((SUBLANES, t), tok), pl.BlockSpec((t, LANES), tok),
                   pl.BlockSpec((1, n_exp, 1), lambda i: (i, 0, 0))],
        out_shape=[jax.ShapeDtypeStruct((n_tiles * SUBLANES, t), F32), jax.ShapeDtypeStruct((n_tok, LANES), F32),
                   jax.ShapeDtypeStruct((n_tiles, n_exp, 1), F32)],
        compiler_params=_params("arbitrary"),
        name="moe_router",
    )(x, mod, router_w.T, router_b.reshape(n_exp, 1))

    i32 = jnp.int32
    count = cnt.reshape(n_tiles, n_exp).astype(i32)
    cap = (count + SEG_ALIGN - 1) // SEG_ALIGN * SEG_ALIGN
    e_rows = jnp.sum(cap, axis=0)
    e_pad = (e_rows + MOE_BLOCK - 1) // MOE_BLOCK * MOE_BLOCK
    e_end = jnp.cumsum(e_pad)
    e_start = e_end - e_pad
    seg_global = e_start[None, :] + jnp.cumsum(cap, axis=0) - cap
    seg_local = jnp.cumsum(cap, axis=1) - cap
    fill = jnp.zeros((n_tiles, LANES - 3 * n_exp - 1), i32)
    seg = jnp.concatenate([seg_global, cap, seg_local, jnp.sum(cap, axis=1, keepdims=True), fill], axis=1)
    seg = seg.astype(i32).reshape(n_tiles, 1, LANES)
    tail = jnp.concatenate([e_start + e_rows, e_pad - e_rows, jnp.zeros((LANES - 2 * n_exp,), i32)])
    tail = tail.astype(i32).reshape(1, 1, LANES)
    max_slots = n_tok * TOP_K + n_tiles * n_exp * (SEG_ALIGN - 1) + n_exp * (MOE_BLOCK - 1)
    n_blocks = (max_slots + MOE_BLOCK - 1) // MOE_BLOCK
    n_slots = n_blocks * MOE_BLOCK
    n_used = (e_end[-1] // MOE_BLOCK).astype(i32)
    blk = jnp.minimum(jnp.arange(n_blocks, dtype=i32), n_used - 1) * MOE_BLOCK
    block_expert = jnp.minimum(jnp.sum((e_end[None, :] <= blk[:, None]).astype(i32), axis=1), n_exp - 1)

    smem_tile = lambda index_map: pl.BlockSpec((1, 1, LANES), index_map, memory_space=pltpu.SMEM)
    xs = pl.pallas_call(
        functools.partial(_dispatch_kernel, n_exp=n_exp, top_k=TOP_K),
        grid=(n_tiles,),
        in_specs=[smem_tile(lambda i: (0, 0, 0)), smem_tile(lambda i: (i, 0, 0)),
                  smem_tile(lambda i: (jnp.maximum(i - 1, 0), 0, 0)),
                  pl.BlockSpec((t, d), row), pl.BlockSpec((1, 6, d), modi), pl.BlockSpec((SUBLANES, t), tok)],
        out_specs=pl.BlockSpec(memory_space=pl.ANY),
        out_shape=jax.ShapeDtypeStruct((n_slots, dp), jnp.uint32),
        scratch_shapes=[pltpu.VMEM((2, seg_rows, dp), jnp.uint32), pltpu.VMEM((MOE_BLOCK, dp), jnp.uint32),
                        pltpu.SemaphoreType.DMA((2,)), pltpu.SemaphoreType.DMA(())],
        compiler_params=_params("arbitrary"),
        name="moe_dispatch",
    )(tail, seg, seg, x, mod, slot_t)

    bg = b_gu[:, 0::2].reshape(n_exp, 1, d_ff)
    bu = b_gu[:, 1::2].reshape(n_exp, 1, d_ff)
    used = lambda i, be, nu: (jnp.minimum(i, nu[0] - 1), 0)
    wmap = lambda i, be, nu: (be[i], 0, 0)
    wmap_all = lambda i, be, nu: (be[i] + layer * n_exp, 0, 0)
    y = pl.pallas_call(
        _expert_kernel,
        grid_spec=pltpu.PrefetchScalarGridSpec(
            num_scalar_prefetch=2,
            grid=(n_blocks,),
            in_specs=[pl.BlockSpec((MOE_BLOCK, dp), used),
                      pl.BlockSpec((1, d, d_ff), wmap_all), pl.BlockSpec((1, d, d_ff), wmap_all),
                      pl.BlockSpec((1, 1, d_ff), wmap), pl.BlockSpec((1, 1, d_ff), wmap),
                      pl.BlockSpec((1, d_ff, d), wmap_all), pl.BlockSpec((1, 1, d), wmap)],
            out_specs=pl.BlockSpec((MOE_BLOCK, dp), used)),
        out_shape=jax.ShapeDtypeStruct((n_slots, dp), jnp.uint32),
        compiler_params=_params("arbitrary"),
        name="moe_experts",
    )(block_expert, n_used.reshape(1), xs, wg, wu, bg, bu, wd, b_down.reshape(n_exp, 1, d))

    return pl.pallas_call(
        functools.partial(_combine_kernel, n_exp=n_exp, top_k=TOP_K, alpha=alpha),
        grid=(n_tiles,),
        in_specs=[smem_tile(lambda i: (i, 0, 0)), smem_tile(lambda i: (jnp.minimum(i + 1, n_tiles - 1), 0, 0)),
                  pl.BlockSpec((t, d), row), pl.BlockSpec((1, 6, d), modi),
                  pl.BlockSpec((t, LANES), tok),
                  pl.BlockSpec((1, d), lambda i: (0, 0)), pl.BlockSpec((1, d), lambda i: (0, 0)),
                  pl.BlockSpec(memory_space=pl.ANY)],
        out_specs=pl.BlockSpec((t, d), row if need_ctx else tok),
        out_shape=jax.ShapeDtypeStruct(x.shape if need_ctx else (n_tok, d), F32),
        scratch_shapes=[pltpu.VMEM((2, seg_rows, dp), jnp.uint32), pltpu.SemaphoreType.DMA((2,))],
        input_output_aliases={2: 0} if need_ctx else {},
        compiler_params=_params("arbitrary"),
        name="moe_combine_ln",
    )(seg, seg, x, mod, route, ln_g.reshape(1, d), ln_b.reshape(1, d), y)


def kernel(x, c, ctx, c_ctx, ada_w, ada_b, ln1_g, ln1_b, ln2_g, ln2_b, router_w, router_b, exp_gu_w, exp_gu_b, exp_down_w, exp_down_b, rg_w_in, rg_conv_w, rg_conv_b, rg_gate_a_w, rg_gate_a_b, rg_gate_x_w, rg_gate_x_b, rg_lambda, rg_w_out, gqa_w_qkv, gqa_b_qkv, gqa_sinks, gqa_w_o, gqa_b_o, mla_w_down, mla_q_norm, mla_kv_norm, mla_w_uq, mla_w_ukv, mla_w_o):
    batch, seq, d = x.shape
    n_ctx = ctx.shape[1]
    depth = ada_w.shape[0]
    alpha = (2 * depth) ** 0.25
    rows = _Rows(batch, n_ctx, seq, ROW_TILE)
    prows = _Rows(batch, n_ctx, seq, PROJ_TILE) if (batch * n_ctx) % PROJ_TILE == 0 and seq % PROJ_TILE == 0 else rows
    xa = jnp.concatenate([ctx.reshape(batch * n_ctx, d), x.reshape(batch * seq, d)], axis=0)
    mods = _ada_table(jnp.concatenate([c_ctx[None, :], c], axis=0), ada_w, ada_b)
    wg, wu = _split_gate_up(exp_gu_w)
    wd = exp_down_w.reshape((-1,) + exp_down_w.shape[2:]).astype(BF16)
    for i in range(depth):
        need_ctx = i < depth - 1
        kind, j = i % 3, i // 3
        mod = mods[i]
        if kind == 0:
            prm = dict(w_in=rg_w_in[j], conv_w=rg_conv_w[j], conv_b=rg_conv_b[j], gate_a_w=rg_gate_a_w[j],
                       gate_a_b=rg_gate_a_b[j], gate_x_w=rg_gate_x_w[j], gate_x_b=rg_gate_x_b[j],
                       lam=rg_lambda[j], w_out=rg_w_out[j])
            xa = _rglru_layer(rows, prows, xa, mod, prm, ln1_g[i], ln1_b[i], alpha, need_ctx)
        elif kind == 1:
            prm = dict(w_qkv=gqa_w_qkv[j], b_qkv=gqa_b_qkv[j], sinks=gqa_sinks[j], w_o=gqa_w_o[j], b_o=gqa_b_o[j])
            xa = _gqa_layer(prows, xa, mod, prm, ln1_g[i], ln1_b[i], alpha, need_ctx)
        else:
            prm = dict(w_down=mla_w_down[j], q_norm=mla_q_norm[j], kv_norm=mla_kv_norm[j], w_uq=mla_w_uq[j],
                       w_ukv=mla_w_ukv[j], w_o=mla_w_o[j])
            xa = _mla_layer(prows, xa, mod, prm, ln1_g[i], ln1_b[i], alpha, need_ctx)
        xa = _moe_layer(rows, xa, mod, router_w[i], router_b[i], wg, wu, wd, i, exp_gu_b[i], exp_down_b[i],
                        ln2_g[i], ln2_b[i], alpha, need_ctx)
    return xa.reshape(batch, seq, d)
```

```python
import functools
import math

import jax
import jax.numpy as jnp
from jax import lax
from jax.experimental import pallas as pl
from jax.experimental.pallas import tpu as pltpu

F32 = jnp.float32
BF16 = jnp.bfloat16

GRID_W = 64
LN_EPS = 1e-5
RMS_EPS = 1e-6
ROPE_THETA = 10000.0
NEG_INF = -1e30
RG_C = 8.0
SQRT_FLOOR = 1e-30
GQA_KV = 2
GQA_HD = 64
WINDOW = 128
MLA_HEADS = 16
QK_NOPE = 64
QK_ROPE = 32
V_HD = 64
TOP_K = 4
SWIGLU_LIMIT = 7.0
SWIGLU_ALPHA = 1.702
MOE_BLOCK = 1024
MOE_ROWS = 512

LANES = 128
SUBLANES = 8
BF16_ROWS = 16
ROW_TILE = 256
PROJ_TILE = 512
ATTN_Q = 256
GQA_LOOKAHEAD = 4
MLA_TQ = 1024
MLA_TK = 256
MLA_SUB_K = 128
MLA_SUB_Q = 256
MLA_LOOKAHEAD = 12
MXU_DIM = 256
SEG_ALIGN = SUBLANES
SEG_PIECES = (8, 16, 32)
WAIT_PIECES = (8, 16, 32, 64, 128, 256)
VMEM_LIMIT = 56 * 1024 * 1024


def _params(*sem):
    return pltpu.CompilerParams(dimension_semantics=sem, vmem_limit_bytes=VMEM_LIMIT)


def _bdot(a, b):
    return jnp.dot(a.astype(BF16), b.astype(BF16), preferred_element_type=F32)


def _bdot_nt(a, b):
    return lax.dot_general(a.astype(BF16), b.astype(BF16), (((1,), (1,)), ((), ())),
                           preferred_element_type=F32)


def _layer_norm(r, g, b):
    mu = jnp.mean(r, axis=-1, keepdims=True)
    d = r - mu
    var = jnp.mean(d * d, axis=-1, keepdims=True)
    return d * lax.rsqrt(var + LN_EPS) * g + b


def _gelu_tanh(x):
    return 0.5 * x * (1.0 + jnp.tanh(math.sqrt(2.0 / math.pi) * (x + 0.044715 * (x * x * x))))


class _Rows:
    def __init__(self, batch, n_ctx, seq, tile):
        assert (batch * n_ctx) % tile == 0 and seq % tile == 0
        self.batch, self.n_ctx, self.seq, self.tile = batch, n_ctx, seq, tile
        self.ctx_tiles = batch * n_ctx // tile
        self.lat_tiles = batch * seq // tile
        self.tiles = self.ctx_tiles + self.lat_tiles
        self.lat_per_batch = seq // tile
        self.ctx_per_batch = n_ctx // tile
        self.rows = batch * (n_ctx + seq)

    def mod_index(self, i):
        return jnp.where(i < self.ctx_tiles, 0, 1 + (i - self.ctx_tiles) // self.lat_per_batch)

    def rope_index(self, i):
        return jnp.where(i < self.ctx_tiles, self.lat_per_batch, (i - self.ctx_tiles) % self.lat_per_batch)


def _ada_kernel(c_ref, w_ref, b_ref, o_ref):
    cv = c_ref[...]
    s = cv * jax.nn.sigmoid(cv)
    o_ref[0] = jnp.dot(s, w_ref[0], preferred_element_type=F32,
                       precision=lax.Precision.HIGHEST) + b_ref[0]


def _ada_table(cvec, ada_w, ada_b):
    depth, d, d6 = ada_w.shape
    n = cvec.shape[0]
    chunk = d
    out = pl.pallas_call(
        _ada_kernel,
        grid=(depth, d6 // chunk),
        in_specs=[pl.BlockSpec((n, d), lambda l, j: (0, 0)),
                  pl.BlockSpec((1, d, chunk), lambda l, j: (l, 0, j)),
                  pl.BlockSpec((1, 1, chunk), lambda l, j: (l, 0, j))],
        out_specs=pl.BlockSpec((1, n, chunk), lambda l, j: (l, 0, j)),
        out_shape=jax.ShapeDtypeStruct((depth, n, d6), F32),
        compiler_params=_params("arbitrary", "arbitrary"),
        name="ada_table",
    )(cvec, ada_w, ada_b.reshape(depth, 1, d6))
    return out.reshape(depth, n, 6, d)


def _out_ln_kernel(z_ref, w_ref, b_ref, x_ref, mod_ref, g_ref, bt_ref, o_ref, *, gate_row, alpha):
    y = _bdot(z_ref[...], w_ref[...]) + b_ref[...]
    m = mod_ref[0]
    r = alpha * x_ref[...] + m[gate_row:gate_row + 1] * y
    o_ref[...] = _layer_norm(r, g_ref[...], bt_ref[...])


def _out_ln(rows, z, w, bias, x, mod, ln_g, ln_b, *, alpha, first_tile=0):
    t, d = rows.tile, x.shape[1]
    kdim = z.shape[1]
    n_tiles = rows.tiles - first_tile
    row = lambda i: (i + first_tile, 0)
    return pl.pallas_call(
        functools.partial(_out_ln_kernel, gate_row=2, alpha=alpha),
        grid=(n_tiles,),
        in_specs=[pl.BlockSpec((t, kdim), row),
                  pl.BlockSpec((kdim, d), lambda i: (0, 0)),
                  pl.BlockSpec((1, d), lambda i: (0, 0)),
                  pl.BlockSpec((t, d), row),
                  pl.BlockSpec((1, 6, d), lambda i: (rows.mod_index(i + first_tile), 0, 0)),
                  pl.BlockSpec((1, d), lambda i: (0, 0)),
                  pl.BlockSpec((1, d), lambda i: (0, 0))],
        out_specs=pl.BlockSpec((t, d), row),
        out_shape=jax.ShapeDtypeStruct(x.shape, F32),
        input_output_aliases={3: 0},
        compiler_params=_params("arbitrary"),
        name="out_proj_ln",
    )(z, w.astype(BF16), bias.reshape(1, d), x, mod, ln_g.reshape(1, d), ln_b.reshape(1, d))


def _rg_in_kernel(x_ref, mod_ref, w_ref, gel_ref, rec_ref, *, d_rnn):
    m = mod_ref[0]
    h = (x_ref[...] * (1.0 + m[1:2]) + m[0:1]).astype(BF16)
    gel_ref[...] = _gelu_tanh(_bdot(h, w_ref[:, :d_rnn])).astype(BF16)
    rec_ref[...] = _bdot(h, w_ref[:, d_rnn:])


def _rg_in(rows, x, mod, w_in):
    t, d = rows.tile, x.shape[1]
    d_rnn = w_in.shape[1] // 2
    return pl.pallas_call(
        functools.partial(_rg_in_kernel, d_rnn=d_rnn),
        grid=(rows.tiles,),
        in_specs=[pl.BlockSpec((t, d), lambda i: (i, 0)),
                  pl.BlockSpec((1, 6, d), lambda i: (rows.mod_index(i), 0, 0)),
                  pl.BlockSpec((d, 2 * d_rnn), lambda i: (0, 0))],
        out_specs=[pl.BlockSpec((t, d_rnn), lambda i: (i, 0)),
                   pl.BlockSpec((t, d_rnn), lambda i: (i, 0))],
        out_shape=[jax.ShapeDtypeStruct((rows.rows, d_rnn), BF16),
                   jax.ShapeDtypeStruct((rows.rows, d_rnn), F32)],
        compiler_params=_params("arbitrary"),
        name="rg_in_proj",
    )(x, mod, w_in.astype(BF16))


def _rg_scan_kernel(*refs, reverse, fuse_out, n_blocks, block_w, tile, ctx_tiles, lat_tiles):
    if fuse_out:
        (x_ref, xp_ref, xn_ref, cw_ref, cb_ref, wa_ref, ba_ref, wx_ref, bx_ref, lam_ref,
         hf_ref, gel_ref, out_ref, a_scr, u_scr, h_scr, carry_scr) = refs
    else:
        (x_ref, xp_ref, xn_ref, cw_ref, cb_ref, wa_ref, ba_ref, wx_ref, bx_ref, lam_ref,
         out_ref, a_scr, u_scr, carry_scr) = refs
        h_scr = out_ref
    j = pl.program_id(1)
    is_ctx = j < ctx_tiles
    n_seq = jnp.where(is_ctx, ctx_tiles, lat_tiles)
    step = jnp.where(is_ctx, j, j - ctx_tiles)
    pos = (n_seq - 1 - step) if reverse else step
    prev_ok = (pos > 0).astype(F32)
    next_ok = (pos < n_seq - 1).astype(F32)

    @pl.when(j == 0)
    def _():
        carry_scr[...] = jnp.zeros_like(carry_scr)

    row8 = lax.broadcasted_iota(jnp.int32, (SUBLANES, block_w), 0)
    for n in range(n_blocks):
        cols = slice(n * block_w, (n + 1) * block_w)
        x = x_ref[:, cols]
        prev = xp_ref[SUBLANES - 1:SUBLANES, cols] * prev_ok
        nxt0 = xn_ref[0:1, cols] * next_ok
        nxt1 = xn_ref[1:2, cols] * next_ok
        cw = [cw_ref[k:k + 1, cols] for k in range(4)]
        xc = (cw[0] * pltpu.roll(x, 1, 0) + cw[1] * x + cw[2] * pltpu.roll(x, tile - 1, 0)
              + cw[3] * pltpu.roll(x, tile - 2, 0) + cb_ref[:, cols])
        first, last = x[0:1, :], x[tile - 1:tile, :]
        head = xc[:SUBLANES] + jnp.where(row8 == 0, cw[0] * (prev - last), 0.0)
        tail = xc[tile - SUBLANES:] + jnp.where(
            row8 == SUBLANES - 2, cw[3] * (nxt0 - first),
            jnp.where(row8 == SUBLANES - 1, cw[2] * (nxt0 - first) + cw[3] * (nxt1 - x[1:2, :]), 0.0))
        xc = jnp.concatenate([head, xc[SUBLANES:tile - SUBLANES], tail], axis=0)
        xb = xc.astype(BF16)
        tanh_r = jnp.tanh(_bdot(xb, wa_ref[n]) + ba_ref[:, cols])
        gi = 0.5 * jnp.tanh(_bdot(xb, wx_ref[n]) + bx_ref[:, cols]) + 0.5
        z = -lam_ref[:, cols]
        softplus = jnp.maximum(z, 0.0) + jnp.log1p(jnp.exp(-jnp.abs(z)))
        half_k = (-0.5 * RG_C * math.log2(math.e)) * softplus
        a = jnp.exp2(half_k * tanh_r + half_k)
        a_scr[:, cols] = a
        v = 1.0 - a * a
        u_scr[:, cols] = (v * lax.rsqrt(jnp.maximum(v, SQRT_FLOOR))) * (gi * xc)

    width = n_blocks * block_w
    sub = lax.broadcasted_iota(jnp.int32, (SUBLANES, width), 0)
    groups = tile // SUBLANES

    def body(g, carry):
        gg = (groups - 1 - g) if reverse else g
        r0 = pl.multiple_of(gg * SUBLANES, SUBLANES)
        a8 = a_scr[pl.ds(r0, SUBLANES), :]
        u8 = u_scr[pl.ds(r0, SUBLANES), :]
        for s in (1, 2, 4):
            if reverse:
                a_sh, u_sh, ok = pltpu.roll(a8, SUBLANES - s, 0), pltpu.roll(u8, SUBLANES - s, 0), sub < SUBLANES - s
            else:
                a_sh, u_sh, ok = pltpu.roll(a8, s, 0), pltpu.roll(u8, s, 0), sub >= s
            u8 = jnp.where(ok, a8 * u_sh + u8, u8)
            a8 = jnp.where(ok, a8 * a_sh, a8)
        h8 = a8 * carry + u8
        h_scr[pl.ds(r0, SUBLANES), :] = h8
        return h8[0:1, :] if reverse else h8[SUBLANES - 1:SUBLANES, :]

    carry_scr[...] = lax.fori_loop(0, groups, body, carry_scr[...], unroll=2)
    if fuse_out:
        out_ref[...] = ((hf_ref[...] + h_scr[...]) * gel_ref[...].astype(F32)).astype(BF16)


def _rg_scan(rows, rec, conv_w, conv_b, wa, ba, wx, bx, lam, *, reverse, h_fwd=None, gel=None):
    t = rows.tile
    c = rec.shape[1]
    n_blocks, block_w = wa.shape[0], wa.shape[1]
    nc, nl = rows.ctx_per_batch, rows.lat_per_batch
    halo = t // SUBLANES
    last_halo = rows.rows // SUBLANES - 1
    fuse_out = h_fwd is not None

    def tile_index(b, j):
        is_ctx = j < nc
        step = jnp.where(is_ctx, j, j - nc)
        n_seq = jnp.where(is_ctx, nc, nl)
        pos = (n_seq - 1 - step) if reverse else step
        return jnp.where(is_ctx, b * nc + pos, rows.ctx_tiles + b * nl + pos)

    cur = lambda b, j: (tile_index(b, j), 0)
    prv = lambda b, j: (jnp.maximum(tile_index(b, j) * halo - 1, 0), 0)
    nxt = lambda b, j: (jnp.minimum((tile_index(b, j) + 1) * halo, last_halo), 0)
    full2 = lambda b, j: (0, 0)
    full3 = lambda b, j: (0, 0, 0)
    in_specs = [pl.BlockSpec((t, c), cur), pl.BlockSpec((SUBLANES, c), prv), pl.BlockSpec((SUBLANES, c), nxt),
                pl.BlockSpec((4, c), full2), pl.BlockSpec((1, c), full2),
                pl.BlockSpec((n_blocks, block_w, block_w), full3), pl.BlockSpec((1, c), full2),
                pl.BlockSpec((n_blocks, block_w, block_w), full3), pl.BlockSpec((1, c), full2),
                pl.BlockSpec((1, c), full2)]
    args = [rec, rec, rec, conv_w, conv_b.reshape(1, c), (0.5 * wa).astype(BF16), 0.5 * ba.reshape(1, c),
            (0.5 * wx).astype(BF16), 0.5 * bx.reshape(1, c), lam.reshape(1, c)]
    scratch = [pltpu.VMEM((t, c), F32), pltpu.VMEM((t, c), F32)]
    if fuse_out:
        in_specs += [pl.BlockSpec((t, c), cur), pl.BlockSpec((t, c), cur)]
        args += [h_fwd, gel]
        scratch.append(pltpu.VMEM((t, c), F32))
    scratch.append(pltpu.VMEM((1, c), F32))
    return pl.pallas_call(
        functools.partial(_rg_scan_kernel, reverse=reverse, fuse_out=fuse_out, n_blocks=n_blocks,
                          block_w=block_w, tile=t, ctx_tiles=nc, lat_tiles=nl),
        grid=(rows.batch, nc + nl),
        in_specs=in_specs,
        out_specs=pl.BlockSpec((t, c), cur),
        out_shape=jax.ShapeDtypeStruct((rows.rows, c), BF16 if fuse_out else F32),
        scratch_shapes=scratch,
        compiler_params=_params("arbitrary", "arbitrary"),
        name="rg_scan_bwd" if reverse else "rg_scan_fwd",
    )(*args)


def _rglru_layer(rows, prows, x, mod, p, ln_g, ln_b, alpha, need_ctx):
    gel, rec = _rg_in(prows, x, mod, p["w_in"])
    h_fwd = _rg_scan(rows, rec, p["conv_w"], p["conv_b"], p["gate_a_w"][0], p["gate_a_b"][0],
                     p["gate_x_w"][0], p["gate_x_b"][0], p["lam"][0], reverse=False)
    z = _rg_scan(rows, rec, p["conv_w"], p["conv_b"], p["gate_a_w"][1], p["gate_a_b"][1],
                 p["gate_x_w"][1], p["gate_x_b"][1], p["lam"][1], reverse=True, h_fwd=h_fwd, gel=gel)
    d = x.shape[1]
    return _out_ln(prows, z, p["w_out"], jnp.zeros((d,), F32), x, mod, ln_g, ln_b, alpha=alpha,
                   first_tile=0 if need_ctx else prows.ctx_tiles)


def _axial_angles(seq, rot_dim):
    pos = jnp.arange(seq, dtype=jnp.int32)
    row = (pos // GRID_W).astype(F32)
    col = (pos % GRID_W).astype(F32)
    n_freq = rot_dim // 4
    inv_freq = ROPE_THETA ** (-jnp.arange(n_freq, dtype=F32) / n_freq)
    return jnp.concatenate([row[:, None] * inv_freq, col[:, None] * inv_freq], axis=-1)


def _rope_tables(rows, rot_dim, lead, trail):
    ang = _axial_angles(rows.seq, rot_dim)
    cos, sin = jnp.cos(ang), jnp.sin(ang)
    ones = lambda w: jnp.ones((rows.seq, w), F32)
    zeros = lambda w: jnp.zeros((rows.seq, w), F32)
    c = jnp.concatenate([ones(lead), cos, cos, ones(trail)], axis=-1)
    s = jnp.concatenate([zeros(lead), -sin, sin, zeros(trail)], axis=-1)
    reps = LANES // c.shape[1]
    c, s = jnp.tile(c, (1, reps)), jnp.tile(s, (1, reps))
    t = rows.tile
    c = jnp.concatenate([c.reshape(rows.lat_per_batch, t, LANES), jnp.ones((1, t, LANES), F32)], axis=0)
    s = jnp.concatenate([s.reshape(rows.lat_per_batch, t, LANES), jnp.zeros((1, t, LANES), F32)], axis=0)
    return c, s


def _rope_chunk(x, cos, sin, half):
    lane = lax.broadcasted_iota(jnp.int32, x.shape, 1)
    partner = jnp.where((lane % (2 * half)) < half, pltpu.roll(x, LANES - half, 1), pltpu.roll(x, half, 1))
    return x * cos + partner * sin


def _gqa_proj_kernel(x_ref, mod_ref, w_ref, b_ref, cos_ref, sin_ref, q_ref, kv_ref, *, q_dim, scale, half):
    m = mod_ref[0]
    h = (x_ref[...] * (1.0 + m[1:2]) + m[0:1]).astype(BF16)
    cos, sin = cos_ref[0], sin_ref[0]
    n_q = q_dim // LANES
    n_all = w_ref.shape[1] // LANES
    for c in range(n_all):
        cols = slice(c * LANES, (c + 1) * LANES)
        p = _bdot(h, w_ref[:, cols]) + b_ref[:, cols]
        is_v = c in (n_q + 1, n_q + 3)
        if not is_v:
            p = _rope_chunk(p, cos, sin, half)
        if c < n_q:
            q_ref[:, cols] = (p * scale).astype(BF16)
        else:
            kv_ref[:, (c - n_q) * LANES:(c - n_q + 1) * LANES] = p.astype(BF16)


def _gqa_attn_kernel(*refs, windowed, window, seq, n_pairs, pairs_per_kv):
    if windowed:
        q_ref, kv_ref, kvc_ref, sink_ref, o_ref = refs
    else:
        q_ref, kvc_ref, sink_ref, _, o_ref = refs
    tq = q_ref.shape[0]
    kvc = kvc_ref[...]
    if windowed:
        span = tq + 2 * window
        qs = pl.program_id(1) * tq
        ws = pl.multiple_of(jnp.clip(qs - window, 0, seq - span), LANES)
        kv = jnp.concatenate([kv_ref[pl.ds(ws, span), :], kvc], axis=0)
        n_keys = kv.shape[0]
        qpos = qs + lax.broadcasted_iota(jnp.int32, (tq, n_keys), 0)
        col = lax.broadcasted_iota(jnp.int32, (tq, n_keys), 1)
        mask = (jnp.abs(ws + col - qpos) <= window) | (col >= span)
    else:
        kv = kvc
        mask = None
    lane = lax.broadcasted_iota(jnp.int32, (kv.shape[0], LANES), 1)
    low = lane < GQA_HD
    zero = jnp.zeros((kv.shape[0], LANES), BF16)
    one_hi = (lane == GQA_HD).astype(F32).astype(BF16)
    one_lo = (lane == 0).astype(F32).astype(BF16)
    low_q = lax.broadcasted_iota(jnp.int32, (tq, LANES), 1) < GQA_HD
    k_plain, v_plain = kv[:, 0:LANES], kv[:, LANES:2 * LANES]
    k_swap, v_swap = kv[:, 2 * LANES:3 * LANES], kv[:, 3 * LANES:4 * LANES]
    k_half, v_half = {}, {}
    for g in range(GQA_KV):
        k_half[g, 0] = jnp.where(low, k_plain if g == 0 else k_swap, zero)
        k_half[g, 1] = jnp.where(low, zero, k_swap if g == 0 else k_plain)
        v_half[g, 0] = jnp.where(low, v_plain if g == 0 else v_swap, one_hi)
        v_half[g, 1] = jnp.where(low, one_lo, v_swap if g == 0 else v_plain)
    items = [(pr, hh) for pr in range(n_pairs) for hh in range(2)]

    def scores(item):
        pr, hh = item
        return _bdot_nt(q_ref[:, pr * LANES:(pr + 1) * LANES], k_half[pr // pairs_per_kv, hh])

    ahead = [scores(it) for it in items[:GQA_LOOKAHEAD]]
    acc = None
    for n, (pr, hh) in enumerate(items):
        s = ahead.pop(0)
        if n + GQA_LOOKAHEAD < len(items):
            ahead.append(scores(items[n + GQA_LOOKAHEAD]))
        head = 2 * pr + hh
        if mask is not None:
            s = jnp.where(mask, s, NEG_INF)
        sk = sink_ref[head:head + 1, 0:1]
        mx = jnp.maximum(jnp.max(s, axis=1, keepdims=True), sk)
        p = jnp.exp((s - mx).astype(BF16))
        pv = _bdot(p, v_half[pr // pairs_per_kv, hh])
        sum_lane = GQA_HD if hh == 0 else 0
        denom = pv[:, sum_lane:sum_lane + 1] + jnp.exp(sk - mx)
        part = pv * (1.0 / denom)
        if hh == 1:
            o_ref[:, pr * LANES:(pr + 1) * LANES] = jnp.where(low_q, acc, part).astype(BF16)
        acc = part


def _gqa_layer(rows, x, mod, p, ln_g, ln_b, alpha, need_ctx):
    t, d = rows.tile, x.shape[1]
    w_qkv, b_qkv = p["w_qkv"], p["b_qkv"]
    kv_dim = GQA_KV * GQA_HD
    q_dim = w_qkv.shape[1] - 2 * kv_dim
    n_heads = q_dim // GQA_HD
    assert kv_dim == LANES and GQA_KV == 2 and q_dim % LANES == 0
    swap = lambda a: jnp.concatenate([a[..., GQA_HD:], a[..., :GQA_HD]], axis=-1)
    wk, wv = w_qkv[:, q_dim:q_dim + kv_dim], w_qkv[:, q_dim + kv_dim:]
    bk, bv = b_qkv[q_dim:q_dim + kv_dim], b_qkv[q_dim + kv_dim:]
    w_ext = jnp.concatenate([w_qkv, swap(wk), swap(wv)], axis=1).astype(BF16)
    b_ext = jnp.concatenate([b_qkv, swap(bk), swap(bv)]).reshape(1, -1)
    n_ext = w_ext.shape[1]
    cos, sin = _rope_tables(rows, GQA_HD, 0, 0)
    q, kv = pl.pallas_call(
        functools.partial(_gqa_proj_kernel, q_dim=q_dim, scale=GQA_HD ** -0.5, half=GQA_HD // 2),
        grid=(rows.tiles,),
        in_specs=[pl.BlockSpec((t, d), lambda i: (i, 0)),
                  pl.BlockSpec((1, 6, d), lambda i: (rows.mod_index(i), 0, 0)),
                  pl.BlockSpec((d, n_ext), lambda i: (0, 0)),
                  pl.BlockSpec((1, n_ext), lambda i: (0, 0)),
                  pl.BlockSpec((1, t, LANES), lambda i: (rows.rope_index(i), 0, 0)),
                  pl.BlockSpec((1, t, LANES), lambda i: (rows.rope_index(i), 0, 0))],
        out_specs=[pl.BlockSpec((t, q_dim), lambda i: (i, 0)),
                   pl.BlockSpec((t, 4 * LANES), lambda i: (i, 0))],
        out_shape=[jax.ShapeDtypeStruct((rows.rows, q_dim), BF16),
                   jax.ShapeDtypeStruct((rows.rows, 4 * LANES), BF16)],
        compiler_params=_params("arbitrary"),
        name="gqa_qkv_proj",
    )(x, mod, w_ext, b_ext, cos, sin)

    sinks = jnp.broadcast_to(p["sinks"].astype(F32)[:, None], (n_heads, LANES))
    b_, s_, n_ctx = rows.batch, rows.seq, rows.n_ctx
    ctx_rows = b_ * n_ctx
    assert ctx_rows % s_ == 0 and s_ >= ATTN_Q + 2 * WINDOW
    q_blocks = s_ // ATTN_Q
    n_pairs = q_dim // LANES
    common = dict(window=WINDOW, seq=s_, n_pairs=n_pairs, pairs_per_kv=n_pairs // GQA_KV)
    o_shape = jax.ShapeDtypeStruct((rows.rows, q_dim), BF16)
    o_lat = pl.pallas_call(
        functools.partial(_gqa_attn_kernel, windowed=True, **common),
        grid=(b_, q_blocks),
        in_specs=[pl.BlockSpec((ATTN_Q, q_dim), lambda b, j: (ctx_rows // ATTN_Q + b * q_blocks + j, 0)),
                  pl.BlockSpec((s_, 4 * LANES), lambda b, j: (ctx_rows // s_ + b, 0)),
                  pl.BlockSpec((n_ctx, 4 * LANES), lambda b, j: (b, 0)),
                  pl.BlockSpec((n_heads, LANES), lambda b, j: (0, 0))],
        out_specs=pl.BlockSpec((ATTN_Q, q_dim), lambda b, j: (ctx_rows // ATTN_Q + b * q_blocks + j, 0)),
        out_shape=o_shape,
        compiler_params=_params("arbitrary", "arbitrary"),
        name="gqa_window_attn",
    )(q, kv, kv, sinks)
    if need_ctx:
        cq_blocks = n_ctx // ATTN_Q
        o = pl.pallas_call(
            functools.partial(_gqa_attn_kernel, windowed=False, **common),
            grid=(b_, cq_blocks),
            in_specs=[pl.BlockSpec((ATTN_Q, q_dim), lambda b, j: (b * cq_blocks + j, 0)),
                      pl.BlockSpec((n_ctx, 4 * LANES), lambda b, j: (b, 0)),
                      pl.BlockSpec((n_heads, LANES), lambda b, j: (0, 0)),
                      pl.BlockSpec(memory_space=pl.ANY)],
            out_specs=pl.BlockSpec((ATTN_Q, q_dim), lambda b, j: (b * cq_blocks + j, 0)),
            out_shape=o_shape,
            input_output_aliases={3: 0},
            compiler_params=_params("arbitrary", "arbitrary"),
            name="gqa_ctx_attn",
        )(q, kv, sinks, o_lat)
    else:
        o = o_lat
    return _out_ln(rows, o, p["w_o"], p["b_o"], x, mod, ln_g, ln_b, alpha=alpha,
                   first_tile=0 if need_ctx else rows.ctx_tiles)


def _mla_proj_kernel(x_ref, mod_ref, wd_ref, qn_ref, kvn_ref, wq_ref, wk_ref, wv_ref, cos_ref, sin_ref,
                     q_ref, k_ref, vt_ref, *, q_lora, kv_lora, scale, n_heads):
    m = mod_ref[0]
    h = (x_ref[...] * (1.0 + m[1:2]) + m[0:1]).astype(BF16)
    p = _bdot(h, wd_ref[...])
    cq, ckv = p[:, :q_lora], p[:, q_lora:q_lora + kv_lora]
    cos, sin = cos_ref[0], sin_ref[0]
    k_rope = _rope_chunk(p[:, q_lora + kv_lora:], cos, sin, QK_ROPE // 2)
    cq = (cq * lax.rsqrt(jnp.mean(cq * cq, axis=-1, keepdims=True) + RMS_EPS) * qn_ref[...]).astype(BF16)
    ckv = (ckv * lax.rsqrt(jnp.mean(ckv * ckv, axis=-1, keepdims=True) + RMS_EPS) * kvn_ref[...]).astype(BF16)
    for hd in range(n_heads):
        cols = slice(hd * LANES, (hd + 1) * LANES)
        qh = _rope_chunk(_bdot(cq, wq_ref[:, cols]), cos, sin, QK_ROPE // 2)
        q_ref[:, cols] = (qh * scale).astype(BF16)
        k_ref[:, cols] = (_bdot(ckv, wk_ref[:, cols]) + k_rope).astype(BF16)
    vt_ref[...] = _bdot(ckv, wv_ref[...]).T.astype(BF16)


def _mla_attn_kernel(*refs, n_heads):
    q_ref, k_ref, vt_ref = refs[:3]
    o_ref, m_scr, l_scr, acc_scr = refs[-4:]
    kt = pl.program_id(2)

    @pl.when(kt == 0)
    def _():
        m_scr[...] = jnp.full_like(m_scr, -jnp.inf)
        l_scr[...] = jnp.zeros_like(l_scr)
        acc_scr[...] = jnp.zeros_like(acc_scr)

    tq, tk = q_ref.shape[0], k_ref.shape[0]
    sub_q, sub_k = min(MLA_SUB_Q, tq), min(MLA_SUB_K, tk)
    n_kh = tk // sub_k
    ones = jnp.ones((BF16_ROWS, sub_k), BF16)
    items = [(hd, kh, qh) for hd in range(n_heads) for qh in range(tq // sub_q) for kh in range(n_kh)]

    def scores(item):
        hd, kh, qh = item
        cols = slice(hd * LANES, (hd + 1) * LANES)
        s = _bdot_nt(k_ref[kh * sub_k:(kh + 1) * sub_k, cols], q_ref[qh * sub_q:(qh + 1) * sub_q, cols])
        return s, jnp.max(s, axis=0, keepdims=True)

    ahead = [scores(it) for it in items[:MLA_LOOKAHEAD]]
    state = {}
    for n, (hd, kh, qh) in enumerate(items):
        rws = slice(hd * V_HD, (hd + 1) * V_HD)
        qcols = slice(qh * sub_q, (qh + 1) * sub_q)
        s_t, s_max = ahead.pop(0)
        if n + MLA_LOOKAHEAD < len(items):
            ahead.append(scores(items[n + MLA_LOOKAHEAD]))
        if kh == 0:
            state[hd, qh] = (m_scr[hd:hd + 1, qcols], l_scr[hd:hd + 1, qcols], acc_scr[rws, qcols])
        m_old, l_old, acc = state[hd, qh]
        m_new = jnp.maximum(m_old, s_max)
        p_t = jnp.exp2((s_t - m_new).astype(BF16))
        corr = jnp.exp2(m_old - m_new)
        vt = vt_ref[rws, kh * sub_k:(kh + 1) * sub_k]
        pv = _bdot(jnp.concatenate([vt, ones], axis=0), p_t)
        state[hd, qh] = (m_new, corr * l_old + pv[V_HD:V_HD + 1, :], acc * corr + pv[:V_HD, :])
        if kh == n_kh - 1:
            m_scr[hd:hd + 1, qcols], l_scr[hd:hd + 1, qcols], acc_scr[rws, qcols] = state.pop((hd, qh))

    @pl.when(kt == pl.num_programs(2) - 1)
    def _():
        for hd in range(n_heads):
            rws = slice(hd * V_HD, (hd + 1) * V_HD)
            acc_scr[rws, :] = acc_scr[rws, :] * (1.0 / l_scr[hd:hd + 1, :])
        o_ref[...] = acc_scr[...].T.astype(BF16)


def _mla_layer(rows, x, mod, p, ln_g, ln_b, alpha, need_ctx):
    t, d = rows.tile, x.shape[1]
    h_ = MLA_HEADS
    w_down, w_uq, w_ukv = p["w_down"], p["w_uq"], p["w_ukv"]
    q_lora = w_uq.shape[0]
    kv_lora = w_ukv.shape[0]
    qk = QK_NOPE + QK_ROPE
    assert QK_NOPE == V_HD == LANES // 2 and h_ % 2 == 0 and q_lora % LANES == 0 and kv_lora % LANES == 0
    pad = LANES - qk
    zc = lambda r, w: jnp.zeros((r, w), F32)
    wd_p = jnp.concatenate([w_down[:, :q_lora + kv_lora], zc(d, QK_NOPE), w_down[:, q_lora + kv_lora:],
                            zc(d, pad)], axis=1).astype(BF16)
    wq_p = jnp.concatenate([w_uq.reshape(q_lora, h_, qk), jnp.zeros((q_lora, h_, pad), F32)],
                           axis=-1).reshape(q_lora, h_ * LANES).astype(BF16)
    ukv = w_ukv.reshape(kv_lora, h_, QK_NOPE + V_HD)
    wk_p = jnp.concatenate([ukv[..., :QK_NOPE], jnp.zeros((kv_lora, h_, LANES - QK_NOPE), F32)],
                           axis=-1).reshape(kv_lora, h_ * LANES).astype(BF16)
    wv_p = ukv[..., QK_NOPE:].reshape(kv_lora, h_ * V_HD).astype(BF16)
    cos, sin = _rope_tables(rows, QK_ROPE, QK_NOPE, pad)
    n_down = wd_p.shape[1]
    q, k, vt = pl.pallas_call(
        functools.partial(_mla_proj_kernel, q_lora=q_lora, kv_lora=kv_lora, scale=qk ** -0.5 * math.log2(math.e),
                          n_heads=h_),
        grid=(rows.tiles,),
        in_specs=[pl.BlockSpec((t, d), lambda i: (i, 0)),
                  pl.BlockSpec((1, 6, d), lambda i: (rows.mod_index(i), 0, 0)),
                  pl.BlockSpec((d, n_down), lambda i: (0, 0)),
                  pl.BlockSpec((1, q_lora), lambda i: (0, 0)),
                  pl.BlockSpec((1, kv_lora), lambda i: (0, 0)),
                  pl.BlockSpec((q_lora, h_ * LANES), lambda i: (0, 0)),
                  pl.BlockSpec((kv_lora, h_ * LANES), lambda i: (0, 0)),
                  pl.BlockSpec((kv_lora, h_ * V_HD), lambda i: (0, 0)),
                  pl.BlockSpec((1, t, LANES), lambda i: (rows.rope_index(i), 0, 0)),
                  pl.BlockSpec((1, t, LANES), lambda i: (rows.rope_index(i), 0, 0))],
        out_specs=[pl.BlockSpec((t, h_ * LANES), lambda i: (i, 0)),
                   pl.BlockSpec((t, h_ * LANES), lambda i: (i, 0)),
                   pl.BlockSpec((h_ * V_HD, t), lambda i: (0, i))],
        out_shape=[jax.ShapeDtypeStruct((rows.rows, h_ * LANES), BF16),
                   jax.ShapeDtypeStruct((rows.rows, h_ * LANES), BF16),
                   jax.ShapeDtypeStruct((h_ * V_HD, rows.rows), BF16)],
        compiler_params=_params("arbitrary"),
        name="mla_proj",
    )(x, mod, wd_p, p["q_norm"].reshape(1, -1), p["kv_norm"].reshape(1, -1), wq_p, wk_p, wv_p, cos, sin)

    b_, s_, n_ctx = rows.batch, rows.seq, rows.n_ctx
    tk = MLA_TK
    tq, tq_ctx = min(MLA_TQ, s_), min(MLA_TQ, n_ctx)
    assert n_ctx % tk == 0 and s_ % tk == 0 and n_ctx % tq_ctx == 0 and s_ % tq == 0
    ck, lk = n_ctx // tk, s_ // tk
    ctx_kblocks = b_ * ck

    def kv_block(b, kt):
        return jnp.where(kt < ck, b * ck + kt, ctx_kblocks + b * lk + (kt - ck))

    def scratch(rows_q):
        return [pltpu.VMEM((h_, rows_q), F32), pltpu.VMEM((h_, rows_q), F32), pltpu.VMEM((h_ * V_HD, rows_q), F32)]

    o_shape = jax.ShapeDtypeStruct((rows.rows, h_ * V_HD), BF16)
    lq = s_ // tq
    ctx_qblocks = b_ * n_ctx // tq
    o_lat = pl.pallas_call(
        functools.partial(_mla_attn_kernel, n_heads=h_),
        grid=(b_, lq, ck + lk),
        in_specs=[pl.BlockSpec((tq, h_ * LANES), lambda b, i, kt: (ctx_qblocks + b * lq + i, 0)),
                  pl.BlockSpec((tk, h_ * LANES), lambda b, i, kt: (kv_block(b, kt), 0)),
                  pl.BlockSpec((h_ * V_HD, tk), lambda b, i, kt: (0, kv_block(b, kt)))],
        out_specs=pl.BlockSpec((tq, h_ * V_HD), lambda b, i, kt: (ctx_qblocks + b * lq + i, 0)),
        out_shape=o_shape,
        scratch_shapes=scratch(tq),
        compiler_params=_params("arbitrary", "arbitrary", "arbitrary"),
        name="mla_attn",
    )(q, k, vt)
    if need_ctx:
        cq = n_ctx // tq_ctx
        o = pl.pallas_call(
            functools.partial(_mla_attn_kernel, n_heads=h_),
            grid=(b_, cq, ck),
            in_specs=[pl.BlockSpec((tq_ctx, h_ * LANES), lambda b, i, kt: (b * cq + i, 0)),
                      pl.BlockSpec((tk, h_ * LANES), lambda b, i, kt: (b * ck + kt, 0)),
                      pl.BlockSpec((h_ * V_HD, tk), lambda b, i, kt: (0, b * ck + kt)),
                      pl.BlockSpec(memory_space=pl.ANY)],
            out_specs=pl.BlockSpec((tq_ctx, h_ * V_HD), lambda b, i, kt: (b * cq + i, 0)),
            out_shape=o_shape,
            scratch_shapes=scratch(tq_ctx),
            input_output_aliases={3: 0},
            compiler_params=_params("arbitrary", "arbitrary", "arbitrary"),
            name="mla_ctx_attn",
        )(q, k, vt, o_lat)
    else:
        o = o_lat
    return _out_ln(rows, o, p["w_o"], jnp.zeros((d,), F32), x, mod, ln_g, ln_b, alpha=alpha,
                   first_tile=0 if need_ctx else rows.ctx_tiles)


def _split_gu_kernel(w_ref, g_ref, u_ref):
    win = 2 * MXU_DIM
    r_i = lax.broadcasted_iota(jnp.int32, (win, MXU_DIM), 0)
    c_i = lax.broadcasted_iota(jnp.int32, (win, MXU_DIM), 1)
    pick_even = (r_i == 2 * c_i).astype(BF16)
    pick_odd = (r_i == 2 * c_i + 1).astype(BF16)
    for j in range(w_ref.shape[2] // win):
        w = w_ref[0, :, j * win:(j + 1) * win].astype(BF16)
        g_ref[0, :, j * MXU_DIM:(j + 1) * MXU_DIM] = _bdot(w, pick_even).astype(BF16)
        u_ref[0, :, j * MXU_DIM:(j + 1) * MXU_DIM] = _bdot(w, pick_odd).astype(BF16)


def _split_gate_up(w_gu):
    depth, n_exp, d, f2 = w_gu.shape
    w = w_gu.reshape(depth * n_exp, d, f2)
    tr = PROJ_TILE
    spec_out = pl.BlockSpec((1, tr, f2 // 2), lambda e, r: (e, r, 0))
    shape_out = jax.ShapeDtypeStruct((depth * n_exp, d, f2 // 2), BF16)
    return pl.pallas_call(
        _split_gu_kernel,
        grid=(depth * n_exp, d // tr),
        in_specs=[pl.BlockSpec((1, tr, f2), lambda e, r: (e, r, 0))],
        out_specs=[spec_out, spec_out],
        out_shape=[shape_out, shape_out],
        compiler_params=_params("arbitrary", "arbitrary"),
        name="moe_split_gate_up",
    )(w)


def _pack_pairs(a):
    w = a.shape[1] // 2
    r = a.astype(BF16).astype(F32)
    lo = lax.bitcast_convert_type(r[:, :w], jnp.uint32)
    hi = lax.bitcast_convert_type(r[:, w:], jnp.uint32)
    return (lo >> 16) | (hi & jnp.uint32(0xFFFF0000))


def _unpack_pairs(u):
    lo = lax.bitcast_convert_type(u << 16, F32)
    hi = lax.bitcast_convert_type(u & jnp.uint32(0xFFFF0000), F32)
    return jnp.concatenate([lo, hi], axis=1).astype(BF16)


def _seg_rows(t, n_exp):
    return t * TOP_K + n_exp * SEG_ALIGN


def _router_kernel(x_ref, mod_ref, rw_ref, rb_ref, by_token_ref, by_col_ref, cnt_ref, *, top_k):
    m = mod_ref[0]
    h = x_ref[...] * (1.0 + m[4:5]) + m[3:4]
    w_t = rw_ref[...]
    h_hi, w_hi = h.astype(BF16), w_t.astype(BF16)
    h_lo = (h - h_hi.astype(F32)).astype(BF16)
    w_lo = (w_t - w_hi.astype(F32)).astype(BF16)
    logits = _bdot_nt(w_hi, h_hi) + (_bdot_nt(w_lo, h_hi) + _bdot_nt(w_hi, h_lo)) + rb_ref[...]
    n_exp, t = logits.shape
    e_idx = lax.broadcasted_iota(jnp.int32, (n_exp, t), 0).astype(F32)
    work = logits
    sel, val = [], []
    for _ in range(top_k):
        mx = jnp.max(work, axis=0, keepdims=True)
        pick = jnp.min(jnp.where(work == mx, e_idx, float(n_exp)), axis=0, keepdims=True)
        sel.append(pick)
        val.append(mx)
        work = jnp.where(e_idx == pick, -jnp.inf, work)
    ex = [jnp.exp(v - val[0]) for v in val]
    tot = ex[0]
    for e in ex[1:]:
        tot = tot + e
    onehot = jnp.zeros((n_exp, t), F32)
    for pick in sel:
        onehot = onehot + (e_idx == pick).astype(F32)
    r_i = lax.broadcasted_iota(jnp.int32, (t, t), 0)
    c_i = lax.broadcasted_iota(jnp.int32, (t, t), 1)
    before = _bdot(onehot, (r_i < c_i).astype(BF16))
    count = jnp.sum(onehot, axis=1, keepdims=True)
    cap = jnp.floor((count + (SEG_ALIGN - 1)) * (1.0 / SEG_ALIGN)) * SEG_ALIGN
    e_r = lax.broadcasted_iota(jnp.int32, (n_exp, n_exp), 0)
    e_c = lax.broadcasted_iota(jnp.int32, (n_exp, n_exp), 1)
    seg_start = _bdot((e_c < e_r).astype(BF16), jnp.broadcast_to(cap, (n_exp, LANES)))[:, 0:1]
    base = before + seg_start
    rows_out = [jnp.sum(jnp.where(e_idx == sel[k], base, 0.0), axis=0, keepdims=True) for k in range(top_k)]
    rows_out += [ex[k] / tot for k in range(top_k)]
    r8 = lax.broadcasted_iota(jnp.int32, (SUBLANES, t), 0)
    slab = jnp.zeros((SUBLANES, t), F32)
    for k, v in enumerate(rows_out):
        slab = jnp.where(r8 == k, v, slab)
    by_token_ref[...] = slab
    by_col_ref[...] = jnp.concatenate([slab, jnp.zeros((LANES - SUBLANES, t), F32)], axis=0).T
    cnt_ref[0] = count


def _segment_copies(seg_ref, n_exp, make_copy, act):
    *small, big = SEG_PIECES
    for e in range(n_exp):
        g0, rows, l0 = seg_ref[0, 0, e], seg_ref[0, 0, n_exp + e], seg_ref[0, 0, 2 * n_exp + e]

        def whole(j, carry, g0=g0, l0=l0):
            act(make_copy(pl.multiple_of(l0 + j * big, SEG_ALIGN), pl.multiple_of(g0 + j * big, SEG_ALIGN), big))
            return carry

        lax.fori_loop(0, lax.shift_right_logical(rows, big.bit_length() - 1), whole, 0)
        for size in small:
            done = rows & (-2 * size)

            @pl.when((rows & size) != 0)
            def _(g0=g0, l0=l0, done=done, size=size):
                act(make_copy(pl.multiple_of(l0 + done, SEG_ALIGN), pl.multiple_of(g0 + done, SEG_ALIGN), size))


def _segment_waits(seg_ref, n_exp, make_copy):
    total = seg_ref[0, 0, 3 * n_exp]
    *small, big = WAIT_PIECES

    def whole(j, carry):
        make_copy(0, 0, big).wait()
        return carry

    lax.fori_loop(0, lax.shift_right_logical(total, big.bit_length() - 1), whole, 0)
    for size in small:
        @pl.when((total & size) != 0)
        def _(size=size):
            make_copy(0, 0, size).wait()


def _dispatch_kernel(tail_ref, seg_ref, seg_prev_ref, x_ref, mod_ref, slot_t_ref, xs_ref, buf, zbuf, sem, zsem, *,
                     n_exp, top_k):
    i = pl.program_id(0)
    cur = i % 2

    def to_slots(buf_slot, sem_slot):
        return lambda l, g, rows: pltpu.make_async_copy(buf.at[buf_slot, pl.ds(l, rows)], xs_ref.at[pl.ds(g, rows)],
                                                        sem.at[sem_slot])

    @pl.when(i == 0)
    def _():
        zbuf[...] = jnp.zeros_like(zbuf)
        zero_fill = lambda l, g, rows: pltpu.make_async_copy(zbuf.at[pl.ds(l, rows)], xs_ref.at[pl.ds(g, rows)], zsem)
        _segment_copies(tail_ref, n_exp, zero_fill, lambda c: c.start())
        _segment_copies(tail_ref, n_exp, zero_fill, lambda c: c.wait())

    m = mod_ref[0]
    h = (x_ref[...] * (1.0 + m[4:5]) + m[3:4]).astype(BF16)
    n_rows, t = buf.shape[1], x_ref.shape[0]
    row = lax.broadcasted_iota(jnp.int32, (n_rows, t), 0)
    slot_t = slot_t_ref[...].astype(jnp.int32)
    pick = row == slot_t[0:1, :]
    for k in range(1, top_k):
        pick = pick | (row == slot_t[k:k + 1, :])
    buf[cur] = _pack_pairs(_bdot(pick.astype(BF16), h))
    _segment_copies(seg_ref, n_exp, to_slots(cur, cur), lambda c: c.start())

    @pl.when(i > 0)
    def _():
        _segment_waits(seg_prev_ref, n_exp, to_slots(1 - cur, 1 - cur))

    @pl.when(i == pl.num_programs(0) - 1)
    def _():
        _segment_waits(seg_ref, n_exp, to_slots(cur, cur))


def _expert_kernel(be_ref, nu_ref, x_ref, wg_ref, wu_ref, bg_ref, bu_ref, wd_ref, bd_ref, y_ref):
    del be_ref

    @pl.when(pl.program_id(0) < nu_ref[0])
    def _():
        def gate_up(part):
            xb = _unpack_pairs(x_ref[part * MOE_ROWS:(part + 1) * MOE_ROWS, :])
            return _bdot(xb, wg_ref[0]), _bdot(xb, wu_ref[0])

        n_parts = x_ref.shape[0] // MOE_ROWS
        nxt = gate_up(0)
        for part in range(n_parts):
            g, u = nxt
            if part + 1 < n_parts:
                nxt = gate_up(part + 1)
            g = jnp.minimum(g + bg_ref[0], SWIGLU_LIMIT)
            u = jnp.clip(u + bu_ref[0], -SWIGLU_LIMIT, SWIGLU_LIMIT)
            act = (u + 1.0) * (g * jax.nn.sigmoid(SWIGLU_ALPHA * g))
            y_ref[part * MOE_ROWS:(part + 1) * MOE_ROWS, :] = _pack_pairs(_bdot(act, wd_ref[0]) + bd_ref[0])


def _combine_kernel(seg_ref, seg_next_ref, x_ref, mod_ref, route_ref, g_ref, bt_ref, y_ref, o_ref,
                    ybuf, sem, *, n_exp, top_k, alpha):
    i = pl.program_id(0)
    cur = i % 2

    def from_slots(buf_slot):
        return lambda l, g, rows: pltpu.make_async_copy(y_ref.at[pl.ds(g, rows)], ybuf.at[buf_slot, pl.ds(l, rows)],
                                                        sem.at[buf_slot])

    @pl.when(i == 0)
    def _():
        ybuf[...] = jnp.zeros_like(ybuf)
        _segment_copies(seg_ref, n_exp, from_slots(0), lambda c: c.start())

    @pl.when(i < pl.num_programs(0) - 1)
    def _():
        _segment_copies(seg_next_ref, n_exp, from_slots(1 - cur), lambda c: c.start())

    _segment_waits(seg_ref, n_exp, from_slots(cur))
    t, n_rows = x_ref.shape[0], ybuf.shape[1]
    lane = lax.broadcasted_iota(jnp.int32, (t, n_rows), 1)
    route = route_ref[...]
    slot = route[:, :top_k].astype(jnp.int32)
    weights = jnp.zeros((t, n_rows), F32)
    for k in range(top_k):
        weights = jnp.where(lane == slot[:, k:k + 1], route[:, top_k + k:top_k + k + 1], weights)
    acc = _bdot(weights, _unpack_pairs(ybuf[cur]))
    m = mod_ref[0]
    r = alpha * x_ref[...] + m[5:6] * acc
    o_ref[...] = _layer_norm(r, g_ref[...], bt_ref[...])


def _moe_layer(rows, x, mod, router_w, router_b, wg, wu, wd, layer, b_gu, b_down, ln_g, ln_b, alpha, need_ctx):
    t, d = rows.tile, x.shape[1]
    n_exp = router_w.shape[1]
    d_ff = wd.shape[1]
    first = 0 if need_ctx else rows.ctx_tiles
    n_tiles = rows.tiles - first
    n_tok = n_tiles * t
    row = lambda i: (i + first, 0)
    modi = lambda i: (rows.mod_index(i + first), 0, 0)
    tok = lambda i: (i, 0)

    seg_rows = _seg_rows(t, n_exp)
    dp = d // 2
    assert 4 * n_exp <= LANES and t % SEG_PIECES[-1] == 0 and MOE_BLOCK % SEG_PIECES[-1] == 0
    assert 2 * TOP_K <= SUBLANES
    slot_t, route, cnt = pl.pallas_call(
        functools.partial(_router_kernel, top_k=TOP_K),
        grid=(n_tiles,),
        in_specs=[pl.BlockSpec((t, d), row), pl.BlockSpec((1, 6, d), modi),
                  pl.BlockSpec((n_exp, d), lambda i: (0, 0)), pl.BlockSpec((n_exp, 1), lambda i: (0, 0))],
        out_specs=[pl.BlockSpec((SUBLANES, t), tok), pl.BlockSpec((t, LANES), tok),
                   pl.BlockSpec((1, n_exp, 1), lambda i: (i, 0, 0))],
        out_shape=[jax.ShapeDtypeStruct((n_tiles * SUBLANES, t), F32), jax.ShapeDtypeStruct((n_tok, LANES), F32),
                   jax.ShapeDtypeStruct((n_tiles, n_exp, 1), F32)],
        compiler_params=_params("arbitrary"),
        name="moe_router",
    )(x, mod, router_w.T, router_b.reshape(n_exp, 1))

    i32 = jnp.int32
    count = cnt.reshape(n_tiles, n_exp).astype(i32)
    cap = (count + SEG_ALIGN - 1) // SEG_ALIGN * SEG_ALIGN
    e_rows = jnp.sum(cap, axis=0)
    e_pad = (e_rows + MOE_BLOCK - 1) // MOE_BLOCK * MOE_BLOCK
    e_end = jnp.cumsum(e_pad)
    e_start = e_end - e_pad
    seg_global = e_start[None, :] + jnp.cumsum(cap, axis=0) - cap
    seg_local = jnp.cumsum(cap, axis=1) - cap
    fill = jnp.zeros((n_tiles, LANES - 3 * n_exp - 1), i32)
    seg = jnp.concatenate([seg_global, cap, seg_local, jnp.sum(cap, axis=1, keepdims=True), fill], axis=1)
    seg = seg.astype(i32).reshape(n_tiles, 1, LANES)
    tail = jnp.concatenate([e_start + e_rows, e_pad - e_rows, jnp.zeros((LANES - 2 * n_exp,), i32)])
    tail = tail.astype(i32).reshape(1, 1, LANES)
    max_slots = n_tok * TOP_K + n_tiles * n_exp * (SEG_ALIGN - 1) + n_exp * (MOE_BLOCK - 1)
    n_blocks = (max_slots + MOE_BLOCK - 1) // MOE_BLOCK
    n_slots = n_blocks * MOE_BLOCK
    n_used = (e_end[-1] // MOE_BLOCK).astype(i32)
    blk = jnp.minimum(jnp.arange(n_blocks, dtype=i32), n_used - 1) * MOE_BLOCK
    block_expert = jnp.minimum(jnp.sum((e_end[None, :] <= blk[:, None]).astype(i32), axis=1), n_exp - 1)

    smem_tile = lambda index_map: pl.BlockSpec((1, 1, LANES), index_map, memory_space=pltpu.SMEM)
    xs = pl.pallas_call(
        functools.partial(_dispatch_kernel, n_exp=n_exp, top_k=TOP_K),
        grid=(n_tiles,),
        in_specs=[smem_tile(lambda i: (0, 0, 0)), smem_tile(lambda i: (i, 0, 0)),
                  smem_tile(lambda i: (jnp.maximum(i - 1, 0), 0, 0)),
                  pl.BlockSpec((t, d), row), pl.BlockSpec((1, 6, d), modi), pl.BlockSpec((SUBLANES, t), tok)],
        out_specs=pl.BlockSpec(memory_space=pl.ANY),
        out_shape=jax.ShapeDtypeStruct((n_slots, dp), jnp.uint32),
        scratch_shapes=[pltpu.VMEM((2, seg_rows, dp), jnp.uint32), pltpu.VMEM((MOE_BLOCK, dp), jnp.uint32),
                        pltpu.SemaphoreType.DMA((2,)), pltpu.SemaphoreType.DMA(())],
        compiler_params=_params("arbitrary"),
        name="moe_dispatch",
    )(tail, seg, seg, x, mod, slot_t)

    bg = b_gu[:, 0::2].reshape(n_exp, 1, d_ff)
    bu = b_gu[:, 1::2].reshape(n_exp, 1, d_ff)
    used = lambda i, be, nu: (jnp.minimum(i, nu[0] - 1), 0)
    wmap = lambda i, be, nu: (be[i], 0, 0)
    wmap_all = lambda i, be, nu: (be[i] + layer * n_exp, 0, 0)
    y = pl.pallas_call(
        _expert_kernel,
        grid_spec=pltpu.PrefetchScalarGridSpec(
            num_scalar_prefetch=2,
            grid=(n_blocks,),
            in_specs=[pl.BlockSpec((MOE_BLOCK, dp), used),
                      pl.BlockSpec((1, d, d_ff), wmap_all), pl.BlockSpec((1, d, d_ff), wmap_all),
                      pl.BlockSpec((1, 1, d_ff), wmap), pl.BlockSpec((1, 1, d_ff), wmap),
                      pl.BlockSpec((1, d_ff, d), wmap_all), pl.BlockSpec((1, 1, d), wmap)],
            out_specs=pl.BlockSpec((MOE_BLOCK, dp), used)),
        out_shape=jax.ShapeDtypeStruct((n_slots, dp), jnp.uint32),
        compiler_params=_params("arbitrary"),
        name="moe_experts",
    )(block_expert, n_used.reshape(1), xs, wg, wu, bg, bu, wd, b_down.reshape(n_exp, 1, d))

    return pl.pallas_call(
        functools.partial(_combine_kernel, n_exp=n_exp, top_k=TOP_K, alpha=alpha),
        grid=(n_tiles,),
        in_specs=[smem_tile(lambda i: (i, 0, 0)), smem_tile(lambda i: (jnp.minimum(i + 1, n_tiles - 1), 0, 0)),
                  pl.BlockSpec((t, d), row), pl.BlockSpec((1, 6, d), modi),
                  pl.BlockSpec((t, LANES), tok),
                  pl.BlockSpec((1, d), lambda i: (0, 0)), pl.BlockSpec((1, d), lambda i: (0, 0)),
                  pl.BlockSpec(memory_space=pl.ANY)],
        out_specs=pl.BlockSpec((t, d), row if need_ctx else tok),
        out_shape=jax.ShapeDtypeStruct(x.shape if need_ctx else (n_tok, d), F32),
        scratch_shapes=[pltpu.VMEM((2, seg_rows, dp), jnp.uint32), pltpu.SemaphoreType.DMA((2,))],
        input_output_aliases={2: 0} if need_ctx else {},
        compiler_params=_params("arbitrary"),
        name="moe_combine_ln",
    )(seg, seg, x, mod, route, ln_g.reshape(1, d), ln_b.reshape(1, d), y)


def kernel(x, c, ctx, c_ctx, ada_w, ada_b, ln1_g, ln1_b, ln2_g, ln2_b, router_w, router_b, exp_gu_w, exp_gu_b, exp_down_w, exp_down_b, rg_w_in, rg_conv_w, rg_conv_b, rg_gate_a_w, rg_gate_a_b, rg_gate_x_w, rg_gate_x_b, rg_lambda, rg_w_out, gqa_w_qkv, gqa_b_qkv, gqa_sinks, gqa_w_o, gqa_b_o, mla_w_down, mla_q_norm, mla_kv_norm, mla_w_uq, mla_w_ukv, mla_w_o):
    batch, seq, d = x.shape
    n_ctx = ctx.shape[1]
    depth = ada_w.shape[0]
    alpha = (2 * depth) ** 0.25
    rows = _Rows(batch, n_ctx, seq, ROW_TILE)
    prows = _Rows(batch, n_ctx, seq, PROJ_TILE) if (batch * n_ctx) % PROJ_TILE == 0 and seq % PROJ_TILE == 0 else rows
    xa = jnp.concatenate([ctx.reshape(batch * n_ctx, d), x.reshape(batch * seq, d)], axis=0)
    mods = _ada_table(jnp.concatenate([c_ctx[None, :], c], axis=0), ada_w, ada_b)
    wg, wu = _split_gate_up(exp_gu_w)
    wd = exp_down_w.reshape((-1,) + exp_down_w.shape[2:]).astype(BF16)
    for i in range(depth):
        need_ctx = i < depth - 1
        kind, j = i % 3, i // 3
        mod = mods[i]
        if kind == 0:
            prm = dict(w_in=rg_w_in[j], conv_w=rg_conv_w[j], conv_b=rg_conv_b[j], gate_a_w=rg_gate_a_w[j],
                       gate_a_b=rg_gate_a_b[j], gate_x_w=rg_gate_x_w[j], gate_x_b=rg_gate_x_b[j],
                       lam=rg_lambda[j], w_out=rg_w_out[j])
            xa = _rglru_layer(rows, prows, xa, mod, prm, ln1_g[i], ln1_b[i], alpha, need_ctx)
        elif kind == 1:
            prm = dict(w_qkv=gqa_w_qkv[j], b_qkv=gqa_b_qkv[j], sinks=gqa_sinks[j], w_o=gqa_w_o[j], b_o=gqa_b_o[j])
            xa = _gqa_layer(prows, xa, mod, prm, ln1_g[i], ln1_b[i], alpha, need_ctx)
        else:
            prm = dict(w_down=mla_w_down[j], q_norm=mla_q_norm[j], kv_norm=mla_kv_norm[j], w_uq=mla_w_uq[j],
                       w_ukv=mla_w_ukv[j], w_o=mla_w_o[j])
            xa = _mla_layer(prows, xa, mod, prm, ln1_g[i], ln1_b[i], alpha, need_ctx)
        xa = _moe_layer(rows, xa, mod, router_w[i], router_b[i], wg, wu, wd, i, exp_gu_b[i], exp_down_b[i],
                        ln2_g[i], ln2_b[i], alpha, need_ctx)
    return xa.reshape(batch, seq, d)
```

```python
import functools
import math

import jax
import jax.numpy as jnp
from jax import lax
from jax.experimental import pallas as pl
from jax.experimental.pallas import tpu as pltpu

F32 = jnp.float32
BF16 = jnp.bfloat16

GRID_W = 64
LN_EPS = 1e-5
RMS_EPS = 1e-6
ROPE_THETA = 10000.0
NEG_INF = -1e30
RG_C = 8.0
SQRT_FLOOR = 1e-30
GQA_KV = 2
GQA_HD = 64
WINDOW = 128
MLA_HEADS = 16
QK_NOPE = 64
QK_ROPE = 32
V_HD = 64
TOP_K = 4
SWIGLU_LIMIT = 7.0
SWIGLU_ALPHA = 1.702
MOE_BLOCK = 1024
MOE_ROWS = 512

LANES = 128
SUBLANES = 8
BF16_ROWS = 16
ROW_TILE = 256
PROJ_TILE = 1024
ATTN_Q = 256
GQA_LOOKAHEAD = 4
MLA_TQ = 1024
MLA_TK = 256
MLA_SUB_K = 128
MLA_SUB_Q = 256
MLA_LOOKAHEAD = 12
MXU_DIM = 256
SEG_ALIGN = SUBLANES
SEG_PIECES = (8, 16, 32)
WAIT_PIECES = (8, 16, 32, 64, 128, 256)
VMEM_LIMIT = 56 * 1024 * 1024


def _params(*sem):
    return pltpu.CompilerParams(dimension_semantics=sem, vmem_limit_bytes=VMEM_LIMIT)


def _bdot(a, b):
    return jnp.dot(a.astype(BF16), b.astype(BF16), preferred_element_type=F32)


def _bdot_nt(a, b):
    return lax.dot_general(a.astype(BF16), b.astype(BF16), (((1,), (1,)), ((), ())),
                           preferred_element_type=F32)


def _layer_norm(r, g, b):
    mu = jnp.mean(r, axis=-1, keepdims=True)
    d = r - mu
    var = jnp.mean(d * d, axis=-1, keepdims=True)
    return d * lax.rsqrt(var + LN_EPS) * g + b


def _gelu_tanh(x):
    return 0.5 * x * (1.0 + jnp.tanh(math.sqrt(2.0 / math.pi) * (x + 0.044715 * (x * x * x))))


class _Rows:
    def __init__(self, batch, n_ctx, seq, tile):
        assert (batch * n_ctx) % tile == 0 and seq % tile == 0
        self.batch, self.n_ctx, self.seq, self.tile = batch, n_ctx, seq, tile
        self.ctx_tiles = batch * n_ctx // tile
        self.lat_tiles = batch * seq // tile
        self.tiles = self.ctx_tiles + self.lat_tiles
        self.lat_per_batch = seq // tile
        self.ctx_per_batch = n_ctx // tile
        self.rows = batch * (n_ctx + seq)

    def mod_index(self, i):
        return jnp.where(i < self.ctx_tiles, 0, 1 + (i - self.ctx_tiles) // self.lat_per_batch)

    def rope_index(self, i):
        return jnp.where(i < self.ctx_tiles, self.lat_per_batch, (i - self.ctx_tiles) % self.lat_per_batch)


def _ada_kernel(c_ref, w_ref, b_ref, o_ref):
    cv = c_ref[...]
    s = cv * jax.nn.sigmoid(cv)
    o_ref[0] = jnp.dot(s, w_ref[0], preferred_element_type=F32,
                       precision=lax.Precision.HIGHEST) + b_ref[0]


def _ada_table(cvec, ada_w, ada_b):
    depth, d, d6 = ada_w.shape
    n = cvec.shape[0]
    chunk = d
    out = pl.pallas_call(
        _ada_kernel,
        grid=(depth, d6 // chunk),
        in_specs=[pl.BlockSpec((n, d), lambda l, j: (0, 0)),
                  pl.BlockSpec((1, d, chunk), lambda l, j: (l, 0, j)),
                  pl.BlockSpec((1, 1, chunk), lambda l, j: (l, 0, j))],
        out_specs=pl.BlockSpec((1, n, chunk), lambda l, j: (l, 0, j)),
        out_shape=jax.ShapeDtypeStruct((depth, n, d6), F32),
        compiler_params=_params("arbitrary", "arbitrary"),
        name="ada_table",
    )(cvec, ada_w, ada_b.reshape(depth, 1, d6))
    return out.reshape(depth, n, 6, d)


def _out_ln_kernel(z_ref, w_ref, b_ref, x_ref, mod_ref, g_ref, bt_ref, o_ref, *, gate_row, alpha):
    y = _bdot(z_ref[...], w_ref[...]) + b_ref[...]
    m = mod_ref[0]
    r = alpha * x_ref[...] + m[gate_row:gate_row + 1] * y
    o_ref[...] = _layer_norm(r, g_ref[...], bt_ref[...])


def _out_ln(rows, z, w, bias, x, mod, ln_g, ln_b, *, alpha, first_tile=0):
    t, d = rows.tile, x.shape[1]
    kdim = z.shape[1]
    n_tiles = rows.tiles - first_tile
    row = lambda i: (i + first_tile, 0)
    return pl.pallas_call(
        functools.partial(_out_ln_kernel, gate_row=2, alpha=alpha),
        grid=(n_tiles,),
        in_specs=[pl.BlockSpec((t, kdim), row),
                  pl.BlockSpec((kdim, d), lambda i: (0, 0)),
                  pl.BlockSpec((1, d), lambda i: (0, 0)),
                  pl.BlockSpec((t, d), row),
                  pl.BlockSpec((1, 6, d), lambda i: (rows.mod_index(i + first_tile), 0, 0)),
                  pl.BlockSpec((1, d), lambda i: (0, 0)),
                  pl.BlockSpec((1, d), lambda i: (0, 0))],
        out_specs=pl.BlockSpec((t, d), row),
        out_shape=jax.ShapeDtypeStruct(x.shape, F32),
        input_output_aliases={3: 0},
        compiler_params=_params("arbitrary"),
        name="out_proj_ln",
    )(z, w.astype(BF16), bias.reshape(1, d), x, mod, ln_g.reshape(1, d), ln_b.reshape(1, d))


def _rg_in_kernel(x_ref, mod_ref, w_ref, gel_ref, rec_ref, *, d_rnn):
    m = mod_ref[0]
    h = (x_ref[...] * (1.0 + m[1:2]) + m[0:1]).astype(BF16)
    gel_ref[...] = _gelu_tanh(_bdot(h, w_ref[:, :d_rnn])).astype(BF16)
    rec_ref[...] = _bdot(h, w_ref[:, d_rnn:])


def _rg_in(rows, x, mod, w_in):
    t, d = rows.tile, x.shape[1]
    d_rnn = w_in.shape[1] // 2
    return pl.pallas_call(
        functools.partial(_rg_in_kernel, d_rnn=d_rnn),
        grid=(rows.tiles,),
        in_specs=[pl.BlockSpec((t, d), lambda i: (i, 0)),
                  pl.BlockSpec((1, 6, d), lambda i: (rows.mod_index(i), 0, 0)),
                  pl.BlockSpec((d, 2 * d_rnn), lambda i: (0, 0))],
        out_specs=[pl.BlockSpec((t, d_rnn), lambda i: (i, 0)),
                   pl.BlockSpec((t, d_rnn), lambda i: (i, 0))],
        out_shape=[jax.ShapeDtypeStruct((rows.rows, d_rnn), BF16),
                   jax.ShapeDtypeStruct((rows.rows, d_rnn), F32)],
        compiler_params=_params("arbitrary"),
        name="rg_in_proj",
    )(x, mod, w_in.astype(BF16))


def _rg_scan_kernel(*refs, reverse, fuse_out, n_blocks, block_w, tile, ctx_tiles, lat_tiles):
    if fuse_out:
        (x_ref, xp_ref, xn_ref, cw_ref, cb_ref, wa_ref, ba_ref, wx_ref, bx_ref, lam_ref,
         hf_ref, gel_ref, out_ref, a_scr, u_scr, h_scr, carry_scr) = refs
    else:
        (x_ref, xp_ref, xn_ref, cw_ref, cb_ref, wa_ref, ba_ref, wx_ref, bx_ref, lam_ref,
         out_ref, a_scr, u_scr, carry_scr) = refs
        h_scr = out_ref
    j = pl.program_id(1)
    is_ctx = j < ctx_tiles
    n_seq = jnp.where(is_ctx, ctx_tiles, lat_tiles)
    step = jnp.where(is_ctx, j, j - ctx_tiles)
    pos = (n_seq - 1 - step) if reverse else step
    prev_ok = (pos > 0).astype(F32)
    next_ok = (pos < n_seq - 1).astype(F32)

    @pl.when(j == 0)
    def _():
        carry_scr[...] = jnp.zeros_like(carry_scr)

    row8 = lax.broadcasted_iota(jnp.int32, (SUBLANES, block_w), 0)
    for n in range(n_blocks):
        cols = slice(n * block_w, (n + 1) * block_w)
        x = x_ref[:, cols]
        prev = xp_ref[SUBLANES - 1:SUBLANES, cols] * prev_ok
        nxt0 = xn_ref[0:1, cols] * next_ok
        nxt1 = xn_ref[1:2, cols] * next_ok
        cw = [cw_ref[k:k + 1, cols] for k in range(4)]
        xc = (cw[0] * pltpu.roll(x, 1, 0) + cw[1] * x + cw[2] * pltpu.roll(x, tile - 1, 0)
              + cw[3] * pltpu.roll(x, tile - 2, 0) + cb_ref[:, cols])
        first, last = x[0:1, :], x[tile - 1:tile, :]
        head = xc[:SUBLANES] + jnp.where(row8 == 0, cw[0] * (prev - last), 0.0)
        tail = xc[tile - SUBLANES:] + jnp.where(
            row8 == SUBLANES - 2, cw[3] * (nxt0 - first),
            jnp.where(row8 == SUBLANES - 1, cw[2] * (nxt0 - first) + cw[3] * (nxt1 - x[1:2, :]), 0.0))
        xc = jnp.concatenate([head, xc[SUBLANES:tile - SUBLANES], tail], axis=0)
        xb = xc.astype(BF16)
        tanh_r = jnp.tanh(_bdot(xb, wa_ref[n]) + ba_ref[:, cols])
        gi = 0.5 * jnp.tanh(_bdot(xb, wx_ref[n]) + bx_ref[:, cols]) + 0.5
        z = -lam_ref[:, cols]
        softplus = jnp.maximum(z, 0.0) + jnp.log1p(jnp.exp(-jnp.abs(z)))
        half_k = (-0.5 * RG_C * math.log2(math.e)) * softplus
        a = jnp.exp2(half_k * tanh_r + half_k)
        a_scr[:, cols] = a
        v = 1.0 - a * a
        u_scr[:, cols] = (v * lax.rsqrt(jnp.maximum(v, SQRT_FLOOR))) * (gi * xc)

    width = n_blocks * block_w
    sub = lax.broadcasted_iota(jnp.int32, (SUBLANES, width), 0)
    groups = tile // SUBLANES

    def body(g, carry):
        gg = (groups - 1 - g) if reverse else g
        r0 = pl.multiple_of(gg * SUBLANES, SUBLANES)
        a8 = a_scr[pl.ds(r0, SUBLANES), :]
        u8 = u_scr[pl.ds(r0, SUBLANES), :]
        for s in (1, 2, 4):
            if reverse:
                a_sh, u_sh, ok = pltpu.roll(a8, SUBLANES - s, 0), pltpu.roll(u8, SUBLANES - s, 0), sub < SUBLANES - s
            else:
                a_sh, u_sh, ok = pltpu.roll(a8, s, 0), pltpu.roll(u8, s, 0), sub >= s
            u8 = jnp.where(ok, a8 * u_sh + u8, u8)
            a8 = jnp.where(ok, a8 * a_sh, a8)
        h8 = a8 * carry + u8
        h_scr[pl.ds(r0, SUBLANES), :] = h8
        return h8[0:1, :] if reverse else h8[SUBLANES - 1:SUBLANES, :]

    carry_scr[...] = lax.fori_loop(0, groups, body, carry_scr[...], unroll=2)
    if fuse_out:
        out_ref[...] = ((hf_ref[...] + h_scr[...]) * gel_ref[...].astype(F32)).astype(BF16)


def _rg_scan(rows, rec, conv_w, conv_b, wa, ba, wx, bx, lam, *, reverse, h_fwd=None, gel=None):
    t = rows.tile
    c = rec.shape[1]
    n_blocks, block_w = wa.shape[0], wa.shape[1]
    nc, nl = rows.ctx_per_batch, rows.lat_per_batch
    halo = t // SUBLANES
    last_halo = rows.rows // SUBLANES - 1
    fuse_out = h_fwd is not None

    def tile_index(b, j):
        is_ctx = j < nc
        step = jnp.where(is_ctx, j, j - nc)
        n_seq = jnp.where(is_ctx, nc, nl)
        pos = (n_seq - 1 - step) if reverse else step
        return jnp.where(is_ctx, b * nc + pos, rows.ctx_tiles + b * nl + pos)

    cur = lambda b, j: (tile_index(b, j), 0)
    prv = lambda b, j: (jnp.maximum(tile_index(b, j) * halo - 1, 0), 0)
    nxt = lambda b, j: (jnp.minimum((tile_index(b, j) + 1) * halo, last_halo), 0)
    full2 = lambda b, j: (0, 0)
    full3 = lambda b, j: (0, 0, 0)
    in_specs = [pl.BlockSpec((t, c), cur), pl.BlockSpec((SUBLANES, c), prv), pl.BlockSpec((SUBLANES, c), nxt),
                pl.BlockSpec((4, c), full2), pl.BlockSpec((1, c), full2),
                pl.BlockSpec((n_blocks, block_w, block_w), full3), pl.BlockSpec((1, c), full2),
                pl.BlockSpec((n_blocks, block_w, block_w), full3), pl.BlockSpec((1, c), full2),
                pl.BlockSpec((1, c), full2)]
    args = [rec, rec, rec, conv_w, conv_b.reshape(1, c), (0.5 * wa).astype(BF16), 0.5 * ba.reshape(1, c),
            (0.5 * wx).astype(BF16), 0.5 * bx.reshape(1, c), lam.reshape(1, c)]
    scratch = [pltpu.VMEM((t, c), F32), pltpu.VMEM((t, c), F32)]
    if fuse_out:
        in_specs += [pl.BlockSpec((t, c), cur), pl.BlockSpec((t, c), cur)]
        args += [h_fwd, gel]
        scratch.append(pltpu.VMEM((t, c), F32))
    scratch.append(pltpu.VMEM((1, c), F32))
    return pl.pallas_call(
        functools.partial(_rg_scan_kernel, reverse=reverse, fuse_out=fuse_out, n_blocks=n_blocks,
                          block_w=block_w, tile=t, ctx_tiles=nc, lat_tiles=nl),
        grid=(rows.batch, nc + nl),
        in_specs=in_specs,
        out_specs=pl.BlockSpec((t, c), cur),
        out_shape=jax.ShapeDtypeStruct((rows.rows, c), BF16 if fuse_out else F32),
        scratch_shapes=scratch,
        compiler_params=_params("arbitrary", "arbitrary"),
        name="rg_scan_bwd" if reverse else "rg_scan_fwd",
    )(*args)


def _rglru_layer(rows, prows, x, mod, p, ln_g, ln_b, alpha, need_ctx):
    gel, rec = _rg_in(prows, x, mod, p["w_in"])
    h_fwd = _rg_scan(rows, rec, p["conv_w"], p["conv_b"], p["gate_a_w"][0], p["gate_a_b"][0],
                     p["gate_x_w"][0], p["gate_x_b"][0], p["lam"][0], reverse=False)
    z = _rg_scan(rows, rec, p["conv_w"], p["conv_b"], p["gate_a_w"][1], p["gate_a_b"][1],
                 p["gate_x_w"][1], p["gate_x_b"][1], p["lam"][1], reverse=True, h_fwd=h_fwd, gel=gel)
    d = x.shape[1]
    return _out_ln(prows, z, p["w_out"], jnp.zeros((d,), F32), x, mod, ln_g, ln_b, alpha=alpha,
                   first_tile=0 if need_ctx else prows.ctx_tiles)


def _axial_angles(seq, rot_dim):
    pos = jnp.arange(seq, dtype=jnp.int32)
    row = (pos // GRID_W).astype(F32)
    col = (pos % GRID_W).astype(F32)
    n_freq = rot_dim // 4
    inv_freq = ROPE_THETA ** (-jnp.arange(n_freq, dtype=F32) / n_freq)
    return jnp.concatenate([row[:, None] * inv_freq, col[:, None] * inv_freq], axis=-1)


def _rope_tables(rows, rot_dim, lead, trail):
    ang = _axial_angles(rows.seq, rot_dim)
    cos, sin = jnp.cos(ang), jnp.sin(ang)
    ones = lambda w: jnp.ones((rows.seq, w), F32)
    zeros = lambda w: jnp.zeros((rows.seq, w), F32)
    c = jnp.concatenate([ones(lead), cos, cos, ones(trail)], axis=-1)
    s = jnp.concatenate([zeros(lead), -sin, sin, zeros(trail)], axis=-1)
    reps = LANES // c.shape[1]
    c, s = jnp.tile(c, (1, reps)), jnp.tile(s, (1, reps))
    t = rows.tile
    c = jnp.concatenate([c.reshape(rows.lat_per_batch, t, LANES), jnp.ones((1, t, LANES), F32)], axis=0)
    s = jnp.concatenate([s.reshape(rows.lat_per_batch, t, LANES), jnp.zeros((1, t, LANES), F32)], axis=0)
    return c, s


def _rope_chunk(x, cos, sin, half):
    lane = lax.broadcasted_iota(jnp.int32, x.shape, 1)
    partner = jnp.where((lane % (2 * half)) < half, pltpu.roll(x, LANES - half, 1), pltpu.roll(x, half, 1))
    return x * cos + partner * sin


def _gqa_proj_kernel(x_ref, mod_ref, w_ref, b_ref, cos_ref, sin_ref, q_ref, kv_ref, *, q_dim, scale, half):
    m = mod_ref[0]
    h = (x_ref[...] * (1.0 + m[1:2]) + m[0:1]).astype(BF16)
    cos, sin = cos_ref[0], sin_ref[0]
    n_q = q_dim // LANES
    n_all = w_ref.shape[1] // LANES
    for c in range(n_all):
        cols = slice(c * LANES, (c + 1) * LANES)
        p = _bdot(h, w_ref[:, cols]) + b_ref[:, cols]
        is_v = c in (n_q + 1, n_q + 3)
        if not is_v:
            p = _rope_chunk(p, cos, sin, half)
        if c < n_q:
            q_ref[:, cols] = (p * scale).astype(BF16)
        else:
            kv_ref[:, (c - n_q) * LANES:(c - n_q + 1) * LANES] = p.astype(BF16)


def _gqa_attn_kernel(*refs, windowed, window, seq, n_pairs, pairs_per_kv):
    if windowed:
        q_ref, kv_ref, kvc_ref, sink_ref, o_ref = refs
    else:
        q_ref, kvc_ref, sink_ref, _, o_ref = refs
    tq = q_ref.shape[0]
    kvc = kvc_ref[...]
    if windowed:
        span = tq + 2 * window
        qs = pl.program_id(1) * tq
        ws = pl.multiple_of(jnp.clip(qs - window, 0, seq - span), LANES)
        kv = jnp.concatenate([kv_ref[pl.ds(ws, span), :], kvc], axis=0)
        n_keys = kv.shape[0]
        qpos = qs + lax.broadcasted_iota(jnp.int32, (tq, n_keys), 0)
        col = lax.broadcasted_iota(jnp.int32, (tq, n_keys), 1)
        mask = (jnp.abs(ws + col - qpos) <= window) | (col >= span)
    else:
        kv = kvc
        mask = None
    lane = lax.broadcasted_iota(jnp.int32, (kv.shape[0], LANES), 1)
    low = lane < GQA_HD
    zero = jnp.zeros((kv.shape[0], LANES), BF16)
    one_hi = (lane == GQA_HD).astype(F32).astype(BF16)
    one_lo = (lane == 0).astype(F32).astype(BF16)
    low_q = lax.broadcasted_iota(jnp.int32, (tq, LANES), 1) < GQA_HD
    k_plain, v_plain = kv[:, 0:LANES], kv[:, LANES:2 * LANES]
    k_swap, v_swap = kv[:, 2 * LANES:3 * LANES], kv[:, 3 * LANES:4 * LANES]
    k_half, v_half = {}, {}
    for g in range(GQA_KV):
        k_half[g, 0] = jnp.where(low, k_plain if g == 0 else k_swap, zero)
        k_half[g, 1] = jnp.where(low, zero, k_swap if g == 0 else k_plain)
        v_half[g, 0] = jnp.where(low, v_plain if g == 0 else v_swap, one_hi)
        v_half[g, 1] = jnp.where(low, one_lo, v_swap if g == 0 else v_plain)
    items = [(pr, hh) for pr in range(n_pairs) for hh in range(2)]

    def scores(item):
        pr, hh = item
        return _bdot_nt(q_ref[:, pr * LANES:(pr + 1) * LANES], k_half[pr // pairs_per_kv, hh])

    ahead = [scores(it) for it in items[:GQA_LOOKAHEAD]]
    acc = None
    for n, (pr, hh) in enumerate(items):
        s = ahead.pop(0)
        if n + GQA_LOOKAHEAD < len(items):
            ahead.append(scores(items[n + GQA_LOOKAHEAD]))
        head = 2 * pr + hh
        if mask is not None:
            s = jnp.where(mask, s, NEG_INF)
        sk = sink_ref[head:head + 1, 0:1]
        mx = jnp.maximum(jnp.max(s, axis=1, keepdims=True), sk)
        p = jnp.exp((s - mx).astype(BF16))
        pv = _bdot(p, v_half[pr // pairs_per_kv, hh])
        sum_lane = GQA_HD if hh == 0 else 0
        denom = pv[:, sum_lane:sum_lane + 1] + jnp.exp(sk - mx)
        part = pv * (1.0 / denom)
        if hh == 1:
            o_ref[:, pr * LANES:(pr + 1) * LANES] = jnp.where(low_q, acc, part).astype(BF16)
        acc = part


def _gqa_layer(rows, x, mod, p, ln_g, ln_b, alpha, need_ctx):
    t, d = rows.tile, x.shape[1]
    w_qkv, b_qkv = p["w_qkv"], p["b_qkv"]
    kv_dim = GQA_KV * GQA_HD
    q_dim = w_qkv.shape[1] - 2 * kv_dim
    n_heads = q_dim // GQA_HD
    assert kv_dim == LANES and GQA_KV == 2 and q_dim % LANES == 0
    swap = lambda a: jnp.concatenate([a[..., GQA_HD:], a[..., :GQA_HD]], axis=-1)
    wk, wv = w_qkv[:, q_dim:q_dim + kv_dim], w_qkv[:, q_dim + kv_dim:]
    bk, bv = b_qkv[q_dim:q_dim + kv_dim], b_qkv[q_dim + kv_dim:]
    w_ext = jnp.concatenate([w_qkv, swap(wk), swap(wv)], axis=1).astype(BF16)
    b_ext = jnp.concatenate([b_qkv, swap(bk), swap(bv)]).reshape(1, -1)
    n_ext = w_ext.shape[1]
    cos, sin = _rope_tables(rows, GQA_HD, 0, 0)
    q, kv = pl.pallas_call(
        functools.partial(_gqa_proj_kernel, q_dim=q_dim, scale=GQA_HD ** -0.5, half=GQA_HD // 2),
        grid=(rows.tiles,),
        in_specs=[pl.BlockSpec((t, d), lambda i: (i, 0)),
                  pl.BlockSpec((1, 6, d), lambda i: (rows.mod_index(i), 0, 0)),
                  pl.BlockSpec((d, n_ext), lambda i: (0, 0)),
                  pl.BlockSpec((1, n_ext), lambda i: (0, 0)),
                  pl.BlockSpec((1, t, LANES), lambda i: (rows.rope_index(i), 0, 0)),
                  pl.BlockSpec((1, t, LANES), lambda i: (rows.rope_index(i), 0, 0))],
        out_specs=[pl.BlockSpec((t, q_dim), lambda i: (i, 0)),
                   pl.BlockSpec((t, 4 * LANES), lambda i: (i, 0))],
        out_shape=[jax.ShapeDtypeStruct((rows.rows, q_dim), BF16),
                   jax.ShapeDtypeStruct((rows.rows, 4 * LANES), BF16)],
        compiler_params=_params("arbitrary"),
        name="gqa_qkv_proj",
    )(x, mod, w_ext, b_ext, cos, sin)

    sinks = jnp.broadcast_to(p["sinks"].astype(F32)[:, None], (n_heads, LANES))
    b_, s_, n_ctx = rows.batch, rows.seq, rows.n_ctx
    ctx_rows = b_ * n_ctx
    assert ctx_rows % s_ == 0 and s_ >= ATTN_Q + 2 * WINDOW
    q_blocks = s_ // ATTN_Q
    n_pairs = q_dim // LANES
    common = dict(window=WINDOW, seq=s_, n_pairs=n_pairs, pairs_per_kv=n_pairs // GQA_KV)
    o_shape = jax.ShapeDtypeStruct((rows.rows, q_dim), BF16)
    o_lat = pl.pallas_call(
        functools.partial(_gqa_attn_kernel, windowed=True, **common),
        grid=(b_, q_blocks),
        in_specs=[pl.BlockSpec((ATTN_Q, q_dim), lambda b, j: (ctx_rows // ATTN_Q + b * q_blocks + j, 0)),
                  pl.BlockSpec((s_, 4 * LANES), lambda b, j: (ctx_rows // s_ + b, 0)),
                  pl.BlockSpec((n_ctx, 4 * LANES), lambda b, j: (b, 0)),
                  pl.BlockSpec((n_heads, LANES), lambda b, j: (0, 0))],
        out_specs=pl.BlockSpec((ATTN_Q, q_dim), lambda b, j: (ctx_rows // ATTN_Q + b * q_blocks + j, 0)),
        out_shape=o_shape,
        compiler_params=_params("arbitrary", "arbitrary"),
        name="gqa_window_attn",
    )(q, kv, kv, sinks)
    if need_ctx:
        cq_blocks = n_ctx // ATTN_Q
        o = pl.pallas_call(
            functools.partial(_gqa_attn_kernel, windowed=False, **common),
            grid=(b_, cq_blocks),
            in_specs=[pl.BlockSpec((ATTN_Q, q_dim), lambda b, j: (b * cq_blocks + j, 0)),
                      pl.BlockSpec((n_ctx, 4 * LANES), lambda b, j: (b, 0)),
                      pl.BlockSpec((n_heads, LANES), lambda b, j: (0, 0)),
                      pl.BlockSpec(memory_space=pl.ANY)],
            out_specs=pl.BlockSpec((ATTN_Q, q_dim), lambda b, j: (b * cq_blocks + j, 0)),
            out_shape=o_shape,
            input_output_aliases={3: 0},
            compiler_params=_params("arbitrary", "arbitrary"),
            name="gqa_ctx_attn",
        )(q, kv, sinks, o_lat)
    else:
        o = o_lat
    return _out_ln(rows, o, p["w_o"], p["b_o"], x, mod, ln_g, ln_b, alpha=alpha,
                   first_tile=0 if need_ctx else rows.ctx_tiles)


def _mla_proj_kernel(x_ref, mod_ref, wd_ref, qn_ref, kvn_ref, wq_ref, wk_ref, wv_ref, cos_ref, sin_ref,
                     q_ref, k_ref, vt_ref, *, q_lora, kv_lora, scale, n_heads):
    m = mod_ref[0]
    h = (x_ref[...] * (1.0 + m[1:2]) + m[0:1]).astype(BF16)
    p = _bdot(h, wd_ref[...])
    cq, ckv = p[:, :q_lora], p[:, q_lora:q_lora + kv_lora]
    cos, sin = cos_ref[0], sin_ref[0]
    k_rope = _rope_chunk(p[:, q_lora + kv_lora:], cos, sin, QK_ROPE // 2)
    cq = (cq * lax.rsqrt(jnp.mean(cq * cq, axis=-1, keepdims=True) + RMS_EPS) * qn_ref[...]).astype(BF16)
    ckv = (ckv * lax.rsqrt(jnp.mean(ckv * ckv, axis=-1, keepdims=True) + RMS_EPS) * kvn_ref[...]).astype(BF16)
    for hd in range(n_heads):
        cols = slice(hd * LANES, (hd + 1) * LANES)
        qh = _rope_chunk(_bdot(cq, wq_ref[:, cols]), cos, sin, QK_ROPE // 2)
        q_ref[:, cols] = (qh * scale).astype(BF16)
        k_ref[:, cols] = (_bdot(ckv, wk_ref[:, cols]) + k_rope).astype(BF16)
    vt_ref[...] = _bdot(ckv, wv_ref[...]).T.astype(BF16)


def _mla_attn_kernel(*refs, n_heads):
    q_ref, k_ref, vt_ref = refs[:3]
    o_ref, m_scr, l_scr, acc_scr = refs[-4:]
    kt = pl.program_id(2)

    @pl.when(kt == 0)
    def _():
        m_scr[...] = jnp.full_like(m_scr, -jnp.inf)
        l_scr[...] = jnp.zeros_like(l_scr)
        acc_scr[...] = jnp.zeros_like(acc_scr)

    tq, tk = q_ref.shape[0], k_ref.shape[0]
    sub_q, sub_k = min(MLA_SUB_Q, tq), min(MLA_SUB_K, tk)
    n_kh = tk // sub_k
    ones = jnp.ones((BF16_ROWS, sub_k), BF16)
    items = [(hd, kh, qh) for hd in range(n_heads) for qh in range(tq // sub_q) for kh in range(n_kh)]

    def scores(item):
        hd, kh, qh = item
        cols = slice(hd * LANES, (hd + 1) * LANES)
        s = _bdot_nt(k_ref[kh * sub_k:(kh + 1) * sub_k, cols], q_ref[qh * sub_q:(qh + 1) * sub_q, cols])
        return s, jnp.max(s, axis=0, keepdims=True)

    ahead = [scores(it) for it in items[:MLA_LOOKAHEAD]]
    state = {}
    for n, (hd, kh, qh) in enumerate(items):
        rws = slice(hd * V_HD, (hd + 1) * V_HD)
        qcols = slice(qh * sub_q, (qh + 1) * sub_q)
        s_t, s_max = ahead.pop(0)
        if n + MLA_LOOKAHEAD < len(items):
            ahead.append(scores(items[n + MLA_LOOKAHEAD]))
        if kh == 0:
            state[hd, qh] = (m_scr[hd:hd + 1, qcols], l_scr[hd:hd + 1, qcols], acc_scr[rws, qcols])
        m_old, l_old, acc = state[hd, qh]
        m_new = jnp.maximum(m_old, s_max)
        p_t = jnp.exp2((s_t - m_new).astype(BF16))
        corr = jnp.exp2(m_old - m_new)
        vt = vt_ref[rws, kh * sub_k:(kh + 1) * sub_k]
        pv = _bdot(jnp.concatenate([vt, ones], axis=0), p_t)
        state[hd, qh] = (m_new, corr * l_old + pv[V_HD:V_HD + 1, :], acc * corr + pv[:V_HD, :])
        if kh == n_kh - 1:
            m_scr[hd:hd + 1, qcols], l_scr[hd:hd + 1, qcols], acc_scr[rws, qcols] = state.pop((hd, qh))

    @pl.when(kt == pl.num_programs(2) - 1)
    def _():
        for hd in range(n_heads):
            rws = slice(hd * V_HD, (hd + 1) * V_HD)
            acc_scr[rws, :] = acc_scr[rws, :] * (1.0 / l_scr[hd:hd + 1, :])
        o_ref[...] = acc_scr[...].T.astype(BF16)


def _mla_layer(rows, x, mod, p, ln_g, ln_b, alpha, need_ctx):
    t, d = rows.tile, x.shape[1]
    h_ = MLA_HEADS
    w_down, w_uq, w_ukv = p["w_down"], p["w_uq"], p["w_ukv"]
    q_lora = w_uq.shape[0]
    kv_lora = w_ukv.shape[0]
    qk = QK_NOPE + QK_ROPE
    assert QK_NOPE == V_HD == LANES // 2 and h_ % 2 == 0 and q_lora % LANES == 0 and kv_lora % LANES == 0
    pad = LANES - qk
    zc = lambda r, w: jnp.zeros((r, w), F32)
    wd_p = jnp.concatenate([w_down[:, :q_lora + kv_lora], zc(d, QK_NOPE), w_down[:, q_lora + kv_lora:],
                            zc(d, pad)], axis=1).astype(BF16)
    wq_p = jnp.concatenate([w_uq.reshape(q_lora, h_, qk), jnp.zeros((q_lora, h_, pad), F32)],
                           axis=-1).reshape(q_lora, h_ * LANES).astype(BF16)
    ukv = w_ukv.reshape(kv_lora, h_, QK_NOPE + V_HD)
    wk_p = jnp.concatenate([ukv[..., :QK_NOPE], jnp.zeros((kv_lora, h_, LANES - QK_NOPE), F32)],
                           axis=-1).reshape(kv_lora, h_ * LANES).astype(BF16)
    wv_p = ukv[..., QK_NOPE:].reshape(kv_lora, h_ * V_HD).astype(BF16)
    cos, sin = _rope_tables(rows, QK_ROPE, QK_NOPE, pad)
    n_down = wd_p.shape[1]
    q, k, vt = pl.pallas_call(
        functools.partial(_mla_proj_kernel, q_lora=q_lora, kv_lora=kv_lora, scale=qk ** -0.5 * math.log2(math.e),
                          n_heads=h_),
        grid=(rows.tiles,),
        in_specs=[pl.BlockSpec((t, d), lambda i: (i, 0)),
                  pl.BlockSpec((1, 6, d), lambda i: (rows.mod_index(i), 0, 0)),
                  pl.BlockSpec((d, n_down), lambda i: (0, 0)),
                  pl.BlockSpec((1, q_lora), lambda i: (0, 0)),
                  pl.BlockSpec((1, kv_lora), lambda i: (0, 0)),
                  pl.BlockSpec((q_lora, h_ * LANES), lambda i: (0, 0)),
                  pl.BlockSpec((kv_lora, h_ * LANES), lambda i: (0, 0)),
                  pl.BlockSpec((kv_lora, h_ * V_HD), lambda i: (0, 0)),
                  pl.BlockSpec((1, t, LANES), lambda i: (rows.rope_index(i), 0, 0)),
                  pl.BlockSpec((1, t, LANES), lambda i: (rows.rope_index(i), 0, 0))],
        out_specs=[pl.BlockSpec((t, h_ * LANES), lambda i: (i, 0)),
                   pl.BlockSpec((t, h_ * LANES), lambda i: (i, 0)),
                   pl.BlockSpec((h_ * V_HD, t), lambda i: (0, i))],
        out_shape=[jax.ShapeDtypeStruct((rows.rows, h_ * LANES), BF16),
                   jax.ShapeDtypeStruct((rows.rows, h_ * LANES), BF16),
                   jax.ShapeDtypeStruct((h_ * V_HD, rows.rows), BF16)],
        compiler_params=_params("arbitrary"),
        name="mla_proj",
    )(x, mod, wd_p, p["q_norm"].reshape(1, -1), p["kv_norm"].reshape(1, -1), wq_p, wk_p, wv_p, cos, sin)

    b_, s_, n_ctx = rows.batch, rows.seq, rows.n_ctx
    tk = MLA_TK
    tq, tq_ctx = min(MLA_TQ, s_), min(MLA_TQ, n_ctx)
    assert n_ctx % tk == 0 and s_ % tk == 0 and n_ctx % tq_ctx == 0 and s_ % tq == 0
    ck, lk = n_ctx // tk, s_ // tk
    ctx_kblocks = b_ * ck

    def kv_block(b, kt):
        return jnp.where(kt < ck, b * ck + kt, ctx_kblocks + b * lk + (kt - ck))

    def scratch(rows_q):
        return [pltpu.VMEM((h_, rows_q), F32), pltpu.VMEM((h_, rows_q), F32), pltpu.VMEM((h_ * V_HD, rows_q), F32)]

    o_shape = jax.ShapeDtypeStruct((rows.rows, h_ * V_HD), BF16)
    lq = s_ // tq
    ctx_qblocks = b_ * n_ctx // tq
    o_lat = pl.pallas_call(
        functools.partial(_mla_attn_kernel, n_heads=h_),
        grid=(b_, lq, ck + lk),
        in_specs=[pl.BlockSpec((tq, h_ * LANES), lambda b, i, kt: (ctx_qblocks + b * lq + i, 0)),
                  pl.BlockSpec((tk, h_ * LANES), lambda b, i, kt: (kv_block(b, kt), 0)),
                  pl.BlockSpec((h_ * V_HD, tk), lambda b, i, kt: (0, kv_block(b, kt)))],
        out_specs=pl.BlockSpec((tq, h_ * V_HD), lambda b, i, kt: (ctx_qblocks + b * lq + i, 0)),
        out_shape=o_shape,
        scratch_shapes=scratch(tq),
        compiler_params=_params("arbitrary", "arbitrary", "arbitrary"),
        name="mla_attn",
    )(q, k, vt)
    if need_ctx:
        cq = n_ctx // tq_ctx
        o = pl.pallas_call(
            functools.partial(_mla_attn_kernel, n_heads=h_),
            grid=(b_, cq, ck),
            in_specs=[pl.BlockSpec((tq_ctx, h_ * LANES), lambda b, i, kt: (b * cq + i, 0)),
                      pl.BlockSpec((tk, h_ * LANES), lambda b, i, kt: (b * ck + kt, 0)),
                      pl.BlockSpec((h_ * V_HD, tk), lambda b, i, kt: (0, b * ck + kt)),
                      pl.BlockSpec(memory_space=pl.ANY)],
            out_specs=pl.BlockSpec((tq_ctx, h_ * V_HD), lambda b, i, kt: (b * cq + i, 0)),
            out_shape=o_shape,
            scratch_shapes=scratch(tq_ctx),
            input_output_aliases={3: 0},
            compiler_params=_params("arbitrary", "arbitrary", "arbitrary"),
            name="mla_ctx_attn",
        )(q, k, vt, o_lat)
    else:
        o = o_lat
    return _out_ln(rows, o, p["w_o"], jnp.zeros((d,), F32), x, mod, ln_g, ln_b, alpha=alpha,
                   first_tile=0 if need_ctx else rows.ctx_tiles)


def _split_gu_kernel(w_ref, g_ref, u_ref):
    win = 2 * MXU_DIM
    r_i = lax.broadcasted_iota(jnp.int32, (win, MXU_DIM), 0)
    c_i = lax.broadcasted_iota(jnp.int32, (win, MXU_DIM), 1)
    pick_even = (r_i == 2 * c_i).astype(BF16)
    pick_odd = (r_i == 2 * c_i + 1).astype(BF16)
    for j in range(w_ref.shape[2] // win):
        w = w_ref[0, :, j * win:(j + 1) * win].astype(BF16)
        g_ref[0, :, j * MXU_DIM:(j + 1) * MXU_DIM] = _bdot(w, pick_even).astype(BF16)
        u_ref[0, :, j * MXU_DIM:(j + 1) * MXU_DIM] = _bdot(w, pick_odd).astype(BF16)


def _split_gate_up(w_gu):
    depth, n_exp, d, f2 = w_gu.shape
    w = w_gu.reshape(depth * n_exp, d, f2)
    tr = PROJ_TILE
    spec_out = pl.BlockSpec((1, tr, f2 // 2), lambda e, r: (e, r, 0))
    shape_out = jax.ShapeDtypeStruct((depth * n_exp, d, f2 // 2), BF16)
    return pl.pallas_call(
        _split_gu_kernel,
        grid=(depth * n_exp, d // tr),
        in_specs=[pl.BlockSpec((1, tr, f2), lambda e, r: (e, r, 0))],
        out_specs=[spec_out, spec_out],
        out_shape=[shape_out, shape_out],
        compiler_params=_params("arbitrary", "arbitrary"),
        name="moe_split_gate_up",
    )(w)


def _pack_pairs(a):
    w = a.shape[1] // 2
    r = a.astype(BF16).astype(F32)
    lo = lax.bitcast_convert_type(r[:, :w], jnp.uint32)
    hi = lax.bitcast_convert_type(r[:, w:], jnp.uint32)
    return (lo >> 16) | (hi & jnp.uint32(0xFFFF0000))


def _unpack_pairs(u):
    lo = lax.bitcast_convert_type(u << 16, F32)
    hi = lax.bitcast_convert_type(u & jnp.uint32(0xFFFF0000), F32)
    return jnp.concatenate([lo, hi], axis=1).astype(BF16)


def _seg_rows(t, n_exp):
    return t * TOP_K + n_exp * SEG_ALIGN


def _router_kernel(x_ref, mod_ref, rw_ref, rb_ref, by_token_ref, by_col_ref, cnt_ref, *, top_k):
    m = mod_ref[0]
    h = x_ref[...] * (1.0 + m[4:5]) + m[3:4]
    w_t = rw_ref[...]
    h_hi, w_hi = h.astype(BF16), w_t.astype(BF16)
    h_lo = (h - h_hi.astype(F32)).astype(BF16)
    w_lo = (w_t - w_hi.astype(F32)).astype(BF16)
    logits = _bdot_nt(w_hi, h_hi) + (_bdot_nt(w_lo, h_hi) + _bdot_nt(w_hi, h_lo)) + rb_ref[...]
    n_exp, t = logits.shape
    e_idx = lax.broadcasted_iota(jnp.int32, (n_exp, t), 0).astype(F32)
    work = logits
    sel, val = [], []
    for _ in range(top_k):
        mx = jnp.max(work, axis=0, keepdims=True)
        pick = jnp.min(jnp.where(work == mx, e_idx, float(n_exp)), axis=0, keepdims=True)
        sel.append(pick)
        val.append(mx)
        work = jnp.where(e_idx == pick, -jnp.inf, work)
    ex = [jnp.exp(v - val[0]) for v in val]
    tot = ex[0]
    for e in ex[1:]:
        tot = tot + e
    onehot = jnp.zeros((n_exp, t), F32)
    for pick in sel:
        onehot = onehot + (e_idx == pick).astype(F32)
    r_i = lax.broadcasted_iota(jnp.int32, (t, t), 0)
    c_i = lax.broadcasted_iota(jnp.int32, (t, t), 1)
    before = _bdot(onehot, (r_i < c_i).astype(BF16))
    count = jnp.sum(onehot, axis=1, keepdims=True)
    cap = jnp.floor((count + (SEG_ALIGN - 1)) * (1.0 / SEG_ALIGN)) * SEG_ALIGN
    e_r = lax.broadcasted_iota(jnp.int32, (n_exp, n_exp), 0)
    e_c = lax.broadcasted_iota(jnp.int32, (n_exp, n_exp), 1)
    seg_start = _bdot((e_c < e_r).astype(BF16), jnp.broadcast_to(cap, (n_exp, LANES)))[:, 0:1]
    base = before + seg_start
    rows_out = [jnp.sum(jnp.where(e_idx == sel[k], base, 0.0), axis=0, keepdims=True) for k in range(top_k)]
    rows_out += [ex[k] / tot for k in range(top_k)]
    r8 = lax.broadcasted_iota(jnp.int32, (SUBLANES, t), 0)
    slab = jnp.zeros((SUBLANES, t), F32)
    for k, v in enumerate(rows_out):
        slab = jnp.where(r8 == k, v, slab)
    by_token_ref[...] = slab
    by_col_ref[...] = jnp.concatenate([slab, jnp.zeros((LANES - SUBLANES, t), F32)], axis=0).T
    cnt_ref[0] = count


def _segment_copies(seg_ref, n_exp, make_copy, act):
    *small, big = SEG_PIECES
    for e in range(n_exp):
        g0, rows, l0 = seg_ref[0, 0, e], seg_ref[0, 0, n_exp + e], seg_ref[0, 0, 2 * n_exp + e]

        def whole(j, carry, g0=g0, l0=l0):
            act(make_copy(pl.multiple_of(l0 + j * big, SEG_ALIGN), pl.multiple_of(g0 + j * big, SEG_ALIGN), big))
            return carry

        lax.fori_loop(0, lax.shift_right_logical(rows, big.bit_length() - 1), whole, 0)
        for size in small:
            done = rows & (-2 * size)

            @pl.when((rows & size) != 0)
            def _(g0=g0, l0=l0, done=done, size=size):
                act(make_copy(pl.multiple_of(l0 + done, SEG_ALIGN), pl.multiple_of(g0 + done, SEG_ALIGN), size))


def _segment_waits(seg_ref, n_exp, make_copy):
    total = seg_ref[0, 0, 3 * n_exp]
    *small, big = WAIT_PIECES

    def whole(j, carry):
        make_copy(0, 0, big).wait()
        return carry

    lax.fori_loop(0, lax.shift_right_logical(total, big.bit_length() - 1), whole, 0)
    for size in small:
        @pl.when((total & size) != 0)
        def _(size=size):
            make_copy(0, 0, size).wait()


def _dispatch_kernel(tail_ref, seg_ref, seg_prev_ref, x_ref, mod_ref, slot_t_ref, xs_ref, buf, zbuf, sem, zsem, *,
                     n_exp, top_k):
    i = pl.program_id(0)
    cur = i % 2

    def to_slots(buf_slot, sem_slot):
        return lambda l, g, rows: pltpu.make_async_copy(buf.at[buf_slot, pl.ds(l, rows)], xs_ref.at[pl.ds(g, rows)],
                                                        sem.at[sem_slot])

    @pl.when(i == 0)
    def _():
        zbuf[...] = jnp.zeros_like(zbuf)
        zero_fill = lambda l, g, rows: pltpu.make_async_copy(zbuf.at[pl.ds(l, rows)], xs_ref.at[pl.ds(g, rows)], zsem)
        _segment_copies(tail_ref, n_exp, zero_fill, lambda c: c.start())
        _segment_copies(tail_ref, n_exp, zero_fill, lambda c: c.wait())

    m = mod_ref[0]
    h = (x_ref[...] * (1.0 + m[4:5]) + m[3:4]).astype(BF16)
    n_rows, t = buf.shape[1], x_ref.shape[0]
    row = lax.broadcasted_iota(jnp.int32, (n_rows, t), 0)
    slot_t = slot_t_ref[...].astype(jnp.int32)
    pick = row == slot_t[0:1, :]
    for k in range(1, top_k):
        pick = pick | (row == slot_t[k:k + 1, :])
    buf[cur] = _pack_pairs(_bdot(pick.astype(BF16), h))
    _segment_copies(seg_ref, n_exp, to_slots(cur, cur), lambda c: c.start())

    @pl.when(i > 0)
    def _():
        _segment_waits(seg_prev_ref, n_exp, to_slots(1 - cur, 1 - cur))

    @pl.when(i == pl.num_programs(0) - 1)
    def _():
        _segment_waits(seg_ref, n_exp, to_slots(cur, cur))


def _expert_kernel(be_ref, nu_ref, x_ref, wg_ref, wu_ref, bg_ref, bu_ref, wd_ref, bd_ref, y_ref):
    del be_ref

    @pl.when(pl.program_id(0) < nu_ref[0])
    def _():
        def gate_up(part):
            xb = _unpack_pairs(x_ref[part * MOE_ROWS:(part + 1) * MOE_ROWS, :])
            return _bdot(xb, wg_ref[0]), _bdot(xb, wu_ref[0])

        n_parts = x_ref.shape[0] // MOE_ROWS
        nxt = gate_up(0)
        for part in range(n_parts):
            g, u = nxt
            if part + 1 < n_parts:
                nxt = gate_up(part + 1)
            g = jnp.minimum(g + bg_ref[0], SWIGLU_LIMIT)
            u = jnp.clip(u + bu_ref[0], -SWIGLU_LIMIT, SWIGLU_LIMIT)
            act = (u + 1.0) * (g * jax.nn.sigmoid(SWIGLU_ALPHA * g))
            y_ref[part * MOE_ROWS:(part + 1) * MOE_ROWS, :] = _pack_pairs(_bdot(act, wd_ref[0]) + bd_ref[0])


def _combine_kernel(seg_ref, seg_next_ref, x_ref, mod_ref, route_ref, g_ref, bt_ref, y_ref, o_ref,
                    ybuf, sem, *, n_exp, top_k, alpha):
    i = pl.program_id(0)
    cur = i % 2

    def from_slots(buf_slot):
        return lambda l, g, rows: pltpu.make_async_copy(y_ref.at[pl.ds(g, rows)], ybuf.at[buf_slot, pl.ds(l, rows)],
                                                        sem.at[buf_slot])

    @pl.when(i == 0)
    def _():
        ybuf[...] = jnp.zeros_like(ybuf)
        _segment_copies(seg_ref, n_exp, from_slots(0), lambda c: c.start())

    @pl.when(i < pl.num_programs(0) - 1)
    def _():
        _segment_copies(seg_next_ref, n_exp, from_slots(1 - cur), lambda c: c.start())

    _segment_waits(seg_ref, n_exp, from_slots(cur))
    t, n_rows = x_ref.shape[0], ybuf.shape[1]
    lane = lax.broadcasted_iota(jnp.int32, (t, n_rows), 1)
    route = route_ref[...]
    slot = route[:, :top_k].astype(jnp.int32)
    weights = jnp.zeros((t, n_rows), F32)
    for k in range(top_k):
        weights = jnp.where(lane == slot[:, k:k + 1], route[:, top_k + k:top_k + k + 1], weights)
    acc = _bdot(weights, _unpack_pairs(ybuf[cur]))
    m = mod_ref[0]
    r = alpha * x_ref[...] + m[5:6] * acc
    o_ref[...] = _layer_norm(r, g_ref[...], bt_ref[...])


def _moe_layer(rows, x, mod, router_w, router_b, wg, wu, wd, layer, b_gu, b_down, ln_g, ln_b, alpha, need_ctx):
    t, d = rows.tile, x.shape[1]
    n_exp = router_w.shape[1]
    d_ff = wd.shape[1]
    first = 0 if need_ctx else rows.ctx_tiles
    n_tiles = rows.tiles - first
    n_tok = n_tiles * t
    row = lambda i: (i + first, 0)
    modi = lambda i: (rows.mod_index(i + first), 0, 0)
    tok = lambda i: (i, 0)

    seg_rows = _seg_rows(t, n_exp)
    dp = d // 2
    assert 4 * n_exp <= LANES and t % SEG_PIECES[-1] == 0 and MOE_BLOCK % SEG_PIECES[-1] == 0
    assert 2 * TOP_K <= SUBLANES
    slot_t, route, cnt = pl.pallas_call(
        functools.partial(_router_kernel, top_k=TOP_K),
        grid=(n_tiles,),
        in_specs=[pl.BlockSpec((t, d), row), pl.BlockSpec((1, 6, d), modi),
                  pl.BlockSpec((n_exp, d), lambda i: (0, 0)), pl.BlockSpec((n_exp, 1), lambda i: (0, 0))],
        out_specs=[pl.BlockSpec((SUBLANES, t), tok), pl.BlockSpec((t, LANES), tok),
                   pl.BlockSpec((1, n_exp, 1), lambda i: (i, 0, 0))],
        out_shape=[jax.ShapeDtypeStruct((n_tiles * SUBLANES, t), F32), jax.ShapeDtypeStruct((n_tok, LANES), F32),
                   jax.ShapeDtypeStruct((n_tiles, n_exp, 1), F32)],
        compiler_params=_params("arbitrary"),
        name="moe_router",
    )(x, mod, router_w.T, router_b.reshape(n_exp, 1))

    i32 = jnp.int32
    count = cnt.reshape(n_tiles, n_exp).astype(i32)
    cap = (count + SEG_ALIGN - 1) // SEG_ALIGN * SEG_ALIGN
    e_rows = jnp.sum(cap, axis=0)
    e_pad = (e_rows + MOE_BLOCK - 1) // MOE_BLOCK * MOE_BLOCK
    e_end = jnp.cumsum(e_pad)
    e_start = e_end - e_pad
    seg_global = e_start[None, :] + jnp.cumsum(cap, axis=0) - cap
    seg_local = jnp.cumsum(cap, axis=1) - cap
    fill = jnp.zeros((n_tiles, LANES - 3 * n_exp - 1), i32)
    seg = jnp.concatenate([seg_global, cap, seg_local, jnp.sum(cap, axis=1, keepdims=True), fill], axis=1)
    seg = seg.astype(i32).reshape(n_tiles, 1, LANES)
    tail = jnp.concatenate([e_start + e_rows, e_pad - e_rows, jnp.zeros((LANES - 2 * n_exp,), i32)])
    tail = tail.astype(i32).reshape(1, 1, LANES)
    max_slots = n_tok * TOP_K + n_tiles * n_exp * (SEG_ALIGN - 1) + n_exp * (MOE_BLOCK - 1)
    n_blocks = (max_slots + MOE_BLOCK - 1) // MOE_BLOCK
    n_slots = n_blocks * MOE_BLOCK
    n_used = (e_end[-1] // MOE_BLOCK).astype(i32)
    blk = jnp.minimum(jnp.arange(n_blocks, dtype=i32), n_used - 1) * MOE_BLOCK
    block_expert = jnp.minimum(jnp.sum((e_end[None, :] <= blk[:, None]).astype(i32), axis=1), n_exp - 1)

    smem_tile = lambda index_map: pl.BlockSpec((1, 1, LANES), index_map, memory_space=pltpu.SMEM)
    xs = pl.pallas_call(
        functools.partial(_dispatch_kernel, n_exp=n_exp, top_k=TOP_K),
        grid=(n_tiles,),
        in_specs=[smem_tile(lambda i: (0, 0, 0)), smem_tile(lambda i: (i, 0, 0)),
                  smem_tile(lambda i: (jnp.maximum(i - 1, 0), 0, 0)),
                  pl.BlockSpec((t, d), row), pl.BlockSpec((1, 6, d), modi), pl.BlockSpec((SUBLANES, t), tok)],
        out_specs=pl.BlockSpec(memory_space=pl.ANY),
        out_shape=jax.ShapeDtypeStruct((n_slots, dp), jnp.uint32),
        scratch_shapes=[pltpu.VMEM((2, seg_rows, dp), jnp.uint32), pltpu.VMEM((MOE_BLOCK, dp), jnp.uint32),
                        pltpu.SemaphoreType.DMA((2,)), pltpu.SemaphoreType.DMA(())],
        compiler_params=_params("arbitrary"),
        name="moe_dispatch",
    )(tail, seg, seg, x, mod, slot_t)

    bg = b_gu[:, 0::2].reshape(n_exp, 1, d_ff)
    bu = b_gu[:, 1::2].reshape(n_exp, 1, d_ff)
    used = lambda i, be, nu: (jnp.minimum(i, nu[0] - 1), 0)
    wmap = lambda i, be, nu: (be[i], 0, 0)
    wmap_all = lambda i, be, nu: (be[i] + layer * n_exp, 0, 0)
    y = pl.pallas_call(
        _expert_kernel,
        grid_spec=pltpu.PrefetchScalarGridSpec(
            num_scalar_prefetch=2,
            grid=(n_blocks,),
            in_specs=[pl.BlockSpec((MOE_BLOCK, dp), used),
                      pl.BlockSpec((1, d, d_ff), wmap_all), pl.BlockSpec((1, d, d_ff), wmap_all),
                      pl.BlockSpec((1, 1, d_ff), wmap), pl.BlockSpec((1, 1, d_ff), wmap),
                      pl.BlockSpec((1, d_ff, d), wmap_all), pl.BlockSpec((1, 1, d), wmap)],
            out_specs=pl.BlockSpec((MOE_BLOCK, dp), used)),
        out_shape=jax.ShapeDtypeStruct((n_slots, dp), jnp.uint32),
        compiler_params=_params("arbitrary"),
        name="moe_experts",
    )(block_expert, n_used.reshape(1), xs, wg, wu, bg, bu, wd, b_down.reshape(n_exp, 1, d))

    return pl.pallas_call(
        functools.partial(_combine_kernel, n_exp=n_exp, top_k=TOP_K, alpha=alpha),
        grid=(n_tiles,),
        in_specs=[smem_tile(lambda i: (i, 0, 0)), smem_tile(lambda i: (jnp.minimum(i + 1, n_tiles - 1), 0, 0)),
                  pl.BlockSpec((t, d), row), pl.BlockSpec((1, 6, d), modi),
                  pl.BlockSpec((t, LANES), tok),
                  pl.BlockSpec((1, d), lambda i: (0, 0)), pl.BlockSpec((1, d), lambda i: (0, 0)),
                  pl.BlockSpec(memory_space=pl.ANY)],
        out_specs=pl.BlockSpec((t, d), row if need_ctx else tok),
        out_shape=jax.ShapeDtypeStruct(x.shape if need_ctx else (n_tok, d), F32),
        scratch_shapes=[pltpu.VMEM((2, seg_rows, dp), jnp.uint32), pltpu.SemaphoreType.DMA((2,))],
        input_output_aliases={2: 0} if need_ctx else {},
        compiler_params=_params("arbitrary"),
        name="moe_combine_ln",
    )(seg, seg, x, mod, route, ln_g.reshape(1, d), ln_b.reshape(1, d), y)


def kernel(x, c, ctx, c_ctx, ada_w, ada_b, ln1_g, ln1_b, ln2_g, ln2_b, router_w, router_b, exp_gu_w, exp_gu_b, exp_down_w, exp_down_b, rg_w_in, rg_conv_w, rg_conv_b, rg_gate_a_w, rg_gate_a_b, rg_gate_x_w, rg_gate_x_b, rg_lambda, rg_w_out, gqa_w_qkv, gqa_b_qkv, gqa_sinks, gqa_w_o, gqa_b_o, mla_w_down, mla_q_norm, mla_kv_norm, mla_w_uq, mla_w_ukv, mla_w_o):
    batch, seq, d = x.shape
    n_ctx = ctx.shape[1]
    depth = ada_w.shape[0]
    alpha = (2 * depth) ** 0.25
    rows = _Rows(batch, n_ctx, seq, ROW_TILE)
    prows = _Rows(batch, n_ctx, seq, PROJ_TILE) if (batch * n_ctx) % PROJ_TILE == 0 and seq % PROJ_TILE == 0 else rows
    xa = jnp.concatenate([ctx.reshape(batch * n_ctx, d), x.reshape(batch * seq, d)], axis=0)
    mods = _ada_table(jnp.concatenate([c_ctx[None, :], c], axis=0), ada_w, ada_b)
    wg, wu = _split_gate_up(exp_gu_w)
    wd = exp_down_w.reshape((-1,) + exp_down_w.shape[2:]).astype(BF16)
    for i in range(depth):
        need_ctx = i < depth - 1
        kind, j = i % 3, i // 3
        mod = mods[i]
        if kind == 0:
            prm = dict(w_in=rg_w_in[j], conv_w=rg_conv_w[j], conv_b=rg_conv_b[j], gate_a_w=rg_gate_a_w[j],
                       gate_a_b=rg_gate_a_b[j], gate_x_w=rg_gate_x_w[j], gate_x_b=rg_gate_x_b[j],
                       lam=rg_lambda[j], w_out=rg_w_out[j])
            xa = _rglru_layer(rows, prows, xa, mod, prm, ln1_g[i], ln1_b[i], alpha, need_ctx)
        elif kind == 1:
            prm = dict(w_qkv=gqa_w_qkv[j], b_qkv=gqa_b_qkv[j], sinks=gqa_sinks[j], w_o=gqa_w_o[j], b_o=gqa_b_o[j])
            xa = _gqa_layer(prows, xa, mod, prm, ln1_g[i], ln1_b[i], alpha, need_ctx)
        else:
            prm = dict(w_down=mla_w_down[j], q_norm=mla_q_norm[j], kv_norm=mla_kv_norm[j], w_uq=mla_w_uq[j],
                       w_ukv=mla_w_ukv[j], w_o=mla_w_o[j])
            xa = _mla_layer(prows, xa, mod, prm, ln1_g[i], ln1_b[i], alpha, need_ctx)
        xa = _moe_layer(rows, xa, mod, router_w[i], router_b[i], wg, wu, wd, i, exp_gu_b[i], exp_down_b[i],
                        ln2_g[i], ln2_b[i], alpha, need_ctx)
    return xa.reshape(batch, seq, d)
```

```python
import functools
import math

import jax
import jax.numpy as jnp
from jax import lax
from jax.experimental import pallas as pl
from jax.experimental.pallas import tpu as pltpu

F32 = jnp.float32
BF16 = jnp.bfloat16

GRID_W = 64
LN_EPS = 1e-5
RMS_EPS = 1e-6
ROPE_THETA = 10000.0
NEG_INF = -1e30
RG_C = 8.0
SQRT_FLOOR = 1e-30
GQA_KV = 2
GQA_HD = 64
WINDOW = 128
MLA_HEADS = 16
QK_NOPE = 64
QK_ROPE = 32
V_HD = 64
TOP_K = 4
SWIGLU_LIMIT = 7.0
SWIGLU_ALPHA = 1.702
MOE_BLOCK = 1024
MOE_ROWS = 512

LANES = 128
SUBLANES = 8
BF16_ROWS = 16
ROW_TILE = 256
PROJ_TILE = 1024
ATTN_Q = 256
GQA_LOOKAHEAD = 4
MLA_TQ = 1024
MLA_TK = 256
MLA_SUB_K = 128
MLA_SUB_Q = 256
MLA_LOOKAHEAD = 12
MXU_DIM = 256
SEG_ALIGN = SUBLANES
SEG_PIECES = (8, 16, 32)
WAIT_PIECES = (8, 16, 32, 64, 128, 256)
VMEM_LIMIT = 56 * 1024 * 1024


def _params(*sem):
    return pltpu.CompilerParams(dimension_semantics=sem, vmem_limit_bytes=VMEM_LIMIT)


def _bdot(a, b):
    return jnp.dot(a.astype(BF16), b.astype(BF16), preferred_element_type=F32)


def _bdot_nt(a, b):
    return lax.dot_general(a.astype(BF16), b.astype(BF16), (((1,), (1,)), ((), ())),
                           preferred_element_type=F32)


def _layer_norm(r, g, b):
    mu = jnp.mean(r, axis=-1, keepdims=True)
    d = r - mu
    var = jnp.mean(d * d, axis=-1, keepdims=True)
    return d * lax.rsqrt(var + LN_EPS) * g + b


def _gelu_tanh(x):
    return 0.5 * x * (1.0 + jnp.tanh(math.sqrt(2.0 / math.pi) * (x + 0.044715 * (x * x * x))))


class _Rows:
    def __init__(self, batch, n_ctx, seq, tile):
        assert (batch * n_ctx) % tile == 0 and seq % tile == 0
        self.batch, self.n_ctx, self.seq, self.tile = batch, n_ctx, seq, tile
        self.ctx_tiles = batch * n_ctx // tile
        self.lat_tiles = batch * seq // tile
        self.tiles = self.ctx_tiles + self.lat_tiles
        self.lat_per_batch = seq // tile
        self.ctx_per_batch = n_ctx // tile
        self.rows = batch * (n_ctx + seq)

    def mod_index(self, i):
        return jnp.where(i < self.ctx_tiles, 0, 1 + (i - self.ctx_tiles) // self.lat_per_batch)

    def rope_index(self, i):
        return jnp.where(i < self.ctx_tiles, self.lat_per_batch, (i - self.ctx_tiles) % self.lat_per_batch)


def _ada_kernel(c_ref, w_ref, b_ref, o_ref):
    cv = c_ref[...]
    s = cv * jax.nn.sigmoid(cv)
    o_ref[0] = jnp.dot(s, w_ref[0], preferred_element_type=F32,
                       precision=lax.Precision.HIGHEST) + b_ref[0]


def _ada_table(cvec, ada_w, ada_b):
    depth, d, d6 = ada_w.shape
    n = cvec.shape[0]
    chunk = d
    out = pl.pallas_call(
        _ada_kernel,
        grid=(depth, d6 // chunk),
        in_specs=[pl.BlockSpec((n, d), lambda l, j: (0, 0)),
                  pl.BlockSpec((1, d, chunk), lambda l, j: (l, 0, j)),
                  pl.BlockSpec((1, 1, chunk), lambda l, j: (l, 0, j))],
        out_specs=pl.BlockSpec((1, n, chunk), lambda l, j: (l, 0, j)),
        out_shape=jax.ShapeDtypeStruct((depth, n, d6), F32),
        compiler_params=_params("arbitrary", "arbitrary"),
        name="ada_table",
    )(cvec, ada_w, ada_b.reshape(depth, 1, d6))
    return out.reshape(depth, n, 6, d)


def _out_ln_kernel(z_ref, w_ref, b_ref, x_ref, mod_ref, g_ref, bt_ref, o_ref, *, gate_row, alpha):
    y = _bdot(z_ref[...], w_ref[...]) + b_ref[...]
    m = mod_ref[0]
    r = alpha * x_ref[...] + m[gate_row:gate_row + 1] * y
    o_ref[...] = _layer_norm(r, g_ref[...], bt_ref[...])


def _out_ln(rows, z, w, bias, x, mod, ln_g, ln_b, *, alpha, first_tile=0):
    t, d = rows.tile, x.shape[1]
    kdim = z.shape[1]
    n_tiles = rows.tiles - first_tile
    row = lambda i: (i + first_tile, 0)
    return pl.pallas_call(
        functools.partial(_out_ln_kernel, gate_row=2, alpha=alpha),
        grid=(n_tiles,),
        in_specs=[pl.BlockSpec((t, kdim), row),
                  pl.BlockSpec((kdim, d), lambda i: (0, 0)),
                  pl.BlockSpec((1, d), lambda i: (0, 0)),
                  pl.BlockSpec((t, d), row),
                  pl.BlockSpec((1, 6, d), lambda i: (rows.mod_index(i + first_tile), 0, 0)),
                  pl.BlockSpec((1, d), lambda i: (0, 0)),
                  pl.BlockSpec((1, d), lambda i: (0, 0))],
        out_specs=pl.BlockSpec((t, d), row),
        out_shape=jax.ShapeDtypeStruct(x.shape, F32),
        input_output_aliases={3: 0},
        compiler_params=_params("arbitrary"),
        name="out_proj_ln",
    )(z, w.astype(BF16), bias.reshape(1, d), x, mod, ln_g.reshape(1, d), ln_b.reshape(1, d))


def _rg_in_kernel(x_ref, mod_ref, w_ref, gel_ref, rec_ref, *, d_rnn):
    m = mod_ref[0]
    h = (x_ref[...] * (1.0 + m[1:2]) + m[0:1]).astype(BF16)
    gel_ref[...] = _gelu_tanh(_bdot(h, w_ref[:, :d_rnn])).astype(BF16)
    rec_ref[...] = _bdot(h, w_ref[:, d_rnn:])


def _rg_in(rows, x, mod, w_in):
    t, d = rows.tile, x.shape[1]
    d_rnn = w_in.shape[1] // 2
    return pl.pallas_call(
        functools.partial(_rg_in_kernel, d_rnn=d_rnn),
        grid=(rows.tiles,),
        in_specs=[pl.BlockSpec((t, d), lambda i: (i, 0)),
                  pl.BlockSpec((1, 6, d), lambda i: (rows.mod_index(i), 0, 0)),
                  pl.BlockSpec((d, 2 * d_rnn), lambda i: (0, 0))],
        out_specs=[pl.BlockSpec((t, d_rnn), lambda i: (i, 0)),
                   pl.BlockSpec((t, d_rnn), lambda i: (i, 0))],
        out_shape=[jax.ShapeDtypeStruct((rows.rows, d_rnn), BF16),
                   jax.ShapeDtypeStruct((rows.rows, d_rnn), F32)],
        compiler_params=_params("arbitrary"),
        name="rg_in_proj",
    )(x, mod, w_in.astype(BF16))


def _rg_scan_kernel(*refs, reverse, fuse_out, n_blocks, block_w, tile, ctx_tiles, lat_tiles):
    if fuse_out:
        (x_ref, xp_ref, xn_ref, cw_ref, cb_ref, wa_ref, ba_ref, wx_ref, bx_ref, lam_ref,
         hf_ref, gel_ref, out_ref, a_scr, u_scr, h_scr, carry_scr) = refs
    else:
        (x_ref, xp_ref, xn_ref, cw_ref, cb_ref, wa_ref, ba_ref, wx_ref, bx_ref, lam_ref,
         out_ref, a_scr, u_scr, carry_scr) = refs
        h_scr = out_ref
    j = pl.program_id(1)
    is_ctx = j < ctx_tiles
    n_seq = jnp.where(is_ctx, ctx_tiles, lat_tiles)
    step = jnp.where(is_ctx, j, j - ctx_tiles)
    pos = (n_seq - 1 - step) if reverse else step
    prev_ok = (pos > 0).astype(F32)
    next_ok = (pos < n_seq - 1).astype(F32)

    @pl.when(j == 0)
    def _():
        carry_scr[...] = jnp.zeros_like(carry_scr)

    row8 = lax.broadcasted_iota(jnp.int32, (SUBLANES, block_w), 0)
    for n in range(n_blocks):
        cols = slice(n * block_w, (n + 1) * block_w)
        x = x_ref[:, cols]
        prev = xp_ref[SUBLANES - 1:SUBLANES, cols] * prev_ok
        nxt0 = xn_ref[0:1, cols] * next_ok
        nxt1 = xn_ref[1:2, cols] * next_ok
        cw = [cw_ref[k:k + 1, cols] for k in range(4)]
        xc = (cw[0] * pltpu.roll(x, 1, 0) + cw[1] * x + cw[2] * pltpu.roll(x, tile - 1, 0)
              + cw[3] * pltpu.roll(x, tile - 2, 0) + cb_ref[:, cols])
        first, last = x[0:1, :], x[tile - 1:tile, :]
        head = xc[:SUBLANES] + jnp.where(row8 == 0, cw[0] * (prev - last), 0.0)
        tail = xc[tile - SUBLANES:] + jnp.where(
            row8 == SUBLANES - 2, cw[3] * (nxt0 - first),
            jnp.where(row8 == SUBLANES - 1, cw[2] * (nxt0 - first) + cw[3] * (nxt1 - x[1:2, :]), 0.0))
        xc = jnp.concatenate([head, xc[SUBLANES:tile - SUBLANES], tail], axis=0)
        xb = xc.astype(BF16)
        tanh_r = jnp.tanh(_bdot(xb, wa_ref[n]) + ba_ref[:, cols])
        gi = 0.5 * jnp.tanh(_bdot(xb, wx_ref[n]) + bx_ref[:, cols]) + 0.5
        z = -lam_ref[:, cols]
        softplus = jnp.maximum(z, 0.0) + jnp.log1p(jnp.exp(-jnp.abs(z)))
        half_k = (-0.5 * RG_C * math.log2(math.e)) * softplus
        a = jnp.exp2(half_k * tanh_r + half_k)
        a_scr[:, cols] = a
        v = 1.0 - a * a
        u_scr[:, cols] = (v * lax.rsqrt(jnp.maximum(v, SQRT_FLOOR))) * (gi * xc)

    width = n_blocks * block_w
    sub = lax.broadcasted_iota(jnp.int32, (SUBLANES, width), 0)
    groups = tile // SUBLANES

    def body(g, carry):
        gg = (groups - 1 - g) if reverse else g
        r0 = pl.multiple_of(gg * SUBLANES, SUBLANES)
        a8 = a_scr[pl.ds(r0, SUBLANES), :]
        u8 = u_scr[pl.ds(r0, SUBLANES), :]
        for s in (1, 2, 4):
            if reverse:
                a_sh, u_sh, ok = pltpu.roll(a8, SUBLANES - s, 0), pltpu.roll(u8, SUBLANES - s, 0), sub < SUBLANES - s
            else:
                a_sh, u_sh, ok = pltpu.roll(a8, s, 0), pltpu.roll(u8, s, 0), sub >= s
            u8 = jnp.where(ok, a8 * u_sh + u8, u8)
            a8 = jnp.where(ok, a8 * a_sh, a8)
        h8 = a8 * carry + u8
        h_scr[pl.ds(r0, SUBLANES), :] = h8
        return h8[0:1, :] if reverse else h8[SUBLANES - 1:SUBLANES, :]

    carry_scr[...] = lax.fori_loop(0, groups, body, carry_scr[...], unroll=2)
    if fuse_out:
        out_ref[...] = ((hf_ref[...] + h_scr[...]) * gel_ref[...].astype(F32)).astype(BF16)


def _rg_scan(rows, rec, conv_w, conv_b, wa, ba, wx, bx, lam, *, reverse, h_fwd=None, gel=None):
    t = rows.tile
    c = rec.shape[1]
    n_blocks, block_w = wa.shape[0], wa.shape[1]
    nc, nl = rows.ctx_per_batch, rows.lat_per_batch
    halo = t // SUBLANES
    last_halo = rows.rows // SUBLANES - 1
    fuse_out = h_fwd is not None

    def tile_index(b, j):
        is_ctx = j < nc
        step = jnp.where(is_ctx, j, j - nc)
        n_seq = jnp.where(is_ctx, nc, nl)
        pos = (n_seq - 1 - step) if reverse else step
        return jnp.where(is_ctx, b * nc + pos, rows.ctx_tiles + b * nl + pos)

    cur = lambda b, j: (tile_index(b, j), 0)
    prv = lambda b, j: (jnp.maximum(tile_index(b, j) * halo - 1, 0), 0)
    nxt = lambda b, j: (jnp.minimum((tile_index(b, j) + 1) * halo, last_halo), 0)
    full2 = lambda b, j: (0, 0)
    full3 = lambda b, j: (0, 0, 0)
    in_specs = [pl.BlockSpec((t, c), cur), pl.BlockSpec((SUBLANES, c), prv), pl.BlockSpec((SUBLANES, c), nxt),
                pl.BlockSpec((4, c), full2), pl.BlockSpec((1, c), full2),
                pl.BlockSpec((n_blocks, block_w, block_w), full3), pl.BlockSpec((1, c), full2),
                pl.BlockSpec((n_blocks, block_w, block_w), full3), pl.BlockSpec((1, c), full2),
                pl.BlockSpec((1, c), full2)]
    args = [rec, rec, rec, conv_w, conv_b.reshape(1, c), (0.5 * wa).astype(BF16), 0.5 * ba.reshape(1, c),
            (0.5 * wx).astype(BF16), 0.5 * bx.reshape(1, c), lam.reshape(1, c)]
    scratch = [pltpu.VMEM((t, c), F32), pltpu.VMEM((t, c), F32)]
    if fuse_out:
        in_specs += [pl.BlockSpec((t, c), cur), pl.BlockSpec((t, c), cur)]
        args += [h_fwd, gel]
        scratch.append(pltpu.VMEM((t, c), F32))
    scratch.append(pltpu.VMEM((1, c), F32))
    return pl.pallas_call(
        functools.partial(_rg_scan_kernel, reverse=reverse, fuse_out=fuse_out, n_blocks=n_blocks,
                          block_w=block_w, tile=t, ctx_tiles=nc, lat_tiles=nl),
        grid=(rows.batch, nc + nl),
        in_specs=in_specs,
        out_specs=pl.BlockSpec((t, c), cur),
        out_shape=jax.ShapeDtypeStruct((rows.rows, c), BF16 if fuse_out else F32),
        scratch_shapes=scratch,
        compiler_params=_params("arbitrary", "arbitrary"),
        name="rg_scan_bwd" if reverse else "rg_scan_fwd",
    )(*args)


def _rglru_layer(rows, prows, x, mod, p, ln_g, ln_b, alpha, need_ctx):
    gel, rec = _rg_in(prows, x, mod, p["w_in"])
    h_fwd = _rg_scan(rows, rec, p["conv_w"], p["conv_b"], p["gate_a_w"][0], p["gate_a_b"][0],
                     p["gate_x_w"][0], p["gate_x_b"][0], p["lam"][0], reverse=False)
    z = _rg_scan(rows, rec, p["conv_w"], p["conv_b"], p["gate_a_w"][1], p["gate_a_b"][1],
                 p["gate_x_w"][1], p["gate_x_b"][1], p["lam"][1], reverse=True, h_fwd=h_fwd, gel=gel)
    d = x.shape[1]
    return _out_ln(prows, z, p["w_out"], jnp.zeros((d,), F32), x, mod, ln_g, ln_b, alpha=alpha,
                   first_tile=0 if need_ctx else prows.ctx_tiles)


def _axial_angles(seq, rot_dim):
    pos = jnp.arange(seq, dtype=jnp.int32)
    row = (pos // GRID_W).astype(F32)
    col = (pos % GRID_W).astype(F32)
    n_freq = rot_dim // 4
    inv_freq = ROPE_THETA ** (-jnp.arange(n_freq, dtype=F32) / n_freq)
    return jnp.concatenate([row[:, None] * inv_freq, col[:, None] * inv_freq], axis=-1)


def _rope_tables(rows, rot_dim, lead, trail):
    ang = _axial_angles(rows.seq, rot_dim)
    cos, sin = jnp.cos(ang), jnp.sin(ang)
    ones = lambda w: jnp.ones((rows.seq, w), F32)
    zeros = lambda w: jnp.zeros((rows.seq, w), F32)
    c = jnp.concatenate([ones(lead), cos, cos, ones(trail)], axis=-1)
    s = jnp.concatenate([zeros(lead), -sin, sin, zeros(trail)], axis=-1)
    reps = LANES // c.shape[1]
    c, s = jnp.tile(c, (1, reps)), jnp.tile(s, (1, reps))
    t = rows.tile
    c = jnp.concatenate([c.reshape(rows.lat_per_batch, t, LANES), jnp.ones((1, t, LANES), F32)], axis=0)
    s = jnp.concatenate([s.reshape(rows.lat_per_batch, t, LANES), jnp.zeros((1, t, LANES), F32)], axis=0)
    return c, s


def _rope_chunk(x, cos, sin, half):
    lane = lax.broadcasted_iota(jnp.int32, x.shape, 1)
    partner = jnp.where((lane % (2 * half)) < half, pltpu.roll(x, LANES - half, 1), pltpu.roll(x, half, 1))
    return x * cos + partner * sin


def _gqa_proj_kernel(x_ref, mod_ref, w_ref, b_ref, cos_ref, sin_ref, q_ref, kv_ref, *, q_dim, scale, half):
    m = mod_ref[0]
    h = (x_ref[...] * (1.0 + m[1:2]) + m[0:1]).astype(BF16)
    cos, sin = cos_ref[0], sin_ref[0]
    n_q = q_dim // LANES
    n_all = w_ref.shape[1] // LANES
    for c in range(n_all):
        cols = slice(c * LANES, (c + 1) * LANES)
        p = _bdot(h, w_ref[:, cols]) + b_ref[:, cols]
        is_v = c in (n_q + 1, n_q + 3)
        if not is_v:
            p = _rope_chunk(p, cos, sin, half)
        if c < n_q:
            q_ref[:, cols] = (p * scale).astype(BF16)
        else:
            kv_ref[:, (c - n_q) * LANES:(c - n_q + 1) * LANES] = p.astype(BF16)


def _gqa_attn_kernel(*refs, windowed, window, seq, n_pairs, pairs_per_kv):
    if windowed:
        q_ref, kv_ref, kvc_ref, sink_ref, o_ref = refs
    else:
        q_ref, kvc_ref, sink_ref, _, o_ref = refs
    tq = q_ref.shape[0]
    kvc = kvc_ref[...]
    if windowed:
        span = tq + 2 * window
        qs = pl.program_id(1) * tq
        ws = pl.multiple_of(jnp.clip(qs - window, 0, seq - span), LANES)
        kv = jnp.concatenate([kv_ref[pl.ds(ws, span), :], kvc], axis=0)
        n_keys = kv.shape[0]
        qpos = qs + lax.broadcasted_iota(jnp.int32, (tq, n_keys), 0)
        col = lax.broadcasted_iota(jnp.int32, (tq, n_keys), 1)
        mask = (jnp.abs(ws + col - qpos) <= window) | (col >= span)
    else:
        kv = kvc
        mask = None
    lane = lax.broadcasted_iota(jnp.int32, (kv.shape[0], LANES), 1)
    low = lane < GQA_HD
    zero = jnp.zeros((kv.shape[0], LANES), BF16)
    one_hi = (lane == GQA_HD).astype(F32).astype(BF16)
    one_lo = (lane == 0).astype(F32).astype(BF16)
    low_q = lax.broadcasted_iota(jnp.int32, (tq, LANES), 1) < GQA_HD
    k_plain, v_plain = kv[:, 0:LANES], kv[:, LANES:2 * LANES]
    k_swap, v_swap = kv[:, 2 * LANES:3 * LANES], kv[:, 3 * LANES:4 * LANES]
    k_half, v_half = {}, {}
    for g in range(GQA_KV):
        k_half[g, 0] = jnp.where(low, k_plain if g == 0 else k_swap, zero)
        k_half[g, 1] = jnp.where(low, zero, k_swap if g == 0 else k_plain)
        v_half[g, 0] = jnp.where(low, v_plain if g == 0 else v_swap, one_hi)
        v_half[g, 1] = jnp.where(low, one_lo, v_swap if g == 0 else v_plain)
    items = [(pr, hh) for pr in range(n_pairs) for hh in range(2)]

    def scores(item):
        pr, hh = item
        return _bdot_nt(q_ref[:, pr * LANES:(pr + 1) * LANES], k_half[pr // pairs_per_kv, hh])

    ahead = [scores(it) for it in items[:GQA_LOOKAHEAD]]
    acc = None
    for n, (pr, hh) in enumerate(items):
        s = ahead.pop(0)
        if n + GQA_LOOKAHEAD < len(items):
            ahead.append(scores(items[n + GQA_LOOKAHEAD]))
        head = 2 * pr + hh
        if mask is not None:
            s = jnp.where(mask, s, NEG_INF)
        sk = sink_ref[head:head + 1, 0:1]
        mx = jnp.maximum(jnp.max(s, axis=1, keepdims=True), sk)
        p = jnp.exp((s - mx).astype(BF16))
        pv = _bdot(p, v_half[pr // pairs_per_kv, hh])
        sum_lane = GQA_HD if hh == 0 else 0
        denom = pv[:, sum_lane:sum_lane + 1] + jnp.exp(sk - mx)
        part = pv * (1.0 / denom)
        if hh == 1:
            o_ref[:, pr * LANES:(pr + 1) * LANES] = jnp.where(low_q, acc, part).astype(BF16)
        acc = part


def _gqa_layer(rows, x, mod, p, ln_g, ln_b, alpha, need_ctx):
    t, d = rows.tile, x.shape[1]
    w_qkv, b_qkv = p["w_qkv"], p["b_qkv"]
    kv_dim = GQA_KV * GQA_HD
    q_dim = w_qkv.shape[1] - 2 * kv_dim
    n_heads = q_dim // GQA_HD
    assert kv_dim == LANES and GQA_KV == 2 and q_dim % LANES == 0
    swap = lambda a: jnp.concatenate([a[..., GQA_HD:], a[..., :GQA_HD]], axis=-1)
    wk, wv = w_qkv[:, q_dim:q_dim + kv_dim], w_qkv[:, q_dim + kv_dim:]
    bk, bv = b_qkv[q_dim:q_dim + kv_dim], b_qkv[q_dim + kv_dim:]
    w_ext = jnp.concatenate([w_qkv, swap(wk), swap(wv)], axis=1).astype(BF16)
    b_ext = jnp.concatenate([b_qkv, swap(bk), swap(bv)]).reshape(1, -1)
    n_ext = w_ext.shape[1]
    cos, sin = _rope_tables(rows, GQA_HD, 0, 0)
    q, kv = pl.pallas_call(
        functools.partial(_gqa_proj_kernel, q_dim=q_dim, scale=GQA_HD ** -0.5, half=GQA_HD // 2),
        grid=(rows.tiles,),
        in_specs=[pl.BlockSpec((t, d), lambda i: (i, 0)),
                  pl.BlockSpec((1, 6, d), lambda i: (rows.mod_index(i), 0, 0)),
                  pl.BlockSpec((d, n_ext), lambda i: (0, 0)),
                  pl.BlockSpec((1, n_ext), lambda i: (0, 0)),
                  pl.BlockSpec((1, t, LANES), lambda i: (rows.rope_index(i), 0, 0)),
                  pl.BlockSpec((1, t, LANES), lambda i: (rows.rope_index(i), 0, 0))],
        out_specs=[pl.BlockSpec((t, q_dim), lambda i: (i, 0)),
                   pl.BlockSpec((t, 4 * LANES), lambda i: (i, 0))],
        out_shape=[jax.ShapeDtypeStruct((rows.rows, q_dim), BF16),
                   jax.ShapeDtypeStruct((rows.rows, 4 * LANES), BF16)],
        compiler_params=_params("arbitrary"),
        name="gqa_qkv_proj",
    )(x, mod, w_ext, b_ext, cos, sin)

    sinks = jnp.broadcast_to(p["sinks"].astype(F32)[:, None], (n_heads, LANES))
    b_, s_, n_ctx = rows.batch, rows.seq, rows.n_ctx
    ctx_rows = b_ * n_ctx
    assert ctx_rows % s_ == 0 and s_ >= ATTN_Q + 2 * WINDOW
    q_blocks = s_ // ATTN_Q
    n_pairs = q_dim // LANES
    common = dict(window=WINDOW, seq=s_, n_pairs=n_pairs, pairs_per_kv=n_pairs // GQA_KV)
    o_shape = jax.ShapeDtypeStruct((rows.rows, q_dim), BF16)
    o_lat = pl.pallas_call(
        functools.partial(_gqa_attn_kernel, windowed=True, **common),
        grid=(b_, q_blocks),
        in_specs=[pl.BlockSpec((ATTN_Q, q_dim), lambda b, j: (ctx_rows // ATTN_Q + b * q_blocks + j, 0)),
                  pl.BlockSpec((s_, 4 * LANES), lambda b, j: (ctx_rows // s_ + b, 0)),
                  pl.BlockSpec((n_ctx, 4 * LANES), lambda b, j: (b, 0)),
                  pl.BlockSpec((n_heads, LANES), lambda b, j: (0, 0))],
        out_specs=pl.BlockSpec((ATTN_Q, q_dim), lambda b, j: (ctx_rows // ATTN_Q + b * q_blocks + j, 0)),
        out_shape=o_shape,
        compiler_params=_params("arbitrary", "arbitrary"),
        name="gqa_window_attn",
    )(q, kv, kv, sinks)
    if need_ctx:
        cq_blocks = n_ctx // ATTN_Q
        o = pl.pallas_call(
            functools.partial(_gqa_attn_kernel, windowed=False, **common),
            grid=(b_, cq_blocks),
            in_specs=[pl.BlockSpec((ATTN_Q, q_dim), lambda b, j: (b * cq_blocks + j, 0)),
                      pl.BlockSpec((n_ctx, 4 * LANES), lambda b, j: (b, 0)),
                      pl.BlockSpec((n_heads, LANES), lambda b, j: (0, 0)),
                      pl.BlockSpec(memory_space=pl.ANY)],
            out_specs=pl.BlockSpec((ATTN_Q, q_dim), lambda b, j: (b * cq_blocks + j, 0)),
            out_shape=o_shape,
            input_output_aliases={3: 0},
            compiler_params=_params("arbitrary", "arbitrary"),
            name="gqa_ctx_attn",
        )(q, kv, sinks, o_lat)
    else:
        o = o_lat
    return _out_ln(rows, o, p["w_o"], p["b_o"], x, mod, ln_g, ln_b, alpha=alpha,
                   first_tile=0 if need_ctx else rows.ctx_tiles)


def _mla_proj_kernel(x_ref, mod_ref, wd_ref, qn_ref, kvn_ref, wq_ref, wqs_ref, wk_ref, wv_ref, cos_ref, sin_ref,
                     q_ref, k_ref, vt_ref, *, q_lora, kv_lora, scale, n_heads):
    m = mod_ref[0]
    h = (x_ref[...] * (1.0 + m[1:2]) + m[0:1]).astype(BF16)
    p = _bdot(h, wd_ref[...])
    cq, ckv = p[:, :q_lora], p[:, q_lora:q_lora + kv_lora]
    cos, sin = cos_ref[0], sin_ref[0]
    k_rope = _rope_chunk(p[:, q_lora + kv_lora:], cos, sin, QK_ROPE // 2)
    cq = (cq * lax.rsqrt(jnp.mean(cq * cq, axis=-1, keepdims=True) + RMS_EPS) * qn_ref[...]).astype(BF16)
    ckv = (ckv * lax.rsqrt(jnp.mean(ckv * ckv, axis=-1, keepdims=True) + RMS_EPS) * kvn_ref[...]).astype(BF16)
    for hd in range(n_heads):
        cols = slice(hd * LANES, (hd + 1) * LANES)
        qh = _bdot(cq, wq_ref[:, cols]) * cos + _bdot(cq, wqs_ref[:, cols]) * sin
        q_ref[:, cols] = (qh * scale).astype(BF16)
        k_ref[:, cols] = (_bdot(ckv, wk_ref[:, cols]) + k_rope).astype(BF16)
    vt_ref[...] = _bdot(ckv, wv_ref[...]).T.astype(BF16)


def _mla_attn_kernel(*refs, n_heads):
    q_ref, k_ref, vt_ref = refs[:3]
    o_ref, m_scr, l_scr, acc_scr = refs[-4:]
    kt = pl.program_id(2)

    @pl.when(kt == 0)
    def _():
        m_scr[...] = jnp.full_like(m_scr, -jnp.inf)
        l_scr[...] = jnp.zeros_like(l_scr)
        acc_scr[...] = jnp.zeros_like(acc_scr)

    tq, tk = q_ref.shape[0], k_ref.shape[0]
    sub_q, sub_k = min(MLA_SUB_Q, tq), min(MLA_SUB_K, tk)
    n_kh = tk // sub_k
    ones = jnp.ones((BF16_ROWS, sub_k), BF16)
    items = [(hd, kh, qh) for hd in range(n_heads) for qh in range(tq // sub_q) for kh in range(n_kh)]

    def scores(item):
        hd, kh, qh = item
        cols = slice(hd * LANES, (hd + 1) * LANES)
        s = _bdot_nt(k_ref[kh * sub_k:(kh + 1) * sub_k, cols], q_ref[qh * sub_q:(qh + 1) * sub_q, cols])
        return s, jnp.max(s, axis=0, keepdims=True)

    ahead = [scores(it) for it in items[:MLA_LOOKAHEAD]]
    state = {}
    for n, (hd, kh, qh) in enumerate(items):
        rws = slice(hd * V_HD, (hd + 1) * V_HD)
        qcols = slice(qh * sub_q, (qh + 1) * sub_q)
        s_t, s_max = ahead.pop(0)
        if n + MLA_LOOKAHEAD < len(items):
            ahead.append(scores(items[n + MLA_LOOKAHEAD]))
        if kh == 0:
            state[hd, qh] = (m_scr[hd:hd + 1, qcols], l_scr[hd:hd + 1, qcols], acc_scr[rws, qcols])
        m_old, l_old, acc = state[hd, qh]
        m_new = jnp.maximum(m_old, s_max)
        p_t = jnp.exp2((s_t - m_new).astype(BF16))
        corr = jnp.exp2(m_old - m_new)
        vt = vt_ref[rws, kh * sub_k:(kh + 1) * sub_k]
        pv = _bdot(jnp.concatenate([vt, ones], axis=0), p_t)
        state[hd, qh] = (m_new, corr * l_old + pv[V_HD:V_HD + 1, :], acc * corr + pv[:V_HD, :])
        if kh == n_kh - 1:
            m_scr[hd:hd + 1, qcols], l_scr[hd:hd + 1, qcols], acc_scr[rws, qcols] = state.pop((hd, qh))

    @pl.when(kt == pl.num_programs(2) - 1)
    def _():
        for hd in range(n_heads):
            rws = slice(hd * V_HD, (hd + 1) * V_HD)
            acc_scr[rws, :] = acc_scr[rws, :] * (1.0 / l_scr[hd:hd + 1, :])
        o_ref[...] = acc_scr[...].T.astype(BF16)


def _mla_layer(rows, x, mod, p, ln_g, ln_b, alpha, need_ctx):
    t, d = rows.tile, x.shape[1]
    h_ = MLA_HEADS
    w_down, w_uq, w_ukv = p["w_down"], p["w_uq"], p["w_ukv"]
    q_lora = w_uq.shape[0]
    kv_lora = w_ukv.shape[0]
    qk = QK_NOPE + QK_ROPE
    assert QK_NOPE == V_HD == LANES // 2 and h_ % 2 == 0 and q_lora % LANES == 0 and kv_lora % LANES == 0
    pad = LANES - qk
    zc = lambda r, w: jnp.zeros((r, w), F32)
    wd_p = jnp.concatenate([w_down[:, :q_lora + kv_lora], zc(d, QK_NOPE), w_down[:, q_lora + kv_lora:],
                            zc(d, pad)], axis=1).astype(BF16)
    wq_p = jnp.concatenate([w_uq.reshape(q_lora, h_, qk), jnp.zeros((q_lora, h_, pad), F32)],
                           axis=-1).reshape(q_lora, h_ * LANES).astype(BF16)
    half = QK_ROPE // 2
    wq3 = wq_p.reshape(q_lora, h_, LANES)
    wq_s = jnp.concatenate([wq3[..., :QK_NOPE], wq3[..., QK_NOPE + half:qk], wq3[..., QK_NOPE:QK_NOPE + half],
                            wq3[..., qk:]], axis=-1).reshape(q_lora, h_ * LANES)
    ukv = w_ukv.reshape(kv_lora, h_, QK_NOPE + V_HD)
    wk_p = jnp.concatenate([ukv[..., :QK_NOPE], jnp.zeros((kv_lora, h_, LANES - QK_NOPE), F32)],
                           axis=-1).reshape(kv_lora, h_ * LANES).astype(BF16)
    wv_p = ukv[..., QK_NOPE:].reshape(kv_lora, h_ * V_HD).astype(BF16)
    cos, sin = _rope_tables(rows, QK_ROPE, QK_NOPE, pad)
    n_down = wd_p.shape[1]
    q, k, vt = pl.pallas_call(
        functools.partial(_mla_proj_kernel, q_lora=q_lora, kv_lora=kv_lora, scale=qk ** -0.5 * math.log2(math.e),
                          n_heads=h_),
        grid=(rows.tiles,),
        in_specs=[pl.BlockSpec((t, d), lambda i: (i, 0)),
                  pl.BlockSpec((1, 6, d), lambda i: (rows.mod_index(i), 0, 0)),
                  pl.BlockSpec((d, n_down), lambda i: (0, 0)),
                  pl.BlockSpec((1, q_lora), lambda i: (0, 0)),
                  pl.BlockSpec((1, kv_lora), lambda i: (0, 0)),
                  pl.BlockSpec((q_lora, h_ * LANES), lambda i: (0, 0)),
                  pl.BlockSpec((q_lora, h_ * LANES), lambda i: (0, 0)),
                  pl.BlockSpec((kv_lora, h_ * LANES), lambda i: (0, 0)),
                  pl.BlockSpec((kv_lora, h_ * V_HD), lambda i: (0, 0)),
                  pl.BlockSpec((1, t, LANES), lambda i: (rows.rope_index(i), 0, 0)),
                  pl.BlockSpec((1, t, LANES), lambda i: (rows.rope_index(i), 0, 0))],
        out_specs=[pl.BlockSpec((t, h_ * LANES), lambda i: (i, 0)),
                   pl.BlockSpec((t, h_ * LANES), lambda i: (i, 0)),
                   pl.BlockSpec((h_ * V_HD, t), lambda i: (0, i))],
        out_shape=[jax.ShapeDtypeStruct((rows.rows, h_ * LANES), BF16),
                   jax.ShapeDtypeStruct((rows.rows, h_ * LANES), BF16),
                   jax.ShapeDtypeStruct((h_ * V_HD, rows.rows), BF16)],
        compiler_params=_params("arbitrary"),
        name="mla_proj",
    )(x, mod, wd_p, p["q_norm"].reshape(1, -1), p["kv_norm"].reshape(1, -1), wq_p, wq_s, wk_p, wv_p, cos, sin)

    b_, s_, n_ctx = rows.batch, rows.seq, rows.n_ctx
    tk = MLA_TK
    tq, tq_ctx = min(MLA_TQ, s_), min(MLA_TQ, n_ctx)
    assert n_ctx % tk == 0 and s_ % tk == 0 and n_ctx % tq_ctx == 0 and s_ % tq == 0
    ck, lk = n_ctx // tk, s_ // tk
    ctx_kblocks = b_ * ck

    def kv_block(b, kt):
        return jnp.where(kt < ck, b * ck + kt, ctx_kblocks + b * lk + (kt - ck))

    def scratch(rows_q):
        return [pltpu.VMEM((h_, rows_q), F32), pltpu.VMEM((h_, rows_q), F32), pltpu.VMEM((h_ * V_HD, rows_q), F32)]

    o_shape = jax.ShapeDtypeStruct((rows.rows, h_ * V_HD), BF16)
    lq = s_ // tq
    ctx_qblocks = b_ * n_ctx // tq
    o_lat = pl.pallas_call(
        functools.partial(_mla_attn_kernel, n_heads=h_),
        grid=(b_, lq, ck + lk),
        in_specs=[pl.BlockSpec((tq, h_ * LANES), lambda b, i, kt: (ctx_qblocks + b * lq + i, 0)),
                  pl.BlockSpec((tk, h_ * LANES), lambda b, i, kt: (kv_block(b, kt), 0)),
                  pl.BlockSpec((h_ * V_HD, tk), lambda b, i, kt: (0, kv_block(b, kt)))],
        out_specs=pl.BlockSpec((tq, h_ * V_HD), lambda b, i, kt: (ctx_qblocks + b * lq + i, 0)),
        out_shape=o_shape,
        scratch_shapes=scratch(tq),
        compiler_params=_params("arbitrary", "arbitrary", "arbitrary"),
        name="mla_attn",
    )(q, k, vt)
    if need_ctx:
        cq = n_ctx // tq_ctx
        o = pl.pallas_call(
            functools.partial(_mla_attn_kernel, n_heads=h_),
            grid=(b_, cq, ck),
            in_specs=[pl.BlockSpec((tq_ctx, h_ * LANES), lambda b, i, kt: (b * cq + i, 0)),
                      pl.BlockSpec((tk, h_ * LANES), lambda b, i, kt: (b * ck + kt, 0)),
                      pl.BlockSpec((h_ * V_HD, tk), lambda b, i, kt: (0, b * ck + kt)),
                      pl.BlockSpec(memory_space=pl.ANY)],
            out_specs=pl.BlockSpec((tq_ctx, h_ * V_HD), lambda b, i, kt: (b * cq + i, 0)),
            out_shape=o_shape,
            scratch_shapes=scratch(tq_ctx),
            input_output_aliases={3: 0},
            compiler_params=_params("arbitrary", "arbitrary", "arbitrary"),
            name="mla_ctx_attn",
        )(q, k, vt, o_lat)
    else:
        o = o_lat
    return _out_ln(rows, o, p["w_o"], jnp.zeros((d,), F32), x, mod, ln_g, ln_b, alpha=alpha,
                   first_tile=0 if need_ctx else rows.ctx_tiles)


def _split_gu_kernel(w_ref, g_ref, u_ref):
    win = 2 * MXU_DIM
    r_i = lax.broadcasted_iota(jnp.int32, (win, MXU_DIM), 0)
    c_i = lax.broadcasted_iota(jnp.int32, (win, MXU_DIM), 1)
    pick_even = (r_i == 2 * c_i).astype(BF16)
    pick_odd = (r_i == 2 * c_i + 1).astype(BF16)
    for j in range(w_ref.shape[2] // win):
        w = w_ref[0, :, j * win:(j + 1) * win].astype(BF16)
        g_ref[0, :, j * MXU_DIM:(j + 1) * MXU_DIM] = _bdot(w, pick_even).astype(BF16)
        u_ref[0, :, j * MXU_DIM:(j + 1) * MXU_DIM] = _bdot(w, pick_odd).astype(BF16)


def _split_gate_up(w_gu):
    depth, n_exp, d, f2 = w_gu.shape
    w = w_gu.reshape(depth * n_exp, d, f2)
    tr = PROJ_TILE
    spec_out = pl.BlockSpec((1, tr, f2 // 2), lambda e, r: (e, r, 0))
    shape_out = jax.ShapeDtypeStruct((depth * n_exp, d, f2 // 2), BF16)
    return pl.pallas_call(
        _split_gu_kernel,
        grid=(depth * n_exp, d // tr),
        in_specs=[pl.BlockSpec((1, tr, f2), lambda e, r: (e, r, 0))],
        out_specs=[spec_out, spec_out],
        out_shape=[shape_out, shape_out],
        compiler_params=_params("arbitrary", "arbitrary"),
        name="moe_split_gate_up",
    )(w)


def _pack_pairs(a):
    w = a.shape[1] // 2
    r = a.astype(BF16).astype(F32)
    lo = lax.bitcast_convert_type(r[:, :w], jnp.uint32)
    hi = lax.bitcast_convert_type(r[:, w:], jnp.uint32)
    return (lo >> 16) | (hi & jnp.uint32(0xFFFF0000))


def _unpack_pairs(u):
    lo = lax.bitcast_convert_type(u << 16, F32)
    hi = lax.bitcast_convert_type(u & jnp.uint32(0xFFFF0000), F32)
    return jnp.concatenate([lo, hi], axis=1).astype(BF16)


def _seg_rows(t, n_exp):
    return t * TOP_K + n_exp * SEG_ALIGN


def _router_kernel(x_ref, mod_ref, rw_ref, rb_ref, by_token_ref, by_col_ref, cnt_ref, *, top_k):
    m = mod_ref[0]
    h = x_ref[...] * (1.0 + m[4:5]) + m[3:4]
    w_t = rw_ref[...]
    h_hi, w_hi = h.astype(BF16), w_t.astype(BF16)
    h_lo = (h - h_hi.astype(F32)).astype(BF16)
    w_lo = (w_t - w_hi.astype(F32)).astype(BF16)
    logits = _bdot_nt(w_hi, h_hi) + (_bdot_nt(w_lo, h_hi) + _bdot_nt(w_hi, h_lo)) + rb_ref[...]
    n_exp, t = logits.shape
    e_idx = lax.broadcasted_iota(jnp.int32, (n_exp, t), 0).astype(F32)
    work = logits
    sel, val = [], []
    for _ in range(top_k):
        mx = jnp.max(work, axis=0, keepdims=True)
        pick = jnp.min(jnp.where(work == mx, e_idx, float(n_exp)), axis=0, keepdims=True)
        sel.append(pick)
        val.append(mx)
        work = jnp.where(e_idx == pick, -jnp.inf, work)
    ex = [jnp.exp(v - val[0]) for v in val]
    tot = ex[0]
    for e in ex[1:]:
        tot = tot + e
    onehot = jnp.zeros((n_exp, t), F32)
    for pick in sel:
        onehot = onehot + (e_idx == pick).astype(F32)
    r_i = lax.broadcasted_iota(jnp.int32, (t, t), 0)
    c_i = lax.broadcasted_iota(jnp.int32, (t, t), 1)
    before = _bdot(onehot, (r_i < c_i).astype(BF16))
    count = jnp.sum(onehot, axis=1, keepdims=True)
    cap = jnp.floor((count + (SEG_ALIGN - 1)) * (1.0 / SEG_ALIGN)) * SEG_ALIGN
    e_r = lax.broadcasted_iota(jnp.int32, (n_exp, n_exp), 0)
    e_c = lax.broadcasted_iota(jnp.int32, (n_exp, n_exp), 1)
    seg_start = _bdot((e_c < e_r).astype(BF16), jnp.broadcast_to(cap, (n_exp, LANES)))[:, 0:1]
    base = before + seg_start
    rows_out = [jnp.sum(jnp.where(e_idx == sel[k], base, 0.0), axis=0, keepdims=True) for k in range(top_k)]
    rows_out += [ex[k] / tot for k in range(top_k)]
    r8 = lax.broadcasted_iota(jnp.int32, (SUBLANES, t), 0)
    slab = jnp.zeros((SUBLANES, t), F32)
    for k, v in enumerate(rows_out):
        slab = jnp.where(r8 == k, v, slab)
    by_token_ref[...] = slab
    by_col_ref[...] = jnp.concatenate([slab, jnp.zeros((LANES - SUBLANES, t), F32)], axis=0).T
    cnt_ref[0] = count


def _segment_copies(seg_ref, n_exp, make_copy, act):
    *small, big = SEG_PIECES
    for e in range(n_exp):
        g0, rows, l0 = seg_ref[0, 0, e], seg_ref[0, 0, n_exp + e], seg_ref[0, 0, 2 * n_exp + e]

        def whole(j, carry, g0=g0, l0=l0):
            act(make_copy(pl.multiple_of(l0 + j * big, SEG_ALIGN), pl.multiple_of(g0 + j * big, SEG_ALIGN), big))
            return carry

        lax.fori_loop(0, lax.shift_right_logical(rows, big.bit_length() - 1), whole, 0)
        for size in small:
            done = rows & (-2 * size)

            @pl.when((rows & size) != 0)
            def _(g0=g0, l0=l0, done=done, size=size):
                act(make_copy(pl.multiple_of(l0 + done, SEG_ALIGN), pl.multiple_of(g0 + done, SEG_ALIGN), size))


def _segment_waits(seg_ref, n_exp, make_copy):
    total = seg_ref[0, 0, 3 * n_exp]
    *small, big = WAIT_PIECES

    def whole(j, carry):
        make_copy(0, 0, big).wait()
        return carry

    lax.fori_loop(0, lax.shift_right_logical(total, big.bit_length() - 1), whole, 0)
    for size in small:
        @pl.when((total & size) != 0)
        def _(size=size):
            make_copy(0, 0, size).wait()


def _dispatch_kernel(tail_ref, seg_ref, seg_prev_ref, x_ref, mod_ref, slot_t_ref, xs_ref, buf, zbuf, sem, zsem, *,
                     n_exp, top_k):
    i = pl.program_id(0)
    cur = i % 2

    def to_slots(buf_slot, sem_slot):
        return lambda l, g, rows: pltpu.make_async_copy(buf.at[buf_slot, pl.ds(l, rows)], xs_ref.at[pl.ds(g, rows)],
                                                        sem.at[sem_slot])

    @pl.when(i == 0)
    def _():
        zbuf[...] = jnp.zeros_like(zbuf)
        zero_fill = lambda l, g, rows: pltpu.make_async_copy(zbuf.at[pl.ds(l, rows)], xs_ref.at[pl.ds(g, rows)], zsem)
        _segment_copies(tail_ref, n_exp, zero_fill, lambda c: c.start())
        _segment_copies(tail_ref, n_exp, zero_fill, lambda c: c.wait())

    m = mod_ref[0]
    h = (x_ref[...] * (1.0 + m[4:5]) + m[3:4]).astype(BF16)
    n_rows, t = buf.shape[1], x_ref.shape[0]
    row = lax.broadcasted_iota(jnp.int32, (n_rows, t), 0)
    slot_t = slot_t_ref[...].astype(jnp.int32)
    pick = row == slot_t[0:1, :]
    for k in range(1, top_k):
        pick = pick | (row == slot_t[k:k + 1, :])
    buf[cur] = _pack_pairs(_bdot(pick.astype(BF16), h))
    _segment_copies(seg_ref, n_exp, to_slots(cur, cur), lambda c: c.start())

    @pl.when(i > 0)
    def _():
        _segment_waits(seg_prev_ref, n_exp, to_slots(1 - cur, 1 - cur))

    @pl.when(i == pl.num_programs(0) - 1)
    def _():
        _segment_waits(seg_ref, n_exp, to_slots(cur, cur))


def _expert_kernel(be_ref, nu_ref, x_ref, wg_ref, wu_ref, bg_ref, bu_ref, wd_ref, bd_ref, y_ref):
    del be_ref

    @pl.when(pl.program_id(0) < nu_ref[0])
    def _():
        def gate_up(part):
            xb = _unpack_pairs(x_ref[part * MOE_ROWS:(part + 1) * MOE_ROWS, :])
            return _bdot(xb, wg_ref[0]), _bdot(xb, wu_ref[0])

        n_parts = x_ref.shape[0] // MOE_ROWS
        nxt = gate_up(0)
        for part in range(n_parts):
            g, u = nxt
            if part + 1 < n_parts:
                nxt = gate_up(part + 1)
            g = jnp.minimum(g + bg_ref[0], SWIGLU_LIMIT)
            u = jnp.clip(u + bu_ref[0], -SWIGLU_LIMIT, SWIGLU_LIMIT)
            act = (u + 1.0) * (g * jax.nn.sigmoid(SWIGLU_ALPHA * g))
            y_ref[part * MOE_ROWS:(part + 1) * MOE_ROWS, :] = _pack_pairs(_bdot(act, wd_ref[0]) + bd_ref[0])


def _combine_kernel(seg_ref, seg_next_ref, x_ref, mod_ref, route_ref, g_ref, bt_ref, y_ref, o_ref,
                    ybuf, sem, *, n_exp, top_k, alpha):
    i = pl.program_id(0)
    cur = i % 2

    def from_slots(buf_slot):
        return lambda l, g, rows: pltpu.make_async_copy(y_ref.at[pl.ds(g, rows)], ybuf.at[buf_slot, pl.ds(l, rows)],
                                                        sem.at[buf_slot])

    @pl.when(i == 0)
    def _():
        ybuf[...] = jnp.zeros_like(ybuf)
        _segment_copies(seg_ref, n_exp, from_slots(0), lambda c: c.start())

    @pl.when(i < pl.num_programs(0) - 1)
    def _():
        _segment_copies(seg_next_ref, n_exp, from_slots(1 - cur), lambda c: c.start())

    _segment_waits(seg_ref, n_exp, from_slots(cur))
    t, n_rows = x_ref.shape[0], ybuf.shape[1]
    lane = lax.broadcasted_iota(jnp.int32, (t, n_rows), 1)
    route = route_ref[...]
    slot = route[:, :top_k].astype(jnp.int32)
    weights = jnp.zeros((t, n_rows), F32)
    for k in range(top_k):
        weights = jnp.where(lane == slot[:, k:k + 1], route[:, top_k + k:top_k + k + 1], weights)
    acc = _bdot(weights, _unpack_pairs(ybuf[cur]))
    m = mod_ref[0]
    r = alpha * x_ref[...] + m[5:6] * acc
    o_ref[...] = _layer_norm(r, g_ref[...], bt_ref[...])


def _moe_layer(rows, x, mod, router_w, router_b, wg, wu, wd, layer, b_gu, b_down, ln_g, ln_b, alpha, need_ctx):
    t, d = rows.tile, x.shape[1]
    n_exp = router_w.shape[1]
    d_ff = wd.shape[1]
    first = 0 if need_ctx else rows.ctx_tiles
    n_tiles = rows.tiles - first
    n_tok = n_tiles * t
    row = lambda i: (i + first, 0)
    modi = lambda i: (rows.mod_index(i + first), 0, 0)
    tok = lambda i: (i, 0)

    seg_rows = _seg_rows(t, n_exp)
    dp = d // 2
    assert 4 * n_exp <= LANES and t % SEG_PIECES[-1] == 0 and MOE_BLOCK % SEG_PIECES[-1] == 0
    assert 2 * TOP_K <= SUBLANES
    slot_t, route, cnt = pl.pallas_call(
        functools.partial(_router_kernel, top_k=TOP_K),
        grid=(n_tiles,),
        in_specs=[pl.BlockSpec((t, d), row), pl.BlockSpec((1, 6, d), modi),
                  pl.BlockSpec((n_exp, d), lambda i: (0, 0)), pl.BlockSpec((n_exp, 1), lambda i: (0, 0))],
        out_specs=[pl.BlockSpec((SUBLANES, t), tok), pl.BlockSpec((t, LANES), tok),
                   pl.BlockSpec((1, n_exp, 1), lambda i: (i, 0, 0))],
        out_shape=[jax.ShapeDtypeStruct((n_tiles * SUBLANES, t), F32), jax.ShapeDtypeStruct((n_tok, LANES), F32),
                   jax.ShapeDtypeStruct((n_tiles, n_exp, 1), F32)],
        compiler_params=_params("arbitrary"),
        name="moe_router",
    )(x, mod, router_w.T, router_b.reshape(n_exp, 1))

    i32 = jnp.int32
    count = cnt.reshape(n_tiles, n_exp).astype(i32)
    cap = (count + SEG_ALIGN - 1) // SEG_ALIGN * SEG_ALIGN
    e_rows = jnp.sum(cap, axis=0)
    e_pad = (e_rows + MOE_BLOCK - 1) // MOE_BLOCK * MOE_BLOCK
    e_end = jnp.cumsum(e_pad)
    e_start = e_end - e_pad
    seg_global = e_start[None, :] + jnp.cumsum(cap, axis=0) - cap
    seg_local = jnp.cumsum(cap, axis=1) - cap
    fill = jnp.zeros((n_tiles, LANES - 3 * n_exp - 1), i32)
    seg = jnp.concatenate([seg_global, cap, seg_local, jnp.sum(cap, axis=1, keepdims=True), fill], axis=1)
    seg = seg.astype(i32).reshape(n_tiles, 1, LANES)
    tail = jnp.concatenate([e_start + e_rows, e_pad - e_rows, jnp.zeros((LANES - 2 * n_exp,), i32)])
    tail = tail.astype(i32).reshape(1, 1, LANES)
    max_slots = n_tok * TOP_K + n_tiles * n_exp * (SEG_ALIGN - 1) + n_exp * (MOE_BLOCK - 1)
    n_blocks = (max_slots + MOE_BLOCK - 1) // MOE_BLOCK
    n_slots = n_blocks * MOE_BLOCK
    n_used = (e_end[-1] // MOE_BLOCK).astype(i32)
    blk = jnp.minimum(jnp.arange(n_blocks, dtype=i32), n_used - 1) * MOE_BLOCK
    block_expert = jnp.minimum(jnp.sum((e_end[None, :] <= blk[:, None]).astype(i32), axis=1), n_exp - 1)

    smem_tile = lambda index_map: pl.BlockSpec((1, 1, LANES), index_map, memory_space=pltpu.SMEM)
    xs = pl.pallas_call(
        functools.partial(_dispatch_kernel, n_exp=n_exp, top_k=TOP_K),
        grid=(n_tiles,),
        in_specs=[smem_tile(lambda i: (0, 0, 0)), smem_tile(lambda i: (i, 0, 0)),
                  smem_tile(lambda i: (jnp.maximum(i - 1, 0), 0, 0)),
                  pl.BlockSpec((t, d), row), pl.BlockSpec((1, 6, d), modi), pl.BlockSpec((SUBLANES, t), tok)],
        out_specs=pl.BlockSpec(memory_space=pl.ANY),
        out_shape=jax.ShapeDtypeStruct((n_slots, dp), jnp.uint32),
        scratch_shapes=[pltpu.VMEM((2, seg_rows, dp), jnp.uint32), pltpu.VMEM((MOE_BLOCK, dp), jnp.uint32),
                        pltpu.SemaphoreType.DMA((2,)), pltpu.SemaphoreType.DMA(())],
        compiler_params=_params("arbitrary"),
        name="moe_dispatch",
    )(tail, seg, seg, x, mod, slot_t)

    bg = b_gu[:, 0::2].reshape(n_exp, 1, d_ff)
    bu = b_gu[:, 1::2].reshape(n_exp, 1, d_ff)
    used = lambda i, be, nu: (jnp.minimum(i, nu[0] - 1), 0)
    wmap = lambda i, be, nu: (be[i], 0, 0)
    wmap_all = lambda i, be, nu: (be[i] + layer * n_exp, 0, 0)
    y = pl.pallas_call(
        _expert_kernel,
        grid_spec=pltpu.PrefetchScalarGridSpec(
            num_scalar_prefetch=2,
            grid=(n_blocks,),
            in_specs=[pl.BlockSpec((MOE_BLOCK, dp), used),
                      pl.BlockSpec((1, d, d_ff), wmap_all), pl.BlockSpec((1, d, d_ff), wmap_all),
                      pl.BlockSpec((1, 1, d_ff), wmap), pl.BlockSpec((1, 1, d_ff), wmap),
                      pl.BlockSpec((1, d_ff, d), wmap_all), pl.BlockSpec((1, 1, d), wmap)],
            out_specs=pl.BlockSpec((MOE_BLOCK, dp), used)),
        out_shape=jax.ShapeDtypeStruct((n_slots, dp), jnp.uint32),
        compiler_params=_params("arbitrary"),
        name="moe_experts",
    )(block_expert, n_used.reshape(1), xs, wg, wu, bg, bu, wd, b_down.reshape(n_exp, 1, d))

    return pl.pallas_call(
        functools.partial(_combine_kernel, n_exp=n_exp, top_k=TOP_K, alpha=alpha),
        grid=(n_tiles,),
        in_specs=[smem_tile(lambda i: (i, 0, 0)), smem_tile(lambda i: (jnp.minimum(i + 1, n_tiles - 1), 0, 0)),
                  pl.BlockSpec((t, d), row), pl.BlockSpec((1, 6, d), modi),
                  pl.BlockSpec((t, LANES), tok),
                  pl.BlockSpec((1, d), lambda i: (0, 0)), pl.BlockSpec((1, d), lambda i: (0, 0)),
                  pl.BlockSpec(memory_space=pl.ANY)],
        out_specs=pl.BlockSpec((t, d), row if need_ctx else tok),
        out_shape=jax.ShapeDtypeStruct(x.shape if need_ctx else (n_tok, d), F32),
        scratch_shapes=[pltpu.VMEM((2, seg_rows, dp), jnp.uint32), pltpu.SemaphoreType.DMA((2,))],
        input_output_aliases={2: 0} if need_ctx else {},
        compiler_params=_params("arbitrary"),
        name="moe_combine_ln",
    )(seg, seg, x, mod, route, ln_g.reshape(1, d), ln_b.reshape(1, d), y)


def kernel(x, c, ctx, c_ctx, ada_w, ada_b, ln1_g, ln1_b, ln2_g, ln2_b, router_w, router_b, exp_gu_w, exp_gu_b, exp_down_w, exp_down_b, rg_w_in, rg_conv_w, rg_conv_b, rg_gate_a_w, rg_gate_a_b, rg_gate_x_w, rg_gate_x_b, rg_lambda, rg_w_out, gqa_w_qkv, gqa_b_qkv, gqa_sinks, gqa_w_o, gqa_b_o, mla_w_down, mla_q_norm, mla_kv_norm, mla_w_uq, mla_w_ukv, mla_w_o):
    batch, seq, d = x.shape
    n_ctx = ctx.shape[1]
    depth = ada_w.shape[0]
    alpha = (2 * depth) ** 0.25
    rows = _Rows(batch, n_ctx, seq, ROW_TILE)
    prows = _Rows(batch, n_ctx, seq, PROJ_TILE) if (batch * n_ctx) % PROJ_TILE == 0 and seq % PROJ_TILE == 0 else rows
    xa = jnp.concatenate([ctx.reshape(batch * n_ctx, d), x.reshape(batch * seq, d)], axis=0)
    mods = _ada_table(jnp.concatenate([c_ctx[None, :], c], axis=0), ada_w, ada_b)
    wg, wu = _split_gate_up(exp_gu_w)
    wd = exp_down_w.reshape((-1,) + exp_down_w.shape[2:]).astype(BF16)
    for i in range(depth):
        need_ctx = i < depth - 1
        kind, j = i % 3, i // 3
        mod = mods[i]
        if kind == 0:
            prm = dict(w_in=rg_w_in[j], conv_w=rg_conv_w[j], conv_b=rg_conv_b[j], gate_a_w=rg_gate_a_w[j],
                       gate_a_b=rg_gate_a_b[j], gate_x_w=rg_gate_x_w[j], gate_x_b=rg_gate_x_b[j],
                       lam=rg_lambda[j], w_out=rg_w_out[j])
            xa = _rglru_layer(rows, prows, xa, mod, prm, ln1_g[i], ln1_b[i], alpha, need_ctx)
        elif kind == 1:
            prm = dict(w_qkv=gqa_w_qkv[j], b_qkv=gqa_b_qkv[j], sinks=gqa_sinks[j], w_o=gqa_w_o[j], b_o=gqa_b_o[j])
            xa = _gqa_layer(prows, xa, mod, prm, ln1_g[i], ln1_b[i], alpha, need_ctx)
        else:
            prm = dict(w_down=mla_w_down[j], q_norm=mla_q_norm[j], kv_norm=mla_kv_norm[j], w_uq=mla_w_uq[j],
                       w_ukv=mla_w_ukv[j], w_o=mla_w_o[j])
            xa = _mla_layer(prows, xa, mod, prm, ln1_g[i], ln1_b[i], alpha, need_ctx)
        xa = _moe_layer(rows, xa, mod, router_w[i], router_b[i], wg, wu, wd, i, exp_gu_b[i], exp_down_b[i],
                        ln2_g[i], ln2_b[i], alpha, need_ctx)
    return xa.reshape(batch, seq, d)
```
